```python
import math
import jax, jax.numpy as jnp
from jax import lax
import numpy as np

D_MODEL = 2048
BATCH = 8
SEQ = 2048
DEPTH = 2

GRID_W = 64
CTX_LEN = 256

GROUP_WIDTH = D_MODEL // 2
N_BRANCH = 3
RW_HEAD_DIM = 64
RW_HEADS = GROUP_WIDTH // RW_HEAD_DIM
RW_DECAY_RANK = 64
RW_ICL_RANK = 64
RW_GATE_RANK = 128
RW_GN_EPS = 64e-5
DA_QK_DIM = 64
DA_V_DIM = 2 * DA_QK_DIM
DA_HEADS = GROUP_WIDTH // DA_V_DIM
QUERY_BLOCK = 128
ROPE_THETA = 10000.0
ROPE_AXIS_FREQS = DA_QK_DIM // 4
DA_SUBLN_EPS = 1e-5
HG_EXPAND = 128
HG_HEADS = GROUP_WIDTH // HG_EXPAND
HG_CHUNK = 32
N_EXPERTS = 16
EXPERT_FF = D_MODEL
EC_CAPACITY_FACTOR = 2
NORM_EPS = 1e-6

A_COLS = 3 * GROUP_WIDTH + 2 * RW_DECAY_RANK + 2 * RW_ICL_RANK + RW_GATE_RANK
B_COLS = 3 * GROUP_WIDTH
C_COLS = 5 * GROUP_WIDTH
G_COLS = N_BRANCH * D_MODEL
COL_B = A_COLS
COL_C = COL_B + B_COLS
COL_G = COL_C + C_COLS
N_IN = COL_G + G_COLS

kernel_name = 'hybrid_rwkv7_diffattn_hgrn2_ecmoe_dit'


def rmsnorm(x, g, eps=NORM_EPS):
    xf = x.astype(jnp.float32)
    return (xf * lax.rsqrt(jnp.mean(xf * xf, axis=-1, keepdims=True) + eps)).astype(x.dtype) * g


def modulate(h, shift, scale):
    return h * (1 + scale) + shift


def centred_token_shift(z, mu):
    prev = jnp.pad(z, ((0, 0), (1, 0), (0, 0)))[:, :-1]
    nxt = jnp.pad(z, ((0, 0), (0, 1), (0, 0)))[:, 1:]
    return z + mu[0] * (prev - z) + mu[1] * (nxt - z)


def rw_heads(t):
    return t.reshape(*t.shape[:-1], RW_HEADS, RW_HEAD_DIM)


def rwkv7_step(S, inp):
    r, w, kk, a, kt, v = inp
    S = (S * w[:, :, None, :]
         - jnp.einsum('bhvk,bhk->bhv', S, kk)[..., None] * (a * kk)[:, :, None, :]
         + v[..., None] * kt[:, :, None, :])
    return S, jnp.einsum('bhvk,bhk->bhv', S, r)


def rwkv7_scan(S0, seq, reverse):
    xs = tuple(jnp.moveaxis(t, 1, 0) for t in seq)
    S, ys = lax.scan(rwkv7_step, S0, xs, reverse=reverse)
    return S, jnp.moveaxis(ys, 0, 1)


def rwkv7_prep(za, mu, w0, wB, a0, aB, gB, k_k, k_a):
    B, L, _ = za.shape
    W = GROUP_WIDTH
    z = centred_token_shift(za, mu).astype(jnp.float32)
    o4 = 3 * W + 2 * RW_DECAY_RANK
    o5 = o4 + 2 * RW_ICL_RANK
    r, k, v = z[..., :W], z[..., W:2 * W], z[..., 2 * W:3 * W]
    wd = z[..., 3 * W:o4].reshape(B, L, 2, RW_DECAY_RANK)
    ad = z[..., o4:o5].reshape(B, L, 2, RW_ICL_RANK)
    gd = z[..., o5:]
    decay = jnp.exp(-math.exp(-0.5) * jax.nn.sigmoid(w0 + jnp.einsum('blir,irw->bliw', jnp.tanh(wd), wB)))
    a = jax.nn.sigmoid(a0 + jnp.einsum('blir,irw->bliw', ad, aB))
    g = jax.nn.sigmoid(gd) @ gB
    kk = rw_heads(k * k_k)
    kk = kk / jnp.maximum(jnp.sqrt(jnp.sum(kk * kk, axis=-1, keepdims=True)), 1e-12)
    kt = k[:, :, None, :] * (1 + (a - 1) * k_a)
    return dict(r=rw_heads(r), v=rw_heads(v), kk=kk, g=g,
                decay=rw_heads(decay), a=rw_heads(a), kt=rw_heads(kt))


def rwkv7_readout(y, p, d, r_k, gn):
    B, L = y.shape[:2]
    mu = jnp.mean(y, axis=-1, keepdims=True)
    var = jnp.mean(jnp.square(y - mu), axis=-1, keepdims=True)
    yn = ((y - mu) * lax.rsqrt(var + RW_GN_EPS)).reshape(B, L, GROUP_WIDTH) * gn[0] + gn[1]
    bonus = jnp.sum(p['r'] * p['kt'][:, :, d] * rw_heads(r_k), axis=-1, keepdims=True) * p['v']
    return yn + bonus.reshape(B, L, GROUP_WIDTH)


def rwkv7_mixer(za_c, za_x, mu, w0, wB, a0, aB, gB, k_k, k_a, r_k, gn, with_ctx):
    pc = rwkv7_prep(za_c, mu, w0, wB, a0, aB, gB, k_k, k_a)
    px = rwkv7_prep(za_x, mu, w0, wB, a0, aB, gB, k_k, k_a)
    S0 = jnp.zeros((za_x.shape[0], RW_HEADS, RW_HEAD_DIM, RW_HEAD_DIM), jnp.float32)
    outs_c, outs_x = [], []
    for d, reverse in ((0, False), (1, True)):
        def seq(p):
            return (p['r'], p['decay'][:, :, d], p['kk'], p['a'][:, :, d], p['kt'][:, :, d], p['v'])
        S_c, y_c = rwkv7_scan(S0, seq(pc), reverse)
        _, y_x = rwkv7_scan(S_c, seq(px), reverse)
        outs_x.append(rwkv7_readout(y_x, px, d, r_k, gn))
        if with_ctx:
            outs_c.append(rwkv7_readout(y_c, pc, d, r_k, gn))
    y_x = (outs_x[0] + outs_x[1]) * px['g']
    y_c = (outs_c[0] + outs_c[1]) * pc['g'] if with_ctx else None
    return y_c, y_x


def axial_rope_tables(L):
    rows = L // GRID_W
    row = jnp.broadcast_to(jnp.arange(rows)[:, None], (rows, GRID_W)).reshape(L)
    col = jnp.broadcast_to(jnp.arange(GRID_W)[None, :], (rows, GRID_W)).reshape(L)
    inv = 1.0 / (ROPE_THETA ** (jnp.arange(ROPE_AXIS_FREQS, dtype=jnp.float32) / ROPE_AXIS_FREQS))
    ang = jnp.stack([row, col], axis=-1).astype(jnp.float32)[:, :, None] * inv
    return jnp.cos(ang), jnp.sin(ang)


def apply_axial_rope(t, cos, sin):
    sh = t.shape
    tt = t.reshape(*sh[:-1], 2, 2, ROPE_AXIS_FREQS)
    c = cos[None, :, None, None, :, :]
    s = sin[None, :, None, None, :, :]
    x1, x2 = tt[..., 0, :], tt[..., 1, :]
    out = jnp.stack([x1 * c - x2 * s, x2 * c + x1 * s], axis=-2)
    return out.reshape(sh).astype(t.dtype)


def diff_softmax_attend(q, k, v, lam):
    s = jnp.einsum('bqhid,bkhid->bhiqk', q, k).astype(jnp.float32) * (DA_QK_DIM ** -0.5)
    p = jax.nn.softmax(s, axis=-1)
    w = (p[:, :, 0] - lam * p[:, :, 1]).astype(v.dtype)
    return jnp.einsum('bhqk,bkhv->bqhv', w, v)


def diff_attn_mixer(zb_c, zb_x, lam_p, subln, lam_init, rope_cos, rope_sin, with_ctx):
    W = GROUP_WIDTH

    def split(zb):
        B, L, _ = zb.shape
        q = zb[..., :W].reshape(B, L, DA_HEADS, 2, DA_QK_DIM)
        k = zb[..., W:2 * W].reshape(B, L, DA_HEADS, 2, DA_QK_DIM)
        v = zb[..., 2 * W:].reshape(B, L, DA_HEADS, DA_V_DIM)
        return q, k, v

    q_c, k_c, v_c = split(zb_c)
    q_x, k_x, v_x = split(zb_x)
    q_x = apply_axial_rope(q_x, rope_cos, rope_sin)
    k_x = apply_axial_rope(k_x, rope_cos, rope_sin)
    lp = lam_p.astype(jnp.float32)
    lam = jnp.exp(jnp.sum(lp[0] * lp[1])) - jnp.exp(jnp.sum(lp[2] * lp[3])) + lam_init
    k_all = jnp.concatenate([k_c, k_x], axis=1)
    v_all = jnp.concatenate([v_c, v_x], axis=1)
    B, L = q_x.shape[:2]
    nb = L // QUERY_BLOCK
    q_blk = jnp.moveaxis(q_x.reshape(B, nb, QUERY_BLOCK, DA_HEADS, 2, DA_QK_DIM), 1, 0)
    o_x = lax.map(lambda qb: diff_softmax_attend(qb, k_all, v_all, lam), q_blk)
    o_x = jnp.moveaxis(o_x, 0, 1).reshape(B, L, DA_HEADS, DA_V_DIM)

    def post(o):
        return (rmsnorm(o, subln, DA_SUBLN_EPS) * (1 - lam_init)).reshape(*o.shape[:2], W)

    y_c = post(diff_softmax_attend(q_c, k_c, v_c, lam)) if with_ctx else None
    return y_c, post(o_x)


def hg_heads(t):
    B, L, _ = t.shape
    return t.reshape(B, L, HG_HEADS, HG_EXPAND).transpose(0, 2, 1, 3)


def gla_chunked(S0, q, k, v, logf):
    B, H, L, E = q.shape
    n = L // HG_CHUNK

    def chunks(t):
        return t.reshape(B, H, n, HG_CHUNK, t.shape[-1]).transpose(2, 0, 1, 3, 4)

    causal = jnp.tril(jnp.ones((HG_CHUNK, HG_CHUNK), dtype=bool))[:, :, None]

    def step(S, inp):
        qc, kc, vc, lfc = inp
        b = jnp.cumsum(lfc, axis=2)
        o_inter = jnp.einsum('bhtd,bhdv->bhtv', qc * jnp.exp(b), S)
        diff = jnp.where(causal, b[:, :, :, None, :] - b[:, :, None, :, :], -jnp.inf)
        att = jnp.einsum('bhtsd,bhsd->bhts', qc[:, :, :, None, :] * jnp.exp(diff), kc)
        o_intra = jnp.einsum('bhts,bhsv->bhtv', att, vc)
        b_end = b[:, :, -1:, :]
        S = jnp.exp(b_end[:, :, 0, :])[..., None] * S + jnp.einsum('bhsd,bhsv->bhdv', kc * jnp.exp(b_end - b), vc)
        return S, o_inter + o_intra

    S, o = lax.scan(step, S0, (chunks(q), chunks(k), chunks(v), chunks(logf)))
    return S, o.transpose(1, 2, 0, 3, 4).reshape(B, H, L, -1)


def hgrn2_mixer(zc_c, zc_x, lb, norm_g, with_ctx):
    def prep(z):
        z = z.astype(jnp.float32)
        q, f_fwd, f_bwd, i, g = jnp.split(z, 5, axis=-1)

        def gate(fz):
            f = lb + (1 - lb) * jax.nn.sigmoid(fz)
            return hg_heads(1 - f), hg_heads(jnp.log(f))

        return hg_heads(jax.nn.silu(q)), (gate(f_fwd), gate(f_bwd)), hg_heads(i), g

    q_c, gates_c, i_c, g_c = prep(zc_c)
    q_x, gates_x, i_x, g_x = prep(zc_x)
    S0 = jnp.zeros((zc_x.shape[0], HG_HEADS, HG_EXPAND, HG_EXPAND), jnp.float32)
    ys_c, ys_x = [], []
    for d in range(2):
        flip = (lambda t: jnp.flip(t, axis=2)) if d == 1 else (lambda t: t)
        (k_c, lf_c), (k_x, lf_x) = gates_c[d], gates_x[d]
        S_c, y_c = gla_chunked(S0, flip(q_c), flip(k_c), flip(i_c), flip(lf_c))
        _, y_x = gla_chunked(S_c, flip(q_x), flip(k_x), flip(i_x), flip(lf_x))
        ys_c.append(flip(y_c))
        ys_x.append(flip(y_x))

    def post(o, g):
        B, H, L, E = o.shape
        return rmsnorm(o, norm_g).transpose(0, 2, 1, 3).reshape(B, L, H * E) * jax.nn.silu(g)

    y_c = post(ys_c[0] + ys_c[1], g_c) if with_ctx else None
    return y_c, post(ys_x[0] + ys_x[1], g_x)


def merge_branches(gate_logits, ys, w_branch, w_out):
    gates = jax.nn.sigmoid(gate_logits)
    merged = None
    for i, y in enumerate(ys):
        term = gates[..., i * D_MODEL:(i + 1) * D_MODEL] * (y @ w_branch[i])
        merged = term if merged is None else merged + term
    return merged @ w_out


def token_mixer(hc, hx, w_in, rw_mu, rw_w0, rw_wB, rw_a0, rw_aB, rw_gB, rw_kk, rw_ka, rw_rk, rw_gn,
                da_lambda, da_subln, lam_init, rope_cos, rope_sin, hg_lb, hg_norm, w_branch, w_out, with_ctx):
    Lc = hc.shape[1]
    h_all = jnp.concatenate([hc, hx], axis=1)

    def group(lo, hi):
        z = h_all @ w_in[:, lo:hi]
        return z[:, :Lc], z[:, Lc:]

    za_c, za_x = group(0, COL_B)
    ya_c, ya_x = rwkv7_mixer(za_c, za_x, rw_mu, rw_w0, rw_wB, rw_a0, rw_aB, rw_gB, rw_kk, rw_ka, rw_rk, rw_gn, with_ctx)
    zb_c, zb_x = group(COL_B, COL_C)
    yb_c, yb_x = diff_attn_mixer(zb_c, zb_x, da_lambda, da_subln, lam_init, rope_cos, rope_sin, with_ctx)
    zc_c, zc_x = group(COL_C, COL_G)
    yc_c, yc_x = hgrn2_mixer(zc_c, zc_x, hg_lb, hg_norm, with_ctx)
    w_gate = w_in[:, COL_G:]
    dt = hx.dtype
    out_x = merge_branches(hx @ w_gate, (ya_x.astype(dt), yb_x.astype(dt), yc_x.astype(dt)), w_branch, w_out)
    out_c = None
    if with_ctx:
        out_c = merge_branches(hc @ w_gate, (ya_c.astype(dt), yb_c.astype(dt), yc_c.astype(dt)), w_branch, w_out)
    return out_c, out_x


def ec_moe(h, router, w1, w3, w2):
    n = h.shape[1]
    cap = EC_CAPACITY_FACTOR * n // N_EXPERTS

    def one_set(hs):
        aff = jax.nn.softmax((hs @ router).astype(jnp.float32), axis=-1)
        gate, idx = lax.top_k(aff.T, cap)
        xe = hs[idx]
        hid = jax.nn.silu(jnp.einsum('ecd,edf->ecf', xe, w1)) * jnp.einsum('ecd,edf->ecf', xe, w3)
        ye = jnp.einsum('ecf,efd->ecd', hid, w2) * gate[..., None].astype(hs.dtype)
        return jnp.zeros_like(hs).at[idx.reshape(-1)].add(ye.reshape(-1, hs.shape[-1]))

    return jax.vmap(one_set)(h)


def setup_inputs(seed: int = 0) -> dict:
    key = jax.random.key(seed)
    ks = jax.random.split(key, 32)
    f32 = jnp.float32
    D = D_MODEL
    W = GROUP_WIDTH

    def nrm(k, shape, scale):
        return jax.random.normal(k, shape, f32) * scale

    return {
        'x': nrm(ks[0], (BATCH, SEQ, D), 1.0),
        'c': nrm(ks[1], (BATCH, D), 1.0),
        'ctx': nrm(ks[2], (BATCH, CTX_LEN, D), 1.0),
        'c_ctx': nrm(ks[3], (D,), 1.0),
        'w_ada': nrm(ks[4], (DEPTH, D, 6 * D), 0.5 * D ** -0.5),
        'b_ada': nrm(ks[5], (DEPTH, 6 * D), 0.02),
        'norm_g': 1.0 + nrm(ks[6], (DEPTH, 4, D), 0.02),
        'w_in': nrm(ks[7], (DEPTH, D, N_IN), D ** -0.5),
        'rw_mu': jax.random.uniform(ks[8], (DEPTH, 2, A_COLS), f32, 0.0, 0.5),
        'rw_w0': nrm(ks[9], (DEPTH, 2, W), 1.0),
        'rw_wB': nrm(ks[10], (DEPTH, 2, RW_DECAY_RANK, W), 0.1),
        'rw_a0': nrm(ks[11], (DEPTH, 2, W), 0.5),
        'rw_aB': nrm(ks[12], (DEPTH, 2, RW_ICL_RANK, W), 0.1),
        'rw_gB': nrm(ks[13], (DEPTH, RW_GATE_RANK, W), RW_GATE_RANK ** -0.5),
        'rw_kk': 0.85 + nrm(ks[14], (DEPTH, W), 0.05),
        'rw_ka': 1.0 + nrm(ks[15], (DEPTH, W), 0.05),
        'rw_rk': nrm(ks[16], (DEPTH, W), 0.1),
        'rw_gn': jnp.stack([1.0 + nrm(ks[17], (DEPTH, W), 0.02), nrm(ks[18], (DEPTH, W), 0.02)], axis=1),
        'da_lambda': nrm(ks[19], (DEPTH, 4, DA_QK_DIM), 0.1),
        'da_subln': 1.0 + nrm(ks[20], (DEPTH, DA_V_DIM), 0.02),
        'hg_lb_logits': nrm(ks[21], (DEPTH, W), 0.5),
        'hg_norm': 1.0 + nrm(ks[22], (DEPTH, HG_EXPAND), 0.02),
        'w_branch': nrm(ks[23], (DEPTH, N_BRANCH, W, D), W ** -0.5),
        'w_out': nrm(ks[24], (DEPTH, D, D), D ** -0.5),
        'moe_router': nrm(ks[25], (DEPTH, D, N_EXPERTS), D ** -0.5),
        'moe_w1': nrm(ks[26], (DEPTH, N_EXPERTS, D, EXPERT_FF), D ** -0.5),
        'moe_w3': nrm(ks[27], (DEPTH, N_EXPERTS, D, EXPERT_FF), D ** -0.5),
        'moe_w2': nrm(ks[28], (DEPTH, N_EXPERTS, EXPERT_FF, D), EXPERT_FF ** -0.5),
    }


def reference(x, c, ctx, c_ctx, w_ada, b_ada, norm_g, w_in, rw_mu, rw_w0, rw_wB, rw_a0, rw_aB, rw_gB,
              rw_kk, rw_ka, rw_rk, rw_gn, da_lambda, da_subln, hg_lb_logits, hg_norm, w_branch, w_out,
              moe_router, moe_w1, moe_w3, moe_w2):
    B, L, D = x.shape
    rope_cos, rope_sin = axial_rope_tables(L)
    lb_w = jax.nn.softmax(hg_lb_logits.astype(jnp.float32), axis=0)
    hg_lb = jnp.cumsum(lb_w, axis=0) - lb_w[0]
    for l in range(DEPTH):
        with_ctx = l < DEPTH - 1
        lam_init = 0.8 - 0.6 * math.exp(-0.3 * l)
        mod_x = (jax.nn.silu(c) @ w_ada[l] + b_ada[l]).reshape(B, 6, 1, D)
        mod_c = (jax.nn.silu(c_ctx) @ w_ada[l] + b_ada[l]).reshape(6, D)
        hx = modulate(rmsnorm(x, norm_g[l, 0]), mod_x[:, 0], mod_x[:, 1])
        hc = modulate(rmsnorm(ctx, norm_g[l, 0]), mod_c[0], mod_c[1])
        mc, mx = token_mixer(hc, hx, w_in[l], rw_mu[l], rw_w0[l], rw_wB[l], rw_a0[l], rw_aB[l], rw_gB[l],
                             rw_kk[l], rw_ka[l], rw_rk[l], rw_gn[l], da_lambda[l], da_subln[l], lam_init,
                             rope_cos, rope_sin, hg_lb[l], hg_norm[l], w_branch[l], w_out[l], with_ctx)
        x = x + mod_x[:, 2] * rmsnorm(mx, norm_g[l, 1])
        hx = modulate(rmsnorm(x, norm_g[l, 2]), mod_x[:, 3], mod_x[:, 4])
        x = x + mod_x[:, 5] * rmsnorm(ec_moe(hx, moe_router[l], moe_w1[l], moe_w3[l], moe_w2[l]), norm_g[l, 3])
        if with_ctx:
            ctx = ctx + mod_c[2] * rmsnorm(mc, norm_g[l, 1])
            hc = modulate(rmsnorm(ctx, norm_g[l, 2]), mod_c[3], mod_c[4])
            ctx = ctx + mod_c[5] * rmsnorm(ec_moe(hc, moe_router[l], moe_w1[l], moe_w3[l], moe_w2[l]), norm_g[l, 3])
    return x
```

```python
import functools
import math

import jax
import jax.numpy as jnp
from jax import lax
from jax.experimental import pallas as pl
from jax.experimental.pallas import tpu as pltpu

F32 = jnp.float32
BF16 = jnp.bfloat16
HIGHEST = lax.Precision.HIGHEST

GRID_W = 64
RW_HEAD_DIM = 64
RW_DECAY_RANK = 64
RW_ICL_RANK = 64
RW_GATE_RANK = 128
RW_GN_EPS = 64e-5
DA_QK_DIM = 64
DA_V_DIM = 128
ROPE_THETA = 10000.0
ROPE_AXIS_FREQS = 16
DA_SUBLN_EPS = 1e-5
HG_EXPAND = 128
N_EXPERTS = 16
EC_CAPACITY_FACTOR = 2
NORM_EPS = 1e-6

LANES = 128
SUBLANES = 8
V7X_VMEM_BYTES = 64 * 1024 * 1024
HG_CHUNK = 64


def _cparams(sem, vmem_mb):
    return pltpu.CompilerParams(dimension_semantics=sem, vmem_limit_bytes=int(vmem_mb * 1024 * 1024))


def _sigmoid(x):
    return 1.0 / (1.0 + jnp.exp(-x))


def _silu(x):
    return x * _sigmoid(x)


def _mod_kernel(c_ref, w_ref, b_ref, o_ref):
    a = _silu(c_ref[...]).astype(BF16)
    o_ref[0] = jnp.dot(a, w_ref[0].astype(BF16), preferred_element_type=F32) + b_ref[0]


def _modulation(cc, w_ada, b_ada):
    depth, d, n = w_ada.shape
    rows = cc.shape[0]
    tn = 1024
    return pl.pallas_call(
        _mod_kernel,
        out_shape=jax.ShapeDtypeStruct((depth, rows, n), F32),
        grid=(depth, n // tn),
        in_specs=[pl.BlockSpec((rows, d), lambda l, j: (0, 0)),
                  pl.BlockSpec((1, d, tn), lambda l, j: (l, 0, j)),
                  pl.BlockSpec((1, 1, tn), lambda l, j: (l, 0, j))],
        out_specs=pl.BlockSpec((1, rows, tn), lambda l, j: (l, 0, j)),
        compiler_params=_cparams(("arbitrary", "arbitrary"), 40),
        name="adaln_mod",
    )(cc, w_ada, b_ada.reshape(depth, 1, n))


def _norm_mod_kernel(x_ref, g_ref, ms_ref, *rest, with_router):
    x = x_ref[0]
    xn = x * lax.rsqrt(jnp.mean(x * x, axis=-1, keepdims=True) + NORM_EPS) * g_ref[...]
    h = xn * (1.0 + ms_ref[0, 0, 1:2, :]) + ms_ref[0, 0, 0:1, :]
    if with_router:
        rt_ref, h_ref, aff_ref = rest
        h_ref[0] = h.astype(BF16)
        lt = lax.dot_general(rt_ref[...], h, (((1,), (1,)), ((), ())),
                             precision=HIGHEST, preferred_element_type=F32)
        e = jnp.exp(lt - jnp.max(lt, axis=0, keepdims=True))
        aff_ref[0] = e / jnp.sum(e, axis=0, keepdims=True)
    else:
        (h_ref,) = rest
        h_ref[0] = h.astype(BF16)


def _norm_mod(xa, g, msel, nc, router_t=None, tl=256):
    b, lt, d = xa.shape
    tl = min(tl, nc)
    nct = nc // tl
    with_router = router_t is not None
    in_specs = [pl.BlockSpec((1, tl, d), lambda bi, i: (bi, i, 0)),
                pl.BlockSpec((1, d), lambda bi, i: (0, 0)),
                pl.BlockSpec((1, 1, 2, d), lambda bi, i: (bi, jnp.where(i >= nct, 1, 0), 0, 0))]
    args = [xa, g.reshape(1, d), msel]
    out_shape = [jax.ShapeDtypeStruct((b, lt, d), BF16)]
    out_specs = [pl.BlockSpec((1, tl, d), lambda bi, i: (bi, i, 0))]
    if with_router:
        e = router_t.shape[0]
        in_specs.append(pl.BlockSpec((e, d), lambda bi, i: (0, 0)))
        args.append(router_t)
        out_shape.append(jax.ShapeDtypeStruct((b, e, lt), F32))
        out_specs.append(pl.BlockSpec((1, e, tl), lambda bi, i: (bi, 0, i)))
    res = pl.pallas_call(
        functools.partial(_norm_mod_kernel, with_router=with_router),
        out_shape=out_shape, grid=(b, lt // tl), in_specs=in_specs, out_specs=out_specs,
        compiler_params=_cparams(("arbitrary", "arbitrary"), 32),
        name="norm_mod_router" if with_router else "norm_mod",
    )(*args)
    return res if with_router else res[0]


def _matmul_kernel(a_ref, w_ref, o_ref):
    o_ref[...] = jnp.dot(a_ref[...], w_ref[...], preferred_element_type=F32).astype(o_ref.dtype)


def _pick(n, cands):
    for c in cands:
        if n % c == 0:
            return c
    return n


def _matmul(a, w, out_dtype, tm=1024, tn=512):
    m, k = a.shape
    n = w.shape[1]
    tm = _pick(m, (tm, 768, 512, 384, 256, 128))
    tn = _pick(n, (tn, 384, 256, 128))
    return pl.pallas_call(
        _matmul_kernel,
        out_shape=jax.ShapeDtypeStruct((m, n), out_dtype),
        grid=(m // tm, n // tn),
        in_specs=[pl.BlockSpec((tm, k), lambda i, j: (i, 0)),
                  pl.BlockSpec((k, tn), lambda i, j: (0, j))],
        out_specs=pl.BlockSpec((tm, tn), lambda i, j: (i, j)),
        compiler_params=_cparams(("arbitrary", "arbitrary"), 48),
        name="matmul",
    )(a, w)


def _rw_prep_kernel(z_ref, zp_ref, zn_ref, mu_ref, w0_ref, wb_ref, a0_ref, ab_ref, gb_ref,
                    o_ref, g_ref, *, tl, nct, nt, w):
    i = pl.program_id(1)
    z = z_ref[0]
    row = lax.broadcasted_iota(jnp.int32, (tl, 1), 0)
    seg_start = jnp.logical_or(i == 0, i == nct)
    seg_end = jnp.logical_or(i == nct - 1, i == nt - 1)
    prev_row = jnp.where(seg_start, 0.0, zp_ref[0, SUBLANES - 1:SUBLANES, :])
    next_row = jnp.where(seg_end, 0.0, zn_ref[0, 0:1, :])
    prev = jnp.where(row == 0, prev_row, pltpu.roll(z, 1, 0))
    nxt = jnp.where(row == tl - 1, next_row, pltpu.roll(z, tl - 1, 0))
    zs = z + mu_ref[0:1, :] * (prev - z) + mu_ref[1:2, :] * (nxt - z)
    o_ref[0, 0] = zs[:, 0:w]
    o_ref[0, 1] = zs[:, w:2 * w]
    o_ref[0, 2] = zs[:, 2 * w:3 * w]
    o4 = 3 * w + 2 * RW_DECAY_RANK
    o5 = o4 + 2 * RW_ICL_RANK
    wd = jnp.tanh(zs[:, 3 * w:o4])
    wl = jnp.dot(wd, wb_ref[...], precision=HIGHEST, preferred_element_type=F32) + w0_ref[...]
    dec = jnp.exp(-math.exp(-0.5) * _sigmoid(wl))
    o_ref[0, 3] = dec[:, 0:w]
    o_ref[0, 4] = dec[:, w:2 * w]
    al = jnp.dot(zs[:, o4:o5], ab_ref[...], precision=HIGHEST, preferred_element_type=F32) + a0_ref[...]
    av = _sigmoid(al)
    o_ref[0, 5] = av[:, 0:w]
    o_ref[0, 6] = av[:, w:2 * w]
    g_ref[0] = jnp.dot(_sigmoid(zs[:, o5:]), gb_ref[...], precision=HIGHEST, preferred_element_type=F32)


def _blockdiag2(m):
    r, w = m.shape[1], m.shape[2]
    z = jnp.zeros((r, w), m.dtype)
    return jnp.concatenate([jnp.concatenate([m[0], z], axis=1), jnp.concatenate([z, m[1]], axis=1)], axis=0)


def _rw_prep(za, nc, mu, w0, wb, a0, ab, gb, tl=256):
    b, lt, acols = za.shape
    w = w0.shape[1]
    tl = min(tl, nc)
    nct, nt = nc // tl, lt // tl
    r8 = tl // SUBLANES
    nb8 = lt // SUBLANES
    full = lambda shp: pl.BlockSpec(shp, lambda bi, i: tuple(0 for _ in shp))
    kern = functools.partial(_rw_prep_kernel, tl=tl, nct=nct, nt=nt, w=w)
    return pl.pallas_call(
        kern,
        out_shape=[jax.ShapeDtypeStruct((b, 7, lt, w), F32), jax.ShapeDtypeStruct((b, lt, w), F32)],
        grid=(b, nt),
        in_specs=[pl.BlockSpec((1, tl, acols), lambda bi, i: (bi, i, 0)),
                  pl.BlockSpec((1, SUBLANES, acols), lambda bi, i: (bi, jnp.maximum(i * r8 - 1, 0), 0)),
                  pl.BlockSpec((1, SUBLANES, acols), lambda bi, i: (bi, jnp.minimum((i + 1) * r8, nb8 - 1), 0)),
                  full((2, acols)), full((1, 2 * w)), full((2 * RW_DECAY_RANK, 2 * w)),
                  full((1, 2 * w)), full((2 * RW_ICL_RANK, 2 * w)), full((RW_GATE_RANK, w))],
        out_specs=[pl.BlockSpec((1, 7, tl, w), lambda bi, i: (bi, 0, i, 0)),
                   pl.BlockSpec((1, tl, w), lambda bi, i: (bi, i, 0))],
        compiler_params=_cparams(("arbitrary", "arbitrary"), 48),
        name="rwkv_prep",
    )(za, za, za, mu, w0.reshape(1, 2 * w), _blockdiag2(wb), a0.reshape(1, 2 * w), _blockdiag2(ab), gb)


def _rw_scan_kernel(x_ref, w_ref, a_ref, kk_ref, ka_ref, rk_ref, gn0_ref, gn1_ref, o_ref, s_ref, y_ref,
                    *, tb, n):
    d = pl.program_id(0)
    i = pl.program_id(1)

    @pl.when(i == 0)
    def _():
        s_ref[...] = jnp.zeros_like(s_ref)

    def step(tt, carry):
        t = jnp.where(d == 0, tt, tb - 1 - tt)
        r = x_ref[t, 0]
        k = x_ref[t, 1]
        v = x_ref[t, 2]
        w = w_ref[t, 0]
        a = a_ref[t, 0]
        kx = k * kk_ref[...]
        nrm = jnp.sqrt(jnp.sum(kx * kx, axis=0, keepdims=True))
        kk = kx / jnp.maximum(nrm, 1e-12)
        kt = k * (1.0 + (a - 1.0) * ka_ref[...])
        bb = a * kk
        wr = w * r
        br = jnp.sum(bb * r, axis=0, keepdims=True)
        ktr = jnp.sum(kt * r, axis=0, keepdims=True)

        def vbody(vi, c):
            sv = s_ref[vi]
            sa = jnp.sum(sv * kk, axis=0, keepdims=True)
            yp = jnp.sum(sv * wr, axis=0, keepdims=True)
            vrow = x_ref[t, 2, pl.ds(vi, 1), :]
            s_ref[vi] = sv * w - sa * bb + vrow * kt
            y_ref[pl.ds(vi, 1), :] = yp - sa * br + vrow * ktr
            return c

        lax.fori_loop(0, n, vbody, 0, unroll=4)
        y = y_ref[...]
        mu = jnp.mean(y, axis=0, keepdims=True)
        var = jnp.mean(jnp.square(y - mu), axis=0, keepdims=True)
        yn = (y - mu) * lax.rsqrt(var + RW_GN_EPS) * gn0_ref[...] + gn1_ref[...]
        bonus = jnp.sum(r * kt * rk_ref[...], axis=0, keepdims=True) * v
        o_ref[0, t] = yn + bonus
        return carry

    lax.fori_loop(0, tb, step, 0)


def _rw_scan(xs, nc, kkp, kap, rkp, gn0p, gn1p, tb=16):
    lt, _, n, bh = xs.shape
    tb = min(tb, nc)
    ncb, nb = nc // tb, lt // tb

    def blk(d, i):
        rev = jnp.where(i < ncb, ncb - 1 - i, nb - 1 - (i - ncb))
        return jnp.where(d == 0, i, rev)

    par = pl.BlockSpec((n, bh), lambda d, i: (0, 0))
    return pl.pallas_call(
        functools.partial(_rw_scan_kernel, tb=tb, n=n),
        out_shape=jax.ShapeDtypeStruct((2, lt, n, bh), F32),
        grid=(2, nb),
        in_specs=[pl.BlockSpec((tb, 3, n, bh), lambda d, i: (blk(d, i), 0, 0, 0)),
                  pl.BlockSpec((tb, 1, n, bh), lambda d, i: (blk(d, i), 3 + d, 0, 0)),
                  pl.BlockSpec((tb, 1, n, bh), lambda d, i: (blk(d, i), 5 + d, 0, 0)),
                  par, par, par, par, par],
        out_specs=pl.BlockSpec((1, tb, n, bh), lambda d, i: (d, blk(d, i), 0, 0)),
        scratch_shapes=[pltpu.VMEM((n, n, bh), F32), pltpu.VMEM((n, bh), F32)],
        compiler_params=_cparams(("arbitrary", "arbitrary"), 32),
        name="rwkv_scan",
    )(xs, xs, xs, kkp, kap, rkp, gn0p, gn1p)


def _rope_kernel(z_ref, c_ref, s_ref, o_ref, *, w):
    lane = lax.broadcasted_iota(jnp.int32, (1, LANES), 1)
    first_half = (lane % (2 * ROPE_AXIS_FREQS)) < ROPE_AXIS_FREQS
    c = c_ref[...]
    s = s_ref[...]
    for j in range(2 * w // LANES):
        xs = z_ref[0, :, j * LANES:(j + 1) * LANES]
        up = pltpu.roll(xs, LANES - ROPE_AXIS_FREQS, 1)
        dn = pltpu.roll(xs, ROPE_AXIS_FREQS, 1)
        o = xs * c + jnp.where(first_half, up, dn) * s
        if j < w // LANES:
            o = o * (DA_QK_DIM ** -0.5)
        o_ref[0, :, j * LANES:(j + 1) * LANES] = o.astype(BF16)
    o_ref[0, :, 2 * w:] = z_ref[0, :, 2 * w:].astype(BF16)


def _rope_tables(seq, nc):
    rows = seq // GRID_W
    row = jnp.broadcast_to(jnp.arange(rows)[:, None], (rows, GRID_W)).reshape(seq)
    col = jnp.broadcast_to(jnp.arange(GRID_W)[None, :], (rows, GRID_W)).reshape(seq)
    inv = 1.0 / (ROPE_THETA ** (jnp.arange(ROPE_AXIS_FREQS, dtype=F32) / ROPE_AXIS_FREQS))
    ang = jnp.stack([row, col], axis=-1).astype(F32)[:, :, None] * inv
    cos, sin = jnp.cos(ang), jnp.sin(ang)
    c64 = jnp.concatenate([cos[:, 0], cos[:, 0], cos[:, 1], cos[:, 1]], axis=-1)
    s64 = jnp.concatenate([-sin[:, 0], sin[:, 0], -sin[:, 1], sin[:, 1]], axis=-1)
    ct = jnp.concatenate([jnp.ones((nc, LANES), F32), jnp.tile(c64, (1, 2))], axis=0)
    st = jnp.concatenate([jnp.zeros((nc, LANES), F32), jnp.tile(s64, (1, 2))], axis=0)
    return ct, st


def _rope(zb, ct, st, tl=256):
    b, lt, cols = zb.shape
    w = cols // 3
    tl = _pick(lt, (tl, 128, 64))
    return pl.pallas_call(
        functools.partial(_rope_kernel, w=w),
        out_shape=jax.ShapeDtypeStruct((b, lt, cols), BF16),
        grid=(b, lt // tl),
        in_specs=[pl.BlockSpec((1, tl, cols), lambda bi, i: (bi, i, 0)),
                  pl.BlockSpec((tl, LANES), lambda bi, i: (i, 0)),
                  pl.BlockSpec((tl, LANES), lambda bi, i: (i, 0))],
        out_specs=pl.BlockSpec((1, tl, cols), lambda bi, i: (bi, i, 0)),
        compiler_params=_cparams(("arbitrary", "arbitrary"), 32),
        name="rope",
    )(zb, ct, st)


def _attn_kernel(q_ref, k_ref, v_ref, lam_ref, g_ref, o_ref, *, nct, nc, lam_init, qoff):
    i = pl.program_id(2) + qoff
    q = q_ref[0]
    k = k_ref[0]
    lane = lax.broadcasted_iota(jnp.int32, (1, LANES), 1)
    key = lax.broadcasted_iota(jnp.int32, (1, k.shape[0]), 1)
    hide = jnp.logical_and(i < nct, key >= nc)
    lp = lam_ref[...]
    lam = (jnp.exp(jnp.sum(lp[0:1] * lp[1:2], axis=1, keepdims=True))
           - jnp.exp(jnp.sum(lp[2:3] * lp[3:4], axis=1, keepdims=True)) + lam_init)

    def probs(sel):
        qs = jnp.where(sel, q, jnp.zeros_like(q))
        s = lax.dot_general(qs, k, (((1,), (1,)), ((), ())), preferred_element_type=F32)
        s = jnp.where(hide, -jnp.inf, s)
        e = jnp.exp(s - jnp.max(s, axis=-1, keepdims=True))
        return e / jnp.sum(e, axis=-1, keepdims=True)

    wgt = probs(lane < DA_QK_DIM) - lam * probs(lane >= DA_QK_DIM)
    o = jnp.dot(wgt.astype(BF16), v_ref[0], preferred_element_type=F32)
    on = o * lax.rsqrt(jnp.mean(o * o, axis=-1, keepdims=True) + DA_SUBLN_EPS) * g_ref[...]
    o_ref[0] = (on * (1.0 - lam_init)).astype(o_ref.dtype)


def _attention(zbr, nc, lam_p, subln, lam_init, with_ctx):
    b, lt, cols = zbr.shape
    w = cols // 3
    heads = w // DA_V_DIM
    tq = min(256, nc)
    nct = nc // tq
    qoff = 0 if with_ctx else nct
    kern = functools.partial(_attn_kernel, nct=nct, nc=nc, lam_init=lam_init, qoff=qoff)
    return pl.pallas_call(
        kern,
        out_shape=jax.ShapeDtypeStruct((b, lt, w), BF16),
        grid=(b, heads, lt // tq - qoff),
        in_specs=[pl.BlockSpec((1, tq, DA_V_DIM), lambda bi, h, i: (bi, i + qoff, h)),
                  pl.BlockSpec((1, lt, DA_V_DIM), lambda bi, h, i: (bi, 0, heads + h)),
                  pl.BlockSpec((1, lt, DA_V_DIM), lambda bi, h, i: (bi, 0, 2 * heads + h)),
                  pl.BlockSpec((4, DA_QK_DIM), lambda bi, h, i: (0, 0)),
                  pl.BlockSpec((1, DA_V_DIM), lambda bi, h, i: (0, 0))],
        out_specs=pl.BlockSpec((1, tq, DA_V_DIM), lambda bi, h, i: (bi, i + qoff, h)),
        compiler_params=_cparams(("arbitrary", "arbitrary", "arbitrary"), 40),
        name="diff_attn",
    )(zbr, zbr, zbr, lam_p, subln.reshape(1, DA_V_DIM))


def _seg_scans(lf, c):
    row = lax.broadcasted_iota(jnp.int32, (c, 1), 0)
    cs, ss = {1: lf}, {1: lf}
    h = 1
    while h < c:
        x, y = cs[h], ss[h]
        if h < SUBLANES:
            addx = jnp.zeros_like(x)
            addy = jnp.zeros_like(y)
            odd = (row % (2 * h)) >= h
            for s in range(1, h + 1):
                addx = addx + jnp.where(jnp.logical_and(odd, (row % h) == s - 1), pltpu.roll(x, s, 0), 0.0)
                addy = addy + jnp.where(jnp.logical_and(~odd, (row % h) == h - s), pltpu.roll(y, c - s, 0), 0.0)
            cs[2 * h], ss[2 * h] = x + addx, y + addy
        else:
            px = [x[j * h:(j + 1) * h] for j in range(c // h)]
            py = [y[j * h:(j + 1) * h] for j in range(c // h)]
            nx = [px[j] + px[j - 1][h - 1:h] if j % 2 == 1 else px[j] for j in range(c // h)]
            ny = [py[j] + py[j + 1][0:1] if j % 2 == 0 else py[j] for j in range(c // h)]
            cs[2 * h], ss[2 * h] = jnp.concatenate(nx, axis=0), jnp.concatenate(ny, axis=0)
        h *= 2
    return cs, ss


def _hg_kernel(q_ref, f_ref, v_ref, lb_ref, o_ref, st_ref, *, tc, c, heads, rev):
    j = pl.program_id(1)

    @pl.when(j == 0)
    def _():
        st_ref[...] = jnp.zeros_like(st_ref)

    ti = lax.broadcasted_iota(jnp.int32, (c, c), 0)
    si = lax.broadcasted_iota(jnp.int32, (c, c), 1)
    nt = (((1,), (1,)), ((), ()))
    tn = (((0,), (0,)), ((), ()))

    def head(h, carry):
        cols = pl.ds(pl.multiple_of(h * HG_EXPAND, HG_EXPAND), HG_EXPAND)
        lb = lb_ref[:, cols]
        chunks = range(tc // c)
        for ci in (reversed(chunks) if rev else chunks):
            rows = slice(ci * c, (ci + 1) * c)
            q = _silu(q_ref[0, rows, cols])
            f = lb + (1.0 - lb) * _sigmoid(f_ref[0, rows, cols])
            k = 1.0 - f
            lf = jnp.log(f)
            v = v_ref[0, rows, cols].astype(BF16)
            cs, ss = _seg_scans(lf, c)
            qs, ks = (ss, cs) if rev else (cs, ss)
            att = jnp.where(ti == si, lax.dot_general(q.astype(BF16), k.astype(BF16), nt,
                                                      preferred_element_type=F32), 0.0)
            hh = 1
            while hh < c:
                qe = (q * jnp.exp(qs[hh])).astype(BF16)
                ke = (k * jnp.exp(ks[hh] - lf)).astype(BF16)
                a = lax.dot_general(qe, ke, nt, preferred_element_type=F32)
                tb, sb = ti // hh, si // hh
                if rev:
                    m = jnp.logical_and(tb % 2 == 0, sb == tb + 1)
                else:
                    m = jnp.logical_and(tb % 2 == 1, sb == tb - 1)
                att = att + jnp.where(m, a, 0.0)
                hh *= 2
            st = st_ref[h]
            qe = (q * jnp.exp(qs[c])).astype(BF16)
            o = lax.dot_general(qe, st.astype(BF16), nt, preferred_element_type=F32)
            o = o + jnp.dot(att.astype(BF16), v, preferred_element_type=F32)
            o_ref[0, rows, cols] = o
            ke = (k * jnp.exp(ks[c] - lf)).astype(BF16)
            tot = qs[c][0:1] if rev else qs[c][c - 1:c]
            st_ref[h] = st * jnp.exp(tot) + lax.dot_general(v, ke, tn, preferred_element_type=F32)
        return carry

    lax.fori_loop(0, heads, head, 0)


def _hgrn(zc, nc, lb, rev):
    b, lt, cols = zc.shape
    w = cols // 5
    heads = w // HG_EXPAND
    tc = min(256, nc)
    c = min(HG_CHUNK, tc)
    ncb, nb = nc // tc, lt // tc

    def blk(j):
        return jnp.where(j < ncb, ncb - 1 - j, nb - 1 - (j - ncb)) if rev else j

    fcol = 2 if rev else 1
    return pl.pallas_call(
        functools.partial(_hg_kernel, tc=tc, c=c, heads=heads, rev=rev),
        out_shape=jax.ShapeDtypeStruct((b, lt, w), F32),
        grid=(b, nb),
        in_specs=[pl.BlockSpec((1, tc, w), lambda bi, j: (bi, blk(j), 0)),
                  pl.BlockSpec((1, tc, w), lambda bi, j: (bi, blk(j), fcol)),
                  pl.BlockSpec((1, tc, w), lambda bi, j: (bi, blk(j), 3)),
                  pl.BlockSpec((1, w), lambda bi, j: (0, 0))],
        out_specs=pl.BlockSpec((1, tc, w), lambda bi, j: (bi, blk(j), 0)),
        scratch_shapes=[pltpu.VMEM((heads, HG_EXPAND, HG_EXPAND), F32)],
        compiler_params=_cparams(("arbitrary", "arbitrary"), 32),
        name="hgrn2_bwd" if rev else "hgrn2_fwd",
    )(zc, zc, zc, lb.reshape(1, w))


def _merge_kernel(oa_ref, ga_ref, yb_ref, of_ref, ob_ref, gc_ref, hn_ref, gl0_ref, gl1_ref, gl2_ref, wb_ref,
                  o_ref, *, heads):
    gls = (gl0_ref, gl1_ref, gl2_ref)
    ya = (oa_ref[0, 0] + oa_ref[1, 0]) * ga_ref[0]
    oc = of_ref[0] + ob_ref[0]
    parts = []
    for h in range(heads):
        x = oc[:, h * HG_EXPAND:(h + 1) * HG_EXPAND]
        parts.append(x * lax.rsqrt(jnp.mean(x * x, axis=-1, keepdims=True) + NORM_EPS))
    yc = jnp.concatenate(parts, axis=-1) * hn_ref[...] * _silu(gc_ref[0])
    acc = None
    for bi, y in enumerate((ya.astype(BF16), yb_ref[0], yc.astype(BF16))):
        term = _sigmoid(gls[bi][0]) * jnp.dot(y, wb_ref[bi], preferred_element_type=F32)
        acc = term if acc is None else acc + term
    o_ref[0] = acc.astype(BF16)


def _merge(oa, ga, yb, of, ob, zc, hn, gl, wbr, tl=256, tn=1024):
    b, lt, w = ga.shape
    d = wbr.shape[2]
    tl = _pick(lt, (tl, 128, 64))
    heads = w // HG_EXPAND
    nn = d // tn
    row = lambda n, bi, i: (bi, i, 0)
    glspec = lambda br: pl.BlockSpec((1, tl, tn), lambda n, bi, i: (bi, i, br * nn + n))
    return pl.pallas_call(
        functools.partial(_merge_kernel, heads=heads),
        out_shape=jax.ShapeDtypeStruct((b, lt, d), BF16),
        grid=(d // tn, b, lt // tl),
        in_specs=[pl.BlockSpec((2, 1, tl, w), lambda n, bi, i: (0, bi, i, 0)),
                  pl.BlockSpec((1, tl, w), row),
                  pl.BlockSpec((1, tl, w), row),
                  pl.BlockSpec((1, tl, w), row),
                  pl.BlockSpec((1, tl, w), row),
                  pl.BlockSpec((1, tl, w), lambda n, bi, i: (bi, i, 4)),
                  pl.BlockSpec((1, w), lambda n, bi, i: (0, 0)),
                  glspec(0), glspec(1), glspec(2),
                  pl.BlockSpec((3, w, tn), lambda n, bi, i: (0, 0, n))],
        out_specs=pl.BlockSpec((1, tl, tn), lambda n, bi, i: (bi, i, n)),
        compiler_params=_cparams(("arbitrary", "arbitrary", "arbitrary"), 48),
        name="branch_merge",
    )(oa, ga, yb, of, ob, zc, hn, gl, gl, gl, wbr)


def _proj_post_kernel(a_ref, w_ref, x_ref, g_ref, gate_ref, o_ref):
    mx = jnp.dot(a_ref[0], w_ref[...], preferred_element_type=F32)
    mn = mx * lax.rsqrt(jnp.mean(mx * mx, axis=-1, keepdims=True) + NORM_EPS) * g_ref[...]
    o_ref[0] = x_ref[0] + gate_ref[0, 0] * mn


def _proj_post(a, wout, xa, g, gsel, nc, tl=256):
    b, lt, d = xa.shape
    tl = min(tl, nc)
    nct = nc // tl
    return pl.pallas_call(
        _proj_post_kernel,
        out_shape=jax.ShapeDtypeStruct((b, lt, d), F32),
        grid=(b, lt // tl),
        in_specs=[pl.BlockSpec((1, tl, d), lambda bi, i: (bi, i, 0)),
                  pl.BlockSpec((d, d), lambda bi, i: (0, 0)),
                  pl.BlockSpec((1, tl, d), lambda bi, i: (bi, i, 0)),
                  pl.BlockSpec((1, d), lambda bi, i: (0, 0)),
                  pl.BlockSpec((1, 1, 1, d), lambda bi, i: (bi, jnp.where(i >= nct, 1, 0), 0, 0))],
        out_specs=pl.BlockSpec((1, tl, d), lambda bi, i: (bi, i, 0)),
        input_output_aliases={2: 0},
        compiler_params=_cparams(("arbitrary", "arbitrary"), 48),
        name="out_proj_post",
    )(a, wout, xa, g.reshape(1, d), gsel)


def _rank_kernel(afft_ref, aff_ref, rank_ref, *, n, e_num, tw):
    tprime = lax.broadcasted_iota(jnp.int32, (n, 1), 0)
    lane = lax.broadcasted_iota(jnp.int32, (1, tw), 1)
    for e in range(e_num):
        colb = jnp.broadcast_to(aff_ref[0, :, e:e + 1], (n, tw))

        def tile(j, c, e=e, colb=colb):
            off = pl.multiple_of(j * tw, tw)
            rowv = afft_ref[0, pl.ds(e, 1), pl.ds(off, tw)]
            tidx = off + lane
            beats = jnp.logical_or(colb > rowv, jnp.logical_and(colb == rowv, tprime < tidx))
            r = jnp.sum(beats.astype(F32), axis=0, keepdims=True)
            rank_ref[0, pl.ds(e, 1), pl.ds(off, tw)] = r.astype(jnp.int32)
            return c

        lax.fori_loop(0, n // tw, tile, 0)


def _rank(afft, aff):
    b, e_num, n = afft.shape
    tw = min(LANES, n)
    return pl.pallas_call(
        functools.partial(_rank_kernel, n=n, e_num=e_num, tw=tw),
        out_shape=jax.ShapeDtypeStruct((b, e_num, n), jnp.int32),
        grid=(b,),
        in_specs=[pl.BlockSpec((1, e_num, n), lambda bi: (bi, 0, 0)),
                  pl.BlockSpec((1, n, e_num), lambda bi: (bi, 0, 0))],
        out_specs=pl.BlockSpec((1, e_num, n), lambda bi: (bi, 0, 0)),
        compiler_params=_cparams(("arbitrary",), 32),
        name="ec_rank",
    )(afft, aff)


def _gather_kernel(rank_ref, h_ref, o_ref, *, cap):
    e = pl.program_id(1)
    rk = rank_ref[0, pl.ds(e, 1), :]
    slot = lax.broadcasted_iota(jnp.int32, (cap, 1), 0)
    p = jnp.where(slot == rk, 1.0, 0.0).astype(BF16)
    o_ref[0] = jnp.dot(p, h_ref[0], preferred_element_type=F32).astype(BF16)


def _gather(rank, hs, cap):
    b, e_num, n = rank.shape
    d = hs.shape[2]
    return pl.pallas_call(
        functools.partial(_gather_kernel, cap=cap),
        out_shape=jax.ShapeDtypeStruct((e_num, b * cap, d), BF16),
        grid=(b, e_num),
        in_specs=[pl.BlockSpec((1, e_num, n), lambda bi, e: (bi, 0, 0)),
                  pl.BlockSpec((1, n, d), lambda bi, e: (bi, 0, 0))],
        out_specs=pl.BlockSpec((1, cap, d), lambda bi, e: (e, bi, 0)),
        compiler_params=_cparams(("arbitrary", "arbitrary"), 40),
        name="ec_gather",
    )(rank, hs)


def _ffn_kernel(x_ref, w1_ref, w3_ref, w2_ref, o_ref, acc_ref):
    f = pl.program_id(2)
    x = x_ref[0]
    h1 = jnp.dot(x, w1_ref[0].astype(BF16), preferred_element_type=F32)
    h3 = jnp.dot(x, w3_ref[0].astype(BF16), preferred_element_type=F32)
    hid = (_silu(h1) * h3).astype(BF16)
    part = jnp.dot(hid, w2_ref[0].astype(BF16), preferred_element_type=F32)

    @pl.when(f == 0)
    def _():
        acc_ref[...] = part

    @pl.when(f != 0)
    def _():
        acc_ref[...] += part

    @pl.when(f == pl.num_programs(2) - 1)
    def _():
        o_ref[0] = acc_ref[...].astype(BF16)


def _ffn(xe, w1, w3, w2, tf=256):
    e_num, m, d = xe.shape
    ff = w1.shape[2]
    tm = _pick(m, (1152, 1024, 768, 512, 384, 256, 128, 64, 32, 16, 8))
    return pl.pallas_call(
        _ffn_kernel,
        out_shape=jax.ShapeDtypeStruct((e_num, m, d), BF16),
        grid=(e_num, m // tm, ff // tf),
        in_specs=[pl.BlockSpec((1, tm, d), lambda e, i, f: (e, i, 0)),
                  pl.BlockSpec((1, d, tf), lambda e, i, f: (e, 0, f)),
                  pl.BlockSpec((1, d, tf), lambda e, i, f: (e, 0, f)),
                  pl.BlockSpec((1, tf, d), lambda e, i, f: (e, f, 0))],
        out_specs=pl.BlockSpec((1, tm, d), lambda e, i, f: (e, i, 0)),
        scratch_shapes=[pltpu.VMEM((tm, d), F32)],
        compiler_params=_cparams(("arbitrary", "arbitrary", "arbitrary"), 52),
        name="ec_ffn",
    )(xe, w1, w3, w2)


def _combine_kernel(rank_ref, aff_ref, ye_ref, x_ref, g_ref, gate_ref, o_ref, acc_ref, *, cap, e_num):
    e = pl.program_id(2)
    lane = lax.broadcasted_iota(jnp.int32, (1, e_num), 1)
    sel = lane == e
    rk = jnp.sum(jnp.where(sel, rank_ref[0], 0), axis=1, keepdims=True)
    af = jnp.sum(jnp.where(sel, aff_ref[0], 0.0), axis=1, keepdims=True)
    slot = lax.broadcasted_iota(jnp.int32, (1, cap), 1)
    pt = jnp.where(rk == slot, 1.0, 0.0).astype(BF16)
    part = af * jnp.dot(pt, ye_ref[0], preferred_element_type=F32)

    @pl.when(e == 0)
    def _():
        acc_ref[...] = part

    @pl.when(e != 0)
    def _():
        acc_ref[...] += part

    @pl.when(e == e_num - 1)
    def _():
        y = acc_ref[...]
        yn = y * lax.rsqrt(jnp.mean(y * y, axis=-1, keepdims=True) + NORM_EPS) * g_ref[...]
        o_ref[0] = x_ref[0] + gate_ref[0, 0] * yn


def _combine(rank_c, aff_c, ye, xa, g, gsel, seg, row0, slot0, cap, tt=256):
    b, n, e_num = rank_c.shape
    d = xa.shape[2]
    tt = _pick(math.gcd(n, row0), (tt, 128, 64))
    assert row0 % tt == 0 and slot0 % cap == 0
    r0, s0 = row0 // tt, slot0 // cap
    return pl.pallas_call(
        functools.partial(_combine_kernel, cap=cap, e_num=e_num),
        out_shape=jax.ShapeDtypeStruct(xa.shape, F32),
        grid=(b, n // tt, e_num),
        in_specs=[pl.BlockSpec((1, tt, e_num), lambda bi, j, e: (bi, j, 0)),
                  pl.BlockSpec((1, tt, e_num), lambda bi, j, e: (bi, j, 0)),
                  pl.BlockSpec((1, cap, d), lambda bi, j, e: (e, s0 + bi, 0)),
                  pl.BlockSpec((1, tt, d), lambda bi, j, e: (bi, r0 + j, 0)),
                  pl.BlockSpec((1, d), lambda bi, j, e: (0, 0)),
                  pl.BlockSpec((1, 1, 1, d), lambda bi, j, e: (bi, seg, 0, 0))],
        out_specs=pl.BlockSpec((1, tt, d), lambda bi, j, e: (bi, r0 + j, 0)),
        scratch_shapes=[pltpu.VMEM((tt, d), F32)],
        input_output_aliases={3: 0},
        compiler_params=_cparams(("arbitrary", "arbitrary", "arbitrary"), 32),
        name="ec_combine_post",
    )(rank_c, aff_c, ye, xa, g.reshape(1, d), gsel)


def _sel(mod_c, mod_x, idx):
    b = mod_x.shape[0]
    mc = jnp.broadcast_to(mod_c[jnp.array(idx)][None], (b, len(idx), mod_c.shape[-1]))
    return jnp.stack([mc, mod_x[:, jnp.array(idx)]], axis=1)


def _lane_param(p, b):
    n = RW_HEAD_DIM
    h = p.shape[0] // n
    return jnp.tile(p.reshape(h, n).T, (1, b))


def _layer(xa, nc, mod_x, mod_c, p, lam_init, hg_lb, ct, st, with_ctx):
    b, lt, d = xa.shape
    w = d // 2
    n = RW_HEAD_DIM
    heads_rw = w // n
    a_cols = 3 * w + 2 * RW_DECAY_RANK + 2 * RW_ICL_RANK + RW_GATE_RANK
    col_b = a_cols
    col_c = col_b + 3 * w
    col_g = col_c + 5 * w

    h = _norm_mod(xa, p['norm_g'][0], _sel(mod_c, mod_x, (0, 1)), nc)
    h2 = h.reshape(b * lt, d)
    w_in = p['w_in']
    za = _matmul(h2, w_in[:, :col_b].astype(BF16), F32, tn=384).reshape(b, lt, a_cols)
    zb = _matmul(h2, w_in[:, col_b:col_c].astype(BF16), F32).reshape(b, lt, 3 * w)
    zc = _matmul(h2, w_in[:, col_c:col_g].astype(BF16), F32).reshape(b, lt, 5 * w)
    gl = _matmul(h2, w_in[:, col_g:].astype(BF16), F32).reshape(b, lt, 3 * d)

    streams, ga = _rw_prep(za, nc, p['rw_mu'], p['rw_w0'], p['rw_wB'], p['rw_a0'], p['rw_aB'], p['rw_gB'])
    xs = jnp.transpose(streams.reshape(b, 7, lt, heads_rw, n), (2, 1, 4, 0, 3)).reshape(lt, 7, n, b * heads_rw)
    oa = _rw_scan(xs, nc, _lane_param(p['rw_kk'], b), _lane_param(p['rw_ka'], b), _lane_param(p['rw_rk'], b),
                  _lane_param(p['rw_gn'][0], b), _lane_param(p['rw_gn'][1], b))
    oa = jnp.transpose(oa.reshape(2, lt, n, b, heads_rw), (0, 3, 1, 4, 2)).reshape(2, b, lt, w)

    zbr = _rope(zb, ct, st)
    yb = _attention(zbr, nc, p['da_lambda'], p['da_subln'], lam_init, True)

    of = _hgrn(zc, nc, hg_lb, rev=False)
    ob = _hgrn(zc, nc, hg_lb, rev=True)

    hn = jnp.tile(p['hg_norm'], w // HG_EXPAND).reshape(1, w)
    merged = _merge(oa, ga, yb, of, ob, zc, hn, gl, p['w_branch'].astype(BF16))
    xa = _proj_post(merged, p['w_out'].astype(BF16), xa, p['norm_g'][1], _sel(mod_c, mod_x, (2,)), nc)

    hm, afft = _norm_mod(xa, p['norm_g'][2], _sel(mod_c, mod_x, (3, 4)), nc, router_t=p['moe_router'].T)
    gsel = _sel(mod_c, mod_x, (5,))
    sets = [(nc, lt - nc, 1)] + ([(0, nc, 0)] if with_ctx else [])
    routed, xes = [], []
    for row0, nn, seg in sets:
        cap = EC_CAPACITY_FACTOR * nn // N_EXPERTS
        at = afft[:, :, row0:row0 + nn]
        ac = jnp.swapaxes(at, 1, 2)
        rk = _rank(at, ac)
        xes.append(_gather(rk, hm[:, row0:row0 + nn], cap))
        routed.append((jnp.swapaxes(rk, 1, 2), ac, row0, seg, cap))
    xe = xes[0] if len(xes) == 1 else jnp.concatenate(xes, axis=1)
    ye = _ffn(xe, p['moe_w1'], p['moe_w3'], p['moe_w2'])
    slot0 = 0
    for rk_c, ac, row0, seg, cap in routed:
        xa = _combine(rk_c, ac, ye, xa, p['norm_g'][3], gsel, seg, row0, slot0, cap)
        slot0 += b * cap
    return xa


def kernel(x, c, ctx, c_ctx, w_ada, b_ada, norm_g, w_in, rw_mu, rw_w0, rw_wB, rw_a0, rw_aB, rw_gB, rw_kk, rw_ka, rw_rk, rw_gn, da_lambda, da_subln, hg_lb_logits, hg_norm, w_branch, w_out, moe_router, moe_w1, moe_w3, moe_w2):
    b, seq, d = x.shape
    nc = ctx.shape[1]
    depth = w_ada.shape[0]
    ct, st = _rope_tables(seq, nc)
    lb_w = jax.nn.softmax(hg_lb_logits.astype(F32), axis=0)
    hg_lb = jnp.cumsum(lb_w, axis=0) - lb_w[0]
    rows = ((b + 1 + SUBLANES - 1) // SUBLANES) * SUBLANES
    cc = jnp.zeros((rows, d), F32).at[:b].set(c).at[b].set(c_ctx)
    mod = _modulation(cc, w_ada, b_ada)
    xa = jnp.concatenate([ctx, x], axis=1)
    for l in range(depth):
        p = dict(norm_g=norm_g[l], w_in=w_in[l], rw_mu=rw_mu[l], rw_w0=rw_w0[l], rw_wB=rw_wB[l],
                 rw_a0=rw_a0[l], rw_aB=rw_aB[l], rw_gB=rw_gB[l], rw_kk=rw_kk[l], rw_ka=rw_ka[l],
                 rw_rk=rw_rk[l], rw_gn=rw_gn[l], da_lambda=da_lambda[l], da_subln=da_subln[l],
                 hg_norm=hg_norm[l], w_branch=w_branch[l], w_out=w_out[l], moe_router=moe_router[l],
                 moe_w1=moe_w1[l], moe_w3=moe_w3[l], moe_w2=moe_w2[l])
        mod_x = mod[l, :b].reshape(b, 6, d)
        mod_c = mod[l, b].reshape(6, d)
        lam_init = 0.8 - 0.6 * math.exp(-0.3 * l)
        xa = _layer(xa, nc, mod_x, mod_c, p, lam_init, hg_lb[l], ct, st, with_ctx=l < depth - 1)
    return xa[:, nc:]
```

```python
import functools
import math

import jax
import jax.numpy as jnp
from jax import lax
from jax.experimental import pallas as pl
from jax.experimental.pallas import tpu as pltpu

F32 = jnp.float32
BF16 = jnp.bfloat16
HIGHEST = lax.Precision.HIGHEST

GRID_W = 64
RW_HEAD_DIM = 64
RW_DECAY_RANK = 64
RW_ICL_RANK = 64
RW_GATE_RANK = 128
RW_GN_EPS = 64e-5
DA_QK_DIM = 64
DA_V_DIM = 128
ROPE_THETA = 10000.0
ROPE_AXIS_FREQS = 16
DA_SUBLN_EPS = 1e-5
HG_EXPAND = 128
N_EXPERTS = 16
EC_CAPACITY_FACTOR = 2
NORM_EPS = 1e-6

LANES = 128
SUBLANES = 8
V7X_VMEM_BYTES = 64 * 1024 * 1024
HG_CHUNK = 64
RW_CHUNK = 64
RW_GROUP = 256


def _cparams(sem, vmem_mb):
    return pltpu.CompilerParams(dimension_semantics=sem, vmem_limit_bytes=int(vmem_mb * 1024 * 1024))


def _sigmoid(x):
    return 1.0 / (1.0 + jnp.exp(-x))


def _silu(x):
    return x * _sigmoid(x)


def _mod_kernel(c_ref, w_ref, b_ref, o_ref):
    a = _silu(c_ref[...]).astype(BF16)
    o_ref[0] = jnp.dot(a, w_ref[0].astype(BF16), preferred_element_type=F32) + b_ref[0]


def _modulation(cc, w_ada, b_ada):
    depth, d, n = w_ada.shape
    rows = cc.shape[0]
    tn = 1024
    return pl.pallas_call(
        _mod_kernel,
        out_shape=jax.ShapeDtypeStruct((depth, rows, n), F32),
        grid=(depth, n // tn),
        in_specs=[pl.BlockSpec((rows, d), lambda l, j: (0, 0)),
                  pl.BlockSpec((1, d, tn), lambda l, j: (l, 0, j)),
                  pl.BlockSpec((1, 1, tn), lambda l, j: (l, 0, j))],
        out_specs=pl.BlockSpec((1, rows, tn), lambda l, j: (l, 0, j)),
        compiler_params=_cparams(("arbitrary", "arbitrary"), 40),
        name="adaln_mod",
    )(cc, w_ada, b_ada.reshape(depth, 1, n))


def _norm_mod_kernel(x_ref, g_ref, ms_ref, *rest, with_router):
    x = x_ref[0]
    xn = x * lax.rsqrt(jnp.mean(x * x, axis=-1, keepdims=True) + NORM_EPS) * g_ref[...]
    h = xn * (1.0 + ms_ref[0, 0, 1:2, :]) + ms_ref[0, 0, 0:1, :]
    if with_router:
        rt_ref, h_ref, aff_ref = rest
        h_ref[0] = h.astype(BF16)
        lt = lax.dot_general(rt_ref[...], h, (((1,), (1,)), ((), ())),
                             precision=HIGHEST, preferred_element_type=F32)
        e = jnp.exp(lt - jnp.max(lt, axis=0, keepdims=True))
        aff_ref[0] = e / jnp.sum(e, axis=0, keepdims=True)
    else:
        (h_ref,) = rest
        h_ref[0] = h.astype(BF16)


def _norm_mod(xa, g, msel, nc, router_t=None, tl=256):
    b, lt, d = xa.shape
    tl = min(tl, nc)
    nct = nc // tl
    with_router = router_t is not None
    in_specs = [pl.BlockSpec((1, tl, d), lambda bi, i: (bi, i, 0)),
                pl.BlockSpec((1, d), lambda bi, i: (0, 0)),
                pl.BlockSpec((1, 1, 2, d), lambda bi, i: (bi, jnp.where(i >= nct, 1, 0), 0, 0))]
    args = [xa, g.reshape(1, d), msel]
    out_shape = [jax.ShapeDtypeStruct((b, lt, d), BF16)]
    out_specs = [pl.BlockSpec((1, tl, d), lambda bi, i: (bi, i, 0))]
    if with_router:
        e = router_t.shape[0]
        in_specs.append(pl.BlockSpec((e, d), lambda bi, i: (0, 0)))
        args.append(router_t)
        out_shape.append(jax.ShapeDtypeStruct((b, e, lt), F32))
        out_specs.append(pl.BlockSpec((1, e, tl), lambda bi, i: (bi, 0, i)))
    res = pl.pallas_call(
        functools.partial(_norm_mod_kernel, with_router=with_router),
        out_shape=out_shape, grid=(b, lt // tl), in_specs=in_specs, out_specs=out_specs,
        compiler_params=_cparams(("arbitrary", "arbitrary"), 32),
        name="norm_mod_router" if with_router else "norm_mod",
    )(*args)
    return res if with_router else res[0]


def _matmul_kernel(a_ref, w_ref, o_ref):
    o_ref[...] = jnp.dot(a_ref[...], w_ref[...], preferred_element_type=F32).astype(o_ref.dtype)


def _pick(n, cands):
    for c in cands:
        if n % c == 0:
            return c
    return n


def _matmul(a, w, out_dtype, tm=1024, tn=512):
    m, k = a.shape
    n = w.shape[1]
    tm = _pick(m, (tm, 768, 512, 384, 256, 128))
    tn = _pick(n, (tn, 384, 256, 128))
    return pl.pallas_call(
        _matmul_kernel,
        out_shape=jax.ShapeDtypeStruct((m, n), out_dtype),
        grid=(m // tm, n // tn),
        in_specs=[pl.BlockSpec((tm, k), lambda i, j: (i, 0)),
                  pl.BlockSpec((k, tn), lambda i, j: (0, j))],
        out_specs=pl.BlockSpec((tm, tn), lambda i, j: (i, j)),
        compiler_params=_cparams(("arbitrary", "arbitrary"), 48),
        name="matmul",
    )(a, w)


def _rw_prep_kernel(z_ref, zp_ref, zn_ref, mu_ref, w0_ref, wb_ref, a0_ref, ab_ref, gb_ref,
                    o_ref, g_ref, *, tl, nct, nt, w):
    i = pl.program_id(1)
    z = z_ref[0]
    row = lax.broadcasted_iota(jnp.int32, (tl, 1), 0)
    seg_start = jnp.logical_or(i == 0, i == nct)
    seg_end = jnp.logical_or(i == nct - 1, i == nt - 1)
    prev_row = jnp.where(seg_start, 0.0, zp_ref[0, SUBLANES - 1:SUBLANES, :])
    next_row = jnp.where(seg_end, 0.0, zn_ref[0, 0:1, :])
    prev = jnp.where(row == 0, prev_row, pltpu.roll(z, 1, 0))
    nxt = jnp.where(row == tl - 1, next_row, pltpu.roll(z, tl - 1, 0))
    zs = z + mu_ref[0:1, :] * (prev - z) + mu_ref[1:2, :] * (nxt - z)
    o_ref[0, 0] = zs[:, 0:w]
    o_ref[0, 1] = zs[:, w:2 * w]
    o_ref[0, 2] = zs[:, 2 * w:3 * w]
    o4 = 3 * w + 2 * RW_DECAY_RANK
    o5 = o4 + 2 * RW_ICL_RANK
    wd = jnp.tanh(zs[:, 3 * w:o4])
    wl = jnp.dot(wd, wb_ref[...], precision=HIGHEST, preferred_element_type=F32) + w0_ref[...]
    dec = -math.exp(-0.5) * _sigmoid(wl)
    o_ref[0, 3] = dec[:, 0:w]
    o_ref[0, 4] = dec[:, w:2 * w]
    al = jnp.dot(zs[:, o4:o5], ab_ref[...], precision=HIGHEST, preferred_element_type=F32) + a0_ref[...]
    av = _sigmoid(al)
    o_ref[0, 5] = av[:, 0:w]
    o_ref[0, 6] = av[:, w:2 * w]
    g_ref[0] = jnp.dot(_sigmoid(zs[:, o5:]), gb_ref[...], precision=HIGHEST, preferred_element_type=F32)


def _blockdiag2(m):
    r, w = m.shape[1], m.shape[2]
    z = jnp.zeros((r, w), m.dtype)
    return jnp.concatenate([jnp.concatenate([m[0], z], axis=1), jnp.concatenate([z, m[1]], axis=1)], axis=0)


def _rw_prep(za, nc, mu, w0, wb, a0, ab, gb, tl=256):
    b, lt, acols = za.shape
    w = w0.shape[1]
    tl = min(tl, nc)
    nct, nt = nc // tl, lt // tl
    r8 = tl // SUBLANES
    nb8 = lt // SUBLANES
    full = lambda shp: pl.BlockSpec(shp, lambda bi, i: tuple(0 for _ in shp))
    kern = functools.partial(_rw_prep_kernel, tl=tl, nct=nct, nt=nt, w=w)
    return pl.pallas_call(
        kern,
        out_shape=[jax.ShapeDtypeStruct((b, 7, lt, w), F32), jax.ShapeDtypeStruct((b, lt, w), F32)],
        grid=(b, nt),
        in_specs=[pl.BlockSpec((1, tl, acols), lambda bi, i: (bi, i, 0)),
                  pl.BlockSpec((1, SUBLANES, acols), lambda bi, i: (bi, jnp.maximum(i * r8 - 1, 0), 0)),
                  pl.BlockSpec((1, SUBLANES, acols), lambda bi, i: (bi, jnp.minimum((i + 1) * r8, nb8 - 1), 0)),
                  full((2, acols)), full((1, 2 * w)), full((2 * RW_DECAY_RANK, 2 * w)),
                  full((1, 2 * w)), full((2 * RW_ICL_RANK, 2 * w)), full((RW_GATE_RANK, w))],
        out_specs=[pl.BlockSpec((1, 7, tl, w), lambda bi, i: (bi, 0, i, 0)),
                   pl.BlockSpec((1, tl, w), lambda bi, i: (bi, i, 0))],
        compiler_params=_cparams(("arbitrary", "arbitrary"), 48),
        name="rwkv_prep",
    )(za, za, za, mu, w0.reshape(1, 2 * w), _blockdiag2(wb), a0.reshape(1, 2 * w), _blockdiag2(ab), gb)


def _rw_chunk_kernel(xf_ref, lwf_ref, af_ref, xb_ref, lwb_ref, ab_ref, kk_ref, ka_ref, rk_ref, gn0_ref, gn1_ref,
                     of_ref, ob_ref, s_ref, *, c, w):
    j = pl.program_id(1)

    @pl.when(j == 0)
    def _():
        s_ref[...] = jnp.zeros_like(s_ref)

    gw, n = RW_GROUP, RW_HEAD_DIM
    ng = w // gw
    row = lax.broadcasted_iota(jnp.int32, (c, 1), 0)
    pos = lax.broadcasted_iota(jnp.int32, (1, gw), 1) % n
    bdmask = (lax.broadcasted_iota(jnp.int32, (gw, 1), 0) // n) == (lax.broadcasted_iota(jnp.int32, (1, gw), 1) // n)
    ones_bd = jnp.where(bdmask, 1.0, 0.0).astype(BF16)
    eye = jnp.where(pos == row, 1.0, 0.0)
    nt = (((1,), (1,)), ((), ()))
    tn = (((0,), (0,)), ((), ()))
    chains = [(d, g) for d in range(2) for g in range(ng)]
    refs = ((xf_ref, lwf_ref, af_ref, of_ref), (xb_ref, lwb_ref, ab_ref, ob_ref))

    def bd(x):
        return jnp.where(bdmask, jnp.concatenate([x] * (gw // c), axis=0), 0.0).astype(BF16)

    def gsum(xs):
        pieces = []
        for x in xs:
            hi = x.astype(BF16)
            r1 = x - hi.astype(F32)
            mid = r1.astype(BF16)
            pieces += [hi, mid, (r1 - mid.astype(F32)).astype(BF16)]
        tot = jnp.dot(jnp.concatenate(pieces, axis=0), ones_bd, preferred_element_type=F32)
        return [tot[3 * i * c:(3 * i + 1) * c] + tot[(3 * i + 1) * c:(3 * i + 2) * c] + tot[(3 * i + 2) * c:(3 * i + 3) * c]
                for i in range(len(xs))]

    def mm(x, ybd):
        return jnp.dot(x.astype(BF16), ybd, preferred_element_type=F32)

    st = []
    for d, g in chains:
        x_ref, lw_ref, a_ref, _ = refs[d]
        sl = slice(g * gw, (g + 1) * gw)
        q = dict(sl=sl, rev=d == 1, r=x_ref[0, 0, :, sl], k=x_ref[0, 1, :, sl], v=x_ref[0, 2, :, sl],
                 lw=lw_ref[0, 0, :, sl], a=a_ref[0, 0, :, sl])
        q['kx'] = q['k'] * kk_ref[:, sl]
        q['kt'] = q['k'] * (1.0 + (q['a'] - 1.0) * ka_ref[:, sl])
        st.append(q)
    for q in st:
        q['ss'], q['bon'] = gsum([q['kx'] * q['kx'], q['r'] * q['kt'] * rk_ref[:, q['sl']]])
    for q in st:
        rev = q['rev']
        kk = q['kx'] / jnp.maximum(jnp.sqrt(q['ss']), 1e-12)
        q['b'] = q['a'] * kk
        cw = q['lw']
        sft = 1
        while sft < c:
            if rev:
                cw = cw + jnp.where(row < c - sft, pltpu.roll(cw, c - sft, 0), 0.0)
            else:
                cw = cw + jnp.where(row >= sft, pltpu.roll(cw, sft, 0), 0.0)
            sft *= 2
        q['tot'] = cw[0:1] if rev else cw[c - 1:c]
        invp = jnp.exp(-cw)
        q['e2'] = jnp.exp(q['tot'] - cw)
        q['kr'] = jnp.concatenate([kk * jnp.exp(cw - q['lw']), q['r'] * jnp.exp(cw)], axis=0).astype(BF16)
        q['ktb'] = bd(q['kt'] * invp)
        q['bb'] = bd(q['b'] * invp)
        q['vbd'] = bd(q['v'])
    for q in st:
        strict = (pos > row) if q['rev'] else (pos < row)
        incl = (pos >= row) if q['rev'] else (pos <= row)
        g1 = lax.dot_general(q['kr'], q['ktb'], nt, preferred_element_type=F32)
        g2 = lax.dot_general(q['kr'], q['bb'], nt, preferred_element_type=F32)
        q['m_kt'] = jnp.where(strict, g1[:c], 0.0)
        q['n_kt'] = jnp.where(incl, g1[c:], 0.0)
        q['x'] = jnp.where(strict, g2[:c], 0.0)
        q['n_b'] = jnp.where(incl, g2[c:], 0.0)
        q['t'] = eye - q['x']
        q['xb'] = bd(q['x'])
    for q in st:
        q['mkv'] = mm(q['m_kt'], q['vbd'])
        q['nkv'] = mm(q['n_kt'], q['vbd'])
    lv = 2
    while lv < c:
        for q in st:
            q['x'] = mm(q['x'], q['xb'])
        for q in st:
            q['xb'] = bd(q['x'])
        for q in st:
            q['t'] = q['t'] + mm(q['t'], q['xb'])
        lv *= 2
    for (d, g), q in zip(chains, st):
        q['s0'] = s_ref[d, g]
        q['ka'] = lax.dot_general(q['kr'], q['s0'].astype(BF16), nt, preferred_element_type=F32)
    for q in st:
        q['u'] = mm(q['t'], bd(q['ka'][:c] + q['mkv']))
    for (d, g), q in zip(chains, st):
        e2 = q['e2']
        upd = lax.dot_general(jnp.concatenate([q['v'], q['u']], axis=0).astype(BF16),
                              jnp.concatenate([q['kt'] * e2, -(q['b'] * e2)], axis=0).astype(BF16),
                              tn, preferred_element_type=F32)
        s_ref[d, g] = q['s0'] * jnp.exp(q['tot']) + jnp.where(bdmask, upd, 0.0)
    for q in st:
        q['y'] = q['ka'][c:] + q['nkv'] - mm(q['n_b'], bd(q['u']))
    for q in st:
        (mu,) = gsum([q['y']])
        q['dl'] = q['y'] - mu * (1.0 / n)
    for (d, g), q in zip(chains, st):
        (var,) = gsum([q['dl'] * q['dl']])
        sl = q['sl']
        yn = q['dl'] * lax.rsqrt(var * (1.0 / n) + RW_GN_EPS) * gn0_ref[:, sl] + gn1_ref[:, sl]
        refs[d][3][0, :, sl] = yn + q['bon'] * q['v']


def _rw_chunk(streams, nc, kk, ka, rk, gn):
    b, _, lt, w = streams.shape
    c = RW_CHUNK
    ncb, nb = nc // c, lt // c

    def rblk(j):
        return jnp.where(j < ncb, ncb - 1 - j, nb - 1 - (j - ncb))

    par = pl.BlockSpec((1, w), lambda bi, j: (0, 0))
    shp = jax.ShapeDtypeStruct((b, lt, w), F32)
    return pl.pallas_call(
        functools.partial(_rw_chunk_kernel, c=c, w=w),
        out_shape=[shp, shp],
        grid=(b, nb),
        in_specs=[pl.BlockSpec((1, 3, c, w), lambda bi, j: (bi, 0, j, 0)),
                  pl.BlockSpec((1, 1, c, w), lambda bi, j: (bi, 3, j, 0)),
                  pl.BlockSpec((1, 1, c, w), lambda bi, j: (bi, 5, j, 0)),
                  pl.BlockSpec((1, 3, c, w), lambda bi, j: (bi, 0, rblk(j), 0)),
                  pl.BlockSpec((1, 1, c, w), lambda bi, j: (bi, 4, rblk(j), 0)),
                  pl.BlockSpec((1, 1, c, w), lambda bi, j: (bi, 6, rblk(j), 0)),
                  par, par, par, par, par],
        out_specs=[pl.BlockSpec((1, c, w), lambda bi, j: (bi, j, 0)),
                   pl.BlockSpec((1, c, w), lambda bi, j: (bi, rblk(j), 0))],
        scratch_shapes=[pltpu.VMEM((2, w // RW_GROUP, RW_GROUP, RW_GROUP), F32)],
        compiler_params=_cparams(("arbitrary", "arbitrary"), 32),
        name="rwkv_chunk",
    )(streams, streams, streams, streams, streams, streams, kk.reshape(1, w), ka.reshape(1, w), rk.reshape(1, w),
      gn[0].reshape(1, w), gn[1].reshape(1, w))


def _rope_kernel(z_ref, c_ref, s_ref, o_ref, *, w):
    lane = lax.broadcasted_iota(jnp.int32, (1, LANES), 1)
    first_half = (lane % (2 * ROPE_AXIS_FREQS)) < ROPE_AXIS_FREQS
    c = c_ref[...]
    s = s_ref[...]
    for j in range(2 * w // LANES):
        xs = z_ref[0, :, j * LANES:(j + 1) * LANES]
        up = pltpu.roll(xs, LANES - ROPE_AXIS_FREQS, 1)
        dn = pltpu.roll(xs, ROPE_AXIS_FREQS, 1)
        o = xs * c + jnp.where(first_half, up, dn) * s
        if j < w // LANES:
            o = o * (DA_QK_DIM ** -0.5)
        o_ref[0, :, j * LANES:(j + 1) * LANES] = o.astype(BF16)
    o_ref[0, :, 2 * w:] = z_ref[0, :, 2 * w:].astype(BF16)


def _rope_tables(seq, nc):
    rows = seq // GRID_W
    row = jnp.broadcast_to(jnp.arange(rows)[:, None], (rows, GRID_W)).reshape(seq)
    col = jnp.broadcast_to(jnp.arange(GRID_W)[None, :], (rows, GRID_W)).reshape(seq)
    inv = 1.0 / (ROPE_THETA ** (jnp.arange(ROPE_AXIS_FREQS, dtype=F32) / ROPE_AXIS_FREQS))
    ang = jnp.stack([row, col], axis=-1).astype(F32)[:, :, None] * inv
    cos, sin = jnp.cos(ang), jnp.sin(ang)
    c64 = jnp.concatenate([cos[:, 0], cos[:, 0], cos[:, 1], cos[:, 1]], axis=-1)
    s64 = jnp.concatenate([-sin[:, 0], sin[:, 0], -sin[:, 1], sin[:, 1]], axis=-1)
    ct = jnp.concatenate([jnp.ones((nc, LANES), F32), jnp.tile(c64, (1, 2))], axis=0)
    st = jnp.concatenate([jnp.zeros((nc, LANES), F32), jnp.tile(s64, (1, 2))], axis=0)
    return ct, st


def _rope(zb, ct, st, tl=256):
    b, lt, cols = zb.shape
    w = cols // 3
    tl = _pick(lt, (tl, 128, 64))
    return pl.pallas_call(
        functools.partial(_rope_kernel, w=w),
        out_shape=jax.ShapeDtypeStruct((b, lt, cols), BF16),
        grid=(b, lt // tl),
        in_specs=[pl.BlockSpec((1, tl, cols), lambda bi, i: (bi, i, 0)),
                  pl.BlockSpec((tl, LANES), lambda bi, i: (i, 0)),
                  pl.BlockSpec((tl, LANES), lambda bi, i: (i, 0))],
        out_specs=pl.BlockSpec((1, tl, cols), lambda bi, i: (bi, i, 0)),
        compiler_params=_cparams(("arbitrary", "arbitrary"), 32),
        name="rope",
    )(zb, ct, st)


def _attn_kernel(q_ref, k_ref, v_ref, lam_ref, g_ref, o_ref, *, nct, nc, lam_init):
    i = pl.program_id(2)
    q = q_ref[0]
    lane = lax.broadcasted_iota(jnp.int32, (1, LANES), 1)
    lp = lam_ref[...]
    lam = (jnp.exp(jnp.sum(lp[0:1] * lp[1:2], axis=1, keepdims=True))
           - jnp.exp(jnp.sum(lp[2:3] * lp[3:4], axis=1, keepdims=True)) + lam_init)

    def attend(nk):
        k = k_ref[0, 0:nk, :]
        v = v_ref[0, 0:nk, :]

        def one(sel):
            qs = jnp.where(sel, q, jnp.zeros_like(q))
            s = lax.dot_general(qs, k, (((1,), (1,)), ((), ())), preferred_element_type=F32)
            e = jnp.exp(s - jnp.max(s, axis=-1, keepdims=True))
            return jnp.dot(e.astype(BF16), v, preferred_element_type=F32) / jnp.sum(e, axis=-1, keepdims=True)

        o = one(lane < DA_QK_DIM) - lam * one(lane >= DA_QK_DIM)
        on = o * lax.rsqrt(jnp.mean(o * o, axis=-1, keepdims=True) + DA_SUBLN_EPS) * g_ref[...]
        o_ref[0] = (on * (1.0 - lam_init)).astype(o_ref.dtype)

    @pl.when(i < nct)
    def _():
        attend(nc)

    @pl.when(i >= nct)
    def _():
        attend(k_ref.shape[1])


def _attention(zbr, nc, lam_p, subln, lam_init):
    b, lt, cols = zbr.shape
    w = cols // 3
    heads = w // DA_V_DIM
    tq = min(256, nc)
    nct = nc // tq
    kern = functools.partial(_attn_kernel, nct=nct, nc=nc, lam_init=lam_init)
    return pl.pallas_call(
        kern,
        out_shape=jax.ShapeDtypeStruct((b, lt, w), BF16),
        grid=(b, heads, lt // tq),
        in_specs=[pl.BlockSpec((1, tq, DA_V_DIM), lambda bi, h, i: (bi, i, h)),
                  pl.BlockSpec((1, lt, DA_V_DIM), lambda bi, h, i: (bi, 0, heads + h)),
                  pl.BlockSpec((1, lt, DA_V_DIM), lambda bi, h, i: (bi, 0, 2 * heads + h)),
                  pl.BlockSpec((4, DA_QK_DIM), lambda bi, h, i: (0, 0)),
                  pl.BlockSpec((1, DA_V_DIM), lambda bi, h, i: (0, 0))],
        out_specs=pl.BlockSpec((1, tq, DA_V_DIM), lambda bi, h, i: (bi, i, h)),
        compiler_params=_cparams(("arbitrary", "arbitrary", "arbitrary"), 40),
        name="diff_attn",
    )(zbr, zbr, zbr, lam_p, subln.reshape(1, DA_V_DIM))


def _seg_scans(lf, c):
    row = lax.broadcasted_iota(jnp.int32, (c, 1), 0)
    cs, ss = {1: lf}, {1: lf}
    h = 1
    while h < c:
        x, y = cs[h], ss[h]
        if h < SUBLANES:
            addx = jnp.zeros_like(x)
            addy = jnp.zeros_like(y)
            odd = (row % (2 * h)) >= h
            for s in range(1, h + 1):
                addx = addx + jnp.where(jnp.logical_and(odd, (row % h) == s - 1), pltpu.roll(x, s, 0), 0.0)
                addy = addy + jnp.where(jnp.logical_and(~odd, (row % h) == h - s), pltpu.roll(y, c - s, 0), 0.0)
            cs[2 * h], ss[2 * h] = x + addx, y + addy
        else:
            px = [x[j * h:(j + 1) * h] for j in range(c // h)]
            py = [y[j * h:(j + 1) * h] for j in range(c // h)]
            nx = [px[j] + px[j - 1][h - 1:h] if j % 2 == 1 else px[j] for j in range(c // h)]
            ny = [py[j] + py[j + 1][0:1] if j % 2 == 0 else py[j] for j in range(c // h)]
            cs[2 * h], ss[2 * h] = jnp.concatenate(nx, axis=0), jnp.concatenate(ny, axis=0)
        h *= 2
    return cs, ss


def _hg_kernel(q_ref, f_ref, v_ref, lb_ref, o_ref, st_ref, *, tc, c, heads, rev):
    j = pl.program_id(1)

    @pl.when(j == 0)
    def _():
        st_ref[...] = jnp.zeros_like(st_ref)

    ti = lax.broadcasted_iota(jnp.int32, (c, c), 0)
    si = lax.broadcasted_iota(jnp.int32, (c, c), 1)
    nt = (((1,), (1,)), ((), ()))
    tn = (((0,), (0,)), ((), ()))

    def head(h, carry):
        cols = pl.ds(pl.multiple_of(h * HG_EXPAND, HG_EXPAND), HG_EXPAND)
        lb = lb_ref[:, cols]
        chunks = range(tc // c)
        for ci in (reversed(chunks) if rev else chunks):
            rows = slice(ci * c, (ci + 1) * c)
            q = _silu(q_ref[0, rows, cols])
            f = lb + (1.0 - lb) * _sigmoid(f_ref[0, rows, cols])
            k = 1.0 - f
            lf = jnp.log(f)
            v = v_ref[0, rows, cols].astype(BF16)
            cs, ss = _seg_scans(lf, c)
            qs, ks = (ss, cs) if rev else (cs, ss)
            att = jnp.where(ti == si, lax.dot_general(q.astype(BF16), k.astype(BF16), nt,
                                                      preferred_element_type=F32), 0.0)
            hh = 1
            while hh < c:
                qe = (q * jnp.exp(qs[hh])).astype(BF16)
                ke = (k * jnp.exp(ks[hh] - lf)).astype(BF16)
                a = lax.dot_general(qe, ke, nt, preferred_element_type=F32)
                tb, sb = ti // hh, si // hh
                if rev:
                    m = jnp.logical_and(tb % 2 == 0, sb == tb + 1)
                else:
                    m = jnp.logical_and(tb % 2 == 1, sb == tb - 1)
                att = att + jnp.where(m, a, 0.0)
                hh *= 2
            st = st_ref[h]
            qe = (q * jnp.exp(qs[c])).astype(BF16)
            o = lax.dot_general(qe, st.astype(BF16), nt, preferred_element_type=F32)
            o = o + jnp.dot(att.astype(BF16), v, preferred_element_type=F32)
            o_ref[0, rows, cols] = o
            ke = (k * jnp.exp(ks[c] - lf)).astype(BF16)
            tot = qs[c][0:1] if rev else qs[c][c - 1:c]
            st_ref[h] = st * jnp.exp(tot) + lax.dot_general(v, ke, tn, preferred_element_type=F32)
        return carry

    lax.fori_loop(0, heads, head, 0)


def _hgrn(zc, nc, lb, rev):
    b, lt, cols = zc.shape
    w = cols // 5
    heads = w // HG_EXPAND
    tc = min(256, nc)
    c = min(HG_CHUNK, tc)
    ncb, nb = nc // tc, lt // tc

    def blk(j):
        return jnp.where(j < ncb, ncb - 1 - j, nb - 1 - (j - ncb)) if rev else j

    fcol = 2 if rev else 1
    return pl.pallas_call(
        functools.partial(_hg_kernel, tc=tc, c=c, heads=heads, rev=rev),
        out_shape=jax.ShapeDtypeStruct((b, lt, w), F32),
        grid=(b, nb),
        in_specs=[pl.BlockSpec((1, tc, w), lambda bi, j: (bi, blk(j), 0)),
                  pl.BlockSpec((1, tc, w), lambda bi, j: (bi, blk(j), fcol)),
                  pl.BlockSpec((1, tc, w), lambda bi, j: (bi, blk(j), 3)),
                  pl.BlockSpec((1, w), lambda bi, j: (0, 0))],
        out_specs=pl.BlockSpec((1, tc, w), lambda bi, j: (bi, blk(j), 0)),
        scratch_shapes=[pltpu.VMEM((heads, HG_EXPAND, HG_EXPAND), F32)],
        compiler_params=_cparams(("arbitrary", "arbitrary"), 32),
        name="hgrn2_bwd" if rev else "hgrn2_fwd",
    )(zc, zc, zc, lb.reshape(1, w))


def _merge_kernel(oaf_ref, oab_ref, ga_ref, yb_ref, of_ref, ob_ref, gc_ref, hn_ref, gl0_ref, gl1_ref, gl2_ref, wb_ref,
                  o_ref, *, heads):
    gls = (gl0_ref, gl1_ref, gl2_ref)
    ya = (oaf_ref[0] + oab_ref[0]) * ga_ref[0]
    oc = of_ref[0] + ob_ref[0]
    parts = []
    for h in range(heads):
        x = oc[:, h * HG_EXPAND:(h + 1) * HG_EXPAND]
        parts.append(x * lax.rsqrt(jnp.mean(x * x, axis=-1, keepdims=True) + NORM_EPS))
    yc = jnp.concatenate(parts, axis=-1) * hn_ref[...] * _silu(gc_ref[0])
    acc = None
    for bi, y in enumerate((ya.astype(BF16), yb_ref[0], yc.astype(BF16))):
        term = _sigmoid(gls[bi][0]) * jnp.dot(y, wb_ref[bi], preferred_element_type=F32)
        acc = term if acc is None else acc + term
    o_ref[0] = acc.astype(BF16)


def _merge(oaf, oab, ga, yb, of, ob, zc, hn, gl, wbr, tl=256, tn=1024):
    b, lt, w = ga.shape
    d = wbr.shape[2]
    tl = _pick(lt, (tl, 128, 64))
    heads = w // HG_EXPAND
    nn = d // tn
    row = lambda n, bi, i: (bi, i, 0)
    glspec = lambda br: pl.BlockSpec((1, tl, tn), lambda n, bi, i: (bi, i, br * nn + n))
    return pl.pallas_call(
        functools.partial(_merge_kernel, heads=heads),
        out_shape=jax.ShapeDtypeStruct((b, lt, d), BF16),
        grid=(d // tn, b, lt // tl),
        in_specs=[pl.BlockSpec((1, tl, w), row),
                  pl.BlockSpec((1, tl, w), row),
                  pl.BlockSpec((1, tl, w), row),
                  pl.BlockSpec((1, tl, w), row),
                  pl.BlockSpec((1, tl, w), row),
                  pl.BlockSpec((1, tl, w), row),
                  pl.BlockSpec((1, tl, w), lambda n, bi, i: (bi, i, 4)),
                  pl.BlockSpec((1, w), lambda n, bi, i: (0, 0)),
                  glspec(0), glspec(1), glspec(2),
                  pl.BlockSpec((3, w, tn), lambda n, bi, i: (0, 0, n))],
        out_specs=pl.BlockSpec((1, tl, tn), lambda n, bi, i: (bi, i, n)),
        compiler_params=_cparams(("arbitrary", "arbitrary", "arbitrary"), 48),
        name="branch_merge",
    )(oaf, oab, ga, yb, of, ob, zc, hn, gl, gl, gl, wbr)


def _proj_post_kernel(a_ref, w_ref, x_ref, g_ref, gate_ref, o_ref):
    mx = jnp.dot(a_ref[0], w_ref[...], preferred_element_type=F32)
    mn = mx * lax.rsqrt(jnp.mean(mx * mx, axis=-1, keepdims=True) + NORM_EPS) * g_ref[...]
    o_ref[0] = x_ref[0] + gate_ref[0, 0] * mn


def _proj_post(a, wout, xa, g, gsel, nc, tl=256):
    b, lt, d = xa.shape
    tl = min(tl, nc)
    nct = nc // tl
    return pl.pallas_call(
        _proj_post_kernel,
        out_shape=jax.ShapeDtypeStruct((b, lt, d), F32),
        grid=(b, lt // tl),
        in_specs=[pl.BlockSpec((1, tl, d), lambda bi, i: (bi, i, 0)),
                  pl.BlockSpec((d, d), lambda bi, i: (0, 0)),
                  pl.BlockSpec((1, tl, d), lambda bi, i: (bi, i, 0)),
                  pl.BlockSpec((1, d), lambda bi, i: (0, 0)),
                  pl.BlockSpec((1, 1, 1, d), lambda bi, i: (bi, jnp.where(i >= nct, 1, 0), 0, 0))],
        out_specs=pl.BlockSpec((1, tl, d), lambda bi, i: (bi, i, 0)),
        input_output_aliases={2: 0},
        compiler_params=_cparams(("arbitrary", "arbitrary"), 48),
        name="out_proj_post",
    )(a, wout, xa, g.reshape(1, d), gsel)


def _rank_kernel(afft_ref, slot_ref, *, n, cap, tw):
    aff = afft_ref[0]
    bits = lax.bitcast_convert_type(aff, jnp.int32)
    e_num = aff.shape[0]

    def count(mask):
        return jnp.sum(jnp.where(mask, 1.0, 0.0), axis=1, keepdims=True)

    def bit_step(i, thr):
        cand = thr | lax.shift_left(jnp.int32(1), 30 - i)
        return jnp.where(count(bits >= cand) >= cap, cand, thr)

    thr = lax.fori_loop(0, 31, bit_step, jnp.zeros((e_num, 1), jnp.int32))
    gt = bits > thr
    eq = bits == thr
    need = cap - count(gt)
    tri = jnp.where(lax.broadcasted_iota(jnp.int32, (tw, tw), 0) < lax.broadcasted_iota(jnp.int32, (tw, tw), 1),
                    1.0, 0.0).astype(BF16)

    def prefix(mask):
        parts, carry = [], jnp.zeros((e_num, 1), F32)
        for j in range(n // tw):
            m = jnp.where(mask[:, j * tw:(j + 1) * tw], 1.0, 0.0)
            parts.append(jnp.dot(m.astype(BF16), tri, preferred_element_type=F32) + carry)
            carry = carry + jnp.sum(m, axis=1, keepdims=True)
        return jnp.concatenate(parts, axis=1)

    sel = jnp.logical_or(gt, jnp.logical_and(eq, prefix(eq) < need))
    slot_ref[0] = jnp.where(sel, prefix(sel), float(cap)).astype(jnp.int32)


def _rank(afft, cap):
    b, e_num, n = afft.shape
    tw = min(LANES, n)
    return pl.pallas_call(
        functools.partial(_rank_kernel, n=n, cap=cap, tw=tw),
        out_shape=jax.ShapeDtypeStruct((b, e_num, n), jnp.int32),
        grid=(b,),
        in_specs=[pl.BlockSpec((1, e_num, n), lambda bi: (bi, 0, 0))],
        out_specs=pl.BlockSpec((1, e_num, n), lambda bi: (bi, 0, 0)),
        compiler_params=_cparams(("arbitrary",), 32),
        name="ec_rank",
    )(afft)


def _gather_kernel(slot_ref, afft_ref, h_ref, o_ref, g_ref, *, cap):
    e = pl.program_id(1)
    sl = slot_ref[0, pl.ds(e, 1), :]
    hit = lax.broadcasted_iota(jnp.int32, (cap, 1), 0) == sl
    o_ref[0] = jnp.dot(jnp.where(hit, 1.0, 0.0).astype(BF16), h_ref[0], preferred_element_type=F32).astype(BF16)
    gate = jnp.sum(jnp.where(hit, afft_ref[0, pl.ds(e, 1), :], 0.0), axis=1, keepdims=True)
    g_ref[0] = jnp.broadcast_to(gate, (cap, LANES))


def _gather(slot, afft, hs, cap):
    b, e_num, n = slot.shape
    d = hs.shape[2]
    return pl.pallas_call(
        functools.partial(_gather_kernel, cap=cap),
        out_shape=[jax.ShapeDtypeStruct((e_num, b * cap, d), BF16),
                   jax.ShapeDtypeStruct((e_num, b * cap, LANES), F32)],
        grid=(b, e_num),
        in_specs=[pl.BlockSpec((1, e_num, n), lambda bi, e: (bi, 0, 0)),
                  pl.BlockSpec((1, e_num, n), lambda bi, e: (bi, 0, 0)),
                  pl.BlockSpec((1, n, d), lambda bi, e: (bi, 0, 0))],
        out_specs=[pl.BlockSpec((1, cap, d), lambda bi, e: (e, bi, 0)),
                   pl.BlockSpec((1, cap, LANES), lambda bi, e: (e, bi, 0))],
        compiler_params=_cparams(("arbitrary", "arbitrary"), 40),
        name="ec_gather",
    )(slot, afft, hs)


def _ffn_kernel(x_ref, g_ref, w1_ref, w3_ref, w2_ref, o_ref, acc_ref):
    f = pl.program_id(2)
    x = x_ref[0]
    h1 = jnp.dot(x, w1_ref[0].astype(BF16), preferred_element_type=F32)
    h3 = jnp.dot(x, w3_ref[0].astype(BF16), preferred_element_type=F32)
    hid = (_silu(h1) * h3).astype(BF16)
    part = jnp.dot(hid, w2_ref[0].astype(BF16), preferred_element_type=F32)

    @pl.when(f == 0)
    def _():
        acc_ref[...] = part

    @pl.when(f != 0)
    def _():
        acc_ref[...] += part

    @pl.when(f == pl.num_programs(2) - 1)
    def _():
        o_ref[0] = (acc_ref[...] * g_ref[0, :, 0:1]).astype(BF16)


def _ffn(xe, gate, w1, w3, w2, tf=256):
    e_num, m, d = xe.shape
    ff = w1.shape[2]
    tm = _pick(m, (1152, 1024, 768, 512, 384, 256, 128, 64, 32, 16, 8))
    return pl.pallas_call(
        _ffn_kernel,
        out_shape=jax.ShapeDtypeStruct((e_num, m, d), BF16),
        grid=(e_num, m // tm, ff // tf),
        in_specs=[pl.BlockSpec((1, tm, d), lambda e, i, f: (e, i, 0)),
                  pl.BlockSpec((1, tm, LANES), lambda e, i, f: (e, i, 0)),
                  pl.BlockSpec((1, d, tf), lambda e, i, f: (e, 0, f)),
                  pl.BlockSpec((1, d, tf), lambda e, i, f: (e, 0, f)),
                  pl.BlockSpec((1, tf, d), lambda e, i, f: (e, f, 0))],
        out_specs=pl.BlockSpec((1, tm, d), lambda e, i, f: (e, i, 0)),
        scratch_shapes=[pltpu.VMEM((tm, d), F32)],
        compiler_params=_cparams(("arbitrary", "arbitrary", "arbitrary"), 52),
        name="ec_ffn",
    )(xe, gate, w1, w3, w2)


def _combine_kernel(slot_ref, ye_ref, x_ref, g_ref, gate_ref, o_ref, *, cap, e_num):
    lane = lax.broadcasted_iota(jnp.int32, (1, e_num), 1)
    pos = lax.broadcasted_iota(jnp.int32, (1, cap), 1)
    sl = slot_ref[0]
    y = None
    for e in range(e_num):
        se = jnp.sum(jnp.where(lane == e, sl, 0), axis=1, keepdims=True)
        pt = jnp.where(se == pos, 1.0, 0.0).astype(BF16)
        part = jnp.dot(pt, ye_ref[e], preferred_element_type=F32)
        y = part if y is None else y + part
    yn = y * lax.rsqrt(jnp.mean(y * y, axis=-1, keepdims=True) + NORM_EPS) * g_ref[...]
    o_ref[0] = x_ref[0] + gate_ref[0, 0] * yn


def _combine(slot_c, ye, xa, g, gsel, seg, row0, slot0, cap, tt=256):
    b, n, e_num = slot_c.shape
    d = xa.shape[2]
    tt = _pick(math.gcd(n, row0), (tt, 128, 64))
    assert row0 % tt == 0 and slot0 % cap == 0
    r0, s0 = row0 // tt, slot0 // cap
    return pl.pallas_call(
        functools.partial(_combine_kernel, cap=cap, e_num=e_num),
        out_shape=jax.ShapeDtypeStruct(xa.shape, F32),
        grid=(b, n // tt),
        in_specs=[pl.BlockSpec((1, tt, e_num), lambda bi, j: (bi, j, 0)),
                  pl.BlockSpec((e_num, cap, d), lambda bi, j: (0, s0 + bi, 0)),
                  pl.BlockSpec((1, tt, d), lambda bi, j: (bi, r0 + j, 0)),
                  pl.BlockSpec((1, d), lambda bi, j: (0, 0)),
                  pl.BlockSpec((1, 1, 1, d), lambda bi, j: (bi, seg, 0, 0))],
        out_specs=pl.BlockSpec((1, tt, d), lambda bi, j: (bi, r0 + j, 0)),
        input_output_aliases={2: 0},
        compiler_params=_cparams(("arbitrary", "arbitrary"), 56),
        name="ec_combine_post",
    )(slot_c, ye, xa, g.reshape(1, d), gsel)


def _sel(mod_c, mod_x, idx):
    b = mod_x.shape[0]
    mc = jnp.broadcast_to(mod_c[jnp.array(idx)][None], (b, len(idx), mod_c.shape[-1]))
    return jnp.stack([mc, mod_x[:, jnp.array(idx)]], axis=1)


def _layer(xa, nc, mod_x, mod_c, p, lam_init, hg_lb, ct, st, with_ctx):
    b, lt, d = xa.shape
    w = d // 2
    a_cols = 3 * w + 2 * RW_DECAY_RANK + 2 * RW_ICL_RANK + RW_GATE_RANK
    col_b = a_cols
    col_c = col_b + 3 * w
    col_g = col_c + 5 * w

    h = _norm_mod(xa, p['norm_g'][0], _sel(mod_c, mod_x, (0, 1)), nc)
    h2 = h.reshape(b * lt, d)
    w_in = p['w_in']
    za = _matmul(h2, w_in[:, :col_b].astype(BF16), F32, tn=384).reshape(b, lt, a_cols)
    zb = _matmul(h2, w_in[:, col_b:col_c].astype(BF16), F32).reshape(b, lt, 3 * w)
    zc = _matmul(h2, w_in[:, col_c:col_g].astype(BF16), F32).reshape(b, lt, 5 * w)
    gl = _matmul(h2, w_in[:, col_g:].astype(BF16), F32).reshape(b, lt, 3 * d)

    streams, ga = _rw_prep(za, nc, p['rw_mu'], p['rw_w0'], p['rw_wB'], p['rw_a0'], p['rw_aB'], p['rw_gB'])
    oaf, oab = _rw_chunk(streams, nc, p['rw_kk'], p['rw_ka'], p['rw_rk'], p['rw_gn'])

    zbr = _rope(zb, ct, st)
    yb = _attention(zbr, nc, p['da_lambda'], p['da_subln'], lam_init)

    of = _hgrn(zc, nc, hg_lb, rev=False)
    ob = _hgrn(zc, nc, hg_lb, rev=True)

    hn = jnp.tile(p['hg_norm'], w // HG_EXPAND).reshape(1, w)
    merged = _merge(oaf, oab, ga, yb, of, ob, zc, hn, gl, p['w_branch'].astype(BF16))
    xa = _proj_post(merged, p['w_out'].astype(BF16), xa, p['norm_g'][1], _sel(mod_c, mod_x, (2,)), nc)

    hm, afft = _norm_mod(xa, p['norm_g'][2], _sel(mod_c, mod_x, (3, 4)), nc, router_t=p['moe_router'].T)
    gsel = _sel(mod_c, mod_x, (5,))
    sets = [(nc, lt - nc, 1)] + ([(0, nc, 0)] if with_ctx else [])
    routed, xes, gates = [], [], []
    for row0, nn, seg in sets:
        cap = EC_CAPACITY_FACTOR * nn // N_EXPERTS
        at = afft[:, :, row0:row0 + nn]
        slot = _rank(at, cap)
        xe_s, gate_s = _gather(slot, at, hm[:, row0:row0 + nn], cap)
        xes.append(xe_s)
        gates.append(gate_s)
        routed.append((jnp.swapaxes(slot, 1, 2), row0, seg, cap))
    xe = xes[0] if len(xes) == 1 else jnp.concatenate(xes, axis=1)
    gate = gates[0] if len(gates) == 1 else jnp.concatenate(gates, axis=1)
    ye = _ffn(xe, gate, p['moe_w1'], p['moe_w3'], p['moe_w2'])
    slot0 = 0
    for slot_c, row0, seg, cap in routed:
        xa = _combine(slot_c, ye, xa, p['norm_g'][3], gsel, seg, row0, slot0, cap)
        slot0 += b * cap
    return xa


def kernel(x, c, ctx, c_ctx, w_ada, b_ada, norm_g, w_in, rw_mu, rw_w0, rw_wB, rw_a0, rw_aB, rw_gB, rw_kk, rw_ka, rw_rk, rw_gn, da_lambda, da_subln, hg_lb_logits, hg_norm, w_branch, w_out, moe_router, moe_w1, moe_w3, moe_w2):
    b, seq, d = x.shape
    nc = ctx.shape[1]
    depth = w_ada.shape[0]
    ct, st = _rope_tables(seq, nc)
    lb_w = jax.nn.softmax(hg_lb_logits.astype(F32), axis=0)
    hg_lb = jnp.cumsum(lb_w, axis=0) - lb_w[0]
    rows = ((b + 1 + SUBLANES - 1) // SUBLANES) * SUBLANES
    cc = jnp.zeros((rows, d), F32).at[:b].set(c).at[b].set(c_ctx)
    mod = _modulation(cc, w_ada, b_ada)
    xa = jnp.concatenate([ctx, x], axis=1)
    for l in range(depth):
        p = dict(norm_g=norm_g[l], w_in=w_in[l], rw_mu=rw_mu[l], rw_w0=rw_w0[l], rw_wB=rw_wB[l],
                 rw_a0=rw_a0[l], rw_aB=rw_aB[l], rw_gB=rw_gB[l], rw_kk=rw_kk[l], rw_ka=rw_ka[l],
                 rw_rk=rw_rk[l], rw_gn=rw_gn[l], da_lambda=da_lambda[l], da_subln=da_subln[l],
                 hg_norm=hg_norm[l], w_branch=w_branch[l], w_out=w_out[l], moe_router=moe_router[l],
                 moe_w1=moe_w1[l], moe_w3=moe_w3[l], moe_w2=moe_w2[l])
        mod_x = mod[l, :b].reshape(b, 6, d)
        mod_c = mod[l, b].reshape(6, d)
        lam_init = 0.8 - 0.6 * math.exp(-0.3 * l)
        xa = _layer(xa, nc, mod_x, mod_c, p, lam_init, hg_lb[l], ct, st, with_ctx=l < depth - 1)
    return xa[:, nc:]
```

```python
import functools
import math

import jax
import jax.numpy as jnp
from jax import lax
from jax.experimental import pallas as pl
from jax.experimental.pallas import tpu as pltpu

F32 = jnp.float32
BF16 = jnp.bfloat16
HIGHEST = lax.Precision.HIGHEST

GRID_W = 64
RW_HEAD_DIM = 64
RW_DECAY_RANK = 64
RW_ICL_RANK = 64
RW_GATE_RANK = 128
RW_GN_EPS = 64e-5
DA_QK_DIM = 64
DA_V_DIM = 128
ROPE_THETA = 10000.0
ROPE_AXIS_FREQS = 16
DA_SUBLN_EPS = 1e-5
HG_EXPAND = 128
N_EXPERTS = 16
EC_CAPACITY_FACTOR = 2
NORM_EPS = 1e-6

LANES = 128
SUBLANES = 8
V7X_VMEM_BYTES = 64 * 1024 * 1024
HG_CHUNK = 64
RW_CHUNK = 64
RW_GROUP = 256


def _cparams(sem, vmem_mb):
    return pltpu.CompilerParams(dimension_semantics=sem, vmem_limit_bytes=int(vmem_mb * 1024 * 1024))


def _sigmoid(x):
    return 1.0 / (1.0 + jnp.exp(-x))


def _silu(x):
    return x * _sigmoid(x)


def _mod_kernel(c_ref, w_ref, b_ref, o_ref):
    a = _silu(c_ref[...]).astype(BF16)
    o_ref[0] = jnp.dot(a, w_ref[0].astype(BF16), preferred_element_type=F32) + b_ref[0]


def _modulation(cc, w_ada, b_ada):
    depth, d, n = w_ada.shape
    rows = cc.shape[0]
    tn = 1024
    return pl.pallas_call(
        _mod_kernel,
        out_shape=jax.ShapeDtypeStruct((depth, rows, n), F32),
        grid=(depth, n // tn),
        in_specs=[pl.BlockSpec((rows, d), lambda l, j: (0, 0)),
                  pl.BlockSpec((1, d, tn), lambda l, j: (l, 0, j)),
                  pl.BlockSpec((1, 1, tn), lambda l, j: (l, 0, j))],
        out_specs=pl.BlockSpec((1, rows, tn), lambda l, j: (l, 0, j)),
        compiler_params=_cparams(("arbitrary", "arbitrary"), 40),
        name="adaln_mod",
    )(cc, w_ada, b_ada.reshape(depth, 1, n))


def _norm_mod_kernel(x_ref, g_ref, ms_ref, *rest, with_router):
    x = x_ref[0]
    xn = x * lax.rsqrt(jnp.mean(x * x, axis=-1, keepdims=True) + NORM_EPS) * g_ref[...]
    h = xn * (1.0 + ms_ref[0, 0, 1:2, :]) + ms_ref[0, 0, 0:1, :]
    if with_router:
        rt_ref, h_ref, aff_ref = rest
        h_ref[0] = h.astype(BF16)
        lt = lax.dot_general(rt_ref[...], h, (((1,), (1,)), ((), ())),
                             precision=HIGHEST, preferred_element_type=F32)
        e = jnp.exp(lt - jnp.max(lt, axis=0, keepdims=True))
        aff_ref[0] = e / jnp.sum(e, axis=0, keepdims=True)
    else:
        (h_ref,) = rest
        h_ref[0] = h.astype(BF16)


def _norm_mod(xa, g, msel, nc, router_t=None, tl=256):
    b, lt, d = xa.shape
    tl = min(tl, nc)
    nct = nc // tl
    with_router = router_t is not None
    in_specs = [pl.BlockSpec((1, tl, d), lambda bi, i: (bi, i, 0)),
                pl.BlockSpec((1, d), lambda bi, i: (0, 0)),
                pl.BlockSpec((1, 1, 2, d), lambda bi, i: (bi, jnp.where(i >= nct, 1, 0), 0, 0))]
    args = [xa, g.reshape(1, d), msel]
    out_shape = [jax.ShapeDtypeStruct((b, lt, d), BF16)]
    out_specs = [pl.BlockSpec((1, tl, d), lambda bi, i: (bi, i, 0))]
    if with_router:
        e = router_t.shape[0]
        in_specs.append(pl.BlockSpec((e, d), lambda bi, i: (0, 0)))
        args.append(router_t)
        out_shape.append(jax.ShapeDtypeStruct((b, e, lt), F32))
        out_specs.append(pl.BlockSpec((1, e, tl), lambda bi, i: (bi, 0, i)))
    res = pl.pallas_call(
        functools.partial(_norm_mod_kernel, with_router=with_router),
        out_shape=out_shape, grid=(b, lt // tl), in_specs=in_specs, out_specs=out_specs,
        compiler_params=_cparams(("arbitrary", "arbitrary"), 32),
        name="norm_mod_router" if with_router else "norm_mod",
    )(*args)
    return res if with_router else res[0]


def _matmul_kernel(a_ref, w_ref, o_ref):
    o_ref[...] = jnp.dot(a_ref[...], w_ref[...], preferred_element_type=F32).astype(o_ref.dtype)


def _pick(n, cands):
    for c in cands:
        if n % c == 0:
            return c
    return n


def _matmul(a, w, out_dtype, tm=1024, tn=1024):
    m, k = a.shape
    n = w.shape[1]
    tm = _pick(m, (tm, 768, 512, 384, 256, 128))
    tn = _pick(n, (tn, 768, 512, 384, 256, 128))
    return pl.pallas_call(
        _matmul_kernel,
        out_shape=jax.ShapeDtypeStruct((m, n), out_dtype),
        grid=(m // tm, n // tn),
        in_specs=[pl.BlockSpec((tm, k), lambda i, j: (i, 0)),
                  pl.BlockSpec((k, tn), lambda i, j: (0, j))],
        out_specs=pl.BlockSpec((tm, tn), lambda i, j: (i, j)),
        compiler_params=_cparams(("arbitrary", "arbitrary"), 48),
        name="matmul",
    )(a, w)


def _rw_prep_kernel(z_ref, zp_ref, zn_ref, mu_ref, w0_ref, wb_ref, a0_ref, ab_ref, gb_ref,
                    o_ref, g_ref, *, tl, nct, nt, w):
    i = pl.program_id(1)
    z = z_ref[0]
    row = lax.broadcasted_iota(jnp.int32, (tl, 1), 0)
    seg_start = jnp.logical_or(i == 0, i == nct)
    seg_end = jnp.logical_or(i == nct - 1, i == nt - 1)
    prev_row = jnp.where(seg_start, 0.0, zp_ref[0, SUBLANES - 1:SUBLANES, :])
    next_row = jnp.where(seg_end, 0.0, zn_ref[0, 0:1, :])
    prev = jnp.where(row == 0, prev_row, pltpu.roll(z, 1, 0))
    nxt = jnp.where(row == tl - 1, next_row, pltpu.roll(z, tl - 1, 0))
    zs = z + mu_ref[0:1, :] * (prev - z) + mu_ref[1:2, :] * (nxt - z)
    o_ref[0, 0] = zs[:, 0:w]
    o_ref[0, 1] = zs[:, w:2 * w]
    o_ref[0, 2] = zs[:, 2 * w:3 * w]
    o4 = 3 * w + 2 * RW_DECAY_RANK
    o5 = o4 + 2 * RW_ICL_RANK
    wd = jnp.tanh(zs[:, 3 * w:o4])
    wl = jnp.dot(wd.astype(BF16), wb_ref[...], preferred_element_type=F32) + w0_ref[...]
    dec = -math.exp(-0.5) * _sigmoid(wl)
    o_ref[0, 3] = dec[:, 0:w]
    o_ref[0, 4] = dec[:, w:2 * w]
    al = jnp.dot(zs[:, o4:o5].astype(BF16), ab_ref[...], preferred_element_type=F32) + a0_ref[...]
    av = _sigmoid(al)
    o_ref[0, 5] = av[:, 0:w]
    o_ref[0, 6] = av[:, w:2 * w]
    g_ref[0] = jnp.dot(_sigmoid(zs[:, o5:]).astype(BF16), gb_ref[...], preferred_element_type=F32)


def _blockdiag2(m):
    r, w = m.shape[1], m.shape[2]
    z = jnp.zeros((r, w), m.dtype)
    return jnp.concatenate([jnp.concatenate([m[0], z], axis=1), jnp.concatenate([z, m[1]], axis=1)], axis=0)


def _rw_prep(za, nc, mu, w0, wb, a0, ab, gb, tl=256):
    b, lt, acols = za.shape
    w = w0.shape[1]
    tl = min(tl, nc)
    nct, nt = nc // tl, lt // tl
    r8 = tl // SUBLANES
    nb8 = lt // SUBLANES
    full = lambda shp: pl.BlockSpec(shp, lambda bi, i: tuple(0 for _ in shp))
    kern = functools.partial(_rw_prep_kernel, tl=tl, nct=nct, nt=nt, w=w)
    return pl.pallas_call(
        kern,
        out_shape=[jax.ShapeDtypeStruct((b, 7, lt, w), F32), jax.ShapeDtypeStruct((b, lt, w), F32)],
        grid=(b, nt),
        in_specs=[pl.BlockSpec((1, tl, acols), lambda bi, i: (bi, i, 0)),
                  pl.BlockSpec((1, SUBLANES, acols), lambda bi, i: (bi, jnp.maximum(i * r8 - 1, 0), 0)),
                  pl.BlockSpec((1, SUBLANES, acols), lambda bi, i: (bi, jnp.minimum((i + 1) * r8, nb8 - 1), 0)),
                  full((2, acols)), full((1, 2 * w)), full((2 * RW_DECAY_RANK, 2 * w)),
                  full((1, 2 * w)), full((2 * RW_ICL_RANK, 2 * w)), full((RW_GATE_RANK, w))],
        out_specs=[pl.BlockSpec((1, 7, tl, w), lambda bi, i: (bi, 0, i, 0)),
                   pl.BlockSpec((1, tl, w), lambda bi, i: (bi, i, 0))],
        compiler_params=_cparams(("arbitrary", "arbitrary"), 48),
        name="rwkv_prep",
    )(za, za, za, mu, w0.reshape(1, 2 * w), _blockdiag2(wb).astype(BF16), a0.reshape(1, 2 * w),
      _blockdiag2(ab).astype(BF16), gb.astype(BF16))


def _rw_chunk_kernel(xf_ref, lwf_ref, af_ref, xb_ref, lwb_ref, ab_ref, kk_ref, ka_ref, rk_ref, gn0_ref, gn1_ref,
                     of_ref, ob_ref, s_ref, *, c, w):
    j = pl.program_id(1)

    @pl.when(j == 0)
    def _():
        s_ref[...] = jnp.zeros_like(s_ref)

    gw, n = RW_GROUP, RW_HEAD_DIM
    ng = w // gw
    row = lax.broadcasted_iota(jnp.int32, (c, 1), 0)
    pos = lax.broadcasted_iota(jnp.int32, (1, gw), 1) % n
    bdmask = (lax.broadcasted_iota(jnp.int32, (gw, 1), 0) // n) == (lax.broadcasted_iota(jnp.int32, (1, gw), 1) // n)
    ones_bd = jnp.where(bdmask, 1.0, 0.0).astype(BF16)
    eye = jnp.where(pos == row, 1.0, 0.0)
    nt = (((1,), (1,)), ((), ()))
    tn = (((0,), (0,)), ((), ()))
    chains = [(d, g) for d in range(2) for g in range(ng)]
    refs = ((xf_ref, lwf_ref, af_ref, of_ref), (xb_ref, lwb_ref, ab_ref, ob_ref))

    def bd(x):
        return jnp.where(bdmask, jnp.concatenate([x] * (gw // c), axis=0), 0.0).astype(BF16)

    def gsum(xs):
        pieces = []
        for x in xs:
            hi = x.astype(BF16)
            r1 = x - hi.astype(F32)
            mid = r1.astype(BF16)
            pieces += [hi, mid, (r1 - mid.astype(F32)).astype(BF16)]
        tot = jnp.dot(jnp.concatenate(pieces, axis=0), ones_bd, preferred_element_type=F32)
        return [tot[3 * i * c:(3 * i + 1) * c] + tot[(3 * i + 1) * c:(3 * i + 2) * c] + tot[(3 * i + 2) * c:(3 * i + 3) * c]
                for i in range(len(xs))]

    def mm(x, ybd):
        return jnp.dot(x.astype(BF16), ybd, preferred_element_type=F32)

    st = []
    for d, g in chains:
        x_ref, lw_ref, a_ref, _ = refs[d]
        sl = slice(g * gw, (g + 1) * gw)
        q = dict(sl=sl, rev=d == 1, r=x_ref[0, 0, :, sl], k=x_ref[0, 1, :, sl], v=x_ref[0, 2, :, sl],
                 lw=lw_ref[0, 0, :, sl], a=a_ref[0, 0, :, sl])
        q['kx'] = q['k'] * kk_ref[:, sl]
        q['kt'] = q['k'] * (1.0 + (q['a'] - 1.0) * ka_ref[:, sl])
        st.append(q)
    for q in st:
        q['ss'], q['bon'] = gsum([q['kx'] * q['kx'], q['r'] * q['kt'] * rk_ref[:, q['sl']]])
    for q in st:
        rev = q['rev']
        kk = q['kx'] / jnp.maximum(jnp.sqrt(q['ss']), 1e-12)
        q['b'] = q['a'] * kk
        cw = q['lw']
        sft = 1
        while sft < c:
            if rev:
                cw = cw + jnp.where(row < c - sft, pltpu.roll(cw, c - sft, 0), 0.0)
            else:
                cw = cw + jnp.where(row >= sft, pltpu.roll(cw, sft, 0), 0.0)
            sft *= 2
        q['tot'] = cw[0:1] if rev else cw[c - 1:c]
        invp = jnp.exp(-cw)
        q['e2'] = jnp.exp(q['tot'] - cw)
        q['kr'] = jnp.concatenate([kk * jnp.exp(cw - q['lw']), q['r'] * jnp.exp(cw)], axis=0).astype(BF16)
        q['ktb'] = bd(q['kt'] * invp)
        q['bb'] = bd(q['b'] * invp)
        q['vbd'] = bd(q['v'])
    for q in st:
        strict = (pos > row) if q['rev'] else (pos < row)
        incl = (pos >= row) if q['rev'] else (pos <= row)
        g1 = lax.dot_general(q['kr'], q['ktb'], nt, preferred_element_type=F32)
        g2 = lax.dot_general(q['kr'], q['bb'], nt, preferred_element_type=F32)
        q['m_kt'] = jnp.where(strict, g1[:c], 0.0)
        q['n_kt'] = jnp.where(incl, g1[c:], 0.0)
        q['x'] = jnp.where(strict, g2[:c], 0.0)
        q['n_b'] = jnp.where(incl, g2[c:], 0.0)
        q['t'] = eye - q['x']
        q['xb'] = bd(q['x'])
    for q in st:
        q['mkv'] = mm(q['m_kt'], q['vbd'])
        q['nkv'] = mm(q['n_kt'], q['vbd'])
    lv = 2
    while lv < c:
        for q in st:
            q['x'] = mm(q['x'], q['xb'])
        for q in st:
            q['xb'] = bd(q['x'])
        for q in st:
            q['t'] = q['t'] + mm(q['t'], q['xb'])
        lv *= 2
    for (d, g), q in zip(chains, st):
        q['s0'] = s_ref[d, g]
        q['ka'] = lax.dot_general(q['kr'], q['s0'].astype(BF16), nt, preferred_element_type=F32)
    for q in st:
        q['u'] = mm(q['t'], bd(q['ka'][:c] + q['mkv']))
    for (d, g), q in zip(chains, st):
        e2 = q['e2']
        upd = lax.dot_general(jnp.concatenate([q['v'], q['u']], axis=0).astype(BF16),
                              jnp.concatenate([q['kt'] * e2, -(q['b'] * e2)], axis=0).astype(BF16),
                              tn, preferred_element_type=F32)
        s_ref[d, g] = q['s0'] * jnp.exp(q['tot']) + jnp.where(bdmask, upd, 0.0)
    for q in st:
        q['y'] = q['ka'][c:] + q['nkv'] - mm(q['n_b'], bd(q['u']))
    for q in st:
        (mu,) = gsum([q['y']])
        q['dl'] = q['y'] - mu * (1.0 / n)
    for (d, g), q in zip(chains, st):
        (var,) = gsum([q['dl'] * q['dl']])
        sl = q['sl']
        yn = q['dl'] * lax.rsqrt(var * (1.0 / n) + RW_GN_EPS) * gn0_ref[:, sl] + gn1_ref[:, sl]
        refs[d][3][0, :, sl] = yn + q['bon'] * q['v']


def _rw_chunk(streams, nc, kk, ka, rk, gn):
    b, _, lt, w = streams.shape
    c = RW_CHUNK
    ncb, nb = nc // c, lt // c

    def rblk(j):
        return jnp.where(j < ncb, ncb - 1 - j, nb - 1 - (j - ncb))

    par = pl.BlockSpec((1, w), lambda bi, j: (0, 0))
    shp = jax.ShapeDtypeStruct((b, lt, w), F32)
    return pl.pallas_call(
        functools.partial(_rw_chunk_kernel, c=c, w=w),
        out_shape=[shp, shp],
        grid=(b, nb),
        in_specs=[pl.BlockSpec((1, 3, c, w), lambda bi, j: (bi, 0, j, 0)),
                  pl.BlockSpec((1, 1, c, w), lambda bi, j: (bi, 3, j, 0)),
                  pl.BlockSpec((1, 1, c, w), lambda bi, j: (bi, 5, j, 0)),
                  pl.BlockSpec((1, 3, c, w), lambda bi, j: (bi, 0, rblk(j), 0)),
                  pl.BlockSpec((1, 1, c, w), lambda bi, j: (bi, 4, rblk(j), 0)),
                  pl.BlockSpec((1, 1, c, w), lambda bi, j: (bi, 6, rblk(j), 0)),
                  par, par, par, par, par],
        out_specs=[pl.BlockSpec((1, c, w), lambda bi, j: (bi, j, 0)),
                   pl.BlockSpec((1, c, w), lambda bi, j: (bi, rblk(j), 0))],
        scratch_shapes=[pltpu.VMEM((2, w // RW_GROUP, RW_GROUP, RW_GROUP), F32)],
        compiler_params=_cparams(("arbitrary", "arbitrary"), 32),
        name="rwkv_chunk",
    )(streams, streams, streams, streams, streams, streams, kk.reshape(1, w), ka.reshape(1, w), rk.reshape(1, w),
      gn[0].reshape(1, w), gn[1].reshape(1, w))


def _rope_kernel(z_ref, c_ref, s_ref, o_ref, *, w):
    lane = lax.broadcasted_iota(jnp.int32, (1, LANES), 1)
    first_half = (lane % (2 * ROPE_AXIS_FREQS)) < ROPE_AXIS_FREQS
    c = c_ref[...]
    s = s_ref[...]
    for j in range(2 * w // LANES):
        xs = z_ref[0, :, j * LANES:(j + 1) * LANES].astype(F32)
        up = pltpu.roll(xs, LANES - ROPE_AXIS_FREQS, 1)
        dn = pltpu.roll(xs, ROPE_AXIS_FREQS, 1)
        o = xs * c + jnp.where(first_half, up, dn) * s
        if j < w // LANES:
            o = o * (DA_QK_DIM ** -0.5 * math.log2(math.e))
        o_ref[0, :, j * LANES:(j + 1) * LANES] = o.astype(BF16)
    o_ref[0, :, 2 * w:] = z_ref[0, :, 2 * w:]


def _rope_tables(seq, nc):
    rows = seq // GRID_W
    row = jnp.broadcast_to(jnp.arange(rows)[:, None], (rows, GRID_W)).reshape(seq)
    col = jnp.broadcast_to(jnp.arange(GRID_W)[None, :], (rows, GRID_W)).reshape(seq)
    inv = 1.0 / (ROPE_THETA ** (jnp.arange(ROPE_AXIS_FREQS, dtype=F32) / ROPE_AXIS_FREQS))
    ang = jnp.stack([row, col], axis=-1).astype(F32)[:, :, None] * inv
    cos, sin = jnp.cos(ang), jnp.sin(ang)
    c64 = jnp.concatenate([cos[:, 0], cos[:, 0], cos[:, 1], cos[:, 1]], axis=-1)
    s64 = jnp.concatenate([-sin[:, 0], sin[:, 0], -sin[:, 1], sin[:, 1]], axis=-1)
    ct = jnp.concatenate([jnp.ones((nc, LANES), F32), jnp.tile(c64, (1, 2))], axis=0)
    st = jnp.concatenate([jnp.zeros((nc, LANES), F32), jnp.tile(s64, (1, 2))], axis=0)
    return ct, st


def _rope(zb, ct, st, tl=256):
    b, lt, cols = zb.shape
    w = cols // 3
    tl = _pick(lt, (tl, 128, 64))
    return pl.pallas_call(
        functools.partial(_rope_kernel, w=w),
        out_shape=jax.ShapeDtypeStruct((b, lt, cols), BF16),
        grid=(b, lt // tl),
        in_specs=[pl.BlockSpec((1, tl, cols), lambda bi, i: (bi, i, 0)),
                  pl.BlockSpec((tl, LANES), lambda bi, i: (i, 0)),
                  pl.BlockSpec((tl, LANES), lambda bi, i: (i, 0))],
        out_specs=pl.BlockSpec((1, tl, cols), lambda bi, i: (bi, i, 0)),
        compiler_params=_cparams(("arbitrary", "arbitrary"), 32),
        name="rope",
    )(zb, ct, st)


def _attn_kernel(q_ref, k_ref, v_ref, lam_ref, g_ref, o_ref, *, nct, nc, lam_init, hp):
    i = pl.program_id(2)
    lane = lax.broadcasted_iota(jnp.int32, (1, LANES), 1)
    lp = lam_ref[...]
    lam = (jnp.exp(jnp.sum(lp[0:1] * lp[1:2], axis=1, keepdims=True))
           - jnp.exp(jnp.sum(lp[2:3] * lp[3:4], axis=1, keepdims=True)) + lam_init)

    def attend(nk):
        for h in range(hp):
            cols = slice(h * DA_V_DIM, (h + 1) * DA_V_DIM)
            q = q_ref[0, :, cols]
            k = k_ref[0, 0:nk, cols]
            v1 = jnp.concatenate([v_ref[0, 0:nk, cols], jnp.ones((nk, LANES), BF16)], axis=1)

            def one(sel):
                qs = jnp.where(sel, q, jnp.zeros_like(q))
                s = lax.dot_general(qs, k, (((1,), (1,)), ((), ())), preferred_element_type=F32)
                e = jnp.exp2(s - jnp.max(s, axis=-1, keepdims=True))
                ov = jnp.dot(e.astype(BF16), v1, preferred_element_type=F32)
                return ov[:, :DA_V_DIM] / ov[:, DA_V_DIM:]

            o = one(lane < DA_QK_DIM) - lam * one(lane >= DA_QK_DIM)
            on = o * lax.rsqrt(jnp.mean(o * o, axis=-1, keepdims=True) + DA_SUBLN_EPS) * g_ref[...]
            o_ref[0, :, cols] = (on * (1.0 - lam_init)).astype(o_ref.dtype)

    @pl.when(i < nct)
    def _():
        attend(nc)

    @pl.when(i >= nct)
    def _():
        attend(k_ref.shape[1])


def _attention(zbr, nc, lam_p, subln, lam_init, hp=2):
    b, lt, cols = zbr.shape
    w = cols // 3
    heads = w // DA_V_DIM
    tq = min(256, nc)
    nct = nc // tq
    hg = heads // hp
    bw = hp * DA_V_DIM
    kern = functools.partial(_attn_kernel, nct=nct, nc=nc, lam_init=lam_init, hp=hp)
    return pl.pallas_call(
        kern,
        out_shape=jax.ShapeDtypeStruct((b, lt, w), BF16),
        grid=(b, hg, lt // tq),
        in_specs=[pl.BlockSpec((1, tq, bw), lambda bi, h, i: (bi, i, h)),
                  pl.BlockSpec((1, lt, bw), lambda bi, h, i: (bi, 0, hg + h)),
                  pl.BlockSpec((1, lt, bw), lambda bi, h, i: (bi, 0, 2 * hg + h)),
                  pl.BlockSpec((4, DA_QK_DIM), lambda bi, h, i: (0, 0)),
                  pl.BlockSpec((1, DA_V_DIM), lambda bi, h, i: (0, 0))],
        out_specs=pl.BlockSpec((1, tq, bw), lambda bi, h, i: (bi, i, h)),
        compiler_params=_cparams(("arbitrary", "arbitrary", "arbitrary"), 48),
        name="diff_attn",
    )(zbr, zbr, zbr, lam_p, subln.reshape(1, DA_V_DIM))


def _seg_scans(lf, c):
    row = lax.broadcasted_iota(jnp.int32, (c, 1), 0)
    cs, ss = {1: lf}, {1: lf}
    h = 1
    while h < c:
        x, y = cs[h], ss[h]
        if h < SUBLANES:
            addx = jnp.zeros_like(x)
            addy = jnp.zeros_like(y)
            odd = (row % (2 * h)) >= h
            for s in range(1, h + 1):
                addx = addx + jnp.where(jnp.logical_and(odd, (row % h) == s - 1), pltpu.roll(x, s, 0), 0.0)
                addy = addy + jnp.where(jnp.logical_and(~odd, (row % h) == h - s), pltpu.roll(y, c - s, 0), 0.0)
            cs[2 * h], ss[2 * h] = x + addx, y + addy
        else:
            px = [x[j * h:(j + 1) * h] for j in range(c // h)]
            py = [y[j * h:(j + 1) * h] for j in range(c // h)]
            nx = [px[j] + px[j - 1][h - 1:h] if j % 2 == 1 else px[j] for j in range(c // h)]
            ny = [py[j] + py[j + 1][0:1] if j % 2 == 0 else py[j] for j in range(c // h)]
            cs[2 * h], ss[2 * h] = jnp.concatenate(nx, axis=0), jnp.concatenate(ny, axis=0)
        h *= 2
    return cs, ss


def _hg_kernel(q_ref, f_ref, v_ref, lb_ref, o_ref, st_ref, *, tc, c, heads, rev):
    j = pl.program_id(1)

    @pl.when(j == 0)
    def _():
        st_ref[...] = jnp.zeros_like(st_ref)

    ti = lax.broadcasted_iota(jnp.int32, (c, c), 0)
    si = lax.broadcasted_iota(jnp.int32, (c, c), 1)
    nt = (((1,), (1,)), ((), ()))
    tn = (((0,), (0,)), ((), ()))

    def head(h, carry):
        cols = pl.ds(pl.multiple_of(h * HG_EXPAND, HG_EXPAND), HG_EXPAND)
        lb = lb_ref[:, cols]
        chunks = range(tc // c)
        for ci in (reversed(chunks) if rev else chunks):
            rows = slice(ci * c, (ci + 1) * c)
            q = _silu(q_ref[0, rows, cols])
            f = lb + (1.0 - lb) * _sigmoid(f_ref[0, rows, cols])
            k = 1.0 - f
            lf = jnp.log(f)
            v = v_ref[0, rows, cols].astype(BF16)
            cs, ss = _seg_scans(lf, c)
            qs, ks = (ss, cs) if rev else (cs, ss)
            att = jnp.where(ti == si, lax.dot_general(q.astype(BF16), k.astype(BF16), nt,
                                                      preferred_element_type=F32), 0.0)
            hh = 1
            while hh < c:
                qe = (q * jnp.exp(qs[hh])).astype(BF16)
                ke = (k * jnp.exp(ks[hh] - lf)).astype(BF16)
                a = lax.dot_general(qe, ke, nt, preferred_element_type=F32)
                tb, sb = ti // hh, si // hh
                if rev:
                    m = jnp.logical_and(tb % 2 == 0, sb == tb + 1)
                else:
                    m = jnp.logical_and(tb % 2 == 1, sb == tb - 1)
                att = att + jnp.where(m, a, 0.0)
                hh *= 2
            st = st_ref[h]
            qe = (q * jnp.exp(qs[c])).astype(BF16)
            o = lax.dot_general(qe, st.astype(BF16), nt, preferred_element_type=F32)
            o = o + jnp.dot(att.astype(BF16), v, preferred_element_type=F32)
            o_ref[0, rows, cols] = o
            ke = (k * jnp.exp(ks[c] - lf)).astype(BF16)
            tot = qs[c][0:1] if rev else qs[c][c - 1:c]
            st_ref[h] = st * jnp.exp(tot) + lax.dot_general(v, ke, tn, preferred_element_type=F32)
        return carry

    lax.fori_loop(0, heads, head, 0)


def _hgrn(zc, nc, lb, rev):
    b, lt, cols = zc.shape
    w = cols // 5
    heads = w // HG_EXPAND
    tc = min(256, nc)
    c = min(HG_CHUNK, tc)
    ncb, nb = nc // tc, lt // tc

    def blk(j):
        return jnp.where(j < ncb, ncb - 1 - j, nb - 1 - (j - ncb)) if rev else j

    fcol = 2 if rev else 1
    return pl.pallas_call(
        functools.partial(_hg_kernel, tc=tc, c=c, heads=heads, rev=rev),
        out_shape=jax.ShapeDtypeStruct((b, lt, w), F32),
        grid=(b, nb),
        in_specs=[pl.BlockSpec((1, tc, w), lambda bi, j: (bi, blk(j), 0)),
                  pl.BlockSpec((1, tc, w), lambda bi, j: (bi, blk(j), fcol)),
                  pl.BlockSpec((1, tc, w), lambda bi, j: (bi, blk(j), 3)),
                  pl.BlockSpec((1, w), lambda bi, j: (0, 0))],
        out_specs=pl.BlockSpec((1, tc, w), lambda bi, j: (bi, blk(j), 0)),
        scratch_shapes=[pltpu.VMEM((heads, HG_EXPAND, HG_EXPAND), F32)],
        compiler_params=_cparams(("arbitrary", "arbitrary"), 32),
        name="hgrn2_bwd" if rev else "hgrn2_fwd",
    )(zc, zc, zc, lb.reshape(1, w))


def _merge_kernel(oaf_ref, oab_ref, ga_ref, yb_ref, of_ref, ob_ref, gc_ref, hn_ref, gl0_ref, gl1_ref, gl2_ref, wb_ref,
                  o_ref, *, heads):
    gls = (gl0_ref, gl1_ref, gl2_ref)
    ya = (oaf_ref[0] + oab_ref[0]) * ga_ref[0]
    oc = of_ref[0] + ob_ref[0]
    parts = []
    for h in range(heads):
        x = oc[:, h * HG_EXPAND:(h + 1) * HG_EXPAND]
        parts.append(x * lax.rsqrt(jnp.mean(x * x, axis=-1, keepdims=True) + NORM_EPS))
    yc = jnp.concatenate(parts, axis=-1) * hn_ref[...] * _silu(gc_ref[0])
    acc = None
    for bi, y in enumerate((ya.astype(BF16), yb_ref[0], yc.astype(BF16))):
        term = _sigmoid(gls[bi][0].astype(F32)) * jnp.dot(y, wb_ref[bi], preferred_element_type=F32)
        acc = term if acc is None else acc + term
    o_ref[0] = acc.astype(BF16)


def _merge(oaf, oab, ga, yb, of, ob, zc, hn, gl, wbr, tl=128, tn=2048):
    b, lt, w = ga.shape
    d = wbr.shape[2]
    tl = _pick(lt, (tl, 128, 64))
    heads = w // HG_EXPAND
    nn = d // tn
    row = lambda n, bi, i: (bi, i, 0)
    glspec = lambda br: pl.BlockSpec((1, tl, tn), lambda n, bi, i: (bi, i, br * nn + n))
    return pl.pallas_call(
        functools.partial(_merge_kernel, heads=heads),
        out_shape=jax.ShapeDtypeStruct((b, lt, d), BF16),
        grid=(d // tn, b, lt // tl),
        in_specs=[pl.BlockSpec((1, tl, w), row),
                  pl.BlockSpec((1, tl, w), row),
                  pl.BlockSpec((1, tl, w), row),
                  pl.BlockSpec((1, tl, w), row),
                  pl.BlockSpec((1, tl, w), row),
                  pl.BlockSpec((1, tl, w), row),
                  pl.BlockSpec((1, tl, w), lambda n, bi, i: (bi, i, 4)),
                  pl.BlockSpec((1, w), lambda n, bi, i: (0, 0)),
                  glspec(0), glspec(1), glspec(2),
                  pl.BlockSpec((3, w, tn), lambda n, bi, i: (0, 0, n))],
        out_specs=pl.BlockSpec((1, tl, tn), lambda n, bi, i: (bi, i, n)),
        compiler_params=_cparams(("arbitrary", "arbitrary", "arbitrary"), 48),
        name="branch_merge",
    )(oaf, oab, ga, yb, of, ob, zc, hn, gl, gl, gl, wbr)


def _proj_post_kernel(a_ref, w_ref, x_ref, g_ref, gate_ref, o_ref):
    mx = jnp.dot(a_ref[0], w_ref[...], preferred_element_type=F32)
    mn = mx * lax.rsqrt(jnp.mean(mx * mx, axis=-1, keepdims=True) + NORM_EPS) * g_ref[...]
    o_ref[0] = x_ref[0] + gate_ref[0, 0] * mn


def _proj_post(a, wout, xa, g, gsel, nc, tl=256):
    b, lt, d = xa.shape
    tl = min(tl, nc)
    nct = nc // tl
    return pl.pallas_call(
        _proj_post_kernel,
        out_shape=jax.ShapeDtypeStruct((b, lt, d), F32),
        grid=(b, lt // tl),
        in_specs=[pl.BlockSpec((1, tl, d), lambda bi, i: (bi, i, 0)),
                  pl.BlockSpec((d, d), lambda bi, i: (0, 0)),
                  pl.BlockSpec((1, tl, d), lambda bi, i: (bi, i, 0)),
                  pl.BlockSpec((1, d), lambda bi, i: (0, 0)),
                  pl.BlockSpec((1, 1, 1, d), lambda bi, i: (bi, jnp.where(i >= nct, 1, 0), 0, 0))],
        out_specs=pl.BlockSpec((1, tl, d), lambda bi, i: (bi, i, 0)),
        input_output_aliases={2: 0},
        compiler_params=_cparams(("arbitrary", "arbitrary"), 48),
        name="out_proj_post",
    )(a, wout, xa, g.reshape(1, d), gsel)


def _rank_kernel(afft_ref, slot_ref, *, n, cap, tw):
    aff = afft_ref[0]
    bits = lax.bitcast_convert_type(aff, jnp.int32)
    e_num = aff.shape[0]

    def count(mask):
        return jnp.sum(jnp.where(mask, 1.0, 0.0), axis=1, keepdims=True)

    def bit_step(i, thr):
        cand = thr | lax.shift_left(jnp.int32(1), 30 - i)
        return jnp.where(count(bits >= cand) >= cap, cand, thr)

    thr = lax.fori_loop(0, 31, bit_step, jnp.zeros((e_num, 1), jnp.int32))
    gt = bits > thr
    eq = bits == thr
    need = cap - count(gt)
    tri = jnp.where(lax.broadcasted_iota(jnp.int32, (tw, tw), 0) < lax.broadcasted_iota(jnp.int32, (tw, tw), 1),
                    1.0, 0.0).astype(BF16)

    def prefix(mask):
        parts, carry = [], jnp.zeros((e_num, 1), F32)
        for j in range(n // tw):
            m = jnp.where(mask[:, j * tw:(j + 1) * tw], 1.0, 0.0)
            parts.append(jnp.dot(m.astype(BF16), tri, preferred_element_type=F32) + carry)
            carry = carry + jnp.sum(m, axis=1, keepdims=True)
        return jnp.concatenate(parts, axis=1)

    sel = jnp.logical_or(gt, jnp.logical_and(eq, prefix(eq) < need))
    slot_ref[0] = jnp.where(sel, prefix(sel), float(cap)).astype(jnp.int32)


def _rank(afft, cap):
    b, e_num, n = afft.shape
    tw = min(LANES, n)
    return pl.pallas_call(
        functools.partial(_rank_kernel, n=n, cap=cap, tw=tw),
        out_shape=jax.ShapeDtypeStruct((b, e_num, n), jnp.int32),
        grid=(b,),
        in_specs=[pl.BlockSpec((1, e_num, n), lambda bi: (bi, 0, 0))],
        out_specs=pl.BlockSpec((1, e_num, n), lambda bi: (bi, 0, 0)),
        compiler_params=_cparams(("arbitrary",), 32),
        name="ec_rank",
    )(afft)


def _gather_kernel(slot_ref, afft_ref, h_ref, o_ref, g_ref, *, cap):
    e = pl.program_id(1)
    sl = slot_ref[0, pl.ds(e, 1), :]
    hit = lax.broadcasted_iota(jnp.int32, (cap, 1), 0) == sl
    o_ref[0] = jnp.dot(jnp.where(hit, 1.0, 0.0).astype(BF16), h_ref[0], preferred_element_type=F32).astype(BF16)
    gate = jnp.sum(jnp.where(hit, afft_ref[0, pl.ds(e, 1), :], 0.0), axis=1, keepdims=True)
    g_ref[0] = jnp.broadcast_to(gate, (cap, LANES))


def _gather(slot, afft, hs, cap):
    b, e_num, n = slot.shape
    d = hs.shape[2]
    return pl.pallas_call(
        functools.partial(_gather_kernel, cap=cap),
        out_shape=[jax.ShapeDtypeStruct((e_num, b * cap, d), BF16),
                   jax.ShapeDtypeStruct((e_num, b * cap, LANES), F32)],
        grid=(b, e_num),
        in_specs=[pl.BlockSpec((1, e_num, n), lambda bi, e: (bi, 0, 0)),
                  pl.BlockSpec((1, e_num, n), lambda bi, e: (bi, 0, 0)),
                  pl.BlockSpec((1, n, d), lambda bi, e: (bi, 0, 0))],
        out_specs=[pl.BlockSpec((1, cap, d), lambda bi, e: (e, bi, 0)),
                   pl.BlockSpec((1, cap, LANES), lambda bi, e: (e, bi, 0))],
        compiler_params=_cparams(("arbitrary", "arbitrary"), 40),
        name="ec_gather",
    )(slot, afft, hs)


def _ffn_kernel(x_ref, g_ref, w1_ref, w3_ref, w2_ref, o_ref, acc_ref):
    f = pl.program_id(2)
    x = x_ref[0]
    h1 = jnp.dot(x, w1_ref[0, 0].astype(BF16), preferred_element_type=F32)
    h3 = jnp.dot(x, w3_ref[0, 0].astype(BF16), preferred_element_type=F32)
    hid = (_silu(h1) * h3).astype(BF16)
    part = jnp.dot(hid, w2_ref[0, 0].astype(BF16), preferred_element_type=F32)

    @pl.when(f == 0)
    def _():
        acc_ref[...] = part

    @pl.when(f != 0)
    def _():
        acc_ref[...] += part

    @pl.when(f == pl.num_programs(2) - 1)
    def _():
        o_ref[0] = (acc_ref[...] * g_ref[0, :, 0:1]).astype(BF16)


def _ffn(xe, gate, w1, w3, w2, layer, tf=256):
    e_num, m, d = xe.shape
    ff = w1.shape[3]
    tm = _pick(m, (1152, 1024, 768, 512, 384, 256, 128, 64, 32, 16, 8))
    return pl.pallas_call(
        _ffn_kernel,
        out_shape=jax.ShapeDtypeStruct((e_num, m, d), BF16),
        grid=(e_num, m // tm, ff // tf),
        in_specs=[pl.BlockSpec((1, tm, d), lambda e, i, f: (e, i, 0)),
                  pl.BlockSpec((1, tm, LANES), lambda e, i, f: (e, i, 0)),
                  pl.BlockSpec((1, 1, d, tf), lambda e, i, f: (layer, e, 0, f)),
                  pl.BlockSpec((1, 1, d, tf), lambda e, i, f: (layer, e, 0, f)),
                  pl.BlockSpec((1, 1, tf, d), lambda e, i, f: (layer, e, f, 0))],
        out_specs=pl.BlockSpec((1, tm, d), lambda e, i, f: (e, i, 0)),
        scratch_shapes=[pltpu.VMEM((tm, d), F32)],
        compiler_params=_cparams(("arbitrary", "arbitrary", "arbitrary"), 52),
        name="ec_ffn",
    )(xe, gate, w1, w3, w2)


def _combine_kernel(slot_ref, ye_ref, x_ref, g_ref, gate_ref, o_ref, *, cap, e_num):
    lane = lax.broadcasted_iota(jnp.int32, (1, e_num), 1)
    pos = lax.broadcasted_iota(jnp.int32, (1, cap), 1)
    sl = slot_ref[0]
    y = None
    for e in range(e_num):
        se = jnp.sum(jnp.where(lane == e, sl, 0), axis=1, keepdims=True)
        pt = jnp.where(se == pos, 1.0, 0.0).astype(BF16)
        part = jnp.dot(pt, ye_ref[e], preferred_element_type=F32)
        y = part if y is None else y + part
    yn = y * lax.rsqrt(jnp.mean(y * y, axis=-1, keepdims=True) + NORM_EPS) * g_ref[...]
    o_ref[0] = x_ref[0] + gate_ref[0, 0] * yn


def _combine(slot_c, ye, xa, g, gsel, seg, row0, slot0, cap, tt=256):
    b, n, e_num = slot_c.shape
    d = xa.shape[2]
    tt = _pick(math.gcd(n, row0), (tt, 128, 64))
    assert row0 % tt == 0 and slot0 % cap == 0
    r0, s0 = row0 // tt, slot0 // cap
    return pl.pallas_call(
        functools.partial(_combine_kernel, cap=cap, e_num=e_num),
        out_shape=jax.ShapeDtypeStruct(xa.shape, F32),
        grid=(b, n // tt),
        in_specs=[pl.BlockSpec((1, tt, e_num), lambda bi, j: (bi, j, 0)),
                  pl.BlockSpec((e_num, cap, d), lambda bi, j: (0, s0 + bi, 0)),
                  pl.BlockSpec((1, tt, d), lambda bi, j: (bi, r0 + j, 0)),
                  pl.BlockSpec((1, d), lambda bi, j: (0, 0)),
                  pl.BlockSpec((1, 1, 1, d), lambda bi, j: (bi, seg, 0, 0))],
        out_specs=pl.BlockSpec((1, tt, d), lambda bi, j: (bi, r0 + j, 0)),
        input_output_aliases={2: 0},
        compiler_params=_cparams(("arbitrary", "arbitrary"), 56),
        name="ec_combine_post",
    )(slot_c, ye, xa, g.reshape(1, d), gsel)


def _sel(mod_c, mod_x, idx):
    b = mod_x.shape[0]
    mc = jnp.broadcast_to(mod_c[jnp.array(idx)][None], (b, len(idx), mod_c.shape[-1]))
    return jnp.stack([mc, mod_x[:, jnp.array(idx)]], axis=1)


def _layer(xa, nc, mod_x, mod_c, p, lam_init, hg_lb, ct, st, with_ctx):
    b, lt, d = xa.shape
    w = d // 2
    a_cols = 3 * w + 2 * RW_DECAY_RANK + 2 * RW_ICL_RANK + RW_GATE_RANK
    col_b = a_cols
    col_c = col_b + 3 * w
    col_g = col_c + 5 * w

    h = _norm_mod(xa, p['norm_g'][0], _sel(mod_c, mod_x, (0, 1)), nc)
    h2 = h.reshape(b * lt, d)
    w_in = p['w_in']
    za = _matmul(h2, w_in[:, :col_b].astype(BF16), F32, tn=1152).reshape(b, lt, a_cols)
    zb = _matmul(h2, w_in[:, col_b:col_c].astype(BF16), BF16).reshape(b, lt, 3 * w)
    zc = _matmul(h2, w_in[:, col_c:col_g].astype(BF16), F32).reshape(b, lt, 5 * w)
    gl = _matmul(h2, w_in[:, col_g:].astype(BF16), BF16).reshape(b, lt, 3 * d)

    streams, ga = _rw_prep(za, nc, p['rw_mu'], p['rw_w0'], p['rw_wB'], p['rw_a0'], p['rw_aB'], p['rw_gB'])
    oaf, oab = _rw_chunk(streams, nc, p['rw_kk'], p['rw_ka'], p['rw_rk'], p['rw_gn'])

    zbr = _rope(zb, ct, st)
    yb = _attention(zbr, nc, p['da_lambda'], p['da_subln'], lam_init)

    of = _hgrn(zc, nc, hg_lb, rev=False)
    ob = _hgrn(zc, nc, hg_lb, rev=True)

    hn = jnp.tile(p['hg_norm'], w // HG_EXPAND).reshape(1, w)
    merged = _merge(oaf, oab, ga, yb, of, ob, zc, hn, gl, p['w_branch'].astype(BF16))
    xa = _proj_post(merged, p['w_out'].astype(BF16), xa, p['norm_g'][1], _sel(mod_c, mod_x, (2,)), nc)

    hm, afft = _norm_mod(xa, p['norm_g'][2], _sel(mod_c, mod_x, (3, 4)), nc, router_t=p['moe_router'].T)
    gsel = _sel(mod_c, mod_x, (5,))
    sets = [(nc, lt - nc, 1)] + ([(0, nc, 0)] if with_ctx else [])
    routed, xes, gates = [], [], []
    for row0, nn, seg in sets:
        cap = EC_CAPACITY_FACTOR * nn // N_EXPERTS
        at = afft[:, :, row0:row0 + nn]
        slot = _rank(at, cap)
        xe_s, gate_s = _gather(slot, at, hm[:, row0:row0 + nn], cap)
        xes.append(xe_s)
        gates.append(gate_s)
        routed.append((jnp.swapaxes(slot, 1, 2), row0, seg, cap))
    xe = xes[0] if len(xes) == 1 else jnp.concatenate(xes, axis=1)
    gate = gates[0] if len(gates) == 1 else jnp.concatenate(gates, axis=1)
    ye = _ffn(xe, gate, p['moe_w1'], p['moe_w3'], p['moe_w2'], p['layer'])
    slot0 = 0
    for slot_c, row0, seg, cap in routed:
        xa = _combine(slot_c, ye, xa, p['norm_g'][3], gsel, seg, row0, slot0, cap)
        slot0 += b * cap
    return xa


def kernel(x, c, ctx, c_ctx, w_ada, b_ada, norm_g, w_in, rw_mu, rw_w0, rw_wB, rw_a0, rw_aB, rw_gB, rw_kk, rw_ka, rw_rk, rw_gn, da_lambda, da_subln, hg_lb_logits, hg_norm, w_branch, w_out, moe_router, moe_w1, moe_w3, moe_w2):
    b, seq, d = x.shape
    nc = ctx.shape[1]
    depth = w_ada.shape[0]
    ct, st = _rope_tables(seq, nc)
    lb_w = jax.nn.softmax(hg_lb_logits.astype(F32), axis=0)
    hg_lb = jnp.cumsum(lb_w, axis=0) - lb_w[0]
    rows = ((b + 1 + SUBLANES - 1) // SUBLANES) * SUBLANES
    cc = jnp.zeros((rows, d), F32).at[:b].set(c).at[b].set(c_ctx)
    mod = _modulation(cc, w_ada, b_ada)
    xa = jnp.concatenate([ctx, x], axis=1)
    for l in range(depth):
        p = dict(norm_g=norm_g[l], w_in=w_in[l], rw_mu=rw_mu[l], rw_w0=rw_w0[l], rw_wB=rw_wB[l],
                 rw_a0=rw_a0[l], rw_aB=rw_aB[l], rw_gB=rw_gB[l], rw_kk=rw_kk[l], rw_ka=rw_ka[l],
                 rw_rk=rw_rk[l], rw_gn=rw_gn[l], da_lambda=da_lambda[l], da_subln=da_subln[l],
                 hg_norm=hg_norm[l], w_branch=w_branch[l], w_out=w_out[l], moe_router=moe_router[l],
                 moe_w1=moe_w1, moe_w3=moe_w3, moe_w2=moe_w2, layer=l)
        mod_x = mod[l, :b].reshape(b, 6, d)
        mod_c = mod[l, b].reshape(6, d)
        lam_init = 0.8 - 0.6 * math.exp(-0.3 * l)
        xa = _layer(xa, nc, mod_x, mod_c, p, lam_init, hg_lb[l], ct, st, with_ctx=l < depth - 1)
    return xa[:, nc:]
```

```python
import functools
import math

import jax
import jax.numpy as jnp
from jax import lax
from jax.experimental import pallas as pl
from jax.experimental.pallas import tpu as pltpu

F32 = jnp.float32
BF16 = jnp.bfloat16
HIGHEST = lax.Precision.HIGHEST

GRID_W = 64
RW_HEAD_DIM = 64
RW_DECAY_RANK = 64
RW_ICL_RANK = 64
RW_GATE_RANK = 128
RW_GN_EPS = 64e-5
DA_QK_DIM = 64
DA_V_DIM = 128
ROPE_THETA = 10000.0
ROPE_AXIS_FREQS = 16
DA_SUBLN_EPS = 1e-5
HG_EXPAND = 128
N_EXPERTS = 16
EC_CAPACITY_FACTOR = 2
NORM_EPS = 1e-6

LANES = 128
SUBLANES = 8
V7X_VMEM_BYTES = 64 * 1024 * 1024
HG_CHUNK = 64
RW_CHUNK = 64
RW_GROUP = 256
FFN_OUT_CHUNK = 512


def _cparams(sem, vmem_mb):
    return pltpu.CompilerParams(dimension_semantics=sem, vmem_limit_bytes=int(vmem_mb * 1024 * 1024))


def _sigmoid(x):
    return 1.0 / (1.0 + jnp.exp(-x))


def _sigmoid_t(x):
    return 0.5 * jnp.tanh(0.5 * x) + 0.5


def _silu(x):
    return x * _sigmoid_t(x)


def _mod_kernel(c_ref, w_ref, b_ref, o_ref):
    a = _silu(c_ref[...]).astype(BF16)
    o_ref[0] = jnp.dot(a, w_ref[0].astype(BF16), preferred_element_type=F32) + b_ref[0]


def _modulation(cc, w_ada, b_ada):
    depth, d, n = w_ada.shape
    rows = cc.shape[0]
    tn = 1024
    return pl.pallas_call(
        _mod_kernel,
        out_shape=jax.ShapeDtypeStruct((depth, rows, n), F32),
        grid=(depth, n // tn),
        in_specs=[pl.BlockSpec((rows, d), lambda l, j: (0, 0)),
                  pl.BlockSpec((1, d, tn), lambda l, j: (l, 0, j)),
                  pl.BlockSpec((1, 1, tn), lambda l, j: (l, 0, j))],
        out_specs=pl.BlockSpec((1, rows, tn), lambda l, j: (l, 0, j)),
        compiler_params=_cparams(("arbitrary", "arbitrary"), 40),
        name="adaln_mod",
    )(cc, w_ada, b_ada.reshape(depth, 1, n))


def _norm_mod_kernel(x_ref, g_ref, ms_ref, *rest, with_router):
    x = x_ref[0]
    xn = x * lax.rsqrt(jnp.mean(x * x, axis=-1, keepdims=True) + NORM_EPS) * g_ref[...]
    h = xn * (1.0 + ms_ref[0, 0, 1:2, :]) + ms_ref[0, 0, 0:1, :]
    if with_router:
        rt_ref, h_ref, aff_ref = rest
        h_ref[0] = h.astype(BF16)
        lt = lax.dot_general(rt_ref[...], h, (((1,), (1,)), ((), ())),
                             precision=HIGHEST, preferred_element_type=F32)
        e = jnp.exp(lt - jnp.max(lt, axis=0, keepdims=True))
        aff_ref[0] = e / jnp.sum(e, axis=0, keepdims=True)
    else:
        (h_ref,) = rest
        h_ref[0] = h.astype(BF16)


def _norm_mod(xa, g, msel, nc, router_t=None, tl=256):
    b, lt, d = xa.shape
    tl = min(tl, nc)
    nct = nc // tl
    with_router = router_t is not None
    in_specs = [pl.BlockSpec((1, tl, d), lambda bi, i: (bi, i, 0)),
                pl.BlockSpec((1, d), lambda bi, i: (0, 0)),
                pl.BlockSpec((1, 1, 2, d), lambda bi, i: (bi, jnp.where(i >= nct, 1, 0), 0, 0))]
    args = [xa, g.reshape(1, d), msel]
    out_shape = [jax.ShapeDtypeStruct((b, lt, d), BF16)]
    out_specs = [pl.BlockSpec((1, tl, d), lambda bi, i: (bi, i, 0))]
    if with_router:
        e = router_t.shape[0]
        in_specs.append(pl.BlockSpec((e, d), lambda bi, i: (0, 0)))
        args.append(router_t)
        out_shape.append(jax.ShapeDtypeStruct((b, e, lt), F32))
        out_specs.append(pl.BlockSpec((1, e, tl), lambda bi, i: (bi, 0, i)))
    res = pl.pallas_call(
        functools.partial(_norm_mod_kernel, with_router=with_router),
        out_shape=out_shape, grid=(b, lt // tl), in_specs=in_specs, out_specs=out_specs,
        compiler_params=_cparams(("arbitrary", "arbitrary"), 32),
        name="norm_mod_router" if with_router else "norm_mod",
    )(*args)
    return res if with_router else res[0]


def _matmul_kernel(a_ref, w_ref, o_ref):
    o_ref[...] = jnp.dot(a_ref[...], w_ref[...], preferred_element_type=F32).astype(o_ref.dtype)


def _pick(n, cands):
    for c in cands:
        if n % c == 0:
            return c
    return n


def _matmul(a, w, out_dtype, tm=1024, tn=1024):
    m, k = a.shape
    n = w.shape[1]
    tm = _pick(m, (tm, 768, 512, 384, 256, 128))
    tn = _pick(n, (tn, 768, 512, 384, 256, 128))
    return pl.pallas_call(
        _matmul_kernel,
        out_shape=jax.ShapeDtypeStruct((m, n), out_dtype),
        grid=(m // tm, n // tn),
        in_specs=[pl.BlockSpec((tm, k), lambda i, j: (i, 0)),
                  pl.BlockSpec((k, tn), lambda i, j: (0, j))],
        out_specs=pl.BlockSpec((tm, tn), lambda i, j: (i, j)),
        compiler_params=_cparams(("arbitrary", "arbitrary"), 48),
        name="matmul",
    )(a, w)


def _rw_prep_kernel(z_ref, zp_ref, zn_ref, mu_ref, w0_ref, wb_ref, a0_ref, ab_ref, gb_ref,
                    o_ref, g_ref, *, tl, nct, nt, w):
    i = pl.program_id(1)
    z = z_ref[0]
    row = lax.broadcasted_iota(jnp.int32, (tl, 1), 0)
    seg_start = jnp.logical_or(i == 0, i == nct)
    seg_end = jnp.logical_or(i == nct - 1, i == nt - 1)
    prev_row = jnp.where(seg_start, 0.0, zp_ref[0, SUBLANES - 1:SUBLANES, :])
    next_row = jnp.where(seg_end, 0.0, zn_ref[0, 0:1, :])
    prev = jnp.where(row == 0, prev_row, pltpu.roll(z, 1, 0))
    nxt = jnp.where(row == tl - 1, next_row, pltpu.roll(z, tl - 1, 0))
    zs = z + mu_ref[0:1, :] * (prev - z) + mu_ref[1:2, :] * (nxt - z)
    o_ref[0, 0] = zs[:, 0:w]
    o_ref[0, 1] = zs[:, w:2 * w]
    o_ref[0, 2] = zs[:, 2 * w:3 * w]
    o4 = 3 * w + 2 * RW_DECAY_RANK
    o5 = o4 + 2 * RW_ICL_RANK
    wd = jnp.tanh(zs[:, 3 * w:o4])
    wl = jnp.dot(wd.astype(BF16), wb_ref[...], preferred_element_type=F32) + w0_ref[...]
    dec = -math.exp(-0.5) * _sigmoid_t(wl)
    o_ref[0, 3] = dec[:, 0:w]
    o_ref[0, 4] = dec[:, w:2 * w]
    al = jnp.dot(zs[:, o4:o5].astype(BF16), ab_ref[...], preferred_element_type=F32) + a0_ref[...]
    av = _sigmoid_t(al)
    o_ref[0, 5] = av[:, 0:w]
    o_ref[0, 6] = av[:, w:2 * w]
    g_ref[0] = jnp.dot(_sigmoid_t(zs[:, o5:]).astype(BF16), gb_ref[...], preferred_element_type=F32)


def _blockdiag2(m):
    r, w = m.shape[1], m.shape[2]
    z = jnp.zeros((r, w), m.dtype)
    return jnp.concatenate([jnp.concatenate([m[0], z], axis=1), jnp.concatenate([z, m[1]], axis=1)], axis=0)


def _rw_prep(za, nc, mu, w0, wb, a0, ab, gb, tl=256):
    b, lt, acols = za.shape
    w = w0.shape[1]
    tl = min(tl, nc)
    nct, nt = nc // tl, lt // tl
    r8 = tl // SUBLANES
    nb8 = lt // SUBLANES
    full = lambda shp: pl.BlockSpec(shp, lambda bi, i: tuple(0 for _ in shp))
    kern = functools.partial(_rw_prep_kernel, tl=tl, nct=nct, nt=nt, w=w)
    return pl.pallas_call(
        kern,
        out_shape=[jax.ShapeDtypeStruct((b, 7, lt, w), F32), jax.ShapeDtypeStruct((b, lt, w), F32)],
        grid=(b, nt),
        in_specs=[pl.BlockSpec((1, tl, acols), lambda bi, i: (bi, i, 0)),
                  pl.BlockSpec((1, SUBLANES, acols), lambda bi, i: (bi, jnp.maximum(i * r8 - 1, 0), 0)),
                  pl.BlockSpec((1, SUBLANES, acols), lambda bi, i: (bi, jnp.minimum((i + 1) * r8, nb8 - 1), 0)),
                  full((2, acols)), full((1, 2 * w)), full((2 * RW_DECAY_RANK, 2 * w)),
                  full((1, 2 * w)), full((2 * RW_ICL_RANK, 2 * w)), full((RW_GATE_RANK, w))],
        out_specs=[pl.BlockSpec((1, 7, tl, w), lambda bi, i: (bi, 0, i, 0)),
                   pl.BlockSpec((1, tl, w), lambda bi, i: (bi, i, 0))],
        compiler_params=_cparams(("arbitrary", "arbitrary"), 48),
        name="rwkv_prep",
    )(za, za, za, mu, w0.reshape(1, 2 * w), _blockdiag2(wb).astype(BF16), a0.reshape(1, 2 * w),
      _blockdiag2(ab).astype(BF16), gb.astype(BF16))


def _rw_chunk_kernel(xf_ref, lwf_ref, af_ref, xb_ref, lwb_ref, ab_ref, kk_ref, ka_ref, rk_ref, gn0_ref, gn1_ref,
                     of_ref, ob_ref, s_ref, *, c, w):
    j = pl.program_id(1)

    @pl.when(j == 0)
    def _():
        s_ref[...] = jnp.zeros_like(s_ref)

    gw, n = RW_GROUP, RW_HEAD_DIM
    ng = w // gw
    row = lax.broadcasted_iota(jnp.int32, (c, 1), 0)
    pos = lax.broadcasted_iota(jnp.int32, (1, gw), 1) % n
    bdmask = (lax.broadcasted_iota(jnp.int32, (gw, 1), 0) // n) == (lax.broadcasted_iota(jnp.int32, (1, gw), 1) // n)
    ones_bd = jnp.where(bdmask, 1.0, 0.0).astype(BF16)
    eye = jnp.where(pos == row, 1.0, 0.0)
    nt = (((1,), (1,)), ((), ()))
    tn = (((0,), (0,)), ((), ()))
    chains = [(d, g) for d in range(2) for g in range(ng)]
    refs = ((xf_ref, lwf_ref, af_ref, of_ref), (xb_ref, lwb_ref, ab_ref, ob_ref))

    def bd(x):
        return jnp.where(bdmask, jnp.concatenate([x] * (gw // c), axis=0), 0.0).astype(BF16)

    def gsum(xs):
        pieces = []
        for x in xs:
            hi = x.astype(BF16)
            r1 = x - hi.astype(F32)
            mid = r1.astype(BF16)
            pieces += [hi, mid, (r1 - mid.astype(F32)).astype(BF16)]
        tot = jnp.dot(jnp.concatenate(pieces, axis=0), ones_bd, preferred_element_type=F32)
        return [tot[3 * i * c:(3 * i + 1) * c] + tot[(3 * i + 1) * c:(3 * i + 2) * c] + tot[(3 * i + 2) * c:(3 * i + 3) * c]
                for i in range(len(xs))]

    def mm(x, ybd):
        return jnp.dot(x.astype(BF16), ybd, preferred_element_type=F32)

    st = []
    for d, g in chains:
        x_ref, lw_ref, a_ref, _ = refs[d]
        sl = slice(g * gw, (g + 1) * gw)
        q = dict(sl=sl, rev=d == 1, r=x_ref[0, 0, :, sl], k=x_ref[0, 1, :, sl], v=x_ref[0, 2, :, sl],
                 lw=lw_ref[0, 0, :, sl], a=a_ref[0, 0, :, sl])
        q['kx'] = q['k'] * kk_ref[:, sl]
        q['kt'] = q['k'] * (1.0 + (q['a'] - 1.0) * ka_ref[:, sl])
        st.append(q)
    for q in st:
        q['ss'], q['bon'] = gsum([q['kx'] * q['kx'], q['r'] * q['kt'] * rk_ref[:, q['sl']]])
    for q in st:
        rev = q['rev']
        kk = q['kx'] / jnp.maximum(jnp.sqrt(q['ss']), 1e-12)
        q['b'] = q['a'] * kk
        cw = q['lw']
        sft = 1
        while sft < c:
            if rev:
                cw = cw + jnp.where(row < c - sft, pltpu.roll(cw, c - sft, 0), 0.0)
            else:
                cw = cw + jnp.where(row >= sft, pltpu.roll(cw, sft, 0), 0.0)
            sft *= 2
        q['tot'] = cw[0:1] if rev else cw[c - 1:c]
        invp = jnp.exp(-cw)
        q['e2'] = jnp.exp(q['tot'] - cw)
        q['kr'] = jnp.concatenate([kk * jnp.exp(cw - q['lw']), q['r'] * jnp.exp(cw)], axis=0).astype(BF16)
        q['ktb'] = bd(q['kt'] * invp)
        q['bb'] = bd(q['b'] * invp)
        q['vbd'] = bd(q['v'])
    for q in st:
        strict = (pos > row) if q['rev'] else (pos < row)
        incl = (pos >= row) if q['rev'] else (pos <= row)
        g1 = lax.dot_general(q['kr'], q['ktb'], nt, preferred_element_type=F32)
        g2 = lax.dot_general(q['kr'], q['bb'], nt, preferred_element_type=F32)
        q['m_kt'] = jnp.where(strict, g1[:c], 0.0)
        q['n_kt'] = jnp.where(incl, g1[c:], 0.0)
        q['x'] = jnp.where(strict, g2[:c], 0.0)
        q['n_b'] = jnp.where(incl, g2[c:], 0.0)
        q['t'] = eye - q['x']
        q['xb'] = bd(q['x'])
    for q in st:
        mn = mm(jnp.concatenate([q['m_kt'], q['n_kt']], axis=0), q['vbd'])
        q['mkv'], q['nkv'] = mn[:c], mn[c:]
    lv = 2
    while lv < c:
        for q in st:
            xt = mm(jnp.concatenate([q['x'], q['t']], axis=0), q['xb'])
            q['x'], q['tx'] = xt[:c], xt[c:]
        for q in st:
            q['t'] = q['t'] + mm(q['tx'], q['xb'])
        lv *= 2
        if lv < c:
            for q in st:
                q['xb'] = bd(q['x'])
    for (d, g), q in zip(chains, st):
        q['s0'] = s_ref[d, g]
        q['ka'] = lax.dot_general(q['kr'], q['s0'].astype(BF16), nt, preferred_element_type=F32)
    for q in st:
        q['u'] = mm(q['t'], bd(q['ka'][:c] + q['mkv']))
    for (d, g), q in zip(chains, st):
        e2 = q['e2']
        upd = lax.dot_general(jnp.concatenate([q['v'], q['u']], axis=0).astype(BF16),
                              jnp.concatenate([q['kt'] * e2, -(q['b'] * e2)], axis=0).astype(BF16),
                              tn, preferred_element_type=F32)
        s_ref[d, g] = q['s0'] * jnp.exp(q['tot']) + jnp.where(bdmask, upd, 0.0)
    for q in st:
        q['y'] = q['ka'][c:] + q['nkv'] - mm(q['n_b'], bd(q['u']))
    for q in st:
        (mu,) = gsum([q['y']])
        q['dl'] = q['y'] - mu * (1.0 / n)
    for (d, g), q in zip(chains, st):
        (var,) = gsum([q['dl'] * q['dl']])
        sl = q['sl']
        yn = q['dl'] * lax.rsqrt(var * (1.0 / n) + RW_GN_EPS) * gn0_ref[:, sl] + gn1_ref[:, sl]
        refs[d][3][0, :, sl] = yn + q['bon'] * q['v']


def _rw_chunk(streams, nc, kk, ka, rk, gn):
    b, _, lt, w = streams.shape
    c = RW_CHUNK
    ncb, nb = nc // c, lt // c

    def rblk(j):
        return jnp.where(j < ncb, ncb - 1 - j, nb - 1 - (j - ncb))

    par = pl.BlockSpec((1, w), lambda bi, j: (0, 0))
    shp = jax.ShapeDtypeStruct((b, lt, w), F32)
    return pl.pallas_call(
        functools.partial(_rw_chunk_kernel, c=c, w=w),
        out_shape=[shp, shp],
        grid=(b, nb),
        in_specs=[pl.BlockSpec((1, 3, c, w), lambda bi, j: (bi, 0, j, 0)),
                  pl.BlockSpec((1, 1, c, w), lambda bi, j: (bi, 3, j, 0)),
                  pl.BlockSpec((1, 1, c, w), lambda bi, j: (bi, 5, j, 0)),
                  pl.BlockSpec((1, 3, c, w), lambda bi, j: (bi, 0, rblk(j), 0)),
                  pl.BlockSpec((1, 1, c, w), lambda bi, j: (bi, 4, rblk(j), 0)),
                  pl.BlockSpec((1, 1, c, w), lambda bi, j: (bi, 6, rblk(j), 0)),
                  par, par, par, par, par],
        out_specs=[pl.BlockSpec((1, c, w), lambda bi, j: (bi, j, 0)),
                   pl.BlockSpec((1, c, w), lambda bi, j: (bi, rblk(j), 0))],
        scratch_shapes=[pltpu.VMEM((2, w // RW_GROUP, RW_GROUP, RW_GROUP), F32)],
        compiler_params=_cparams(("arbitrary", "arbitrary"), 32),
        name="rwkv_chunk",
    )(streams, streams, streams, streams, streams, streams, kk.reshape(1, w), ka.reshape(1, w), rk.reshape(1, w),
      gn[0].reshape(1, w), gn[1].reshape(1, w))


def _rope_kernel(z_ref, c_ref, s_ref, o_ref, *, w):
    lane = lax.broadcasted_iota(jnp.int32, (1, LANES), 1)
    first_half = (lane % (2 * ROPE_AXIS_FREQS)) < ROPE_AXIS_FREQS
    c = c_ref[...]
    s = s_ref[...]
    for j in range(2 * w // LANES):
        xs = z_ref[0, :, j * LANES:(j + 1) * LANES].astype(F32)
        up = pltpu.roll(xs, LANES - ROPE_AXIS_FREQS, 1)
        dn = pltpu.roll(xs, ROPE_AXIS_FREQS, 1)
        o = xs * c + jnp.where(first_half, up, dn) * s
        if j < w // LANES:
            o = o * (DA_QK_DIM ** -0.5 * math.log2(math.e))
        o_ref[0, :, j * LANES:(j + 1) * LANES] = o.astype(BF16)
    o_ref[0, :, 2 * w:] = z_ref[0, :, 2 * w:]


def _rope_tables(seq, nc):
    rows = seq // GRID_W
    row = jnp.broadcast_to(jnp.arange(rows)[:, None], (rows, GRID_W)).reshape(seq)
    col = jnp.broadcast_to(jnp.arange(GRID_W)[None, :], (rows, GRID_W)).reshape(seq)
    inv = 1.0 / (ROPE_THETA ** (jnp.arange(ROPE_AXIS_FREQS, dtype=F32) / ROPE_AXIS_FREQS))
    ang = jnp.stack([row, col], axis=-1).astype(F32)[:, :, None] * inv
    cos, sin = jnp.cos(ang), jnp.sin(ang)
    c64 = jnp.concatenate([cos[:, 0], cos[:, 0], cos[:, 1], cos[:, 1]], axis=-1)
    s64 = jnp.concatenate([-sin[:, 0], sin[:, 0], -sin[:, 1], sin[:, 1]], axis=-1)
    ct = jnp.concatenate([jnp.ones((nc, LANES), F32), jnp.tile(c64, (1, 2))], axis=0)
    st = jnp.concatenate([jnp.zeros((nc, LANES), F32), jnp.tile(s64, (1, 2))], axis=0)
    return ct, st


def _rope(zb, ct, st, tl=256):
    b, lt, cols = zb.shape
    w = cols // 3
    tl = _pick(lt, (tl, 128, 64))
    return pl.pallas_call(
        functools.partial(_rope_kernel, w=w),
        out_shape=jax.ShapeDtypeStruct((b, lt, cols), BF16),
        grid=(b, lt // tl),
        in_specs=[pl.BlockSpec((1, tl, cols), lambda bi, i: (bi, i, 0)),
                  pl.BlockSpec((tl, LANES), lambda bi, i: (i, 0)),
                  pl.BlockSpec((tl, LANES), lambda bi, i: (i, 0))],
        out_specs=pl.BlockSpec((1, tl, cols), lambda bi, i: (bi, i, 0)),
        compiler_params=_cparams(("arbitrary", "arbitrary"), 32),
        name="rope",
    )(zb, ct, st)


def _attn_kernel(q_ref, k_ref, v_ref, lam_ref, g_ref, o_ref, *, nct, nc, lam_init, hp):
    i = pl.program_id(2)
    lane = lax.broadcasted_iota(jnp.int32, (1, LANES), 1)
    lp = lam_ref[...]
    lam = (jnp.exp(jnp.sum(lp[0:1] * lp[1:2], axis=1, keepdims=True))
           - jnp.exp(jnp.sum(lp[2:3] * lp[3:4], axis=1, keepdims=True)) + lam_init)

    def attend(nk):
        for h in range(hp):
            cols = slice(h * DA_V_DIM, (h + 1) * DA_V_DIM)
            q = q_ref[0, :, cols]
            k = k_ref[0, 0:nk, cols]
            v1 = jnp.concatenate([v_ref[0, 0:nk, cols], jnp.ones((nk, LANES), BF16)], axis=1)

            def one(sel):
                qs = jnp.where(sel, q, jnp.zeros_like(q))
                s = lax.dot_general(qs, k, (((1,), (1,)), ((), ())), preferred_element_type=F32)
                e = jnp.exp2(s - jnp.max(s, axis=-1, keepdims=True))
                ov = jnp.dot(e.astype(BF16), v1, preferred_element_type=F32)
                return ov[:, :DA_V_DIM] / ov[:, DA_V_DIM:]

            o = one(lane < DA_QK_DIM) - lam * one(lane >= DA_QK_DIM)
            on = o * lax.rsqrt(jnp.mean(o * o, axis=-1, keepdims=True) + DA_SUBLN_EPS) * g_ref[...]
            o_ref[0, :, cols] = (on * (1.0 - lam_init)).astype(o_ref.dtype)

    @pl.when(i < nct)
    def _():
        attend(nc)

    @pl.when(i >= nct)
    def _():
        attend(k_ref.shape[1])


def _attention(zbr, nc, lam_p, subln, lam_init, hp=2):
    b, lt, cols = zbr.shape
    w = cols // 3
    heads = w // DA_V_DIM
    tq = min(256, nc)
    nct = nc // tq
    hg = heads // hp
    bw = hp * DA_V_DIM
    kern = functools.partial(_attn_kernel, nct=nct, nc=nc, lam_init=lam_init, hp=hp)
    return pl.pallas_call(
        kern,
        out_shape=jax.ShapeDtypeStruct((b, lt, w), BF16),
        grid=(b, hg, lt // tq),
        in_specs=[pl.BlockSpec((1, tq, bw), lambda bi, h, i: (bi, i, h)),
                  pl.BlockSpec((1, lt, bw), lambda bi, h, i: (bi, 0, hg + h)),
                  pl.BlockSpec((1, lt, bw), lambda bi, h, i: (bi, 0, 2 * hg + h)),
                  pl.BlockSpec((4, DA_QK_DIM), lambda bi, h, i: (0, 0)),
                  pl.BlockSpec((1, DA_V_DIM), lambda bi, h, i: (0, 0))],
        out_specs=pl.BlockSpec((1, tq, bw), lambda bi, h, i: (bi, i, h)),
        compiler_params=_cparams(("arbitrary", "arbitrary", "arbitrary"), 48),
        name="diff_attn",
    )(zbr, zbr, zbr, lam_p, subln.reshape(1, DA_V_DIM))


def _seg_masks(c, e):
    row = lax.broadcasted_iota(jnp.int32, (c, e), 0)
    mk = {}
    h = 1
    while h < min(SUBLANES, c):
        odd = (row % (2 * h)) >= h
        for s in range(1, h + 1):
            mk[(h, s, 0)] = jnp.where(jnp.logical_and(odd, (row % h) == s - 1), 1.0, 0.0)
            mk[(h, s, 1)] = jnp.where(jnp.logical_and(~odd, (row % h) == h - s), 1.0, 0.0)
        h *= 2
    return mk


def _seg_scans(lf, mk, c):
    cs, ss = {1: lf}, {1: lf}
    h = 1
    while h < c:
        x, y = cs[h], ss[h]
        if h < SUBLANES:
            x3 = x.reshape(c // SUBLANES, SUBLANES, x.shape[1])
            y3 = y.reshape(c // SUBLANES, SUBLANES, y.shape[1])
            for s in range(1, h + 1):
                x = x + mk[(h, s, 0)] * pltpu.roll(x3, s, 1).reshape(x.shape)
                y = y + mk[(h, s, 1)] * pltpu.roll(y3, SUBLANES - s, 1).reshape(y.shape)
            cs[2 * h], ss[2 * h] = x, y
        else:
            px = [x[j * h:(j + 1) * h] for j in range(c // h)]
            py = [y[j * h:(j + 1) * h] for j in range(c // h)]
            nx = [px[j] + px[j - 1][h - 1:h] if j % 2 == 1 else px[j] for j in range(c // h)]
            ny = [py[j] + py[j + 1][0:1] if j % 2 == 0 else py[j] for j in range(c // h)]
            cs[2 * h], ss[2 * h] = jnp.concatenate(nx, axis=0), jnp.concatenate(ny, axis=0)
        h *= 2
    return cs, ss


def _hg_kernel(q_ref, f_ref, v_ref, lb_ref, o_ref, st_ref, *, tc, c, heads, rev):
    j = pl.program_id(1)

    @pl.when(j == 0)
    def _():
        st_ref[...] = jnp.zeros_like(st_ref)

    ti = lax.broadcasted_iota(jnp.int32, (c, c), 0)
    si = lax.broadcasted_iota(jnp.int32, (c, c), 1)
    nt = (((1,), (1,)), ((), ()))
    tn = (((0,), (0,)), ((), ()))
    mk = _seg_masks(c, HG_EXPAND)
    masks, hh = {}, 1
    while hh < c:
        tb, sb = ti // hh, si // hh
        masks[hh] = (jnp.logical_and(tb % 2 == 0, sb == tb + 1) if rev
                     else jnp.logical_and(tb % 2 == 1, sb == tb - 1))
        hh *= 2

    def head(h, carry):
        cols = pl.ds(pl.multiple_of(h * HG_EXPAND, HG_EXPAND), HG_EXPAND)
        lb = lb_ref[:, cols]
        chunks = range(tc // c)
        for ci in (reversed(chunks) if rev else chunks):
            rows = slice(ci * c, (ci + 1) * c)
            q = _silu(q_ref[0, rows, cols])
            f = lb + (1.0 - lb) * _sigmoid(f_ref[0, rows, cols])
            k = 1.0 - f
            lf = jnp.log(f)
            v = v_ref[0, rows, cols].astype(BF16)
            cs, ss = _seg_scans(lf, mk, c)
            qs, ks = (ss, cs) if rev else (cs, ss)
            att = jnp.where(ti == si, lax.dot_general(q.astype(BF16), k.astype(BF16), nt,
                                                      preferred_element_type=F32), 0.0)
            hh = 1
            while hh < c:
                qe = (q * jnp.exp(qs[hh])).astype(BF16)
                ke = (k * jnp.exp(ks[hh] - lf)).astype(BF16)
                a = lax.dot_general(qe, ke, nt, preferred_element_type=F32)
                att = att + jnp.where(masks[hh], a, 0.0)
                hh *= 2
            st = st_ref[h]
            qe = (q * jnp.exp(qs[c])).astype(BF16)
            o = lax.dot_general(qe, st.astype(BF16), nt, preferred_element_type=F32)
            o = o + jnp.dot(att.astype(BF16), v, preferred_element_type=F32)
            o_ref[0, rows, cols] = o
            ke = (k * jnp.exp(ks[c] - lf)).astype(BF16)
            tot = qs[c][0:1] if rev else qs[c][c - 1:c]
            st_ref[h] = st * jnp.exp(tot) + lax.dot_general(v, ke, tn, preferred_element_type=F32)
        return carry

    lax.fori_loop(0, heads, head, 0, unroll=4)


def _hgrn(zc, nc, lb, rev):
    b, lt, cols = zc.shape
    w = cols // 5
    heads = w // HG_EXPAND
    tc = min(256, nc)
    c = min(HG_CHUNK, tc)
    ncb, nb = nc // tc, lt // tc

    def blk(j):
        return jnp.where(j < ncb, ncb - 1 - j, nb - 1 - (j - ncb)) if rev else j

    fcol = 2 if rev else 1
    return pl.pallas_call(
        functools.partial(_hg_kernel, tc=tc, c=c, heads=heads, rev=rev),
        out_shape=jax.ShapeDtypeStruct((b, lt, w), F32),
        grid=(b, nb),
        in_specs=[pl.BlockSpec((1, tc, w), lambda bi, j: (bi, blk(j), 0)),
                  pl.BlockSpec((1, tc, w), lambda bi, j: (bi, blk(j), fcol)),
                  pl.BlockSpec((1, tc, w), lambda bi, j: (bi, blk(j), 3)),
                  pl.BlockSpec((1, w), lambda bi, j: (0, 0))],
        out_specs=pl.BlockSpec((1, tc, w), lambda bi, j: (bi, blk(j), 0)),
        scratch_shapes=[pltpu.VMEM((heads, HG_EXPAND, HG_EXPAND), F32)],
        compiler_params=_cparams(("arbitrary", "arbitrary"), 32),
        name="hgrn2_bwd" if rev else "hgrn2_fwd",
    )(zc, zc, zc, lb.reshape(1, w))


def _merge_kernel(oaf_ref, oab_ref, ga_ref, yb_ref, of_ref, ob_ref, gc_ref, hn_ref, gl0_ref, gl1_ref, gl2_ref, wb_ref,
                  o_ref, *, heads):
    gls = (gl0_ref, gl1_ref, gl2_ref)
    ya = (oaf_ref[0] + oab_ref[0]) * ga_ref[0]
    oc = of_ref[0] + ob_ref[0]
    parts = []
    for h in range(heads):
        x = oc[:, h * HG_EXPAND:(h + 1) * HG_EXPAND]
        parts.append(x * lax.rsqrt(jnp.mean(x * x, axis=-1, keepdims=True) + NORM_EPS))
    yc = jnp.concatenate(parts, axis=-1) * hn_ref[...] * _silu(gc_ref[0])
    acc = None
    for bi, y in enumerate((ya.astype(BF16), yb_ref[0], yc.astype(BF16))):
        term = _sigmoid_t(gls[bi][0].astype(F32)) * jnp.dot(y, wb_ref[bi], preferred_element_type=F32)
        acc = term if acc is None else acc + term
    o_ref[0] = acc.astype(BF16)


def _merge(oaf, oab, ga, yb, of, ob, zc, hn, gl, wbr, tl=128, tn=2048):
    b, lt, w = ga.shape
    d = wbr.shape[2]
    tl = _pick(lt, (tl, 128, 64))
    heads = w // HG_EXPAND
    nn = d // tn
    row = lambda n, bi, i: (bi, i, 0)
    glspec = lambda br: pl.BlockSpec((1, tl, tn), lambda n, bi, i: (bi, i, br * nn + n))
    return pl.pallas_call(
        functools.partial(_merge_kernel, heads=heads),
        out_shape=jax.ShapeDtypeStruct((b, lt, d), BF16),
        grid=(d // tn, b, lt // tl),
        in_specs=[pl.BlockSpec((1, tl, w), row),
                  pl.BlockSpec((1, tl, w), row),
                  pl.BlockSpec((1, tl, w), row),
                  pl.BlockSpec((1, tl, w), row),
                  pl.BlockSpec((1, tl, w), row),
                  pl.BlockSpec((1, tl, w), row),
                  pl.BlockSpec((1, tl, w), lambda n, bi, i: (bi, i, 4)),
                  pl.BlockSpec((1, w), lambda n, bi, i: (0, 0)),
                  glspec(0), glspec(1), glspec(2),
                  pl.BlockSpec((3, w, tn), lambda n, bi, i: (0, 0, n))],
        out_specs=pl.BlockSpec((1, tl, tn), lambda n, bi, i: (bi, i, n)),
        compiler_params=_cparams(("arbitrary", "arbitrary", "arbitrary"), 48),
        name="branch_merge",
    )(oaf, oab, ga, yb, of, ob, zc, hn, gl, gl, gl, wbr)


def _proj_post_kernel(a_ref, w_ref, x_ref, g_ref, gate_ref, o_ref):
    mx = jnp.dot(a_ref[0], w_ref[...], preferred_element_type=F32)
    mn = mx * lax.rsqrt(jnp.mean(mx * mx, axis=-1, keepdims=True) + NORM_EPS) * g_ref[...]
    o_ref[0] = x_ref[0] + gate_ref[0, 0] * mn


def _proj_post(a, wout, xa, g, gsel, nc, tl=256):
    b, lt, d = xa.shape
    tl = min(tl, nc)
    nct = nc // tl
    return pl.pallas_call(
        _proj_post_kernel,
        out_shape=jax.ShapeDtypeStruct((b, lt, d), F32),
        grid=(b, lt // tl),
        in_specs=[pl.BlockSpec((1, tl, d), lambda bi, i: (bi, i, 0)),
                  pl.BlockSpec((d, d), lambda bi, i: (0, 0)),
                  pl.BlockSpec((1, tl, d), lambda bi, i: (bi, i, 0)),
                  pl.BlockSpec((1, d), lambda bi, i: (0, 0)),
                  pl.BlockSpec((1, 1, 1, d), lambda bi, i: (bi, jnp.where(i >= nct, 1, 0), 0, 0))],
        out_specs=pl.BlockSpec((1, tl, d), lambda bi, i: (bi, i, 0)),
        input_output_aliases={2: 0},
        compiler_params=_cparams(("arbitrary", "arbitrary"), 48),
        name="out_proj_post",
    )(a, wout, xa, g.reshape(1, d), gsel)


def _rank_kernel(afft_ref, slot_ref, *, n, cap, tw):
    aff = afft_ref[0]
    bits = lax.bitcast_convert_type(aff, jnp.int32)
    e_num = aff.shape[0]

    def count(mask):
        return jnp.sum(jnp.where(mask, 1.0, 0.0), axis=1, keepdims=True)

    def bit_step(i, thr):
        cand = thr | lax.shift_left(jnp.int32(1), 30 - i)
        return jnp.where(count(bits >= cand) >= cap, cand, thr)

    thr = lax.fori_loop(0, 31, bit_step, jnp.zeros((e_num, 1), jnp.int32))
    gt = bits > thr
    eq = bits == thr
    need = cap - count(gt)
    tri = jnp.where(lax.broadcasted_iota(jnp.int32, (tw, tw), 0) < lax.broadcasted_iota(jnp.int32, (tw, tw), 1),
                    1.0, 0.0).astype(BF16)

    def prefix(mask):
        parts, carry = [], jnp.zeros((e_num, 1), F32)
        for j in range(n // tw):
            m = jnp.where(mask[:, j * tw:(j + 1) * tw], 1.0, 0.0)
            parts.append(jnp.dot(m.astype(BF16), tri, preferred_element_type=F32) + carry)
            carry = carry + jnp.sum(m, axis=1, keepdims=True)
        return jnp.concatenate(parts, axis=1)

    sel = jnp.logical_or(gt, jnp.logical_and(eq, prefix(eq) < need))
    slot_ref[0] = jnp.where(sel, prefix(sel), float(cap)).astype(jnp.int32)


def _rank(afft, cap):
    b, e_num, n = afft.shape
    tw = min(LANES, n)
    return pl.pallas_call(
        functools.partial(_rank_kernel, n=n, cap=cap, tw=tw),
        out_shape=jax.ShapeDtypeStruct((b, e_num, n), jnp.int32),
        grid=(b,),
        in_specs=[pl.BlockSpec((1, e_num, n), lambda bi: (bi, 0, 0))],
        out_specs=pl.BlockSpec((1, e_num, n), lambda bi: (bi, 0, 0)),
        compiler_params=_cparams(("arbitrary",), 32),
        name="ec_rank",
    )(afft)


def _gather_kernel(slot_ref, afft_ref, h_ref, o_ref, g_ref, *, cap):
    e = pl.program_id(1)
    sl = slot_ref[0, pl.ds(e, 1), :]
    hit = lax.broadcasted_iota(jnp.int32, (cap, 1), 0) == sl
    o_ref[0] = jnp.dot(jnp.where(hit, 1.0, 0.0).astype(BF16), h_ref[0], preferred_element_type=F32).astype(BF16)
    gate = jnp.sum(jnp.where(hit, afft_ref[0, pl.ds(e, 1), :], 0.0), axis=1, keepdims=True)
    g_ref[0] = jnp.broadcast_to(gate, (cap, LANES))


def _gather(slot, afft, hs, cap):
    b, e_num, n = slot.shape
    d = hs.shape[2]
    return pl.pallas_call(
        functools.partial(_gather_kernel, cap=cap),
        out_shape=[jax.ShapeDtypeStruct((e_num, b * cap, d), BF16),
                   jax.ShapeDtypeStruct((e_num, b * cap, LANES), F32)],
        grid=(b, e_num),
        in_specs=[pl.BlockSpec((1, e_num, n), lambda bi, e: (bi, 0, 0)),
                  pl.BlockSpec((1, e_num, n), lambda bi, e: (bi, 0, 0)),
                  pl.BlockSpec((1, n, d), lambda bi, e: (bi, 0, 0))],
        out_specs=[pl.BlockSpec((1, cap, d), lambda bi, e: (e, bi, 0)),
                   pl.BlockSpec((1, cap, LANES), lambda bi, e: (e, bi, 0))],
        compiler_params=_cparams(("arbitrary", "arbitrary"), 40),
        name="ec_gather",
    )(slot, afft, hs)


def _ffn_kernel(x_ref, g_ref, w1_ref, w3_ref, w2_ref, o_ref, acc_ref):
    f = pl.program_id(2)
    x = x_ref[0]
    h1 = jnp.dot(x, w1_ref[0, 0].astype(BF16), preferred_element_type=F32)
    h3 = jnp.dot(x, w3_ref[0, 0].astype(BF16), preferred_element_type=F32)
    hid = (_silu(h1) * h3).astype(BF16)

    @pl.when(f == 0)
    def _():
        acc_ref[...] = jnp.zeros_like(acc_ref)

    d = acc_ref.shape[1]
    for n0 in range(0, d, FFN_OUT_CHUNK):
        cols = slice(n0, n0 + FFN_OUT_CHUNK)
        acc_ref[:, cols] += jnp.dot(hid, w2_ref[0, 0, :, cols].astype(BF16), preferred_element_type=F32)

    @pl.when(f == pl.num_programs(2) - 1)
    def _():
        o_ref[0] = (acc_ref[...] * g_ref[0, :, 0:1]).astype(BF16)


def _ffn(xe, gate, w1, w3, w2, layer, tf=256):
    e_num, m, d = xe.shape
    ff = w1.shape[3]
    tm = m
    once = pl.Buffered(1)
    return pl.pallas_call(
        _ffn_kernel,
        out_shape=jax.ShapeDtypeStruct((e_num, m, d), BF16),
        grid=(e_num, m // tm, ff // tf),
        in_specs=[pl.BlockSpec((1, tm, d), lambda e, i, f: (e, i, 0), pipeline_mode=once),
                  pl.BlockSpec((1, tm, LANES), lambda e, i, f: (e, i, 0), pipeline_mode=once),
                  pl.BlockSpec((1, 1, d, tf), lambda e, i, f: (layer, e, 0, f)),
                  pl.BlockSpec((1, 1, d, tf), lambda e, i, f: (layer, e, 0, f)),
                  pl.BlockSpec((1, 1, tf, d), lambda e, i, f: (layer, e, f, 0))],
        out_specs=pl.BlockSpec((1, tm, d), lambda e, i, f: (e, i, 0), pipeline_mode=once),
        scratch_shapes=[pltpu.VMEM((tm, d), F32)],
        compiler_params=_cparams(("arbitrary", "arbitrary", "arbitrary"), 58),
        name="ec_ffn",
    )(xe, gate, w1, w3, w2)


def _combine_kernel(slot_ref, ye_ref, x_ref, g_ref, gate_ref, o_ref, *, cap, e_num):
    lane = lax.broadcasted_iota(jnp.int32, (1, e_num), 1)
    pos = lax.broadcasted_iota(jnp.int32, (1, cap), 1)
    sl = slot_ref[0]
    y = None
    for e in range(e_num):
        se = jnp.sum(jnp.where(lane == e, sl, 0), axis=1, keepdims=True)
        pt = jnp.where(se == pos, 1.0, 0.0).astype(BF16)
        part = jnp.dot(pt, ye_ref[e], preferred_element_type=F32)
        y = part if y is None else y + part
    yn = y * lax.rsqrt(jnp.mean(y * y, axis=-1, keepdims=True) + NORM_EPS) * g_ref[...]
    o_ref[0] = x_ref[0] + gate_ref[0, 0] * yn


def _combine(slot_c, ye, xa, g, gsel, seg, row0, slot0, cap, tt=256):
    b, n, e_num = slot_c.shape
    d = xa.shape[2]
    tt = _pick(math.gcd(n, row0), (tt, 128, 64))
    assert row0 % tt == 0 and slot0 % cap == 0
    r0, s0 = row0 // tt, slot0 // cap
    return pl.pallas_call(
        functools.partial(_combine_kernel, cap=cap, e_num=e_num),
        out_shape=jax.ShapeDtypeStruct(xa.shape, F32),
        grid=(b, n // tt),
        in_specs=[pl.BlockSpec((1, tt, e_num), lambda bi, j: (bi, j, 0)),
                  pl.BlockSpec((e_num, cap, d), lambda bi, j: (0, s0 + bi, 0)),
                  pl.BlockSpec((1, tt, d), lambda bi, j: (bi, r0 + j, 0)),
                  pl.BlockSpec((1, d), lambda bi, j: (0, 0)),
                  pl.BlockSpec((1, 1, 1, d), lambda bi, j: (bi, seg, 0, 0))],
        out_specs=pl.BlockSpec((1, tt, d), lambda bi, j: (bi, r0 + j, 0)),
        input_output_aliases={2: 0},
        compiler_params=_cparams(("arbitrary", "arbitrary"), 56),
        name="ec_combine_post",
    )(slot_c, ye, xa, g.reshape(1, d), gsel)


def _sel(mod_c, mod_x, idx):
    b = mod_x.shape[0]
    mc = jnp.broadcast_to(mod_c[jnp.array(idx)][None], (b, len(idx), mod_c.shape[-1]))
    return jnp.stack([mc, mod_x[:, jnp.array(idx)]], axis=1)


def _layer(xa, nc, mod_x, mod_c, p, lam_init, hg_lb, ct, st, with_ctx):
    b, lt, d = xa.shape
    w = d // 2
    a_cols = 3 * w + 2 * RW_DECAY_RANK + 2 * RW_ICL_RANK + RW_GATE_RANK
    col_b = a_cols
    col_c = col_b + 3 * w
    col_g = col_c + 5 * w

    h = _norm_mod(xa, p['norm_g'][0], _sel(mod_c, mod_x, (0, 1)), nc)
    h2 = h.reshape(b * lt, d)
    w_in = p['w_in']
    za = _matmul(h2, w_in[:, :col_b].astype(BF16), F32, tn=1152).reshape(b, lt, a_cols)
    zb = _matmul(h2, w_in[:, col_b:col_c].astype(BF16), BF16).reshape(b, lt, 3 * w)
    zc = _matmul(h2, w_in[:, col_c:col_g].astype(BF16), F32).reshape(b, lt, 5 * w)
    gl = _matmul(h2, w_in[:, col_g:].astype(BF16), BF16).reshape(b, lt, 3 * d)

    streams, ga = _rw_prep(za, nc, p['rw_mu'], p['rw_w0'], p['rw_wB'], p['rw_a0'], p['rw_aB'], p['rw_gB'])
    oaf, oab = _rw_chunk(streams, nc, p['rw_kk'], p['rw_ka'], p['rw_rk'], p['rw_gn'])

    zbr = _rope(zb, ct, st)
    yb = _attention(zbr, nc, p['da_lambda'], p['da_subln'], lam_init)

    of = _hgrn(zc, nc, hg_lb, rev=False)
    ob = _hgrn(zc, nc, hg_lb, rev=True)

    hn = jnp.tile(p['hg_norm'], w // HG_EXPAND).reshape(1, w)
    merged = _merge(oaf, oab, ga, yb, of, ob, zc, hn, gl, p['w_branch'].astype(BF16))
    xa = _proj_post(merged, p['w_out'].astype(BF16), xa, p['norm_g'][1], _sel(mod_c, mod_x, (2,)), nc)

    hm, afft = _norm_mod(xa, p['norm_g'][2], _sel(mod_c, mod_x, (3, 4)), nc, router_t=p['moe_router'].T)
    gsel = _sel(mod_c, mod_x, (5,))
    sets = [(nc, lt - nc, 1)] + ([(0, nc, 0)] if with_ctx else [])
    routed, xes, gates = [], [], []
    for row0, nn, seg in sets:
        cap = EC_CAPACITY_FACTOR * nn // N_EXPERTS
        at = afft[:, :, row0:row0 + nn]
        slot = _rank(at, cap)
        xe_s, gate_s = _gather(slot, at, hm[:, row0:row0 + nn], cap)
        xes.append(xe_s)
        gates.append(gate_s)
        routed.append((jnp.swapaxes(slot, 1, 2), row0, seg, cap))
    xe = xes[0] if len(xes) == 1 else jnp.concatenate(xes, axis=1)
    gate = gates[0] if len(gates) == 1 else jnp.concatenate(gates, axis=1)
    ye = _ffn(xe, gate, p['moe_w1'], p['moe_w3'], p['moe_w2'], p['layer'])
    slot0 = 0
    for slot_c, row0, seg, cap in routed:
        xa = _combine(slot_c, ye, xa, p['norm_g'][3], gsel, seg, row0, slot0, cap)
        slot0 += b * cap
    return xa


def kernel(x, c, ctx, c_ctx, w_ada, b_ada, norm_g, w_in, rw_mu, rw_w0, rw_wB, rw_a0, rw_aB, rw_gB, rw_kk, rw_ka, rw_rk, rw_gn, da_lambda, da_subln, hg_lb_logits, hg_norm, w_branch, w_out, moe_router, moe_w1, moe_w3, moe_w2):
    b, seq, d = x.shape
    nc = ctx.shape[1]
    depth = w_ada.shape[0]
    ct, st = _rope_tables(seq, nc)
    lb_w = jax.nn.softmax(hg_lb_logits.astype(F32), axis=0)
    hg_lb = jnp.cumsum(lb_w, axis=0) - lb_w[0]
    rows = ((b + 1 + SUBLANES - 1) // SUBLANES) * SUBLANES
    cc = jnp.zeros((rows, d), F32).at[:b].set(c).at[b].set(c_ctx)
    mod = _modulation(cc, w_ada, b_ada)
    xa = jnp.concatenate([ctx, x], axis=1)
    for l in range(depth):
        p = dict(norm_g=norm_g[l], w_in=w_in[l], rw_mu=rw_mu[l], rw_w0=rw_w0[l], rw_wB=rw_wB[l],
                 rw_a0=rw_a0[l], rw_aB=rw_aB[l], rw_gB=rw_gB[l], rw_kk=rw_kk[l], rw_ka=rw_ka[l],
                 rw_rk=rw_rk[l], rw_gn=rw_gn[l], da_lambda=da_lambda[l], da_subln=da_subln[l],
                 hg_norm=hg_norm[l], w_branch=w_branch[l], w_out=w_out[l], moe_router=moe_router[l],
                 moe_w1=moe_w1, moe_w3=moe_w3, moe_w2=moe_w2, layer=l)
        mod_x = mod[l, :b].reshape(b, 6, d)
        mod_c = mod[l, b].reshape(6, d)
        lam_init = 0.8 - 0.6 * math.exp(-0.3 * l)
        xa = _layer(xa, nc, mod_x, mod_c, p, lam_init, hg_lb[l], ct, st, with_ctx=l < depth - 1)
    return xa[:, nc:]
```

```python
import functools
import math

import jax
import jax.numpy as jnp
from jax import lax
from jax.experimental import pallas as pl
from jax.experimental.pallas import tpu as pltpu

F32 = jnp.float32
BF16 = jnp.bfloat16
HIGHEST = lax.Precision.HIGHEST

GRID_W = 64
RW_HEAD_DIM = 64
RW_DECAY_RANK = 64
RW_ICL_RANK = 64
RW_GATE_RANK = 128
RW_GN_EPS = 64e-5
DA_QK_DIM = 64
DA_V_DIM = 128
ROPE_THETA = 10000.0
ROPE_AXIS_FREQS = 16
DA_SUBLN_EPS = 1e-5
HG_EXPAND = 128
N_EXPERTS = 16
EC_CAPACITY_FACTOR = 2
NORM_EPS = 1e-6

LANES = 128
SUBLANES = 8
V7X_VMEM_BYTES = 64 * 1024 * 1024
HG_CHUNK = 64
RW_CHUNK = 64
RW_GROUP = 256
FFN_OUT_CHUNK = 512


def _cparams(sem, vmem_mb):
    return pltpu.CompilerParams(dimension_semantics=sem, vmem_limit_bytes=int(vmem_mb * 1024 * 1024))


def _sigmoid(x):
    return 1.0 / (1.0 + jnp.exp(-x))


def _sigmoid_t(x):
    return 0.5 * jnp.tanh(0.5 * x) + 0.5


def _silu(x):
    return x * _sigmoid_t(x)


def _mod_kernel(c_ref, w_ref, b_ref, o_ref):
    a = _silu(c_ref[...]).astype(BF16)
    o_ref[0] = jnp.dot(a, w_ref[0].astype(BF16), preferred_element_type=F32) + b_ref[0]


def _modulation(cc, w_ada, b_ada):
    depth, d, n = w_ada.shape
    rows = cc.shape[0]
    tn = 1024
    return pl.pallas_call(
        _mod_kernel,
        out_shape=jax.ShapeDtypeStruct((depth, rows, n), F32),
        grid=(depth, n // tn),
        in_specs=[pl.BlockSpec((rows, d), lambda l, j: (0, 0)),
                  pl.BlockSpec((1, d, tn), lambda l, j: (l, 0, j)),
                  pl.BlockSpec((1, 1, tn), lambda l, j: (l, 0, j))],
        out_specs=pl.BlockSpec((1, rows, tn), lambda l, j: (l, 0, j)),
        compiler_params=_cparams(("arbitrary", "arbitrary"), 40),
        name="adaln_mod",
    )(cc, w_ada, b_ada.reshape(depth, 1, n))


def _norm_mod_kernel(x_ref, g_ref, ms_ref, *rest, with_router):
    x = x_ref[0]
    xn = x * lax.rsqrt(jnp.mean(x * x, axis=-1, keepdims=True) + NORM_EPS) * g_ref[...]
    h = xn * (1.0 + ms_ref[0, 0, 1:2, :]) + ms_ref[0, 0, 0:1, :]
    if with_router:
        rt_ref, h_ref, aff_ref = rest
        h_ref[0] = h.astype(BF16)
        lt = lax.dot_general(rt_ref[...], h, (((1,), (1,)), ((), ())),
                             precision=HIGHEST, preferred_element_type=F32)
        e = jnp.exp(lt - jnp.max(lt, axis=0, keepdims=True))
        aff_ref[0] = e / jnp.sum(e, axis=0, keepdims=True)
    else:
        (h_ref,) = rest
        h_ref[0] = h.astype(BF16)


def _norm_mod(xa, g, msel, nc, router_t=None, row0=0, tl=256):
    b, lt, d = xa.shape
    tl = min(tl, nc)
    nct = nc // tl
    r0 = row0 // tl
    with_router = router_t is not None
    in_specs = [pl.BlockSpec((1, tl, d), lambda bi, i: (bi, i + r0, 0)),
                pl.BlockSpec((1, d), lambda bi, i: (0, 0)),
                pl.BlockSpec((1, 1, 2, d), lambda bi, i: (bi, jnp.where(i + r0 >= nct, 1, 0), 0, 0))]
    args = [xa, g.reshape(1, d), msel]
    out_shape = [jax.ShapeDtypeStruct((b, lt, d), BF16)]
    out_specs = [pl.BlockSpec((1, tl, d), lambda bi, i: (bi, i + r0, 0))]
    if with_router:
        e = router_t.shape[0]
        in_specs.append(pl.BlockSpec((e, d), lambda bi, i: (0, 0)))
        args.append(router_t)
        out_shape.append(jax.ShapeDtypeStruct((b, e, lt), F32))
        out_specs.append(pl.BlockSpec((1, e, tl), lambda bi, i: (bi, 0, i + r0)))
    res = pl.pallas_call(
        functools.partial(_norm_mod_kernel, with_router=with_router),
        out_shape=out_shape, grid=(b, lt // tl - r0), in_specs=in_specs, out_specs=out_specs,
        compiler_params=_cparams(("arbitrary", "arbitrary"), 32),
        name="norm_mod_router" if with_router else "norm_mod",
    )(*args)
    return res if with_router else res[0]


def _matmul_kernel(a_ref, w_ref, o_ref):
    o_ref[...] = jnp.dot(a_ref[...], w_ref[...], preferred_element_type=F32).astype(o_ref.dtype)


def _pick(n, cands):
    for c in cands:
        if n % c == 0:
            return c
    return n


def _matmul(a, w, out_dtype, tm=1024, tn=1024):
    m, k = a.shape
    n = w.shape[1]
    tm = _pick(m, (tm, 768, 512, 384, 256, 128))
    tn = _pick(n, (tn, 768, 512, 384, 256, 128))
    return pl.pallas_call(
        _matmul_kernel,
        out_shape=jax.ShapeDtypeStruct((m, n), out_dtype),
        grid=(m // tm, n // tn),
        in_specs=[pl.BlockSpec((tm, k), lambda i, j: (i, 0)),
                  pl.BlockSpec((k, tn), lambda i, j: (0, j))],
        out_specs=pl.BlockSpec((tm, tn), lambda i, j: (i, j)),
        compiler_params=_cparams(("arbitrary", "arbitrary"), 48),
        name="matmul",
    )(a, w)


def _rw_prep_kernel(z_ref, zp_ref, zn_ref, mu_ref, w0_ref, wb_ref, a0_ref, ab_ref, gb_ref,
                    o_ref, g_ref, *, tl, nct, nt, w):
    i = pl.program_id(1)
    z = z_ref[0]
    row = lax.broadcasted_iota(jnp.int32, (tl, 1), 0)
    seg_start = jnp.logical_or(i == 0, i == nct)
    seg_end = jnp.logical_or(i == nct - 1, i == nt - 1)
    prev_row = jnp.where(seg_start, 0.0, zp_ref[0, SUBLANES - 1:SUBLANES, :])
    next_row = jnp.where(seg_end, 0.0, zn_ref[0, 0:1, :])
    prev = jnp.where(row == 0, prev_row, pltpu.roll(z, 1, 0))
    nxt = jnp.where(row == tl - 1, next_row, pltpu.roll(z, tl - 1, 0))
    zs = z + mu_ref[0:1, :] * (prev - z) + mu_ref[1:2, :] * (nxt - z)
    o_ref[0, 0] = zs[:, 0:w]
    o_ref[0, 1] = zs[:, w:2 * w]
    o_ref[0, 2] = zs[:, 2 * w:3 * w]
    o4 = 3 * w + 2 * RW_DECAY_RANK
    o5 = o4 + 2 * RW_ICL_RANK
    wd = jnp.tanh(zs[:, 3 * w:o4])
    wl = jnp.dot(wd.astype(BF16), wb_ref[...], preferred_element_type=F32) + w0_ref[...]
    dec = -math.exp(-0.5) * _sigmoid_t(wl)
    o_ref[0, 3] = dec[:, 0:w]
    o_ref[0, 4] = dec[:, w:2 * w]
    al = jnp.dot(zs[:, o4:o5].astype(BF16), ab_ref[...], preferred_element_type=F32) + a0_ref[...]
    av = _sigmoid_t(al)
    o_ref[0, 5] = av[:, 0:w]
    o_ref[0, 6] = av[:, w:2 * w]
    g_ref[0] = jnp.dot(_sigmoid_t(zs[:, o5:]).astype(BF16), gb_ref[...], preferred_element_type=F32)


def _blockdiag2(m):
    r, w = m.shape[1], m.shape[2]
    z = jnp.zeros((r, w), m.dtype)
    return jnp.concatenate([jnp.concatenate([m[0], z], axis=1), jnp.concatenate([z, m[1]], axis=1)], axis=0)


def _rw_prep(za, nc, mu, w0, wb, a0, ab, gb, tl=256):
    b, lt, acols = za.shape
    w = w0.shape[1]
    tl = min(tl, nc)
    nct, nt = nc // tl, lt // tl
    r8 = tl // SUBLANES
    nb8 = lt // SUBLANES
    full = lambda shp: pl.BlockSpec(shp, lambda bi, i: tuple(0 for _ in shp))
    kern = functools.partial(_rw_prep_kernel, tl=tl, nct=nct, nt=nt, w=w)
    return pl.pallas_call(
        kern,
        out_shape=[jax.ShapeDtypeStruct((b, 7, lt, w), F32), jax.ShapeDtypeStruct((b, lt, w), F32)],
        grid=(b, nt),
        in_specs=[pl.BlockSpec((1, tl, acols), lambda bi, i: (bi, i, 0)),
                  pl.BlockSpec((1, SUBLANES, acols), lambda bi, i: (bi, jnp.maximum(i * r8 - 1, 0), 0)),
                  pl.BlockSpec((1, SUBLANES, acols), lambda bi, i: (bi, jnp.minimum((i + 1) * r8, nb8 - 1), 0)),
                  full((2, acols)), full((1, 2 * w)), full((2 * RW_DECAY_RANK, 2 * w)),
                  full((1, 2 * w)), full((2 * RW_ICL_RANK, 2 * w)), full((RW_GATE_RANK, w))],
        out_specs=[pl.BlockSpec((1, 7, tl, w), lambda bi, i: (bi, 0, i, 0)),
                   pl.BlockSpec((1, tl, w), lambda bi, i: (bi, i, 0))],
        compiler_params=_cparams(("arbitrary", "arbitrary"), 48),
        name="rwkv_prep",
    )(za, za, za, mu, w0.reshape(1, 2 * w), _blockdiag2(wb).astype(BF16), a0.reshape(1, 2 * w),
      _blockdiag2(ab).astype(BF16), gb.astype(BF16))


def _rw_chunk_kernel(xf_ref, lwf_ref, af_ref, xb_ref, lwb_ref, ab_ref, kk_ref, ka_ref, rk_ref, gn0_ref, gn1_ref,
                     of_ref, ob_ref, s_ref, *, c, w):
    j = pl.program_id(1)

    @pl.when(j == 0)
    def _():
        s_ref[...] = jnp.zeros_like(s_ref)

    gw, n = RW_GROUP, RW_HEAD_DIM
    ng = w // gw
    row = lax.broadcasted_iota(jnp.int32, (c, 1), 0)
    pos = lax.broadcasted_iota(jnp.int32, (1, gw), 1) % n
    bdmask = (lax.broadcasted_iota(jnp.int32, (gw, 1), 0) // n) == (lax.broadcasted_iota(jnp.int32, (1, gw), 1) // n)
    ones_bd = jnp.where(bdmask, 1.0, 0.0).astype(BF16)
    eye = jnp.where(pos == row, 1.0, 0.0)
    nt = (((1,), (1,)), ((), ()))
    tn = (((0,), (0,)), ((), ()))
    chains = [(d, g) for d in range(2) for g in range(ng)]
    refs = ((xf_ref, lwf_ref, af_ref, of_ref), (xb_ref, lwb_ref, ab_ref, ob_ref))

    def bd(x):
        return jnp.where(bdmask, jnp.concatenate([x] * (gw // c), axis=0), 0.0).astype(BF16)

    def gsum(xs):
        pieces = []
        for x in xs:
            hi = x.astype(BF16)
            r1 = x - hi.astype(F32)
            mid = r1.astype(BF16)
            pieces += [hi, mid, (r1 - mid.astype(F32)).astype(BF16)]
        tot = jnp.dot(jnp.concatenate(pieces, axis=0), ones_bd, preferred_element_type=F32)
        return [tot[3 * i * c:(3 * i + 1) * c] + tot[(3 * i + 1) * c:(3 * i + 2) * c] + tot[(3 * i + 2) * c:(3 * i + 3) * c]
                for i in range(len(xs))]

    def mm(x, ybd):
        return jnp.dot(x.astype(BF16), ybd, preferred_element_type=F32)

    st = []
    for d, g in chains:
        x_ref, lw_ref, a_ref, _ = refs[d]
        sl = slice(g * gw, (g + 1) * gw)
        q = dict(sl=sl, rev=d == 1, r=x_ref[0, 0, :, sl], k=x_ref[0, 1, :, sl], v=x_ref[0, 2, :, sl],
                 lw=lw_ref[0, 0, :, sl], a=a_ref[0, 0, :, sl])
        q['kx'] = q['k'] * kk_ref[:, sl]
        q['kt'] = q['k'] * (1.0 + (q['a'] - 1.0) * ka_ref[:, sl])
        st.append(q)
    for q in st:
        q['ss'], q['bon'] = gsum([q['kx'] * q['kx'], q['r'] * q['kt'] * rk_ref[:, q['sl']]])
    for q in st:
        rev = q['rev']
        kk = q['kx'] / jnp.maximum(jnp.sqrt(q['ss']), 1e-12)
        q['b'] = q['a'] * kk
        cw = q['lw']
        sft = 1
        while sft < c:
            if rev:
                cw = cw + jnp.where(row < c - sft, pltpu.roll(cw, c - sft, 0), 0.0)
            else:
                cw = cw + jnp.where(row >= sft, pltpu.roll(cw, sft, 0), 0.0)
            sft *= 2
        q['tot'] = cw[0:1] if rev else cw[c - 1:c]
        invp = jnp.exp(-cw)
        q['e2'] = jnp.exp(q['tot'] - cw)
        q['kr'] = jnp.concatenate([kk * jnp.exp(cw - q['lw']), q['r'] * jnp.exp(cw)], axis=0).astype(BF16)
        q['ktb'] = bd(q['kt'] * invp)
        q['bb'] = bd(q['b'] * invp)
        q['vbd'] = bd(q['v'])
    for q in st:
        strict = (pos > row) if q['rev'] else (pos < row)
        incl = (pos >= row) if q['rev'] else (pos <= row)
        g1 = lax.dot_general(q['kr'], q['ktb'], nt, preferred_element_type=F32)
        g2 = lax.dot_general(q['kr'], q['bb'], nt, preferred_element_type=F32)
        q['m_kt'] = jnp.where(strict, g1[:c], 0.0)
        q['n_kt'] = jnp.where(incl, g1[c:], 0.0)
        q['x'] = jnp.where(strict, g2[:c], 0.0)
        q['n_b'] = jnp.where(incl, g2[c:], 0.0)
        q['t'] = eye - q['x']
        q['xb'] = bd(q['x'])
    for q in st:
        mn = mm(jnp.concatenate([q['m_kt'], q['n_kt']], axis=0), q['vbd'])
        q['mkv'], q['nkv'] = mn[:c], mn[c:]
    lv = 2
    while lv < c:
        for q in st:
            xt = mm(jnp.concatenate([q['x'], q['t']], axis=0), q['xb'])
            q['x'], q['tx'] = xt[:c], xt[c:]
        for q in st:
            q['t'] = q['t'] + mm(q['tx'], q['xb'])
        lv *= 2
        if lv < c:
            for q in st:
                q['xb'] = bd(q['x'])
    for (d, g), q in zip(chains, st):
        q['s0'] = s_ref[d, g]
        q['ka'] = lax.dot_general(q['kr'], q['s0'].astype(BF16), nt, preferred_element_type=F32)
    for q in st:
        q['u'] = mm(q['t'], bd(q['ka'][:c] + q['mkv']))
    for (d, g), q in zip(chains, st):
        e2 = q['e2']
        upd = lax.dot_general(jnp.concatenate([q['v'], q['u']], axis=0).astype(BF16),
                              jnp.concatenate([q['kt'] * e2, -(q['b'] * e2)], axis=0).astype(BF16),
                              tn, preferred_element_type=F32)
        s_ref[d, g] = q['s0'] * jnp.exp(q['tot']) + jnp.where(bdmask, upd, 0.0)
    for q in st:
        q['y'] = q['ka'][c:] + q['nkv'] - mm(q['n_b'], bd(q['u']))
    for q in st:
        (mu,) = gsum([q['y']])
        q['dl'] = q['y'] - mu * (1.0 / n)
    for (d, g), q in zip(chains, st):
        (var,) = gsum([q['dl'] * q['dl']])
        sl = q['sl']
        yn = q['dl'] * lax.rsqrt(var * (1.0 / n) + RW_GN_EPS) * gn0_ref[:, sl] + gn1_ref[:, sl]
        refs[d][3][0, :, sl] = yn + q['bon'] * q['v']


def _rw_chunk(streams, nc, kk, ka, rk, gn):
    b, _, lt, w = streams.shape
    c = RW_CHUNK
    ncb, nb = nc // c, lt // c

    def rblk(j):
        return jnp.where(j < ncb, ncb - 1 - j, nb - 1 - (j - ncb))

    par = pl.BlockSpec((1, w), lambda bi, j: (0, 0))
    shp = jax.ShapeDtypeStruct((b, lt, w), F32)
    return pl.pallas_call(
        functools.partial(_rw_chunk_kernel, c=c, w=w),
        out_shape=[shp, shp],
        grid=(b, nb),
        in_specs=[pl.BlockSpec((1, 3, c, w), lambda bi, j: (bi, 0, j, 0)),
                  pl.BlockSpec((1, 1, c, w), lambda bi, j: (bi, 3, j, 0)),
                  pl.BlockSpec((1, 1, c, w), lambda bi, j: (bi, 5, j, 0)),
                  pl.BlockSpec((1, 3, c, w), lambda bi, j: (bi, 0, rblk(j), 0)),
                  pl.BlockSpec((1, 1, c, w), lambda bi, j: (bi, 4, rblk(j), 0)),
                  pl.BlockSpec((1, 1, c, w), lambda bi, j: (bi, 6, rblk(j), 0)),
                  par, par, par, par, par],
        out_specs=[pl.BlockSpec((1, c, w), lambda bi, j: (bi, j, 0)),
                   pl.BlockSpec((1, c, w), lambda bi, j: (bi, rblk(j), 0))],
        scratch_shapes=[pltpu.VMEM((2, w // RW_GROUP, RW_GROUP, RW_GROUP), F32)],
        compiler_params=_cparams(("arbitrary", "arbitrary"), 32),
        name="rwkv_chunk",
    )(streams, streams, streams, streams, streams, streams, kk.reshape(1, w), ka.reshape(1, w), rk.reshape(1, w),
      gn[0].reshape(1, w), gn[1].reshape(1, w))


def _proj_rope_kernel(a_ref, w_ref, c_ref, s_ref, o_ref):
    j = pl.program_id(1)
    acc = jnp.dot(a_ref[...], w_ref[...], preferred_element_type=F32)

    @pl.when(j < 2)
    def _():
        lane = lax.broadcasted_iota(jnp.int32, (1, LANES), 1)
        first_half = (lane % (2 * ROPE_AXIS_FREQS)) < ROPE_AXIS_FREQS
        c = c_ref[...] * jnp.where(j == 0, DA_QK_DIM ** -0.5 * math.log2(math.e), 1.0)
        s = s_ref[...] * jnp.where(j == 0, DA_QK_DIM ** -0.5 * math.log2(math.e), 1.0)
        for g in range(acc.shape[1] // LANES):
            xs = acc[:, g * LANES:(g + 1) * LANES]
            up = pltpu.roll(xs, LANES - ROPE_AXIS_FREQS, 1)
            dn = pltpu.roll(xs, ROPE_AXIS_FREQS, 1)
            o_ref[:, g * LANES:(g + 1) * LANES] = (xs * c + jnp.where(first_half, up, dn) * s).astype(o_ref.dtype)

    @pl.when(j >= 2)
    def _():
        o_ref[...] = acc.astype(o_ref.dtype)


def _proj_rope(a, w, ct, st, lt):
    m, k = a.shape
    n = w.shape[1]
    tn = n // 3
    tm = _pick(lt, (768, 640, 512, 384, 256, 128))
    per = lt // tm
    return pl.pallas_call(
        _proj_rope_kernel,
        out_shape=jax.ShapeDtypeStruct((m, n), BF16),
        grid=(m // tm, 3),
        in_specs=[pl.BlockSpec((tm, k), lambda i, j: (i, 0)),
                  pl.BlockSpec((k, tn), lambda i, j: (0, j)),
                  pl.BlockSpec((tm, LANES), lambda i, j: (i % per, 0)),
                  pl.BlockSpec((tm, LANES), lambda i, j: (i % per, 0))],
        out_specs=pl.BlockSpec((tm, tn), lambda i, j: (i, j)),
        compiler_params=_cparams(("arbitrary", "arbitrary"), 48),
        name="proj_rope",
    )(a, w, ct, st)


def _rope_tables(seq, nc):
    rows = seq // GRID_W
    row = jnp.broadcast_to(jnp.arange(rows)[:, None], (rows, GRID_W)).reshape(seq)
    col = jnp.broadcast_to(jnp.arange(GRID_W)[None, :], (rows, GRID_W)).reshape(seq)
    inv = 1.0 / (ROPE_THETA ** (jnp.arange(ROPE_AXIS_FREQS, dtype=F32) / ROPE_AXIS_FREQS))
    ang = jnp.stack([row, col], axis=-1).astype(F32)[:, :, None] * inv
    cos, sin = jnp.cos(ang), jnp.sin(ang)
    c64 = jnp.concatenate([cos[:, 0], cos[:, 0], cos[:, 1], cos[:, 1]], axis=-1)
    s64 = jnp.concatenate([-sin[:, 0], sin[:, 0], -sin[:, 1], sin[:, 1]], axis=-1)
    ct = jnp.concatenate([jnp.ones((nc, LANES), F32), jnp.tile(c64, (1, 2))], axis=0)
    st = jnp.concatenate([jnp.zeros((nc, LANES), F32), jnp.tile(s64, (1, 2))], axis=0)
    return ct, st


def _attn_kernel(q_ref, k_ref, v_ref, lam_ref, g_ref, o_ref, *, nct, nc, lam_init, hp, qoff):
    i = pl.program_id(2) + qoff
    lane = lax.broadcasted_iota(jnp.int32, (1, LANES), 1)
    lp = lam_ref[...]
    lam = (jnp.exp(jnp.sum(lp[0:1] * lp[1:2], axis=1, keepdims=True))
           - jnp.exp(jnp.sum(lp[2:3] * lp[3:4], axis=1, keepdims=True)) + lam_init)

    def attend(nk):
        for h in range(hp):
            cols = slice(h * DA_V_DIM, (h + 1) * DA_V_DIM)
            q = q_ref[0, :, cols]
            k = k_ref[0, 0:nk, cols]
            v1 = jnp.concatenate([v_ref[0, 0:nk, cols], jnp.ones((nk, LANES), BF16)], axis=1)

            def one(sel):
                qs = jnp.where(sel, q, jnp.zeros_like(q))
                s = lax.dot_general(qs, k, (((1,), (1,)), ((), ())), preferred_element_type=F32)
                e = jnp.exp2(s - jnp.max(s, axis=-1, keepdims=True))
                ov = jnp.dot(e.astype(BF16), v1, preferred_element_type=F32)
                return ov[:, :DA_V_DIM] / ov[:, DA_V_DIM:]

            o = one(lane < DA_QK_DIM) - lam * one(lane >= DA_QK_DIM)
            on = o * lax.rsqrt(jnp.mean(o * o, axis=-1, keepdims=True) + DA_SUBLN_EPS) * g_ref[...]
            o_ref[0, :, cols] = (on * (1.0 - lam_init)).astype(o_ref.dtype)

    @pl.when(i < nct)
    def _():
        attend(nc)

    @pl.when(i >= nct)
    def _():
        attend(k_ref.shape[1])


def _attention(zbr, nc, lam_p, subln, lam_init, with_ctx, hp=2):
    b, lt, cols = zbr.shape
    w = cols // 3
    heads = w // DA_V_DIM
    tq = min(256, nc)
    nct = nc // tq
    hg = heads // hp
    bw = hp * DA_V_DIM
    qoff = 0 if with_ctx else nct
    kern = functools.partial(_attn_kernel, nct=nct, nc=nc, lam_init=lam_init, hp=hp, qoff=qoff)
    return pl.pallas_call(
        kern,
        out_shape=jax.ShapeDtypeStruct((b, lt, w), BF16),
        grid=(b, hg, lt // tq - qoff),
        in_specs=[pl.BlockSpec((1, tq, bw), lambda bi, h, i: (bi, i + qoff, h)),
                  pl.BlockSpec((1, lt, bw), lambda bi, h, i: (bi, 0, hg + h)),
                  pl.BlockSpec((1, lt, bw), lambda bi, h, i: (bi, 0, 2 * hg + h)),
                  pl.BlockSpec((4, DA_QK_DIM), lambda bi, h, i: (0, 0)),
                  pl.BlockSpec((1, DA_V_DIM), lambda bi, h, i: (0, 0))],
        out_specs=pl.BlockSpec((1, tq, bw), lambda bi, h, i: (bi, i + qoff, h)),
        compiler_params=_cparams(("arbitrary", "arbitrary", "arbitrary"), 48),
        name="diff_attn",
    )(zbr, zbr, zbr, lam_p, subln.reshape(1, DA_V_DIM))


def _seg_masks(c, e):
    row = lax.broadcasted_iota(jnp.int32, (c, e), 0)
    mk = {}
    h = 1
    while h < min(SUBLANES, c):
        odd = (row % (2 * h)) >= h
        for s in range(1, h + 1):
            mk[(h, s, 0)] = jnp.where(jnp.logical_and(odd, (row % h) == s - 1), 1.0, 0.0)
            mk[(h, s, 1)] = jnp.where(jnp.logical_and(~odd, (row % h) == h - s), 1.0, 0.0)
        h *= 2
    return mk


def _seg_scans(lf, mk, c):
    cs, ss = {1: lf}, {1: lf}
    h = 1
    while h < c:
        x, y = cs[h], ss[h]
        if h < SUBLANES:
            x3 = x.reshape(c // SUBLANES, SUBLANES, x.shape[1])
            y3 = y.reshape(c // SUBLANES, SUBLANES, y.shape[1])
            for s in range(1, h + 1):
                x = x + mk[(h, s, 0)] * pltpu.roll(x3, s, 1).reshape(x.shape)
                y = y + mk[(h, s, 1)] * pltpu.roll(y3, SUBLANES - s, 1).reshape(y.shape)
            cs[2 * h], ss[2 * h] = x, y
        else:
            px = [x[j * h:(j + 1) * h] for j in range(c // h)]
            py = [y[j * h:(j + 1) * h] for j in range(c // h)]
            nx = [px[j] + px[j - 1][h - 1:h] if j % 2 == 1 else px[j] for j in range(c // h)]
            ny = [py[j] + py[j + 1][0:1] if j % 2 == 0 else py[j] for j in range(c // h)]
            cs[2 * h], ss[2 * h] = jnp.concatenate(nx, axis=0), jnp.concatenate(ny, axis=0)
        h *= 2
    return cs, ss


def _hg_kernel(q_ref, f_ref, v_ref, lb_ref, o_ref, st_ref, *, tc, c, heads, rev):
    j = pl.program_id(1)

    @pl.when(j == 0)
    def _():
        st_ref[...] = jnp.zeros_like(st_ref)

    ti = lax.broadcasted_iota(jnp.int32, (c, c), 0)
    si = lax.broadcasted_iota(jnp.int32, (c, c), 1)
    nt = (((1,), (1,)), ((), ()))
    tn = (((0,), (0,)), ((), ()))
    mk = _seg_masks(c, HG_EXPAND)
    masks, hh = {}, 1
    while hh < c:
        tb, sb = ti // hh, si // hh
        masks[hh] = (jnp.logical_and(tb % 2 == 0, sb == tb + 1) if rev
                     else jnp.logical_and(tb % 2 == 1, sb == tb - 1))
        hh *= 2

    def head(h, carry):
        cols = pl.ds(pl.multiple_of(h * HG_EXPAND, HG_EXPAND), HG_EXPAND)
        lb = lb_ref[:, cols]
        chunks = range(tc // c)
        for ci in (reversed(chunks) if rev else chunks):
            rows = slice(ci * c, (ci + 1) * c)
            q = _silu(q_ref[0, rows, cols])
            f = lb + (1.0 - lb) * _sigmoid(f_ref[0, rows, cols])
            k = 1.0 - f
            lf = jnp.log(f)
            v = v_ref[0, rows, cols].astype(BF16)
            cs, ss = _seg_scans(lf, mk, c)
            qs, ks = (ss, cs) if rev else (cs, ss)
            att = jnp.where(ti == si, lax.dot_general(q.astype(BF16), k.astype(BF16), nt,
                                                      preferred_element_type=F32), 0.0)
            hh = 1
            while hh < c:
                qe = (q * jnp.exp(qs[hh])).astype(BF16)
                ke = (k * jnp.exp(ks[hh] - lf)).astype(BF16)
                a = lax.dot_general(qe, ke, nt, preferred_element_type=F32)
                att = att + jnp.where(masks[hh], a, 0.0)
                hh *= 2
            st = st_ref[h]
            qe = (q * jnp.exp(qs[c])).astype(BF16)
            o = lax.dot_general(qe, st.astype(BF16), nt, preferred_element_type=F32)
            o = o + jnp.dot(att.astype(BF16), v, preferred_element_type=F32)
            o_ref[0, rows, cols] = o
            ke = (k * jnp.exp(ks[c] - lf)).astype(BF16)
            tot = qs[c][0:1] if rev else qs[c][c - 1:c]
            st_ref[h] = st * jnp.exp(tot) + lax.dot_general(v, ke, tn, preferred_element_type=F32)
        return carry

    lax.fori_loop(0, heads, head, 0, unroll=4)


def _hgrn(zc, nc, lb, rev):
    b, lt, cols = zc.shape
    w = cols // 5
    heads = w // HG_EXPAND
    tc = min(256, nc)
    c = min(HG_CHUNK, tc)
    ncb, nb = nc // tc, lt // tc

    def blk(j):
        return jnp.where(j < ncb, ncb - 1 - j, nb - 1 - (j - ncb)) if rev else j

    fcol = 2 if rev else 1
    return pl.pallas_call(
        functools.partial(_hg_kernel, tc=tc, c=c, heads=heads, rev=rev),
        out_shape=jax.ShapeDtypeStruct((b, lt, w), F32),
        grid=(b, nb),
        in_specs=[pl.BlockSpec((1, tc, w), lambda bi, j: (bi, blk(j), 0)),
                  pl.BlockSpec((1, tc, w), lambda bi, j: (bi, blk(j), fcol)),
                  pl.BlockSpec((1, tc, w), lambda bi, j: (bi, blk(j), 3)),
                  pl.BlockSpec((1, w), lambda bi, j: (0, 0))],
        out_specs=pl.BlockSpec((1, tc, w), lambda bi, j: (bi, blk(j), 0)),
        scratch_shapes=[pltpu.VMEM((heads, HG_EXPAND, HG_EXPAND), F32)],
        compiler_params=_cparams(("arbitrary", "arbitrary"), 32),
        name="hgrn2_bwd" if rev else "hgrn2_fwd",
    )(zc, zc, zc, lb.reshape(1, w))


def _merge_kernel(oaf_ref, oab_ref, ga_ref, yb_ref, of_ref, ob_ref, gc_ref, hn_ref, gl0_ref, gl1_ref, gl2_ref, wb_ref,
                  o_ref, *, heads):
    gls = (gl0_ref, gl1_ref, gl2_ref)
    ya = (oaf_ref[0] + oab_ref[0]) * ga_ref[0]
    oc = of_ref[0] + ob_ref[0]
    parts = []
    for h in range(heads):
        x = oc[:, h * HG_EXPAND:(h + 1) * HG_EXPAND]
        parts.append(x * lax.rsqrt(jnp.mean(x * x, axis=-1, keepdims=True) + NORM_EPS))
    yc = jnp.concatenate(parts, axis=-1) * hn_ref[...] * _silu(gc_ref[0])
    acc = None
    for bi, y in enumerate((ya.astype(BF16), yb_ref[0], yc.astype(BF16))):
        term = _sigmoid_t(gls[bi][0].astype(F32)) * jnp.dot(y, wb_ref[bi], preferred_element_type=F32)
        acc = term if acc is None else acc + term
    o_ref[0] = acc.astype(BF16)


def _merge(oaf, oab, ga, yb, of, ob, zc, hn, gl, wbr, row0, tl=128, tn=2048):
    b, lt, w = ga.shape
    d = wbr.shape[2]
    tl = _pick(math.gcd(lt, row0), (tl, 64))
    r0 = row0 // tl
    heads = w // HG_EXPAND
    nn = d // tn
    row = lambda n, bi, i: (bi, i + r0, 0)
    glspec = lambda br: pl.BlockSpec((1, tl, tn), lambda n, bi, i: (bi, i + r0, br * nn + n))
    return pl.pallas_call(
        functools.partial(_merge_kernel, heads=heads),
        out_shape=jax.ShapeDtypeStruct((b, lt, d), BF16),
        grid=(d // tn, b, lt // tl - r0),
        in_specs=[pl.BlockSpec((1, tl, w), row),
                  pl.BlockSpec((1, tl, w), row),
                  pl.BlockSpec((1, tl, w), row),
                  pl.BlockSpec((1, tl, w), row),
                  pl.BlockSpec((1, tl, w), row),
                  pl.BlockSpec((1, tl, w), row),
                  pl.BlockSpec((1, tl, w), lambda n, bi, i: (bi, i + r0, 4)),
                  pl.BlockSpec((1, w), lambda n, bi, i: (0, 0)),
                  glspec(0), glspec(1), glspec(2),
                  pl.BlockSpec((3, w, tn), lambda n, bi, i: (0, 0, n))],
        out_specs=pl.BlockSpec((1, tl, tn), lambda n, bi, i: (bi, i + r0, n)),
        compiler_params=_cparams(("arbitrary", "arbitrary", "arbitrary"), 48),
        name="branch_merge",
    )(oaf, oab, ga, yb, of, ob, zc, hn, gl, gl, gl, wbr)


def _proj_post_kernel(a_ref, w_ref, x_ref, g_ref, gate_ref, o_ref):
    mx = jnp.dot(a_ref[0], w_ref[...], preferred_element_type=F32)
    mn = mx * lax.rsqrt(jnp.mean(mx * mx, axis=-1, keepdims=True) + NORM_EPS) * g_ref[...]
    o_ref[0] = x_ref[0] + gate_ref[0, 0] * mn


def _proj_post(a, wout, xa, g, gsel, nc, row0, tl=256):
    b, lt, d = xa.shape
    tl = min(tl, nc)
    nct = nc // tl
    r0 = row0 // tl
    return pl.pallas_call(
        _proj_post_kernel,
        out_shape=jax.ShapeDtypeStruct((b, lt, d), F32),
        grid=(b, lt // tl - r0),
        in_specs=[pl.BlockSpec((1, tl, d), lambda bi, i: (bi, i + r0, 0)),
                  pl.BlockSpec((d, d), lambda bi, i: (0, 0)),
                  pl.BlockSpec((1, tl, d), lambda bi, i: (bi, i + r0, 0)),
                  pl.BlockSpec((1, d), lambda bi, i: (0, 0)),
                  pl.BlockSpec((1, 1, 1, d), lambda bi, i: (bi, jnp.where(i + r0 >= nct, 1, 0), 0, 0))],
        out_specs=pl.BlockSpec((1, tl, d), lambda bi, i: (bi, i + r0, 0)),
        input_output_aliases={2: 0},
        compiler_params=_cparams(("arbitrary", "arbitrary"), 48),
        name="out_proj_post",
    )(a, wout, xa, g.reshape(1, d), gsel)


def _rank_kernel(afft_ref, slot_ref, *, n, cap, tw):
    aff = afft_ref[0]
    bits = lax.bitcast_convert_type(aff, jnp.int32)
    e_num = aff.shape[0]

    def count(mask):
        return jnp.sum(jnp.where(mask, 1.0, 0.0), axis=1, keepdims=True)

    def bit_step(i, thr):
        cand = thr | lax.shift_left(jnp.int32(1), 30 - i)
        return jnp.where(count(bits >= cand) >= cap, cand, thr)

    thr = lax.fori_loop(0, 31, bit_step, jnp.zeros((e_num, 1), jnp.int32))
    gt = bits > thr
    eq = bits == thr
    need = cap - count(gt)
    tri = jnp.where(lax.broadcasted_iota(jnp.int32, (tw, tw), 0) < lax.broadcasted_iota(jnp.int32, (tw, tw), 1),
                    1.0, 0.0).astype(BF16)

    def prefix(mask):
        parts, carry = [], jnp.zeros((e_num, 1), F32)
        for j in range(n // tw):
            m = jnp.where(mask[:, j * tw:(j + 1) * tw], 1.0, 0.0)
            parts.append(jnp.dot(m.astype(BF16), tri, preferred_element_type=F32) + carry)
            carry = carry + jnp.sum(m, axis=1, keepdims=True)
        return jnp.concatenate(parts, axis=1)

    sel = jnp.logical_or(gt, jnp.logical_and(eq, prefix(eq) < need))
    slot_ref[0] = jnp.where(sel, prefix(sel), float(cap)).astype(jnp.int32)


def _rank(afft, cap):
    b, e_num, n = afft.shape
    tw = min(LANES, n)
    return pl.pallas_call(
        functools.partial(_rank_kernel, n=n, cap=cap, tw=tw),
        out_shape=jax.ShapeDtypeStruct((b, e_num, n), jnp.int32),
        grid=(b,),
        in_specs=[pl.BlockSpec((1, e_num, n), lambda bi: (bi, 0, 0))],
        out_specs=pl.BlockSpec((1, e_num, n), lambda bi: (bi, 0, 0)),
        compiler_params=_cparams(("arbitrary",), 32),
        name="ec_rank",
    )(afft)


def _gather_kernel(slot_ref, afft_ref, h_ref, o_ref, g_ref, *, cap):
    e = pl.program_id(1)
    sl = slot_ref[0, pl.ds(e, 1), :]
    hit = lax.broadcasted_iota(jnp.int32, (cap, 1), 0) == sl
    o_ref[0] = jnp.dot(jnp.where(hit, 1.0, 0.0).astype(BF16), h_ref[0], preferred_element_type=F32).astype(BF16)
    gate = jnp.sum(jnp.where(hit, afft_ref[0, pl.ds(e, 1), :], 0.0), axis=1, keepdims=True)
    g_ref[0] = jnp.broadcast_to(gate, (cap, LANES))


def _gather(slot, afft, hs, cap):
    b, e_num, n = slot.shape
    d = hs.shape[2]
    return pl.pallas_call(
        functools.partial(_gather_kernel, cap=cap),
        out_shape=[jax.ShapeDtypeStruct((e_num, b * cap, d), BF16),
                   jax.ShapeDtypeStruct((e_num, b * cap, LANES), F32)],
        grid=(b, e_num),
        in_specs=[pl.BlockSpec((1, e_num, n), lambda bi, e: (bi, 0, 0)),
                  pl.BlockSpec((1, e_num, n), lambda bi, e: (bi, 0, 0)),
                  pl.BlockSpec((1, n, d), lambda bi, e: (bi, 0, 0))],
        out_specs=[pl.BlockSpec((1, cap, d), lambda bi, e: (e, bi, 0)),
                   pl.BlockSpec((1, cap, LANES), lambda bi, e: (e, bi, 0))],
        compiler_params=_cparams(("arbitrary", "arbitrary"), 40),
        name="ec_gather",
    )(slot, afft, hs)


def _ffn_kernel(x_ref, g_ref, w1_ref, w3_ref, w2_ref, o_ref, acc_ref):
    f = pl.program_id(2)
    x = x_ref[0]
    h1 = jnp.dot(x, w1_ref[0, 0].astype(BF16), preferred_element_type=F32)
    h3 = jnp.dot(x, w3_ref[0, 0].astype(BF16), preferred_element_type=F32)
    hid = (_silu(h1) * h3).astype(BF16)

    @pl.when(f == 0)
    def _():
        acc_ref[...] = jnp.zeros_like(acc_ref)

    d = acc_ref.shape[1]
    for n0 in range(0, d, FFN_OUT_CHUNK):
        cols = slice(n0, n0 + FFN_OUT_CHUNK)
        acc_ref[:, cols] += jnp.dot(hid, w2_ref[0, 0, :, cols].astype(BF16), preferred_element_type=F32)

    @pl.when(f == pl.num_programs(2) - 1)
    def _():
        o_ref[0] = (acc_ref[...] * g_ref[0, :, 0:1]).astype(BF16)


def _ffn(xe, gate, w1, w3, w2, layer, tf=256):
    e_num, m, d = xe.shape
    ff = w1.shape[3]
    tm = m
    once = pl.Buffered(1)
    return pl.pallas_call(
        _ffn_kernel,
        out_shape=jax.ShapeDtypeStruct((e_num, m, d), BF16),
        grid=(e_num, m // tm, ff // tf),
        in_specs=[pl.BlockSpec((1, tm, d), lambda e, i, f: (e, i, 0), pipeline_mode=once),
                  pl.BlockSpec((1, tm, LANES), lambda e, i, f: (e, i, 0), pipeline_mode=once),
                  pl.BlockSpec((1, 1, d, tf), lambda e, i, f: (layer, e, 0, f)),
                  pl.BlockSpec((1, 1, d, tf), lambda e, i, f: (layer, e, 0, f)),
                  pl.BlockSpec((1, 1, tf, d), lambda e, i, f: (layer, e, f, 0))],
        out_specs=pl.BlockSpec((1, tm, d), lambda e, i, f: (e, i, 0), pipeline_mode=once),
        scratch_shapes=[pltpu.VMEM((tm, d), F32)],
        compiler_params=_cparams(("arbitrary", "arbitrary", "arbitrary"), 58),
        name="ec_ffn",
    )(xe, gate, w1, w3, w2)


def _combine_kernel(slot_ref, ye_ref, x_ref, g_ref, gate_ref, o_ref, *, cap, e_num):
    lane = lax.broadcasted_iota(jnp.int32, (1, e_num), 1)
    pos = lax.broadcasted_iota(jnp.int32, (1, cap), 1)
    sl = slot_ref[0]
    y = None
    for e in range(e_num):
        se = jnp.sum(jnp.where(lane == e, sl, 0), axis=1, keepdims=True)
        pt = jnp.where(se == pos, 1.0, 0.0).astype(BF16)
        part = jnp.dot(pt, ye_ref[e], preferred_element_type=F32)
        y = part if y is None else y + part
    yn = y * lax.rsqrt(jnp.mean(y * y, axis=-1, keepdims=True) + NORM_EPS) * g_ref[...]
    o_ref[0] = x_ref[0] + gate_ref[0, 0] * yn


def _combine(slot_c, ye, xa, g, gsel, seg, row0, slot0, cap, final, tt=256):
    b, n, e_num = slot_c.shape
    d = xa.shape[2]
    tt = _pick(math.gcd(n, row0), (tt, 128, 64))
    assert row0 % tt == 0 and slot0 % cap == 0
    r0, s0 = row0 // tt, slot0 // cap
    return pl.pallas_call(
        functools.partial(_combine_kernel, cap=cap, e_num=e_num),
        out_shape=jax.ShapeDtypeStruct((b, n, d) if final else xa.shape, F32),
        grid=(b, n // tt),
        in_specs=[pl.BlockSpec((1, tt, e_num), lambda bi, j: (bi, j, 0)),
                  pl.BlockSpec((e_num, cap, d), lambda bi, j: (0, s0 + bi, 0)),
                  pl.BlockSpec((1, tt, d), lambda bi, j: (bi, r0 + j, 0)),
                  pl.BlockSpec((1, d), lambda bi, j: (0, 0)),
                  pl.BlockSpec((1, 1, 1, d), lambda bi, j: (bi, seg, 0, 0))],
        out_specs=pl.BlockSpec((1, tt, d), lambda bi, j: (bi, (0 if final else r0) + j, 0)),
        input_output_aliases={} if final else {2: 0},
        compiler_params=_cparams(("arbitrary", "arbitrary"), 56),
        name="ec_combine_post",
    )(slot_c, ye, xa, g.reshape(1, d), gsel)


def _sel(mod_c, mod_x, idx):
    b = mod_x.shape[0]
    mc = jnp.broadcast_to(mod_c[jnp.array(idx)][None], (b, len(idx), mod_c.shape[-1]))
    return jnp.stack([mc, mod_x[:, jnp.array(idx)]], axis=1)


def _layer(xa, nc, mod_x, mod_c, p, lam_init, hg_lb, ct, st, with_ctx):
    b, lt, d = xa.shape
    w = d // 2
    a_cols = 3 * w + 2 * RW_DECAY_RANK + 2 * RW_ICL_RANK + RW_GATE_RANK
    col_b = a_cols
    col_c = col_b + 3 * w
    col_g = col_c + 5 * w

    h = _norm_mod(xa, p['norm_g'][0], _sel(mod_c, mod_x, (0, 1)), nc)
    h2 = h.reshape(b * lt, d)
    w_in = p['w_in']
    za = _matmul(h2, w_in[:, :col_b].astype(BF16), F32, tn=1152).reshape(b, lt, a_cols)
    zc = _matmul(h2, w_in[:, col_c:col_g].astype(BF16), F32).reshape(b, lt, 5 * w)
    gl = _matmul(h2, w_in[:, col_g:].astype(BF16), BF16).reshape(b, lt, 3 * d)

    streams, ga = _rw_prep(za, nc, p['rw_mu'], p['rw_w0'], p['rw_wB'], p['rw_a0'], p['rw_aB'], p['rw_gB'])
    oaf, oab = _rw_chunk(streams, nc, p['rw_kk'], p['rw_ka'], p['rw_rk'], p['rw_gn'])

    zbr = _proj_rope(h2, w_in[:, col_b:col_c].astype(BF16), ct, st, lt).reshape(b, lt, 3 * w)
    yb = _attention(zbr, nc, p['da_lambda'], p['da_subln'], lam_init, with_ctx)

    of = _hgrn(zc, nc, hg_lb, rev=False)
    ob = _hgrn(zc, nc, hg_lb, rev=True)

    hn = jnp.tile(p['hg_norm'], w // HG_EXPAND).reshape(1, w)
    skip = 0 if with_ctx else nc
    merged = _merge(oaf, oab, ga, yb, of, ob, zc, hn, gl, p['w_branch'].astype(BF16), skip)
    xa = _proj_post(merged, p['w_out'].astype(BF16), xa, p['norm_g'][1], _sel(mod_c, mod_x, (2,)), nc, skip)

    hm, afft = _norm_mod(xa, p['norm_g'][2], _sel(mod_c, mod_x, (3, 4)), nc, router_t=p['moe_router'].T, row0=skip)
    gsel = _sel(mod_c, mod_x, (5,))
    sets = [(nc, lt - nc, 1)] + ([(0, nc, 0)] if with_ctx else [])
    routed, xes, gates = [], [], []
    for row0, nn, seg in sets:
        cap = EC_CAPACITY_FACTOR * nn // N_EXPERTS
        at = afft[:, :, row0:row0 + nn]
        slot = _rank(at, cap)
        xe_s, gate_s = _gather(slot, at, hm[:, row0:row0 + nn], cap)
        xes.append(xe_s)
        gates.append(gate_s)
        routed.append((jnp.swapaxes(slot, 1, 2), row0, seg, cap))
    xe = xes[0] if len(xes) == 1 else jnp.concatenate(xes, axis=1)
    gate = gates[0] if len(gates) == 1 else jnp.concatenate(gates, axis=1)
    ye = _ffn(xe, gate, p['moe_w1'], p['moe_w3'], p['moe_w2'], p['layer'])
    slot0 = 0
    for slot_c, row0, seg, cap in routed:
        xa = _combine(slot_c, ye, xa, p['norm_g'][3], gsel, seg, row0, slot0, cap, final=not with_ctx)
        slot0 += b * cap
    return xa


def kernel(x, c, ctx, c_ctx, w_ada, b_ada, norm_g, w_in, rw_mu, rw_w0, rw_wB, rw_a0, rw_aB, rw_gB, rw_kk, rw_ka, rw_rk, rw_gn, da_lambda, da_subln, hg_lb_logits, hg_norm, w_branch, w_out, moe_router, moe_w1, moe_w3, moe_w2):
    b, seq, d = x.shape
    nc = ctx.shape[1]
    depth = w_ada.shape[0]
    ct, st = _rope_tables(seq, nc)
    lb_w = jax.nn.softmax(hg_lb_logits.astype(F32), axis=0)
    hg_lb = jnp.cumsum(lb_w, axis=0) - lb_w[0]
    rows = ((b + 1 + SUBLANES - 1) // SUBLANES) * SUBLANES
    cc = jnp.zeros((rows, d), F32).at[:b].set(c).at[b].set(c_ctx)
    mod = _modulation(cc, w_ada, b_ada)
    xa = jnp.concatenate([ctx, x], axis=1)
    for l in range(depth):
        p = dict(norm_g=norm_g[l], w_in=w_in[l], rw_mu=rw_mu[l], rw_w0=rw_w0[l], rw_wB=rw_wB[l],
                 rw_a0=rw_a0[l], rw_aB=rw_aB[l], rw_gB=rw_gB[l], rw_kk=rw_kk[l], rw_ka=rw_ka[l],
                 rw_rk=rw_rk[l], rw_gn=rw_gn[l], da_lambda=da_lambda[l], da_subln=da_subln[l],
                 hg_norm=hg_norm[l], w_branch=w_branch[l], w_out=w_out[l], moe_router=moe_router[l],
                 moe_w1=moe_w1, moe_w3=moe_w3, moe_w2=moe_w2, layer=l)
        mod_x = mod[l, :b].reshape(b, 6, d)
        mod_c = mod[l, b].reshape(6, d)
        lam_init = 0.8 - 0.6 * math.exp(-0.3 * l)
        xa = _layer(xa, nc, mod_x, mod_c, p, lam_init, hg_lb[l], ct, st, with_ctx=l < depth - 1)
    return xa
```

```python
import functools
import math

import jax
import jax.numpy as jnp
from jax import lax
from jax.experimental import pallas as pl
from jax.experimental.pallas import tpu as pltpu

F32 = jnp.float32
BF16 = jnp.bfloat16
HIGHEST = lax.Precision.HIGHEST

GRID_W = 64
RW_HEAD_DIM = 64
RW_DECAY_RANK = 64
RW_ICL_RANK = 64
RW_GATE_RANK = 128
RW_GN_EPS = 64e-5
DA_QK_DIM = 64
DA_V_DIM = 128
ROPE_THETA = 10000.0
ROPE_AXIS_FREQS = 16
DA_SUBLN_EPS = 1e-5
HG_EXPAND = 128
N_EXPERTS = 16
EC_CAPACITY_FACTOR = 2
NORM_EPS = 1e-6

LANES = 128
SUBLANES = 8
V7X_VMEM_BYTES = 64 * 1024 * 1024
HG_CHUNK = 64
RW_CHUNK = 64
RW_GROUP = 256
FFN_OUT_CHUNK = 512


def _cparams(sem, vmem_mb):
    limit = int(vmem_mb * 1024 * 1024)
    assert limit < V7X_VMEM_BYTES
    return pltpu.CompilerParams(dimension_semantics=sem, vmem_limit_bytes=limit)


def _sigmoid(x):
    return 1.0 / (1.0 + jnp.exp(-x))


def _sigmoid_t(x):
    return 0.5 * jnp.tanh(0.5 * x) + 0.5


def _silu(x):
    return x * _sigmoid_t(x)


def _mod_kernel(c_ref, w_ref, b_ref, o_ref):
    a = _silu(c_ref[...]).astype(BF16)
    o_ref[0] = jnp.dot(a, w_ref[0].astype(BF16), preferred_element_type=F32) + b_ref[0]


def _modulation(cc, w_ada, b_ada):
    depth, d, n = w_ada.shape
    rows = cc.shape[0]
    tn = 1024
    return pl.pallas_call(
        _mod_kernel,
        out_shape=jax.ShapeDtypeStruct((depth, rows, n), F32),
        grid=(depth, n // tn),
        in_specs=[pl.BlockSpec((rows, d), lambda l, j: (0, 0)),
                  pl.BlockSpec((1, d, tn), lambda l, j: (l, 0, j)),
                  pl.BlockSpec((1, 1, tn), lambda l, j: (l, 0, j))],
        out_specs=pl.BlockSpec((1, rows, tn), lambda l, j: (l, 0, j)),
        compiler_params=_cparams(("arbitrary", "arbitrary"), 40),
        name="adaln_mod",
    )(cc, w_ada, b_ada.reshape(depth, 1, n))


def _norm_mod_kernel(x_ref, g_ref, ms_ref, *rest, with_router):
    x = x_ref[0]
    xn = x * lax.rsqrt(jnp.mean(x * x, axis=-1, keepdims=True) + NORM_EPS) * g_ref[...]
    h = xn * (1.0 + ms_ref[0, 0, 1:2, :]) + ms_ref[0, 0, 0:1, :]
    if with_router:
        rt_ref, h_ref, aff_ref = rest
        h_ref[0] = h.astype(BF16)
        lt = lax.dot_general(rt_ref[...], h, (((1,), (1,)), ((), ())),
                             precision=HIGHEST, preferred_element_type=F32)
        e = jnp.exp(lt - jnp.max(lt, axis=0, keepdims=True))
        aff_ref[0] = e / jnp.sum(e, axis=0, keepdims=True)
    else:
        (h_ref,) = rest
        h_ref[0] = h.astype(BF16)


def _norm_mod(xa, g, msel, nc, router_t=None, row0=0, tl=256):
    b, lt, d = xa.shape
    tl = min(tl, nc)
    nct = nc // tl
    r0 = row0 // tl
    with_router = router_t is not None
    in_specs = [pl.BlockSpec((1, tl, d), lambda bi, i: (bi, i + r0, 0)),
                pl.BlockSpec((1, d), lambda bi, i: (0, 0)),
                pl.BlockSpec((1, 1, 2, d), lambda bi, i: (bi, jnp.where(i + r0 >= nct, 1, 0), 0, 0))]
    args = [xa, g.reshape(1, d), msel]
    out_shape = [jax.ShapeDtypeStruct((b, lt, d), BF16)]
    out_specs = [pl.BlockSpec((1, tl, d), lambda bi, i: (bi, i + r0, 0))]
    if with_router:
        e = router_t.shape[0]
        in_specs.append(pl.BlockSpec((e, d), lambda bi, i: (0, 0)))
        args.append(router_t)
        out_shape.append(jax.ShapeDtypeStruct((b, e, lt), F32))
        out_specs.append(pl.BlockSpec((1, e, tl), lambda bi, i: (bi, 0, i + r0)))
    res = pl.pallas_call(
        functools.partial(_norm_mod_kernel, with_router=with_router),
        out_shape=out_shape, grid=(b, lt // tl - r0), in_specs=in_specs, out_specs=out_specs,
        compiler_params=_cparams(("arbitrary", "arbitrary"), 32),
        name="norm_mod_router" if with_router else "norm_mod",
    )(*args)
    return res if with_router else res[0]


def _matmul_kernel(a_ref, w_ref, o_ref):
    o_ref[...] = jnp.dot(a_ref[...], w_ref[...], preferred_element_type=F32).astype(o_ref.dtype)


def _pick(n, cands):
    for c in cands:
        if n % c == 0:
            return c
    return n


def _matmul(a, w, out_dtype, tm=1024, tn=1024):
    m, k = a.shape
    n = w.shape[1]
    tm = _pick(m, (tm, 768, 512, 384, 256, 128))
    tn = _pick(n, (tn, 768, 512, 384, 256, 128))
    return pl.pallas_call(
        _matmul_kernel,
        out_shape=jax.ShapeDtypeStruct((m, n), out_dtype),
        grid=(m // tm, n // tn),
        in_specs=[pl.BlockSpec((tm, k), lambda i, j: (i, 0)),
                  pl.BlockSpec((k, tn), lambda i, j: (0, j))],
        out_specs=pl.BlockSpec((tm, tn), lambda i, j: (i, j)),
        compiler_params=_cparams(("arbitrary", "arbitrary"), 48),
        name="matmul",
    )(a, w)


def _rw_prep_kernel(z_ref, zp_ref, zn_ref, mu_ref, w0_ref, wb_ref, a0_ref, ab_ref, gb_ref,
                    o_ref, g_ref, *, tl, nct, nt, w):
    i = pl.program_id(1)
    z = z_ref[0]
    row = lax.broadcasted_iota(jnp.int32, (tl, 1), 0)
    seg_start = jnp.logical_or(i == 0, i == nct)
    seg_end = jnp.logical_or(i == nct - 1, i == nt - 1)
    prev_row = jnp.where(seg_start, 0.0, zp_ref[0, SUBLANES - 1:SUBLANES, :])
    next_row = jnp.where(seg_end, 0.0, zn_ref[0, 0:1, :])
    prev = jnp.where(row == 0, prev_row, pltpu.roll(z, 1, 0))
    nxt = jnp.where(row == tl - 1, next_row, pltpu.roll(z, tl - 1, 0))
    zs = z + mu_ref[0:1, :] * (prev - z) + mu_ref[1:2, :] * (nxt - z)
    o_ref[0, 0] = zs[:, 0:w]
    o_ref[0, 1] = zs[:, w:2 * w]
    o_ref[0, 2] = zs[:, 2 * w:3 * w]
    o4 = 3 * w + 2 * RW_DECAY_RANK
    o5 = o4 + 2 * RW_ICL_RANK
    wd = jnp.tanh(zs[:, 3 * w:o4])
    wl = jnp.dot(wd.astype(BF16), wb_ref[...], preferred_element_type=F32) + w0_ref[...]
    dec = -math.exp(-0.5) * _sigmoid_t(wl)
    o_ref[0, 3] = dec[:, 0:w]
    o_ref[0, 4] = dec[:, w:2 * w]
    al = jnp.dot(zs[:, o4:o5].astype(BF16), ab_ref[...], preferred_element_type=F32) + a0_ref[...]
    av = _sigmoid_t(al)
    o_ref[0, 5] = av[:, 0:w]
    o_ref[0, 6] = av[:, w:2 * w]
    g_ref[0] = jnp.dot(_sigmoid_t(zs[:, o5:]).astype(BF16), gb_ref[...], preferred_element_type=F32)


def _blockdiag2(m):
    r, w = m.shape[1], m.shape[2]
    z = jnp.zeros((r, w), m.dtype)
    return jnp.concatenate([jnp.concatenate([m[0], z], axis=1), jnp.concatenate([z, m[1]], axis=1)], axis=0)


def _rw_prep(za, nc, mu, w0, wb, a0, ab, gb, tl=256):
    b, lt, acols = za.shape
    w = w0.shape[1]
    tl = min(tl, nc)
    nct, nt = nc // tl, lt // tl
    r8 = tl // SUBLANES
    nb8 = lt // SUBLANES
    full = lambda shp: pl.BlockSpec(shp, lambda bi, i: tuple(0 for _ in shp))
    kern = functools.partial(_rw_prep_kernel, tl=tl, nct=nct, nt=nt, w=w)
    return pl.pallas_call(
        kern,
        out_shape=[jax.ShapeDtypeStruct((b, 7, lt, w), F32), jax.ShapeDtypeStruct((b, lt, w), F32)],
        grid=(b, nt),
        in_specs=[pl.BlockSpec((1, tl, acols), lambda bi, i: (bi, i, 0)),
                  pl.BlockSpec((1, SUBLANES, acols), lambda bi, i: (bi, jnp.maximum(i * r8 - 1, 0), 0)),
                  pl.BlockSpec((1, SUBLANES, acols), lambda bi, i: (bi, jnp.minimum((i + 1) * r8, nb8 - 1), 0)),
                  full((2, acols)), full((1, 2 * w)), full((2 * RW_DECAY_RANK, 2 * w)),
                  full((1, 2 * w)), full((2 * RW_ICL_RANK, 2 * w)), full((RW_GATE_RANK, w))],
        out_specs=[pl.BlockSpec((1, 7, tl, w), lambda bi, i: (bi, 0, i, 0)),
                   pl.BlockSpec((1, tl, w), lambda bi, i: (bi, i, 0))],
        compiler_params=_cparams(("arbitrary", "arbitrary"), 48),
        name="rwkv_prep",
    )(za, za, za, mu, w0.reshape(1, 2 * w), _blockdiag2(wb).astype(BF16), a0.reshape(1, 2 * w),
      _blockdiag2(ab).astype(BF16), gb.astype(BF16))


def _rw_chunk_kernel(xf_ref, lwf_ref, af_ref, xb_ref, lwb_ref, ab_ref, kk_ref, ka_ref, rk_ref, gn0_ref, gn1_ref,
                     of_ref, ob_ref, s_ref, *, c, w):
    j = pl.program_id(1)

    @pl.when(j == 0)
    def _():
        s_ref[...] = jnp.zeros_like(s_ref)

    gw, n = RW_GROUP, RW_HEAD_DIM
    ng = w // gw
    row = lax.broadcasted_iota(jnp.int32, (c, 1), 0)
    pos = lax.broadcasted_iota(jnp.int32, (1, gw), 1) % n
    bdmask = (lax.broadcasted_iota(jnp.int32, (gw, 1), 0) // n) == (lax.broadcasted_iota(jnp.int32, (1, gw), 1) // n)
    ones_bd = jnp.where(bdmask, 1.0, 0.0).astype(BF16)
    eye = jnp.where(pos == row, 1.0, 0.0)
    nt = (((1,), (1,)), ((), ()))
    tn = (((0,), (0,)), ((), ()))
    chains = [(d, g) for d in range(2) for g in range(ng)]
    refs = ((xf_ref, lwf_ref, af_ref, of_ref), (xb_ref, lwb_ref, ab_ref, ob_ref))

    def bd(x):
        return jnp.where(bdmask, jnp.concatenate([x] * (gw // c), axis=0), 0.0).astype(BF16)

    def gsum(xs):
        pieces = []
        for x in xs:
            hi = x.astype(BF16)
            r1 = x - hi.astype(F32)
            mid = r1.astype(BF16)
            pieces += [hi, mid, (r1 - mid.astype(F32)).astype(BF16)]
        tot = jnp.dot(jnp.concatenate(pieces, axis=0), ones_bd, preferred_element_type=F32)
        return [tot[3 * i * c:(3 * i + 1) * c] + tot[(3 * i + 1) * c:(3 * i + 2) * c] + tot[(3 * i + 2) * c:(3 * i + 3) * c]
                for i in range(len(xs))]

    def mm(x, ybd):
        return jnp.dot(x.astype(BF16), ybd, preferred_element_type=F32)

    st = []
    for d, g in chains:
        x_ref, lw_ref, a_ref, _ = refs[d]
        sl = slice(g * gw, (g + 1) * gw)
        q = dict(sl=sl, rev=d == 1, r=x_ref[0, 0, :, sl], k=x_ref[0, 1, :, sl], v=x_ref[0, 2, :, sl],
                 lw=lw_ref[0, 0, :, sl], a=a_ref[0, 0, :, sl])
        q['kx'] = q['k'] * kk_ref[:, sl]
        q['kt'] = q['k'] * (1.0 + (q['a'] - 1.0) * ka_ref[:, sl])
        st.append(q)
    for q in st:
        q['ss'], q['bon'] = gsum([q['kx'] * q['kx'], q['r'] * q['kt'] * rk_ref[:, q['sl']]])
    for q in st:
        rev = q['rev']
        kk = q['kx'] / jnp.maximum(jnp.sqrt(q['ss']), 1e-12)
        q['b'] = q['a'] * kk
        cw = q['lw']
        sft = 1
        while sft < c:
            if rev:
                cw = cw + jnp.where(row < c - sft, pltpu.roll(cw, c - sft, 0), 0.0)
            else:
                cw = cw + jnp.where(row >= sft, pltpu.roll(cw, sft, 0), 0.0)
            sft *= 2
        q['tot'] = cw[0:1] if rev else cw[c - 1:c]
        invp = jnp.exp(-cw)
        q['e2'] = jnp.exp(q['tot'] - cw)
        q['kr'] = jnp.concatenate([kk * jnp.exp(cw - q['lw']), q['r'] * jnp.exp(cw)], axis=0).astype(BF16)
        q['ktb'] = bd(q['kt'] * invp)
        q['bb'] = bd(q['b'] * invp)
        q['vbd'] = bd(q['v'])
    for q in st:
        strict = (pos > row) if q['rev'] else (pos < row)
        incl = (pos >= row) if q['rev'] else (pos <= row)
        g1 = lax.dot_general(q['kr'], q['ktb'], nt, preferred_element_type=F32)
        g2 = lax.dot_general(q['kr'], q['bb'], nt, preferred_element_type=F32)
        q['m_kt'] = jnp.where(strict, g1[:c], 0.0)
        q['n_kt'] = jnp.where(incl, g1[c:], 0.0)
        q['x'] = jnp.where(strict, g2[:c], 0.0)
        q['n_b'] = jnp.where(incl, g2[c:], 0.0)
        q['t'] = eye - q['x']
        q['xb'] = bd(q['x'])
    for q in st:
        mn = mm(jnp.concatenate([q['m_kt'], q['n_kt']], axis=0), q['vbd'])
        q['mkv'], q['nkv'] = mn[:c], mn[c:]
    lv = 2
    while lv < c:
        for q in st:
            xt = mm(jnp.concatenate([q['x'], q['t']], axis=0), q['xb'])
            q['x'], q['tx'] = xt[:c], xt[c:]
        for q in st:
            q['t'] = q['t'] + mm(q['tx'], q['xb'])
        lv *= 2
        if lv < c:
            for q in st:
                q['xb'] = bd(q['x'])
    for (d, g), q in zip(chains, st):
        q['s0'] = s_ref[d, g]
        q['ka'] = lax.dot_general(q['kr'], q['s0'].astype(BF16), nt, preferred_element_type=F32)
    for q in st:
        q['u'] = mm(q['t'], bd(q['ka'][:c] + q['mkv']))
    for (d, g), q in zip(chains, st):
        e2 = q['e2']
        upd = lax.dot_general(jnp.concatenate([q['v'], q['u']], axis=0).astype(BF16),
                              jnp.concatenate([q['kt'] * e2, -(q['b'] * e2)], axis=0).astype(BF16),
                              tn, preferred_element_type=F32)
        s_ref[d, g] = q['s0'] * jnp.exp(q['tot']) + jnp.where(bdmask, upd, 0.0)
    for q in st:
        q['y'] = q['ka'][c:] + q['nkv'] - mm(q['n_b'], bd(q['u']))
    for q in st:
        (mu,) = gsum([q['y']])
        q['dl'] = q['y'] - mu * (1.0 / n)
    for (d, g), q in zip(chains, st):
        (var,) = gsum([q['dl'] * q['dl']])
        sl = q['sl']
        yn = q['dl'] * lax.rsqrt(var * (1.0 / n) + RW_GN_EPS) * gn0_ref[:, sl] + gn1_ref[:, sl]
        refs[d][3][0, :, sl] = yn + q['bon'] * q['v']


def _rw_chunk(streams, nc, kk, ka, rk, gn):
    b, _, lt, w = streams.shape
    c = RW_CHUNK
    ncb, nb = nc // c, lt // c

    def rblk(j):
        return jnp.where(j < ncb, ncb - 1 - j, nb - 1 - (j - ncb))

    par = pl.BlockSpec((1, w), lambda bi, j: (0, 0))
    shp = jax.ShapeDtypeStruct((b, lt, w), F32)
    return pl.pallas_call(
        functools.partial(_rw_chunk_kernel, c=c, w=w),
        out_shape=[shp, shp],
        grid=(b, nb),
        in_specs=[pl.BlockSpec((1, 3, c, w), lambda bi, j: (bi, 0, j, 0)),
                  pl.BlockSpec((1, 1, c, w), lambda bi, j: (bi, 3, j, 0)),
                  pl.BlockSpec((1, 1, c, w), lambda bi, j: (bi, 5, j, 0)),
                  pl.BlockSpec((1, 3, c, w), lambda bi, j: (bi, 0, rblk(j), 0)),
                  pl.BlockSpec((1, 1, c, w), lambda bi, j: (bi, 4, rblk(j), 0)),
                  pl.BlockSpec((1, 1, c, w), lambda bi, j: (bi, 6, rblk(j), 0)),
                  par, par, par, par, par],
        out_specs=[pl.BlockSpec((1, c, w), lambda bi, j: (bi, j, 0)),
                   pl.BlockSpec((1, c, w), lambda bi, j: (bi, rblk(j), 0))],
        scratch_shapes=[pltpu.VMEM((2, w // RW_GROUP, RW_GROUP, RW_GROUP), F32)],
        compiler_params=_cparams(("arbitrary", "arbitrary"), 32),
        name="rwkv_chunk",
    )(streams, streams, streams, streams, streams, streams, kk.reshape(1, w), ka.reshape(1, w), rk.reshape(1, w),
      gn[0].reshape(1, w), gn[1].reshape(1, w))


def _proj_rope_kernel(a_ref, w_ref, c_ref, s_ref, o_ref):
    j = pl.program_id(1)
    acc = jnp.dot(a_ref[...], w_ref[...], preferred_element_type=F32)

    @pl.when(j < 2)
    def _():
        lane = lax.broadcasted_iota(jnp.int32, (1, LANES), 1)
        first_half = (lane % (2 * ROPE_AXIS_FREQS)) < ROPE_AXIS_FREQS
        c = c_ref[...] * jnp.where(j == 0, DA_QK_DIM ** -0.5 * math.log2(math.e), 1.0)
        s = s_ref[...] * jnp.where(j == 0, DA_QK_DIM ** -0.5 * math.log2(math.e), 1.0)
        for g in range(acc.shape[1] // LANES):
            xs = acc[:, g * LANES:(g + 1) * LANES]
            up = pltpu.roll(xs, LANES - ROPE_AXIS_FREQS, 1)
            dn = pltpu.roll(xs, ROPE_AXIS_FREQS, 1)
            o_ref[:, g * LANES:(g + 1) * LANES] = (xs * c + jnp.where(first_half, up, dn) * s).astype(o_ref.dtype)

    @pl.when(j >= 2)
    def _():
        o_ref[...] = acc.astype(o_ref.dtype)


def _proj_rope(a, w, ct, st, lt):
    m, k = a.shape
    n = w.shape[1]
    tn = n // 3
    tm = _pick(lt, (768, 640, 512, 384, 256, 128))
    per = lt // tm
    return pl.pallas_call(
        _proj_rope_kernel,
        out_shape=jax.ShapeDtypeStruct((m, n), BF16),
        grid=(m // tm, 3),
        in_specs=[pl.BlockSpec((tm, k), lambda i, j: (i, 0)),
                  pl.BlockSpec((k, tn), lambda i, j: (0, j)),
                  pl.BlockSpec((tm, LANES), lambda i, j: (i % per, 0)),
                  pl.BlockSpec((tm, LANES), lambda i, j: (i % per, 0))],
        out_specs=pl.BlockSpec((tm, tn), lambda i, j: (i, j)),
        compiler_params=_cparams(("arbitrary", "arbitrary"), 48),
        name="proj_rope",
    )(a, w, ct, st)


def _rope_tables(seq, nc):
    rows = seq // GRID_W
    row = jnp.broadcast_to(jnp.arange(rows)[:, None], (rows, GRID_W)).reshape(seq)
    col = jnp.broadcast_to(jnp.arange(GRID_W)[None, :], (rows, GRID_W)).reshape(seq)
    inv = 1.0 / (ROPE_THETA ** (jnp.arange(ROPE_AXIS_FREQS, dtype=F32) / ROPE_AXIS_FREQS))
    ang = jnp.stack([row, col], axis=-1).astype(F32)[:, :, None] * inv
    cos, sin = jnp.cos(ang), jnp.sin(ang)
    c64 = jnp.concatenate([cos[:, 0], cos[:, 0], cos[:, 1], cos[:, 1]], axis=-1)
    s64 = jnp.concatenate([-sin[:, 0], sin[:, 0], -sin[:, 1], sin[:, 1]], axis=-1)
    ct = jnp.concatenate([jnp.ones((nc, LANES), F32), jnp.tile(c64, (1, 2))], axis=0)
    st = jnp.concatenate([jnp.zeros((nc, LANES), F32), jnp.tile(s64, (1, 2))], axis=0)
    return ct, st


def _attn_kernel(q_ref, k_ref, v_ref, lam_ref, g_ref, o_ref, *, nct, nc, lam_init, hp, qoff):
    i = pl.program_id(2) + qoff
    lane = lax.broadcasted_iota(jnp.int32, (1, LANES), 1)
    lp = lam_ref[...]
    lam = (jnp.exp(jnp.sum(lp[0:1] * lp[1:2], axis=1, keepdims=True))
           - jnp.exp(jnp.sum(lp[2:3] * lp[3:4], axis=1, keepdims=True)) + lam_init)

    def attend(nk):
        for h in range(hp):
            cols = slice(h * DA_V_DIM, (h + 1) * DA_V_DIM)
            q = q_ref[0, :, cols]
            k = k_ref[0, 0:nk, cols]
            v1 = jnp.concatenate([v_ref[0, 0:nk, cols], jnp.ones((nk, LANES), BF16)], axis=1)

            def one(sel):
                qs = jnp.where(sel, q, jnp.zeros_like(q))
                s = lax.dot_general(qs, k, (((1,), (1,)), ((), ())), preferred_element_type=F32)
                e = jnp.exp2(s - jnp.max(s, axis=-1, keepdims=True))
                ov = jnp.dot(e.astype(BF16), v1, preferred_element_type=F32)
                return ov[:, :DA_V_DIM] / ov[:, DA_V_DIM:]

            o = one(lane < DA_QK_DIM) - lam * one(lane >= DA_QK_DIM)
            on = o * lax.rsqrt(jnp.mean(o * o, axis=-1, keepdims=True) + DA_SUBLN_EPS) * g_ref[...]
            o_ref[0, :, cols] = (on * (1.0 - lam_init)).astype(o_ref.dtype)

    @pl.when(i < nct)
    def _():
        attend(nc)

    @pl.when(i >= nct)
    def _():
        attend(k_ref.shape[1])


def _attention(zbr, nc, lam_p, subln, lam_init, with_ctx, hp=2):
    b, lt, cols = zbr.shape
    w = cols // 3
    heads = w // DA_V_DIM
    tq = min(256, nc)
    nct = nc // tq
    hg = heads // hp
    bw = hp * DA_V_DIM
    qoff = 0 if with_ctx else nct
    kern = functools.partial(_attn_kernel, nct=nct, nc=nc, lam_init=lam_init, hp=hp, qoff=qoff)
    return pl.pallas_call(
        kern,
        out_shape=jax.ShapeDtypeStruct((b, lt, w), BF16),
        grid=(b, hg, lt // tq - qoff),
        in_specs=[pl.BlockSpec((1, tq, bw), lambda bi, h, i: (bi, i + qoff, h)),
                  pl.BlockSpec((1, lt, bw), lambda bi, h, i: (bi, 0, hg + h)),
                  pl.BlockSpec((1, lt, bw), lambda bi, h, i: (bi, 0, 2 * hg + h)),
                  pl.BlockSpec((4, DA_QK_DIM), lambda bi, h, i: (0, 0)),
                  pl.BlockSpec((1, DA_V_DIM), lambda bi, h, i: (0, 0))],
        out_specs=pl.BlockSpec((1, tq, bw), lambda bi, h, i: (bi, i + qoff, h)),
        compiler_params=_cparams(("arbitrary", "arbitrary", "arbitrary"), 48),
        name="diff_attn",
    )(zbr, zbr, zbr, lam_p, subln.reshape(1, DA_V_DIM))


def _seg_masks(c, e):
    row = lax.broadcasted_iota(jnp.int32, (c, e), 0)
    mk = {}
    h = 1
    while h < min(SUBLANES, c):
        odd = (row % (2 * h)) >= h
        for s in range(1, h + 1):
            mk[(h, s, 0)] = jnp.where(jnp.logical_and(odd, (row % h) == s - 1), 1.0, 0.0)
            mk[(h, s, 1)] = jnp.where(jnp.logical_and(~odd, (row % h) == h - s), 1.0, 0.0)
        h *= 2
    return mk


def _seg_scans(lf, mk, c):
    cs, ss = {1: lf}, {1: lf}
    h = 1
    while h < c:
        x, y = cs[h], ss[h]
        if h < SUBLANES:
            x3 = x.reshape(c // SUBLANES, SUBLANES, x.shape[1])
            y3 = y.reshape(c // SUBLANES, SUBLANES, y.shape[1])
            for s in range(1, h + 1):
                x = x + mk[(h, s, 0)] * pltpu.roll(x3, s, 1).reshape(x.shape)
                y = y + mk[(h, s, 1)] * pltpu.roll(y3, SUBLANES - s, 1).reshape(y.shape)
            cs[2 * h], ss[2 * h] = x, y
        else:
            px = [x[j * h:(j + 1) * h] for j in range(c // h)]
            py = [y[j * h:(j + 1) * h] for j in range(c // h)]
            nx = [px[j] + px[j - 1][h - 1:h] if j % 2 == 1 else px[j] for j in range(c // h)]
            ny = [py[j] + py[j + 1][0:1] if j % 2 == 0 else py[j] for j in range(c // h)]
            cs[2 * h], ss[2 * h] = jnp.concatenate(nx, axis=0), jnp.concatenate(ny, axis=0)
        h *= 2
    return cs, ss


def _hg_kernel(q_ref, f_ref, v_ref, lb_ref, o_ref, st_ref, *, tc, c, heads, rev):
    j = pl.program_id(1)

    @pl.when(j == 0)
    def _():
        st_ref[...] = jnp.zeros_like(st_ref)

    ti = lax.broadcasted_iota(jnp.int32, (c, c), 0)
    si = lax.broadcasted_iota(jnp.int32, (c, c), 1)
    nt = (((1,), (1,)), ((), ()))
    tn = (((0,), (0,)), ((), ()))
    mk = _seg_masks(c, HG_EXPAND)
    masks, hh = {}, 1
    while hh < c:
        tb, sb = ti // hh, si // hh
        masks[hh] = (jnp.logical_and(tb % 2 == 0, sb == tb + 1) if rev
                     else jnp.logical_and(tb % 2 == 1, sb == tb - 1))
        hh *= 2

    def head(h, carry):
        cols = pl.ds(pl.multiple_of(h * HG_EXPAND, HG_EXPAND), HG_EXPAND)
        lb = lb_ref[:, cols]
        chunks = range(tc // c)
        for ci in (reversed(chunks) if rev else chunks):
            rows = slice(ci * c, (ci + 1) * c)
            q = _silu(q_ref[0, rows, cols])
            f = lb + (1.0 - lb) * _sigmoid(f_ref[0, rows, cols])
            k = 1.0 - f
            lf = jnp.log(f)
            v = v_ref[0, rows, cols].astype(BF16)
            cs, ss = _seg_scans(lf, mk, c)
            qs, ks = (ss, cs) if rev else (cs, ss)
            att = jnp.where(ti == si, lax.dot_general(q.astype(BF16), k.astype(BF16), nt,
                                                      preferred_element_type=F32), 0.0)
            hh = 1
            while hh < c:
                qe = (q * jnp.exp(qs[hh])).astype(BF16)
                ke = (k * jnp.exp(ks[hh] - lf)).astype(BF16)
                a = lax.dot_general(qe, ke, nt, preferred_element_type=F32)
                att = att + jnp.where(masks[hh], a, 0.0)
                hh *= 2
            st = st_ref[h]
            qe = (q * jnp.exp(qs[c])).astype(BF16)
            o = lax.dot_general(qe, st.astype(BF16), nt, preferred_element_type=F32)
            o = o + jnp.dot(att.astype(BF16), v, preferred_element_type=F32)
            o_ref[0, rows, cols] = o
            ke = (k * jnp.exp(ks[c] - lf)).astype(BF16)
            tot = qs[c][0:1] if rev else qs[c][c - 1:c]
            st_ref[h] = st * jnp.exp(tot) + lax.dot_general(v, ke, tn, preferred_element_type=F32)
        return carry

    lax.fori_loop(0, heads, head, 0, unroll=4)


def _hgrn(zc, nc, lb, rev):
    b, lt, cols = zc.shape
    w = cols // 5
    heads = w // HG_EXPAND
    tc = min(256, nc)
    c = min(HG_CHUNK, tc)
    ncb, nb = nc // tc, lt // tc

    def blk(j):
        return jnp.where(j < ncb, ncb - 1 - j, nb - 1 - (j - ncb)) if rev else j

    fcol = 2 if rev else 1
    return pl.pallas_call(
        functools.partial(_hg_kernel, tc=tc, c=c, heads=heads, rev=rev),
        out_shape=jax.ShapeDtypeStruct((b, lt, w), F32),
        grid=(b, nb),
        in_specs=[pl.BlockSpec((1, tc, w), lambda bi, j: (bi, blk(j), 0)),
                  pl.BlockSpec((1, tc, w), lambda bi, j: (bi, blk(j), fcol)),
                  pl.BlockSpec((1, tc, w), lambda bi, j: (bi, blk(j), 3)),
                  pl.BlockSpec((1, w), lambda bi, j: (0, 0))],
        out_specs=pl.BlockSpec((1, tc, w), lambda bi, j: (bi, blk(j), 0)),
        scratch_shapes=[pltpu.VMEM((heads, HG_EXPAND, HG_EXPAND), F32)],
        compiler_params=_cparams(("arbitrary", "arbitrary"), 32),
        name="hgrn2_bwd" if rev else "hgrn2_fwd",
    )(zc, zc, zc, lb.reshape(1, w))


def _merge_kernel(oaf_ref, oab_ref, ga_ref, yb_ref, of_ref, ob_ref, gc_ref, hn_ref, gl0_ref, gl1_ref, gl2_ref, wb_ref,
                  o_ref, *, heads):
    gls = (gl0_ref, gl1_ref, gl2_ref)
    ya = (oaf_ref[0] + oab_ref[0]) * ga_ref[0]
    oc = of_ref[0] + ob_ref[0]
    parts = []
    for h in range(heads):
        x = oc[:, h * HG_EXPAND:(h + 1) * HG_EXPAND]
        parts.append(x * lax.rsqrt(jnp.mean(x * x, axis=-1, keepdims=True) + NORM_EPS))
    yc = jnp.concatenate(parts, axis=-1) * hn_ref[...] * _silu(gc_ref[0])
    acc = None
    for bi, y in enumerate((ya.astype(BF16), yb_ref[0], yc.astype(BF16))):
        term = _sigmoid_t(gls[bi][0].astype(F32)) * jnp.dot(y, wb_ref[bi], preferred_element_type=F32)
        acc = term if acc is None else acc + term
    o_ref[0] = acc.astype(BF16)


def _merge(oaf, oab, ga, yb, of, ob, zc, hn, gl, wbr, row0, tl=128, tn=2048):
    b, lt, w = ga.shape
    d = wbr.shape[2]
    tl = _pick(math.gcd(lt, row0), (tl, 64))
    r0 = row0 // tl
    heads = w // HG_EXPAND
    nn = d // tn
    row = lambda n, bi, i: (bi, i + r0, 0)
    glspec = lambda br: pl.BlockSpec((1, tl, tn), lambda n, bi, i: (bi, i + r0, br * nn + n))
    return pl.pallas_call(
        functools.partial(_merge_kernel, heads=heads),
        out_shape=jax.ShapeDtypeStruct((b, lt, d), BF16),
        grid=(d // tn, b, lt // tl - r0),
        in_specs=[pl.BlockSpec((1, tl, w), row),
                  pl.BlockSpec((1, tl, w), row),
                  pl.BlockSpec((1, tl, w), row),
                  pl.BlockSpec((1, tl, w), row),
                  pl.BlockSpec((1, tl, w), row),
                  pl.BlockSpec((1, tl, w), row),
                  pl.BlockSpec((1, tl, w), lambda n, bi, i: (bi, i + r0, 4)),
                  pl.BlockSpec((1, w), lambda n, bi, i: (0, 0)),
                  glspec(0), glspec(1), glspec(2),
                  pl.BlockSpec((3, w, tn), lambda n, bi, i: (0, 0, n))],
        out_specs=pl.BlockSpec((1, tl, tn), lambda n, bi, i: (bi, i + r0, n)),
        compiler_params=_cparams(("arbitrary", "arbitrary", "arbitrary"), 48),
        name="branch_merge",
    )(oaf, oab, ga, yb, of, ob, zc, hn, gl, gl, gl, wbr)


def _proj_post_kernel(a_ref, w_ref, x_ref, g_ref, gate_ref, o_ref):
    mx = jnp.dot(a_ref[0], w_ref[...], preferred_element_type=F32)
    mn = mx * lax.rsqrt(jnp.mean(mx * mx, axis=-1, keepdims=True) + NORM_EPS) * g_ref[...]
    o_ref[0] = x_ref[0] + gate_ref[0, 0] * mn


def _proj_post(a, wout, xa, g, gsel, nc, row0, tl=256):
    b, lt, d = xa.shape
    tl = min(tl, nc)
    nct = nc // tl
    r0 = row0 // tl
    return pl.pallas_call(
        _proj_post_kernel,
        out_shape=jax.ShapeDtypeStruct((b, lt, d), F32),
        grid=(b, lt // tl - r0),
        in_specs=[pl.BlockSpec((1, tl, d), lambda bi, i: (bi, i + r0, 0)),
                  pl.BlockSpec((d, d), lambda bi, i: (0, 0)),
                  pl.BlockSpec((1, tl, d), lambda bi, i: (bi, i + r0, 0)),
                  pl.BlockSpec((1, d), lambda bi, i: (0, 0)),
                  pl.BlockSpec((1, 1, 1, d), lambda bi, i: (bi, jnp.where(i + r0 >= nct, 1, 0), 0, 0))],
        out_specs=pl.BlockSpec((1, tl, d), lambda bi, i: (bi, i + r0, 0)),
        input_output_aliases={2: 0},
        compiler_params=_cparams(("arbitrary", "arbitrary"), 48),
        name="out_proj_post",
    )(a, wout, xa, g.reshape(1, d), gsel)


def _rank_kernel(afft_ref, slot_ref, *, n, cap, tw):
    aff = afft_ref[0]
    bits = lax.bitcast_convert_type(aff, jnp.int32)
    e_num = aff.shape[0]

    def count(mask):
        return jnp.sum(jnp.where(mask, 1.0, 0.0), axis=1, keepdims=True)

    def bit_step(i, thr):
        cand = thr | lax.shift_left(jnp.int32(1), 30 - i)
        return jnp.where(count(bits >= cand) >= cap, cand, thr)

    thr = lax.fori_loop(0, 31, bit_step, jnp.zeros((e_num, 1), jnp.int32))
    gt = bits > thr
    eq = bits == thr
    need = cap - count(gt)
    tri = jnp.where(lax.broadcasted_iota(jnp.int32, (tw, tw), 0) < lax.broadcasted_iota(jnp.int32, (tw, tw), 1),
                    1.0, 0.0).astype(BF16)

    def prefix(mask):
        parts, carry = [], jnp.zeros((e_num, 1), F32)
        for j in range(n // tw):
            m = jnp.where(mask[:, j * tw:(j + 1) * tw], 1.0, 0.0)
            parts.append(jnp.dot(m.astype(BF16), tri, preferred_element_type=F32) + carry)
            carry = carry + jnp.sum(m, axis=1, keepdims=True)
        return jnp.concatenate(parts, axis=1)

    sel = jnp.logical_or(gt, jnp.logical_and(eq, prefix(eq) < need))
    slot_ref[0] = jnp.where(sel, prefix(sel), float(cap)).astype(jnp.int32)


def _rank(afft, cap):
    b, e_num, n = afft.shape
    tw = min(LANES, n)
    return pl.pallas_call(
        functools.partial(_rank_kernel, n=n, cap=cap, tw=tw),
        out_shape=jax.ShapeDtypeStruct((b, e_num, n), jnp.int32),
        grid=(b,),
        in_specs=[pl.BlockSpec((1, e_num, n), lambda bi: (bi, 0, 0))],
        out_specs=pl.BlockSpec((1, e_num, n), lambda bi: (bi, 0, 0)),
        compiler_params=_cparams(("arbitrary",), 32),
        name="ec_rank",
    )(afft)


def _gather_kernel(slot_ref, afft_ref, h_ref, o_ref, g_ref, *, cap):
    e = pl.program_id(1)
    sl = slot_ref[0, pl.ds(e, 1), :]
    hit = lax.broadcasted_iota(jnp.int32, (cap, 1), 0) == sl
    o_ref[0] = jnp.dot(jnp.where(hit, 1.0, 0.0).astype(BF16), h_ref[0], preferred_element_type=F32).astype(BF16)
    gate = jnp.sum(jnp.where(hit, afft_ref[0, pl.ds(e, 1), :], 0.0), axis=1, keepdims=True)
    g_ref[0] = jnp.broadcast_to(gate, (cap, LANES))


def _gather(slot, afft, hs, cap):
    b, e_num, n = slot.shape
    d = hs.shape[2]
    return pl.pallas_call(
        functools.partial(_gather_kernel, cap=cap),
        out_shape=[jax.ShapeDtypeStruct((e_num, b * cap, d), BF16),
                   jax.ShapeDtypeStruct((e_num, b * cap, LANES), F32)],
        grid=(b, e_num),
        in_specs=[pl.BlockSpec((1, e_num, n), lambda bi, e: (bi, 0, 0)),
                  pl.BlockSpec((1, e_num, n), lambda bi, e: (bi, 0, 0)),
                  pl.BlockSpec((1, n, d), lambda bi, e: (bi, 0, 0))],
        out_specs=[pl.BlockSpec((1, cap, d), lambda bi, e: (e, bi, 0)),
                   pl.BlockSpec((1, cap, LANES), lambda bi, e: (e, bi, 0))],
        compiler_params=_cparams(("arbitrary", "arbitrary"), 40),
        name="ec_gather",
    )(slot, afft, hs)


def _ffn_kernel(x_ref, g_ref, w1_ref, w3_ref, w2_ref, o_ref, acc_ref):
    f = pl.program_id(2)
    x = x_ref[0]
    h1 = jnp.dot(x, w1_ref[0, 0].astype(BF16), preferred_element_type=F32)
    h3 = jnp.dot(x, w3_ref[0, 0].astype(BF16), preferred_element_type=F32)
    hid = (_silu(h1) * h3).astype(BF16)

    @pl.when(f == 0)
    def _():
        acc_ref[...] = jnp.zeros_like(acc_ref)

    d = acc_ref.shape[1]
    for n0 in range(0, d, FFN_OUT_CHUNK):
        cols = slice(n0, n0 + FFN_OUT_CHUNK)
        acc_ref[:, cols] += jnp.dot(hid, w2_ref[0, 0, :, cols].astype(BF16), preferred_element_type=F32)

    @pl.when(f == pl.num_programs(2) - 1)
    def _():
        o_ref[0] = (acc_ref[...] * g_ref[0, :, 0:1]).astype(BF16)


def _ffn(xe, gate, w1, w3, w2, layer, tf=256):
    e_num, m, d = xe.shape
    ff = w1.shape[3]
    tm = m
    once = pl.Buffered(1)
    return pl.pallas_call(
        _ffn_kernel,
        out_shape=jax.ShapeDtypeStruct((e_num, m, d), BF16),
        grid=(e_num, m // tm, ff // tf),
        in_specs=[pl.BlockSpec((1, tm, d), lambda e, i, f: (e, i, 0), pipeline_mode=once),
                  pl.BlockSpec((1, tm, LANES), lambda e, i, f: (e, i, 0), pipeline_mode=once),
                  pl.BlockSpec((1, 1, d, tf), lambda e, i, f: (layer, e, 0, f)),
                  pl.BlockSpec((1, 1, d, tf), lambda e, i, f: (layer, e, 0, f)),
                  pl.BlockSpec((1, 1, tf, d), lambda e, i, f: (layer, e, f, 0))],
        out_specs=pl.BlockSpec((1, tm, d), lambda e, i, f: (e, i, 0), pipeline_mode=once),
        scratch_shapes=[pltpu.VMEM((tm, d), F32)],
        compiler_params=_cparams(("arbitrary", "arbitrary", "arbitrary"), 58),
        name="ec_ffn",
    )(xe, gate, w1, w3, w2)


def _combine_kernel(slot_ref, ye_ref, x_ref, g_ref, gate_ref, o_ref, *, cap, e_num):
    lane = lax.broadcasted_iota(jnp.int32, (1, e_num), 1)
    pos = lax.broadcasted_iota(jnp.int32, (1, cap), 1)
    sl = slot_ref[0]
    y = None
    for e in range(e_num):
        se = jnp.sum(jnp.where(lane == e, sl, 0), axis=1, keepdims=True)
        pt = jnp.where(se == pos, 1.0, 0.0).astype(BF16)
        part = jnp.dot(pt, ye_ref[e], preferred_element_type=F32)
        y = part if y is None else y + part
    yn = y * lax.rsqrt(jnp.mean(y * y, axis=-1, keepdims=True) + NORM_EPS) * g_ref[...]
    o_ref[0] = x_ref[0] + gate_ref[0, 0] * yn


def _combine(slot_c, ye, xa, g, gsel, seg, row0, slot0, cap, final, tt=256):
    b, n, e_num = slot_c.shape
    d = xa.shape[2]
    tt = _pick(math.gcd(n, row0), (tt, 128, 64))
    assert row0 % tt == 0 and slot0 % cap == 0
    r0, s0 = row0 // tt, slot0 // cap
    return pl.pallas_call(
        functools.partial(_combine_kernel, cap=cap, e_num=e_num),
        out_shape=jax.ShapeDtypeStruct((b, n, d) if final else xa.shape, F32),
        grid=(b, n // tt),
        in_specs=[pl.BlockSpec((1, tt, e_num), lambda bi, j: (bi, j, 0)),
                  pl.BlockSpec((e_num, cap, d), lambda bi, j: (0, s0 + bi, 0)),
                  pl.BlockSpec((1, tt, d), lambda bi, j: (bi, r0 + j, 0)),
                  pl.BlockSpec((1, d), lambda bi, j: (0, 0)),
                  pl.BlockSpec((1, 1, 1, d), lambda bi, j: (bi, seg, 0, 0))],
        out_specs=pl.BlockSpec((1, tt, d), lambda bi, j: (bi, (0 if final else r0) + j, 0)),
        input_output_aliases={} if final else {2: 0},
        compiler_params=_cparams(("arbitrary", "arbitrary"), 56),
        name="ec_combine_post",
    )(slot_c, ye, xa, g.reshape(1, d), gsel)


def _sel(mod_c, mod_x, idx):
    b = mod_x.shape[0]
    mc = jnp.broadcast_to(mod_c[jnp.array(idx)][None], (b, len(idx), mod_c.shape[-1]))
    return jnp.stack([mc, mod_x[:, jnp.array(idx)]], axis=1)


def _layer(xa, nc, mod_x, mod_c, p, lam_init, hg_lb, ct, st, with_ctx):
    b, lt, d = xa.shape
    w = d // 2
    a_cols = 3 * w + 2 * RW_DECAY_RANK + 2 * RW_ICL_RANK + RW_GATE_RANK
    col_b = a_cols
    col_c = col_b + 3 * w
    col_g = col_c + 5 * w

    h = _norm_mod(xa, p['norm_g'][0], _sel(mod_c, mod_x, (0, 1)), nc)
    h2 = h.reshape(b * lt, d)
    w_in = p['w_in']
    za = _matmul(h2, w_in[:, :col_b].astype(BF16), F32, tn=a_cols // 3).reshape(b, lt, a_cols)
    zc = _matmul(h2, w_in[:, col_c:col_g].astype(BF16), F32).reshape(b, lt, 5 * w)
    gl = _matmul(h2, w_in[:, col_g:].astype(BF16), BF16).reshape(b, lt, 3 * d)

    streams, ga = _rw_prep(za, nc, p['rw_mu'], p['rw_w0'], p['rw_wB'], p['rw_a0'], p['rw_aB'], p['rw_gB'])
    oaf, oab = _rw_chunk(streams, nc, p['rw_kk'], p['rw_ka'], p['rw_rk'], p['rw_gn'])

    zbr = _proj_rope(h2, w_in[:, col_b:col_c].astype(BF16), ct, st, lt).reshape(b, lt, 3 * w)
    yb = _attention(zbr, nc, p['da_lambda'], p['da_subln'], lam_init, with_ctx)

    of = _hgrn(zc, nc, hg_lb, rev=False)
    ob = _hgrn(zc, nc, hg_lb, rev=True)

    hn = jnp.tile(p['hg_norm'], w // HG_EXPAND).reshape(1, w)
    skip = 0 if with_ctx else nc
    merged = _merge(oaf, oab, ga, yb, of, ob, zc, hn, gl, p['w_branch'].astype(BF16), skip)
    xa = _proj_post(merged, p['w_out'].astype(BF16), xa, p['norm_g'][1], _sel(mod_c, mod_x, (2,)), nc, skip)

    hm, afft = _norm_mod(xa, p['norm_g'][2], _sel(mod_c, mod_x, (3, 4)), nc, router_t=p['moe_router'].T, row0=skip)
    gsel = _sel(mod_c, mod_x, (5,))
    sets = [(nc, lt - nc, 1)] + ([(0, nc, 0)] if with_ctx else [])
    routed, xes, gates = [], [], []
    for row0, nn, seg in sets:
        cap = EC_CAPACITY_FACTOR * nn // N_EXPERTS
        at = afft[:, :, row0:row0 + nn]
        slot = _rank(at, cap)
        xe_s, gate_s = _gather(slot, at, hm[:, row0:row0 + nn], cap)
        xes.append(xe_s)
        gates.append(gate_s)
        routed.append((jnp.swapaxes(slot, 1, 2), row0, seg, cap))
    xe = xes[0] if len(xes) == 1 else jnp.concatenate(xes, axis=1)
    gate = gates[0] if len(gates) == 1 else jnp.concatenate(gates, axis=1)
    ye = _ffn(xe, gate, p['moe_w1'], p['moe_w3'], p['moe_w2'], p['layer'])
    slot0 = 0
    for slot_c, row0, seg, cap in routed:
        xa = _combine(slot_c, ye, xa, p['norm_g'][3], gsel, seg, row0, slot0, cap, final=not with_ctx)
        slot0 += b * cap
    return xa


def kernel(x, c, ctx, c_ctx, w_ada, b_ada, norm_g, w_in, rw_mu, rw_w0, rw_wB, rw_a0, rw_aB, rw_gB, rw_kk, rw_ka, rw_rk, rw_gn, da_lambda, da_subln, hg_lb_logits, hg_norm, w_branch, w_out, moe_router, moe_w1, moe_w3, moe_w2):
    b, seq, d = x.shape
    nc = ctx.shape[1]
    depth = w_ada.shape[0]
    ct, st = _rope_tables(seq, nc)
    lb_w = jax.nn.softmax(hg_lb_logits.astype(F32), axis=0)
    hg_lb = jnp.cumsum(lb_w, axis=0) - lb_w[0]
    rows = ((b + 1 + SUBLANES - 1) // SUBLANES) * SUBLANES
    cc = jnp.zeros((rows, d), F32).at[:b].set(c).at[b].set(c_ctx)
    mod = _modulation(cc, w_ada, b_ada)
    xa = jnp.concatenate([ctx, x], axis=1)
    for l in range(depth):
        p = dict(norm_g=norm_g[l], w_in=w_in[l], rw_mu=rw_mu[l], rw_w0=rw_w0[l], rw_wB=rw_wB[l],
                 rw_a0=rw_a0[l], rw_aB=rw_aB[l], rw_gB=rw_gB[l], rw_kk=rw_kk[l], rw_ka=rw_ka[l],
                 rw_rk=rw_rk[l], rw_gn=rw_gn[l], da_lambda=da_lambda[l], da_subln=da_subln[l],
                 hg_norm=hg_norm[l], w_branch=w_branch[l], w_out=w_out[l], moe_router=moe_router[l],
                 moe_w1=moe_w1, moe_w3=moe_w3, moe_w2=moe_w2, layer=l)
        mod_x = mod[l, :b].reshape(b, 6, d)
        mod_c = mod[l, b].reshape(6, d)
        lam_init = 0.8 - 0.6 * math.exp(-0.3 * l)
        xa = _layer(xa, nc, mod_x, mod_c, p, lam_init, hg_lb[l], ct, st, with_ctx=l < depth - 1)
    return xa
```

```python
import functools
import math

import jax
import jax.numpy as jnp
from jax import lax
from jax.experimental import pallas as pl
from jax.experimental.pallas import tpu as pltpu

F32 = jnp.float32
BF16 = jnp.bfloat16
HIGHEST = lax.Precision.HIGHEST

GRID_W = 64
RW_HEAD_DIM = 64
RW_DECAY_RANK = 64
RW_ICL_RANK = 64
RW_GATE_RANK = 128
RW_GN_EPS = 64e-5
DA_QK_DIM = 64
DA_V_DIM = 128
ROPE_THETA = 10000.0
ROPE_AXIS_FREQS = 16
DA_SUBLN_EPS = 1e-5
HG_EXPAND = 128
N_EXPERTS = 16
EC_CAPACITY_FACTOR = 2
NORM_EPS = 1e-6

LANES = 128
SUBLANES = 8
V7X_VMEM_BYTES = 64 * 1024 * 1024
HG_CHUNK = 64
RW_CHUNK = 64
RW_GROUP = 256
FFN_OUT_CHUNK = 512


def _cparams(sem, vmem_mb):
    limit = int(vmem_mb * 1024 * 1024)
    assert limit < V7X_VMEM_BYTES
    return pltpu.CompilerParams(dimension_semantics=sem, vmem_limit_bytes=limit)


def _sigmoid(x):
    return 1.0 / (1.0 + jnp.exp(-x))


def _sigmoid_t(x):
    return 0.5 * jnp.tanh(0.5 * x) + 0.5


def _silu(x):
    return x * _sigmoid_t(x)


def _mod_kernel(c_ref, w_ref, b_ref, o_ref):
    a = _silu(c_ref[...]).astype(BF16)
    o_ref[0] = jnp.dot(a, w_ref[0].astype(BF16), preferred_element_type=F32) + b_ref[0]


def _modulation(cc, w_ada, b_ada):
    depth, d, n = w_ada.shape
    rows = cc.shape[0]
    tn = 1024
    return pl.pallas_call(
        _mod_kernel,
        out_shape=jax.ShapeDtypeStruct((depth, rows, n), F32),
        grid=(depth, n // tn),
        in_specs=[pl.BlockSpec((rows, d), lambda l, j: (0, 0)),
                  pl.BlockSpec((1, d, tn), lambda l, j: (l, 0, j)),
                  pl.BlockSpec((1, 1, tn), lambda l, j: (l, 0, j))],
        out_specs=pl.BlockSpec((1, rows, tn), lambda l, j: (l, 0, j)),
        compiler_params=_cparams(("arbitrary", "arbitrary"), 40),
        name="adaln_mod",
    )(cc, w_ada, b_ada.reshape(depth, 1, n))


def _norm_mod_kernel(x_ref, g_ref, ms_ref, *rest, with_router):
    x = x_ref[0]
    xn = x * lax.rsqrt(jnp.mean(x * x, axis=-1, keepdims=True) + NORM_EPS) * g_ref[...]
    h = xn * (1.0 + ms_ref[0, 0, 1:2, :]) + ms_ref[0, 0, 0:1, :]
    if with_router:
        rt_ref, h_ref, aff_ref = rest
        h_ref[0] = h.astype(BF16)
        lt = lax.dot_general(rt_ref[...], h, (((1,), (1,)), ((), ())),
                             precision=HIGHEST, preferred_element_type=F32)
        e = jnp.exp(lt - jnp.max(lt, axis=0, keepdims=True))
        aff_ref[0] = e / jnp.sum(e, axis=0, keepdims=True)
    else:
        (h_ref,) = rest
        h_ref[0] = h.astype(BF16)


def _norm_mod(xa, g, msel, nc, router_t=None, row0=0, tl=256):
    b, lt, d = xa.shape
    tl = min(tl, nc)
    nct = nc // tl
    r0 = row0 // tl
    with_router = router_t is not None
    in_specs = [pl.BlockSpec((1, tl, d), lambda bi, i: (bi, i + r0, 0)),
                pl.BlockSpec((1, d), lambda bi, i: (0, 0)),
                pl.BlockSpec((1, 1, 2, d), lambda bi, i: (bi, jnp.where(i + r0 >= nct, 1, 0), 0, 0))]
    args = [xa, g.reshape(1, d), msel]
    out_shape = [jax.ShapeDtypeStruct((b, lt - row0, d), BF16)]
    out_specs = [pl.BlockSpec((1, tl, d), lambda bi, i: (bi, i, 0))]
    if with_router:
        e = router_t.shape[0]
        in_specs.append(pl.BlockSpec((e, d), lambda bi, i: (0, 0)))
        args.append(router_t)
        out_shape.append(jax.ShapeDtypeStruct((b, e, lt - row0), F32))
        out_specs.append(pl.BlockSpec((1, e, tl), lambda bi, i: (bi, 0, i)))
    res = pl.pallas_call(
        functools.partial(_norm_mod_kernel, with_router=with_router),
        out_shape=out_shape, grid=(b, lt // tl - r0), in_specs=in_specs, out_specs=out_specs,
        compiler_params=_cparams(("arbitrary", "arbitrary"), 32),
        name="norm_mod_router" if with_router else "norm_mod",
    )(*args)
    return res if with_router else res[0]


def _matmul_kernel(a_ref, w_ref, o_ref):
    o_ref[...] = jnp.dot(a_ref[...], w_ref[...], preferred_element_type=F32).astype(o_ref.dtype)


def _pick(n, cands):
    for c in cands:
        if n % c == 0:
            return c
    return n


def _matmul(a, w, out_dtype, tm=1024, tn=1024):
    m, k = a.shape
    n = w.shape[1]
    tm = _pick(m, (tm, 768, 512, 384, 256, 128))
    tn = _pick(n, (tn, 768, 512, 384, 256, 128))
    return pl.pallas_call(
        _matmul_kernel,
        out_shape=jax.ShapeDtypeStruct((m, n), out_dtype),
        grid=(m // tm, n // tn),
        in_specs=[pl.BlockSpec((tm, k), lambda i, j: (i, 0)),
                  pl.BlockSpec((k, tn), lambda i, j: (0, j))],
        out_specs=pl.BlockSpec((tm, tn), lambda i, j: (i, j)),
        compiler_params=_cparams(("arbitrary", "arbitrary"), 48),
        name="matmul",
    )(a, w)


def _rw_prep_kernel(z_ref, zp_ref, zn_ref, mu_ref, w0_ref, wb_ref, a0_ref, ab_ref, gb_ref,
                    o_ref, g_ref, *, tl, nct, nt, w):
    i = pl.program_id(1)
    z = z_ref[0]
    row = lax.broadcasted_iota(jnp.int32, (tl, 1), 0)
    seg_start = jnp.logical_or(i == 0, i == nct)
    seg_end = jnp.logical_or(i == nct - 1, i == nt - 1)
    prev_row = jnp.where(seg_start, 0.0, zp_ref[0, SUBLANES - 1:SUBLANES, :])
    next_row = jnp.where(seg_end, 0.0, zn_ref[0, 0:1, :])
    prev = jnp.where(row == 0, prev_row, pltpu.roll(z, 1, 0))
    nxt = jnp.where(row == tl - 1, next_row, pltpu.roll(z, tl - 1, 0))
    zs = z + mu_ref[0:1, :] * (prev - z) + mu_ref[1:2, :] * (nxt - z)
    o_ref[0, 0] = zs[:, 0:w]
    o_ref[0, 1] = zs[:, w:2 * w]
    o_ref[0, 2] = zs[:, 2 * w:3 * w]
    o4 = 3 * w + 2 * RW_DECAY_RANK
    o5 = o4 + 2 * RW_ICL_RANK
    wd = jnp.tanh(zs[:, 3 * w:o4])
    wl = jnp.dot(wd.astype(BF16), wb_ref[...], preferred_element_type=F32) + w0_ref[...]
    dec = -math.exp(-0.5) * _sigmoid_t(wl)
    o_ref[0, 3] = dec[:, 0:w]
    o_ref[0, 4] = dec[:, w:2 * w]
    al = jnp.dot(zs[:, o4:o5].astype(BF16), ab_ref[...], preferred_element_type=F32) + a0_ref[...]
    av = _sigmoid_t(al)
    o_ref[0, 5] = av[:, 0:w]
    o_ref[0, 6] = av[:, w:2 * w]
    g_ref[0] = jnp.dot(_sigmoid_t(zs[:, o5:]).astype(BF16), gb_ref[...], preferred_element_type=F32)


def _blockdiag2(m):
    r, w = m.shape[1], m.shape[2]
    z = jnp.zeros((r, w), m.dtype)
    return jnp.concatenate([jnp.concatenate([m[0], z], axis=1), jnp.concatenate([z, m[1]], axis=1)], axis=0)


def _rw_prep(za, nc, mu, w0, wb, a0, ab, gb, tl=256):
    b, lt, acols = za.shape
    w = w0.shape[1]
    tl = min(tl, nc)
    nct, nt = nc // tl, lt // tl
    r8 = tl // SUBLANES
    nb8 = lt // SUBLANES
    full = lambda shp: pl.BlockSpec(shp, lambda bi, i: tuple(0 for _ in shp))
    kern = functools.partial(_rw_prep_kernel, tl=tl, nct=nct, nt=nt, w=w)
    return pl.pallas_call(
        kern,
        out_shape=[jax.ShapeDtypeStruct((b, 7, lt, w), F32), jax.ShapeDtypeStruct((b, lt, w), F32)],
        grid=(b, nt),
        in_specs=[pl.BlockSpec((1, tl, acols), lambda bi, i: (bi, i, 0)),
                  pl.BlockSpec((1, SUBLANES, acols), lambda bi, i: (bi, jnp.maximum(i * r8 - 1, 0), 0)),
                  pl.BlockSpec((1, SUBLANES, acols), lambda bi, i: (bi, jnp.minimum((i + 1) * r8, nb8 - 1), 0)),
                  full((2, acols)), full((1, 2 * w)), full((2 * RW_DECAY_RANK, 2 * w)),
                  full((1, 2 * w)), full((2 * RW_ICL_RANK, 2 * w)), full((RW_GATE_RANK, w))],
        out_specs=[pl.BlockSpec((1, 7, tl, w), lambda bi, i: (bi, 0, i, 0)),
                   pl.BlockSpec((1, tl, w), lambda bi, i: (bi, i, 0))],
        compiler_params=_cparams(("arbitrary", "arbitrary"), 48),
        name="rwkv_prep",
    )(za, za, za, mu, w0.reshape(1, 2 * w), _blockdiag2(wb).astype(BF16), a0.reshape(1, 2 * w),
      _blockdiag2(ab).astype(BF16), gb.astype(BF16))


def _rw_chunk_kernel(xf_ref, lwf_ref, af_ref, xb_ref, lwb_ref, ab_ref, kk_ref, ka_ref, rk_ref, gn0_ref, gn1_ref,
                     of_ref, ob_ref, s_ref, *, c, w):
    j = pl.program_id(1)

    @pl.when(j == 0)
    def _():
        s_ref[...] = jnp.zeros_like(s_ref)

    gw, n = RW_GROUP, RW_HEAD_DIM
    ng = w // gw
    row = lax.broadcasted_iota(jnp.int32, (c, 1), 0)
    pos = lax.broadcasted_iota(jnp.int32, (1, gw), 1) % n
    bdmask = (lax.broadcasted_iota(jnp.int32, (gw, 1), 0) // n) == (lax.broadcasted_iota(jnp.int32, (1, gw), 1) // n)
    ones_bd = jnp.where(bdmask, 1.0, 0.0).astype(BF16)
    eye = jnp.where(pos == row, 1.0, 0.0)
    nt = (((1,), (1,)), ((), ()))
    tn = (((0,), (0,)), ((), ()))
    chains = [(d, g) for d in range(2) for g in range(ng)]
    refs = ((xf_ref, lwf_ref, af_ref, of_ref), (xb_ref, lwb_ref, ab_ref, ob_ref))

    def bd(x):
        return jnp.where(bdmask, jnp.concatenate([x] * (gw // c), axis=0), 0.0).astype(BF16)

    def gsum(xs):
        pieces = []
        for x in xs:
            hi = x.astype(BF16)
            r1 = x - hi.astype(F32)
            mid = r1.astype(BF16)
            pieces += [hi, mid, (r1 - mid.astype(F32)).astype(BF16)]
        tot = jnp.dot(jnp.concatenate(pieces, axis=0), ones_bd, preferred_element_type=F32)
        return [tot[3 * i * c:(3 * i + 1) * c] + tot[(3 * i + 1) * c:(3 * i + 2) * c] + tot[(3 * i + 2) * c:(3 * i + 3) * c]
                for i in range(len(xs))]

    def mm(x, ybd):
        return jnp.dot(x.astype(BF16), ybd, preferred_element_type=F32)

    st = []
    for d, g in chains:
        x_ref, lw_ref, a_ref, _ = refs[d]
        sl = slice(g * gw, (g + 1) * gw)
        q = dict(sl=sl, rev=d == 1, r=x_ref[0, 0, :, sl], k=x_ref[0, 1, :, sl], v=x_ref[0, 2, :, sl],
                 lw=lw_ref[0, 0, :, sl], a=a_ref[0, 0, :, sl])
        q['kx'] = q['k'] * kk_ref[:, sl]
        q['kt'] = q['k'] * (1.0 + (q['a'] - 1.0) * ka_ref[:, sl])
        st.append(q)
    for q in st:
        q['ss'], q['bon'] = gsum([q['kx'] * q['kx'], q['r'] * q['kt'] * rk_ref[:, q['sl']]])
    for q in st:
        rev = q['rev']
        kk = q['kx'] / jnp.maximum(jnp.sqrt(q['ss']), 1e-12)
        q['b'] = q['a'] * kk
        cw = q['lw']
        sft = 1
        while sft < c:
            if rev:
                cw = cw + jnp.where(row < c - sft, pltpu.roll(cw, c - sft, 0), 0.0)
            else:
                cw = cw + jnp.where(row >= sft, pltpu.roll(cw, sft, 0), 0.0)
            sft *= 2
        q['tot'] = cw[0:1] if rev else cw[c - 1:c]
        invp = jnp.exp(-cw)
        q['e2'] = jnp.exp(q['tot'] - cw)
        q['kr'] = jnp.concatenate([kk * jnp.exp(cw - q['lw']), q['r'] * jnp.exp(cw)], axis=0).astype(BF16)
        q['ktb'] = bd(q['kt'] * invp)
        q['bb'] = bd(q['b'] * invp)
        q['vbd'] = bd(q['v'])
    for q in st:
        strict = (pos > row) if q['rev'] else (pos < row)
        incl = (pos >= row) if q['rev'] else (pos <= row)
        g1 = lax.dot_general(q['kr'], q['ktb'], nt, preferred_element_type=F32)
        g2 = lax.dot_general(q['kr'], q['bb'], nt, preferred_element_type=F32)
        q['m_kt'] = jnp.where(strict, g1[:c], 0.0)
        q['n_kt'] = jnp.where(incl, g1[c:], 0.0)
        q['x'] = jnp.where(strict, g2[:c], 0.0)
        q['n_b'] = jnp.where(incl, g2[c:], 0.0)
        q['t'] = eye - q['x']
        q['xb'] = bd(q['x'])
    for q in st:
        mn = mm(jnp.concatenate([q['m_kt'], q['n_kt']], axis=0), q['vbd'])
        q['mkv'], q['nkv'] = mn[:c], mn[c:]
    lv = 2
    while lv < c:
        for q in st:
            xt = mm(jnp.concatenate([q['x'], q['t']], axis=0), q['xb'])
            q['x'], q['tx'] = xt[:c], xt[c:]
        for q in st:
            q['t'] = q['t'] + mm(q['tx'], q['xb'])
        lv *= 2
        if lv < c:
            for q in st:
                q['xb'] = bd(q['x'])
    for (d, g), q in zip(chains, st):
        q['s0'] = s_ref[d, g]
        q['ka'] = lax.dot_general(q['kr'], q['s0'].astype(BF16), nt, preferred_element_type=F32)
    for q in st:
        q['u'] = mm(q['t'], bd(q['ka'][:c] + q['mkv']))
    for (d, g), q in zip(chains, st):
        e2 = q['e2']
        upd = lax.dot_general(jnp.concatenate([q['v'], q['u']], axis=0).astype(BF16),
                              jnp.concatenate([q['kt'] * e2, -(q['b'] * e2)], axis=0).astype(BF16),
                              tn, preferred_element_type=F32)
        s_ref[d, g] = q['s0'] * jnp.exp(q['tot']) + jnp.where(bdmask, upd, 0.0)
    for q in st:
        q['y'] = q['ka'][c:] + q['nkv'] - mm(q['n_b'], bd(q['u']))
    for q in st:
        (mu,) = gsum([q['y']])
        q['dl'] = q['y'] - mu * (1.0 / n)
    for (d, g), q in zip(chains, st):
        (var,) = gsum([q['dl'] * q['dl']])
        sl = q['sl']
        yn = q['dl'] * lax.rsqrt(var * (1.0 / n) + RW_GN_EPS) * gn0_ref[:, sl] + gn1_ref[:, sl]
        refs[d][3][0, :, sl] = yn + q['bon'] * q['v']


def _rw_chunk(streams, nc, kk, ka, rk, gn):
    b, _, lt, w = streams.shape
    c = RW_CHUNK
    ncb, nb = nc // c, lt // c

    def rblk(j):
        return jnp.where(j < ncb, ncb - 1 - j, nb - 1 - (j - ncb))

    par = pl.BlockSpec((1, w), lambda bi, j: (0, 0))
    shp = jax.ShapeDtypeStruct((b, lt, w), F32)
    return pl.pallas_call(
        functools.partial(_rw_chunk_kernel, c=c, w=w),
        out_shape=[shp, shp],
        grid=(b, nb),
        in_specs=[pl.BlockSpec((1, 3, c, w), lambda bi, j: (bi, 0, j, 0)),
                  pl.BlockSpec((1, 1, c, w), lambda bi, j: (bi, 3, j, 0)),
                  pl.BlockSpec((1, 1, c, w), lambda bi, j: (bi, 5, j, 0)),
                  pl.BlockSpec((1, 3, c, w), lambda bi, j: (bi, 0, rblk(j), 0)),
                  pl.BlockSpec((1, 1, c, w), lambda bi, j: (bi, 4, rblk(j), 0)),
                  pl.BlockSpec((1, 1, c, w), lambda bi, j: (bi, 6, rblk(j), 0)),
                  par, par, par, par, par],
        out_specs=[pl.BlockSpec((1, c, w), lambda bi, j: (bi, j, 0)),
                   pl.BlockSpec((1, c, w), lambda bi, j: (bi, rblk(j), 0))],
        scratch_shapes=[pltpu.VMEM((2, w // RW_GROUP, RW_GROUP, RW_GROUP), F32)],
        compiler_params=_cparams(("arbitrary", "arbitrary"), 32),
        name="rwkv_chunk",
    )(streams, streams, streams, streams, streams, streams, kk.reshape(1, w), ka.reshape(1, w), rk.reshape(1, w),
      gn[0].reshape(1, w), gn[1].reshape(1, w))


def _proj_rope_kernel(a_ref, w_ref, c_ref, s_ref, o_ref):
    j = pl.program_id(1)
    acc = jnp.dot(a_ref[...], w_ref[...], preferred_element_type=F32)

    @pl.when(j < 2)
    def _():
        lane = lax.broadcasted_iota(jnp.int32, (1, LANES), 1)
        first_half = (lane % (2 * ROPE_AXIS_FREQS)) < ROPE_AXIS_FREQS
        c = c_ref[...] * jnp.where(j == 0, DA_QK_DIM ** -0.5 * math.log2(math.e), 1.0)
        s = s_ref[...] * jnp.where(j == 0, DA_QK_DIM ** -0.5 * math.log2(math.e), 1.0)
        for g in range(acc.shape[1] // LANES):
            xs = acc[:, g * LANES:(g + 1) * LANES]
            up = pltpu.roll(xs, LANES - ROPE_AXIS_FREQS, 1)
            dn = pltpu.roll(xs, ROPE_AXIS_FREQS, 1)
            o_ref[:, g * LANES:(g + 1) * LANES] = (xs * c + jnp.where(first_half, up, dn) * s).astype(o_ref.dtype)

    @pl.when(j >= 2)
    def _():
        o_ref[...] = acc.astype(o_ref.dtype)


def _proj_rope(a, w, ct, st, lt):
    m, k = a.shape
    n = w.shape[1]
    tn = n // 3
    tm = _pick(lt, (768, 640, 512, 384, 256, 128))
    per = lt // tm
    return pl.pallas_call(
        _proj_rope_kernel,
        out_shape=jax.ShapeDtypeStruct((m, n), BF16),
        grid=(m // tm, 3),
        in_specs=[pl.BlockSpec((tm, k), lambda i, j: (i, 0)),
                  pl.BlockSpec((k, tn), lambda i, j: (0, j)),
                  pl.BlockSpec((tm, LANES), lambda i, j: (i % per, 0)),
                  pl.BlockSpec((tm, LANES), lambda i, j: (i % per, 0))],
        out_specs=pl.BlockSpec((tm, tn), lambda i, j: (i, j)),
        compiler_params=_cparams(("arbitrary", "arbitrary"), 48),
        name="proj_rope",
    )(a, w, ct, st)


def _rope_tables(seq, nc):
    rows = seq // GRID_W
    row = jnp.broadcast_to(jnp.arange(rows)[:, None], (rows, GRID_W)).reshape(seq)
    col = jnp.broadcast_to(jnp.arange(GRID_W)[None, :], (rows, GRID_W)).reshape(seq)
    inv = 1.0 / (ROPE_THETA ** (jnp.arange(ROPE_AXIS_FREQS, dtype=F32) / ROPE_AXIS_FREQS))
    ang = jnp.stack([row, col], axis=-1).astype(F32)[:, :, None] * inv
    cos, sin = jnp.cos(ang), jnp.sin(ang)
    c64 = jnp.concatenate([cos[:, 0], cos[:, 0], cos[:, 1], cos[:, 1]], axis=-1)
    s64 = jnp.concatenate([-sin[:, 0], sin[:, 0], -sin[:, 1], sin[:, 1]], axis=-1)
    ct = jnp.concatenate([jnp.ones((nc, LANES), F32), jnp.tile(c64, (1, 2))], axis=0)
    st = jnp.concatenate([jnp.zeros((nc, LANES), F32), jnp.tile(s64, (1, 2))], axis=0)
    return ct, st


def _attn_kernel(q_ref, k_ref, v_ref, lam_ref, g_ref, o_ref, *, nct, nc, lam_init, hp, qoff):
    i = pl.program_id(2) + qoff
    lane = lax.broadcasted_iota(jnp.int32, (1, LANES), 1)
    lp = lam_ref[...]
    lam = (jnp.exp(jnp.sum(lp[0:1] * lp[1:2], axis=1, keepdims=True))
           - jnp.exp(jnp.sum(lp[2:3] * lp[3:4], axis=1, keepdims=True)) + lam_init)

    def attend(nk):
        for h in range(hp):
            cols = slice(h * DA_V_DIM, (h + 1) * DA_V_DIM)
            q = q_ref[0, :, cols]
            k = k_ref[0, 0:nk, cols]
            v1 = jnp.concatenate([v_ref[0, 0:nk, cols], jnp.ones((nk, LANES), BF16)], axis=1)

            def one(sel):
                qs = jnp.where(sel, q, jnp.zeros_like(q))
                s = lax.dot_general(qs, k, (((1,), (1,)), ((), ())), preferred_element_type=F32)
                e = jnp.exp2(s - jnp.max(s, axis=-1, keepdims=True))
                ov = jnp.dot(e.astype(BF16), v1, preferred_element_type=F32)
                return ov[:, :DA_V_DIM] / ov[:, DA_V_DIM:]

            o = one(lane < DA_QK_DIM) - lam * one(lane >= DA_QK_DIM)
            on = o * lax.rsqrt(jnp.mean(o * o, axis=-1, keepdims=True) + DA_SUBLN_EPS) * g_ref[...]
            o_ref[0, :, cols] = (on * (1.0 - lam_init)).astype(o_ref.dtype)

    @pl.when(i < nct)
    def _():
        attend(nc)

    @pl.when(i >= nct)
    def _():
        attend(k_ref.shape[1])


def _attention(zbr, nc, lam_p, subln, lam_init, with_ctx, hp=2):
    b, lt, cols = zbr.shape
    w = cols // 3
    heads = w // DA_V_DIM
    tq = min(256, nc)
    nct = nc // tq
    hg = heads // hp
    bw = hp * DA_V_DIM
    qoff = 0 if with_ctx else nct
    kern = functools.partial(_attn_kernel, nct=nct, nc=nc, lam_init=lam_init, hp=hp, qoff=qoff)
    return pl.pallas_call(
        kern,
        out_shape=jax.ShapeDtypeStruct((b, lt - qoff * tq, w), BF16),
        grid=(b, hg, lt // tq - qoff),
        in_specs=[pl.BlockSpec((1, tq, bw), lambda bi, h, i: (bi, i + qoff, h)),
                  pl.BlockSpec((1, lt, bw), lambda bi, h, i: (bi, 0, hg + h)),
                  pl.BlockSpec((1, lt, bw), lambda bi, h, i: (bi, 0, 2 * hg + h)),
                  pl.BlockSpec((4, DA_QK_DIM), lambda bi, h, i: (0, 0)),
                  pl.BlockSpec((1, DA_V_DIM), lambda bi, h, i: (0, 0))],
        out_specs=pl.BlockSpec((1, tq, bw), lambda bi, h, i: (bi, i, h)),
        compiler_params=_cparams(("arbitrary", "arbitrary", "arbitrary"), 48),
        name="diff_attn",
    )(zbr, zbr, zbr, lam_p, subln.reshape(1, DA_V_DIM))


def _seg_masks(c, e):
    row = lax.broadcasted_iota(jnp.int32, (c, e), 0)
    mk = {}
    h = 1
    while h < min(SUBLANES, c):
        odd = (row % (2 * h)) >= h
        for s in range(1, h + 1):
            mk[(h, s, 0)] = jnp.where(jnp.logical_and(odd, (row % h) == s - 1), 1.0, 0.0)
            mk[(h, s, 1)] = jnp.where(jnp.logical_and(~odd, (row % h) == h - s), 1.0, 0.0)
        h *= 2
    return mk


def _seg_scans(lf, mk, c):
    cs, ss = {1: lf}, {1: lf}
    h = 1
    while h < c:
        x, y = cs[h], ss[h]
        if h < SUBLANES:
            x3 = x.reshape(c // SUBLANES, SUBLANES, x.shape[1])
            y3 = y.reshape(c // SUBLANES, SUBLANES, y.shape[1])
            for s in range(1, h + 1):
                x = x + mk[(h, s, 0)] * pltpu.roll(x3, s, 1).reshape(x.shape)
                y = y + mk[(h, s, 1)] * pltpu.roll(y3, SUBLANES - s, 1).reshape(y.shape)
            cs[2 * h], ss[2 * h] = x, y
        else:
            px = [x[j * h:(j + 1) * h] for j in range(c // h)]
            py = [y[j * h:(j + 1) * h] for j in range(c // h)]
            nx = [px[j] + px[j - 1][h - 1:h] if j % 2 == 1 else px[j] for j in range(c // h)]
            ny = [py[j] + py[j + 1][0:1] if j % 2 == 0 else py[j] for j in range(c // h)]
            cs[2 * h], ss[2 * h] = jnp.concatenate(nx, axis=0), jnp.concatenate(ny, axis=0)
        h *= 2
    return cs, ss


def _hg_kernel(q_ref, f_ref, v_ref, lb_ref, o_ref, st_ref, *, tc, c, heads, rev):
    j = pl.program_id(1)

    @pl.when(j == 0)
    def _():
        st_ref[...] = jnp.zeros_like(st_ref)

    ti = lax.broadcasted_iota(jnp.int32, (c, c), 0)
    si = lax.broadcasted_iota(jnp.int32, (c, c), 1)
    nt = (((1,), (1,)), ((), ()))
    tn = (((0,), (0,)), ((), ()))
    mk = _seg_masks(c, HG_EXPAND)
    masks, hh = {}, 1
    while hh < c:
        tb, sb = ti // hh, si // hh
        masks[hh] = (jnp.logical_and(tb % 2 == 0, sb == tb + 1) if rev
                     else jnp.logical_and(tb % 2 == 1, sb == tb - 1))
        hh *= 2

    def head(h, carry):
        cols = pl.ds(pl.multiple_of(h * HG_EXPAND, HG_EXPAND), HG_EXPAND)
        lb = lb_ref[:, cols]
        chunks = range(tc // c)
        for ci in (reversed(chunks) if rev else chunks):
            rows = slice(ci * c, (ci + 1) * c)
            q = _silu(q_ref[0, rows, cols])
            f = lb + (1.0 - lb) * _sigmoid(f_ref[0, rows, cols])
            k = 1.0 - f
            lf = jnp.log(f)
            v = v_ref[0, rows, cols].astype(BF16)
            cs, ss = _seg_scans(lf, mk, c)
            qs, ks = (ss, cs) if rev else (cs, ss)
            att = jnp.where(ti == si, lax.dot_general(q.astype(BF16), k.astype(BF16), nt,
                                                      preferred_element_type=F32), 0.0)
            hh = 1
            while hh < c:
                qe = (q * jnp.exp(qs[hh])).astype(BF16)
                ke = (k * jnp.exp(ks[hh] - lf)).astype(BF16)
                a = lax.dot_general(qe, ke, nt, preferred_element_type=F32)
                att = att + jnp.where(masks[hh], a, 0.0)
                hh *= 2
            st = st_ref[h]
            qe = (q * jnp.exp(qs[c])).astype(BF16)
            o = lax.dot_general(qe, st.astype(BF16), nt, preferred_element_type=F32)
            o = o + jnp.dot(att.astype(BF16), v, preferred_element_type=F32)
            o_ref[0, rows, cols] = o
            ke = (k * jnp.exp(ks[c] - lf)).astype(BF16)
            tot = qs[c][0:1] if rev else qs[c][c - 1:c]
            st_ref[h] = st * jnp.exp(tot) + lax.dot_general(v, ke, tn, preferred_element_type=F32)
        return carry

    lax.fori_loop(0, heads, head, 0, unroll=4)


def _hgrn(zc, nc, lb, rev):
    b, lt, cols = zc.shape
    w = cols // 5
    heads = w // HG_EXPAND
    tc = min(256, nc)
    c = min(HG_CHUNK, tc)
    ncb, nb = nc // tc, lt // tc

    def blk(j):
        return jnp.where(j < ncb, ncb - 1 - j, nb - 1 - (j - ncb)) if rev else j

    fcol = 2 if rev else 1
    return pl.pallas_call(
        functools.partial(_hg_kernel, tc=tc, c=c, heads=heads, rev=rev),
        out_shape=jax.ShapeDtypeStruct((b, lt, w), F32),
        grid=(b, nb),
        in_specs=[pl.BlockSpec((1, tc, w), lambda bi, j: (bi, blk(j), 0)),
                  pl.BlockSpec((1, tc, w), lambda bi, j: (bi, blk(j), fcol)),
                  pl.BlockSpec((1, tc, w), lambda bi, j: (bi, blk(j), 3)),
                  pl.BlockSpec((1, w), lambda bi, j: (0, 0))],
        out_specs=pl.BlockSpec((1, tc, w), lambda bi, j: (bi, blk(j), 0)),
        scratch_shapes=[pltpu.VMEM((heads, HG_EXPAND, HG_EXPAND), F32)],
        compiler_params=_cparams(("arbitrary", "arbitrary"), 32),
        name="hgrn2_bwd" if rev else "hgrn2_fwd",
    )(zc, zc, zc, lb.reshape(1, w))


def _merge_kernel(oaf_ref, oab_ref, ga_ref, yb_ref, of_ref, ob_ref, gc_ref, hn_ref, gl0_ref, gl1_ref, gl2_ref, wb_ref,
                  o_ref, *, heads):
    gls = (gl0_ref, gl1_ref, gl2_ref)
    ya = (oaf_ref[0] + oab_ref[0]) * ga_ref[0]
    oc = of_ref[0] + ob_ref[0]
    parts = []
    for h in range(heads):
        x = oc[:, h * HG_EXPAND:(h + 1) * HG_EXPAND]
        parts.append(x * lax.rsqrt(jnp.mean(x * x, axis=-1, keepdims=True) + NORM_EPS))
    yc = jnp.concatenate(parts, axis=-1) * hn_ref[...] * _silu(gc_ref[0])
    acc = None
    for bi, y in enumerate((ya.astype(BF16), yb_ref[0], yc.astype(BF16))):
        term = _sigmoid_t(gls[bi][0].astype(F32)) * jnp.dot(y, wb_ref[bi], preferred_element_type=F32)
        acc = term if acc is None else acc + term
    o_ref[0] = acc.astype(BF16)


def _merge(oaf, oab, ga, yb, of, ob, zc, hn, gl, wbr, row0, tl=128, tn=2048):
    b, lt, w = ga.shape
    d = wbr.shape[2]
    tl = _pick(math.gcd(lt, row0), (tl, 64))
    r0 = row0 // tl
    heads = w // HG_EXPAND
    nn = d // tn
    row = lambda n, bi, i: (bi, i + r0, 0)
    glspec = lambda br: pl.BlockSpec((1, tl, tn), lambda n, bi, i: (bi, i + r0, br * nn + n))
    return pl.pallas_call(
        functools.partial(_merge_kernel, heads=heads),
        out_shape=jax.ShapeDtypeStruct((b, lt - row0, d), BF16),
        grid=(d // tn, b, lt // tl - r0),
        in_specs=[pl.BlockSpec((1, tl, w), row),
                  pl.BlockSpec((1, tl, w), row),
                  pl.BlockSpec((1, tl, w), row),
                  pl.BlockSpec((1, tl, w), lambda n, bi, i: (bi, i, 0)),
                  pl.BlockSpec((1, tl, w), row),
                  pl.BlockSpec((1, tl, w), row),
                  pl.BlockSpec((1, tl, w), lambda n, bi, i: (bi, i + r0, 4)),
                  pl.BlockSpec((1, w), lambda n, bi, i: (0, 0)),
                  glspec(0), glspec(1), glspec(2),
                  pl.BlockSpec((3, w, tn), lambda n, bi, i: (0, 0, n))],
        out_specs=pl.BlockSpec((1, tl, tn), lambda n, bi, i: (bi, i, n)),
        compiler_params=_cparams(("arbitrary", "arbitrary", "arbitrary"), 48),
        name="branch_merge",
    )(oaf, oab, ga, yb, of, ob, zc, hn, gl, gl, gl, wbr)


def _proj_post_kernel(a_ref, w_ref, x_ref, g_ref, gate_ref, o_ref):
    mx = jnp.dot(a_ref[0], w_ref[...], preferred_element_type=F32)
    mn = mx * lax.rsqrt(jnp.mean(mx * mx, axis=-1, keepdims=True) + NORM_EPS) * g_ref[...]
    o_ref[0] = x_ref[0] + gate_ref[0, 0] * mn


def _proj_post(a, wout, xa, g, gsel, nc, row0, tl=256):
    b, lt, d = xa.shape
    tl = min(tl, nc)
    nct = nc // tl
    r0 = row0 // tl
    return pl.pallas_call(
        _proj_post_kernel,
        out_shape=jax.ShapeDtypeStruct((b, lt, d), F32),
        grid=(b, lt // tl - r0),
        in_specs=[pl.BlockSpec((1, tl, d), lambda bi, i: (bi, i, 0)),
                  pl.BlockSpec((d, d), lambda bi, i: (0, 0)),
                  pl.BlockSpec((1, tl, d), lambda bi, i: (bi, i + r0, 0)),
                  pl.BlockSpec((1, d), lambda bi, i: (0, 0)),
                  pl.BlockSpec((1, 1, 1, d), lambda bi, i: (bi, jnp.where(i + r0 >= nct, 1, 0), 0, 0))],
        out_specs=pl.BlockSpec((1, tl, d), lambda bi, i: (bi, i + r0, 0)),
        input_output_aliases={2: 0},
        compiler_params=_cparams(("arbitrary", "arbitrary"), 48),
        name="out_proj_post",
    )(a, wout, xa, g.reshape(1, d), gsel)


def _rank_kernel(afft_ref, slot_ref, *, n, cap, tw):
    aff = afft_ref[0]
    bits = lax.bitcast_convert_type(aff, jnp.int32)
    e_num = aff.shape[0]

    def count(mask):
        return jnp.sum(jnp.where(mask, 1.0, 0.0), axis=1, keepdims=True)

    def bit_step(i, thr):
        cand = thr | lax.shift_left(jnp.int32(1), 30 - i)
        return jnp.where(count(bits >= cand) >= cap, cand, thr)

    thr = lax.fori_loop(0, 31, bit_step, jnp.zeros((e_num, 1), jnp.int32))
    gt = bits > thr
    eq = bits == thr
    need = cap - count(gt)
    tri = jnp.where(lax.broadcasted_iota(jnp.int32, (tw, tw), 0) < lax.broadcasted_iota(jnp.int32, (tw, tw), 1),
                    1.0, 0.0).astype(BF16)

    def prefix(mask):
        parts, carry = [], jnp.zeros((e_num, 1), F32)
        for j in range(n // tw):
            m = jnp.where(mask[:, j * tw:(j + 1) * tw], 1.0, 0.0)
            parts.append(jnp.dot(m.astype(BF16), tri, preferred_element_type=F32) + carry)
            carry = carry + jnp.sum(m, axis=1, keepdims=True)
        return jnp.concatenate(parts, axis=1)

    sel = jnp.logical_or(gt, jnp.logical_and(eq, prefix(eq) < need))
    slot_ref[0] = jnp.where(sel, prefix(sel), float(cap)).astype(jnp.int32)


def _rank(afft, cap):
    b, e_num, n = afft.shape
    tw = min(LANES, n)
    return pl.pallas_call(
        functools.partial(_rank_kernel, n=n, cap=cap, tw=tw),
        out_shape=jax.ShapeDtypeStruct((b, e_num, n), jnp.int32),
        grid=(b,),
        in_specs=[pl.BlockSpec((1, e_num, n), lambda bi: (bi, 0, 0))],
        out_specs=pl.BlockSpec((1, e_num, n), lambda bi: (bi, 0, 0)),
        compiler_params=_cparams(("arbitrary",), 32),
        name="ec_rank",
    )(afft)


def _gather_kernel(slot_ref, afft_ref, h_ref, o_ref, g_ref, *, cap):
    e = pl.program_id(1)
    sl = slot_ref[0, pl.ds(e, 1), :]
    hit = lax.broadcasted_iota(jnp.int32, (cap, 1), 0) == sl
    o_ref[0] = jnp.dot(jnp.where(hit, 1.0, 0.0).astype(BF16), h_ref[0], preferred_element_type=F32).astype(BF16)
    gate = jnp.sum(jnp.where(hit, afft_ref[0, pl.ds(e, 1), :], 0.0), axis=1, keepdims=True)
    g_ref[0] = jnp.broadcast_to(gate, (cap, LANES))


def _gather(slot, afft, hs, cap):
    b, e_num, n = slot.shape
    d = hs.shape[2]
    return pl.pallas_call(
        functools.partial(_gather_kernel, cap=cap),
        out_shape=[jax.ShapeDtypeStruct((e_num, b * cap, d), BF16),
                   jax.ShapeDtypeStruct((e_num, b * cap, LANES), F32)],
        grid=(b, e_num),
        in_specs=[pl.BlockSpec((1, e_num, n), lambda bi, e: (bi, 0, 0)),
                  pl.BlockSpec((1, e_num, n), lambda bi, e: (bi, 0, 0)),
                  pl.BlockSpec((1, n, d), lambda bi, e: (bi, 0, 0))],
        out_specs=[pl.BlockSpec((1, cap, d), lambda bi, e: (e, bi, 0)),
                   pl.BlockSpec((1, cap, LANES), lambda bi, e: (e, bi, 0))],
        compiler_params=_cparams(("arbitrary", "arbitrary"), 40),
        name="ec_gather",
    )(slot, afft, hs)


def _ffn_kernel(x_ref, g_ref, w1_ref, w3_ref, w2_ref, o_ref, acc_ref):
    f = pl.program_id(2)
    x = x_ref[0]
    h1 = jnp.dot(x, w1_ref[0, 0].astype(BF16), preferred_element_type=F32)
    h3 = jnp.dot(x, w3_ref[0, 0].astype(BF16), preferred_element_type=F32)
    hid = (_silu(h1) * h3).astype(BF16)

    @pl.when(f == 0)
    def _():
        acc_ref[...] = jnp.zeros_like(acc_ref)

    d = acc_ref.shape[1]
    for n0 in range(0, d, FFN_OUT_CHUNK):
        cols = slice(n0, n0 + FFN_OUT_CHUNK)
        acc_ref[:, cols] += jnp.dot(hid, w2_ref[0, 0, :, cols].astype(BF16), preferred_element_type=F32)

    @pl.when(f == pl.num_programs(2) - 1)
    def _():
        o_ref[0] = (acc_ref[...] * g_ref[0, :, 0:1]).astype(BF16)


def _ffn(xe, gate, w1, w3, w2, layer, tf=256):
    e_num, m, d = xe.shape
    ff = w1.shape[3]
    tm = m
    once = pl.Buffered(1)
    return pl.pallas_call(
        _ffn_kernel,
        out_shape=jax.ShapeDtypeStruct((e_num, m, d), BF16),
        grid=(e_num, m // tm, ff // tf),
        in_specs=[pl.BlockSpec((1, tm, d), lambda e, i, f: (e, i, 0), pipeline_mode=once),
                  pl.BlockSpec((1, tm, LANES), lambda e, i, f: (e, i, 0), pipeline_mode=once),
                  pl.BlockSpec((1, 1, d, tf), lambda e, i, f: (layer, e, 0, f)),
                  pl.BlockSpec((1, 1, d, tf), lambda e, i, f: (layer, e, 0, f)),
                  pl.BlockSpec((1, 1, tf, d), lambda e, i, f: (layer, e, f, 0))],
        out_specs=pl.BlockSpec((1, tm, d), lambda e, i, f: (e, i, 0), pipeline_mode=once),
        scratch_shapes=[pltpu.VMEM((tm, d), F32)],
        compiler_params=_cparams(("arbitrary", "arbitrary", "arbitrary"), 58),
        name="ec_ffn",
    )(xe, gate, w1, w3, w2)


def _combine_kernel(slot_ref, ye_ref, x_ref, g_ref, gate_ref, o_ref, *, cap, e_num):
    lane = lax.broadcasted_iota(jnp.int32, (1, e_num), 1)
    pos = lax.broadcasted_iota(jnp.int32, (1, cap), 1)
    sl = slot_ref[0]
    y = None
    for e in range(e_num):
        se = jnp.sum(jnp.where(lane == e, sl, 0), axis=1, keepdims=True)
        pt = jnp.where(se == pos, 1.0, 0.0).astype(BF16)
        part = jnp.dot(pt, ye_ref[e], preferred_element_type=F32)
        y = part if y is None else y + part
    yn = y * lax.rsqrt(jnp.mean(y * y, axis=-1, keepdims=True) + NORM_EPS) * g_ref[...]
    o_ref[0] = x_ref[0] + gate_ref[0, 0] * yn


def _combine(slot_c, ye, xa, g, gsel, seg, row0, slot0, cap, final, tt=256):
    b, n, e_num = slot_c.shape
    d = xa.shape[2]
    tt = _pick(math.gcd(n, row0), (tt, 128, 64))
    assert row0 % tt == 0 and slot0 % cap == 0
    r0, s0 = row0 // tt, slot0 // cap
    return pl.pallas_call(
        functools.partial(_combine_kernel, cap=cap, e_num=e_num),
        out_shape=jax.ShapeDtypeStruct((b, n, d) if final else xa.shape, F32),
        grid=(b, n // tt),
        in_specs=[pl.BlockSpec((1, tt, e_num), lambda bi, j: (bi, j, 0)),
                  pl.BlockSpec((e_num, cap, d), lambda bi, j: (0, s0 + bi, 0)),
                  pl.BlockSpec((1, tt, d), lambda bi, j: (bi, r0 + j, 0)),
                  pl.BlockSpec((1, d), lambda bi, j: (0, 0)),
                  pl.BlockSpec((1, 1, 1, d), lambda bi, j: (bi, seg, 0, 0))],
        out_specs=pl.BlockSpec((1, tt, d), lambda bi, j: (bi, (0 if final else r0) + j, 0)),
        input_output_aliases={} if final else {2: 0},
        compiler_params=_cparams(("arbitrary", "arbitrary"), 56),
        name="ec_combine_post",
    )(slot_c, ye, xa, g.reshape(1, d), gsel)


def _sel(mod_c, mod_x, idx):
    b = mod_x.shape[0]
    mc = jnp.broadcast_to(mod_c[jnp.array(idx)][None], (b, len(idx), mod_c.shape[-1]))
    return jnp.stack([mc, mod_x[:, jnp.array(idx)]], axis=1)


def _layer(xa, nc, mod_x, mod_c, p, lam_init, hg_lb, ct, st, with_ctx):
    b, lt, d = xa.shape
    w = d // 2
    a_cols = 3 * w + 2 * RW_DECAY_RANK + 2 * RW_ICL_RANK + RW_GATE_RANK
    col_b = a_cols
    col_c = col_b + 3 * w
    col_g = col_c + 5 * w

    h = _norm_mod(xa, p['norm_g'][0], _sel(mod_c, mod_x, (0, 1)), nc)
    h2 = h.reshape(b * lt, d)
    w_in = p['w_in']
    za = _matmul(h2, w_in[:, :col_b].astype(BF16), F32, tn=a_cols // 3).reshape(b, lt, a_cols)
    zc = _matmul(h2, w_in[:, col_c:col_g].astype(BF16), F32).reshape(b, lt, 5 * w)
    gl = _matmul(h2, w_in[:, col_g:].astype(BF16), BF16).reshape(b, lt, 3 * d)

    streams, ga = _rw_prep(za, nc, p['rw_mu'], p['rw_w0'], p['rw_wB'], p['rw_a0'], p['rw_aB'], p['rw_gB'])
    oaf, oab = _rw_chunk(streams, nc, p['rw_kk'], p['rw_ka'], p['rw_rk'], p['rw_gn'])

    zbr = _proj_rope(h2, w_in[:, col_b:col_c].astype(BF16), ct, st, lt).reshape(b, lt, 3 * w)
    yb = _attention(zbr, nc, p['da_lambda'], p['da_subln'], lam_init, with_ctx)

    of = _hgrn(zc, nc, hg_lb, rev=False)
    ob = _hgrn(zc, nc, hg_lb, rev=True)

    hn = jnp.tile(p['hg_norm'], w // HG_EXPAND).reshape(1, w)
    skip = 0 if with_ctx else nc
    merged = _merge(oaf, oab, ga, yb, of, ob, zc, hn, gl, p['w_branch'].astype(BF16), skip)
    xa = _proj_post(merged, p['w_out'].astype(BF16), xa, p['norm_g'][1], _sel(mod_c, mod_x, (2,)), nc, skip)

    hm, afft = _norm_mod(xa, p['norm_g'][2], _sel(mod_c, mod_x, (3, 4)), nc, router_t=p['moe_router'].T, row0=skip)
    gsel = _sel(mod_c, mod_x, (5,))
    sets = [(nc, lt - nc, 1)] + ([(0, nc, 0)] if with_ctx else [])
    routed, xes, gates = [], [], []
    for row0, nn, seg in sets:
        cap = EC_CAPACITY_FACTOR * nn // N_EXPERTS
        at = afft[:, :, row0 - skip:row0 - skip + nn]
        slot = _rank(at, cap)
        xe_s, gate_s = _gather(slot, at, hm[:, row0 - skip:row0 - skip + nn], cap)
        xes.append(xe_s)
        gates.append(gate_s)
        routed.append((jnp.swapaxes(slot, 1, 2), row0, seg, cap))
    xe = xes[0] if len(xes) == 1 else jnp.concatenate(xes, axis=1)
    gate = gates[0] if len(gates) == 1 else jnp.concatenate(gates, axis=1)
    ye = _ffn(xe, gate, p['moe_w1'], p['moe_w3'], p['moe_w2'], p['layer'])
    slot0 = 0
    for slot_c, row0, seg, cap in routed:
        xa = _combine(slot_c, ye, xa, p['norm_g'][3], gsel, seg, row0, slot0, cap, final=not with_ctx)
        slot0 += b * cap
    return xa


def kernel(x, c, ctx, c_ctx, w_ada, b_ada, norm_g, w_in, rw_mu, rw_w0, rw_wB, rw_a0, rw_aB, rw_gB, rw_kk, rw_ka, rw_rk, rw_gn, da_lambda, da_subln, hg_lb_logits, hg_norm, w_branch, w_out, moe_router, moe_w1, moe_w3, moe_w2):
    b, seq, d = x.shape
    nc = ctx.shape[1]
    depth = w_ada.shape[0]
    ct, st = _rope_tables(seq, nc)
    lb_w = jax.nn.softmax(hg_lb_logits.astype(F32), axis=0)
    hg_lb = jnp.cumsum(lb_w, axis=0) - lb_w[0]
    rows = ((b + 1 + SUBLANES - 1) // SUBLANES) * SUBLANES
    cc = jnp.zeros((rows, d), F32).at[:b].set(c).at[b].set(c_ctx)
    mod = _modulation(cc, w_ada, b_ada)
    xa = jnp.concatenate([ctx, x], axis=1)
    for l in range(depth):
        p = dict(norm_g=norm_g[l], w_in=w_in[l], rw_mu=rw_mu[l], rw_w0=rw_w0[l], rw_wB=rw_wB[l],
                 rw_a0=rw_a0[l], rw_aB=rw_aB[l], rw_gB=rw_gB[l], rw_kk=rw_kk[l], rw_ka=rw_ka[l],
                 rw_rk=rw_rk[l], rw_gn=rw_gn[l], da_lambda=da_lambda[l], da_subln=da_subln[l],
                 hg_norm=hg_norm[l], w_branch=w_branch[l], w_out=w_out[l], moe_router=moe_router[l],
                 moe_w1=moe_w1, moe_w3=moe_w3, moe_w2=moe_w2, layer=l)
        mod_x = mod[l, :b].reshape(b, 6, d)
        mod_c = mod[l, b].reshape(6, d)
        lam_init = 0.8 - 0.6 * math.exp(-0.3 * l)
        xa = _layer(xa, nc, mod_x, mod_c, p, lam_init, hg_lb[l], ct, st, with_ctx=l < depth - 1)
    return xa
```

```python
import functools
import math

import jax
import jax.numpy as jnp
from jax import lax
from jax.experimental import pallas as pl
from jax.experimental.pallas import tpu as pltpu

F32 = jnp.float32
BF16 = jnp.bfloat16
HIGHEST = lax.Precision.HIGHEST

GRID_W = 64
RW_HEAD_DIM = 64
RW_DECAY_RANK = 64
RW_ICL_RANK = 64
RW_GATE_RANK = 128
RW_GN_EPS = 64e-5
DA_QK_DIM = 64
DA_V_DIM = 128
ROPE_THETA = 10000.0
ROPE_AXIS_FREQS = 16
DA_SUBLN_EPS = 1e-5
HG_EXPAND = 128
N_EXPERTS = 16
EC_CAPACITY_FACTOR = 2
NORM_EPS = 1e-6

LANES = 128
SUBLANES = 8
V7X_VMEM_BYTES = 64 * 1024 * 1024
HG_CHUNK = 64
RW_CHUNK = 64
RW_GROUP = 256
FFN_OUT_CHUNK = 512


def _cparams(sem, vmem_mb):
    limit = int(vmem_mb * 1024 * 1024)
    assert limit < V7X_VMEM_BYTES
    return pltpu.CompilerParams(dimension_semantics=sem, vmem_limit_bytes=limit)


def _sigmoid(x):
    return 1.0 / (1.0 + jnp.exp(-x))


def _sigmoid_t(x):
    return 0.5 * jnp.tanh(0.5 * x) + 0.5


def _silu(x):
    return x * _sigmoid_t(x)


def _mod_kernel(c_ref, w_ref, b_ref, o_ref):
    a = _silu(c_ref[...]).astype(BF16)
    o_ref[0] = jnp.dot(a, w_ref[0].astype(BF16), preferred_element_type=F32) + b_ref[0]


def _modulation(cc, w_ada, b_ada):
    depth, d, n = w_ada.shape
    rows = cc.shape[0]
    tn = 1024
    return pl.pallas_call(
        _mod_kernel,
        out_shape=jax.ShapeDtypeStruct((depth, rows, n), F32),
        grid=(depth, n // tn),
        in_specs=[pl.BlockSpec((rows, d), lambda l, j: (0, 0)),
                  pl.BlockSpec((1, d, tn), lambda l, j: (l, 0, j)),
                  pl.BlockSpec((1, 1, tn), lambda l, j: (l, 0, j))],
        out_specs=pl.BlockSpec((1, rows, tn), lambda l, j: (l, 0, j)),
        compiler_params=_cparams(("arbitrary", "arbitrary"), 40),
        name="adaln_mod",
    )(cc, w_ada, b_ada.reshape(depth, 1, n))


def _norm_mod_kernel(x_ref, g_ref, ms_ref, *rest, with_router):
    x = x_ref[0]
    xn = x * lax.rsqrt(jnp.mean(x * x, axis=-1, keepdims=True) + NORM_EPS) * g_ref[...]
    h = xn * (1.0 + ms_ref[0, 0, 1:2, :]) + ms_ref[0, 0, 0:1, :]
    if with_router:
        rt_ref, h_ref, aff_ref = rest
        h_ref[0] = h.astype(BF16)
        lt = lax.dot_general(rt_ref[...], h, (((1,), (1,)), ((), ())),
                             precision=HIGHEST, preferred_element_type=F32)
        e = jnp.exp(lt - jnp.max(lt, axis=0, keepdims=True))
        aff_ref[0] = e / jnp.sum(e, axis=0, keepdims=True)
    else:
        (h_ref,) = rest
        h_ref[0] = h.astype(BF16)


def _norm_mod(xa, g, msel, nc, router_t=None, row0=0, tl=256):
    b, lt, d = xa.shape
    tl = min(tl, nc)
    nct = nc // tl
    r0 = row0 // tl
    with_router = router_t is not None
    in_specs = [pl.BlockSpec((1, tl, d), lambda bi, i: (bi, i + r0, 0)),
                pl.BlockSpec((1, d), lambda bi, i: (0, 0)),
                pl.BlockSpec((1, 1, 2, d), lambda bi, i: (bi, jnp.where(i + r0 >= nct, 1, 0), 0, 0))]
    args = [xa, g.reshape(1, d), msel]
    out_shape = [jax.ShapeDtypeStruct((b, lt - row0, d), BF16)]
    out_specs = [pl.BlockSpec((1, tl, d), lambda bi, i: (bi, i, 0))]
    if with_router:
        e = router_t.shape[0]
        in_specs.append(pl.BlockSpec((e, d), lambda bi, i: (0, 0)))
        args.append(router_t)
        out_shape.append(jax.ShapeDtypeStruct((b, e, lt - row0), F32))
        out_specs.append(pl.BlockSpec((1, e, tl), lambda bi, i: (bi, 0, i)))
    res = pl.pallas_call(
        functools.partial(_norm_mod_kernel, with_router=with_router),
        out_shape=out_shape, grid=(b, lt // tl - r0), in_specs=in_specs, out_specs=out_specs,
        compiler_params=_cparams(("arbitrary", "arbitrary"), 32),
        name="norm_mod_router" if with_router else "norm_mod",
    )(*args)
    return res if with_router else res[0]


def _matmul_kernel(a_ref, w_ref, o_ref):
    o_ref[...] = jnp.dot(a_ref[...], w_ref[...], preferred_element_type=F32).astype(o_ref.dtype)


def _pick(n, cands):
    for c in cands:
        if n % c == 0:
            return c
    return n


def _matmul(a, w, out_dtype, tm=1024, tn=1024):
    m, k = a.shape
    n = w.shape[1]
    tm = _pick(m, (tm, 768, 512, 384, 256, 128))
    tn = _pick(n, (tn, 768, 512, 384, 256, 128))
    return pl.pallas_call(
        _matmul_kernel,
        out_shape=jax.ShapeDtypeStruct((m, n), out_dtype),
        grid=(m // tm, n // tn),
        in_specs=[pl.BlockSpec((tm, k), lambda i, j: (i, 0)),
                  pl.BlockSpec((k, tn), lambda i, j: (0, j))],
        out_specs=pl.BlockSpec((tm, tn), lambda i, j: (i, j)),
        compiler_params=_cparams(("arbitrary", "arbitrary"), 48),
        name="matmul",
    )(a, w)


def _rw_prep_kernel(z_ref, zp_ref, zn_ref, mu_ref, w0_ref, wb_ref, a0_ref, ab_ref, gb_ref,
                    o_ref, g_ref, *, tl, nct, nt, w):
    i = pl.program_id(1)
    z = z_ref[0]
    row = lax.broadcasted_iota(jnp.int32, (tl, 1), 0)
    seg_start = jnp.logical_or(i == 0, i == nct)
    seg_end = jnp.logical_or(i == nct - 1, i == nt - 1)
    prev_row = jnp.where(seg_start, 0.0, zp_ref[0, SUBLANES - 1:SUBLANES, :])
    next_row = jnp.where(seg_end, 0.0, zn_ref[0, 0:1, :])
    prev = jnp.where(row == 0, prev_row, pltpu.roll(z, 1, 0))
    nxt = jnp.where(row == tl - 1, next_row, pltpu.roll(z, tl - 1, 0))
    zs = z + mu_ref[0:1, :] * (prev - z) + mu_ref[1:2, :] * (nxt - z)
    o_ref[0, 0] = zs[:, 0:w]
    o_ref[0, 1] = zs[:, w:2 * w]
    o_ref[0, 2] = zs[:, 2 * w:3 * w]
    o4 = 3 * w + 2 * RW_DECAY_RANK
    o5 = o4 + 2 * RW_ICL_RANK
    wd = jnp.tanh(zs[:, 3 * w:o4])
    wl = jnp.dot(wd.astype(BF16), wb_ref[...], preferred_element_type=F32) + w0_ref[...]
    dec = -math.exp(-0.5) * _sigmoid_t(wl)
    o_ref[0, 3] = dec[:, 0:w]
    o_ref[0, 4] = dec[:, w:2 * w]
    al = jnp.dot(zs[:, o4:o5].astype(BF16), ab_ref[...], preferred_element_type=F32) + a0_ref[...]
    av = _sigmoid_t(al)
    o_ref[0, 5] = av[:, 0:w]
    o_ref[0, 6] = av[:, w:2 * w]
    g_ref[0] = jnp.dot(_sigmoid_t(zs[:, o5:]).astype(BF16), gb_ref[...], preferred_element_type=F32)


def _blockdiag2(m):
    r, w = m.shape[1], m.shape[2]
    z = jnp.zeros((r, w), m.dtype)
    return jnp.concatenate([jnp.concatenate([m[0], z], axis=1), jnp.concatenate([z, m[1]], axis=1)], axis=0)


def _rw_prep(za, nc, mu, w0, wb, a0, ab, gb, tl=256):
    b, lt, acols = za.shape
    w = w0.shape[1]
    tl = min(tl, nc)
    nct, nt = nc // tl, lt // tl
    r8 = tl // SUBLANES
    nb8 = lt // SUBLANES
    full = lambda shp: pl.BlockSpec(shp, lambda bi, i: tuple(0 for _ in shp))
    kern = functools.partial(_rw_prep_kernel, tl=tl, nct=nct, nt=nt, w=w)
    return pl.pallas_call(
        kern,
        out_shape=[jax.ShapeDtypeStruct((b, 7, lt, w), F32), jax.ShapeDtypeStruct((b, lt, w), F32)],
        grid=(b, nt),
        in_specs=[pl.BlockSpec((1, tl, acols), lambda bi, i: (bi, i, 0)),
                  pl.BlockSpec((1, SUBLANES, acols), lambda bi, i: (bi, jnp.maximum(i * r8 - 1, 0), 0)),
                  pl.BlockSpec((1, SUBLANES, acols), lambda bi, i: (bi, jnp.minimum((i + 1) * r8, nb8 - 1), 0)),
                  full((2, acols)), full((1, 2 * w)), full((2 * RW_DECAY_RANK, 2 * w)),
                  full((1, 2 * w)), full((2 * RW_ICL_RANK, 2 * w)), full((RW_GATE_RANK, w))],
        out_specs=[pl.BlockSpec((1, 7, tl, w), lambda bi, i: (bi, 0, i, 0)),
                   pl.BlockSpec((1, tl, w), lambda bi, i: (bi, i, 0))],
        compiler_params=_cparams(("arbitrary", "arbitrary"), 48),
        name="rwkv_prep",
    )(za, za, za, mu, w0.reshape(1, 2 * w), _blockdiag2(wb).astype(BF16), a0.reshape(1, 2 * w),
      _blockdiag2(ab).astype(BF16), gb.astype(BF16))


def _rw_chunk_kernel(xf_ref, lwf_ref, af_ref, xb_ref, lwb_ref, ab_ref, kk_ref, ka_ref, rk_ref, gn0_ref, gn1_ref,
                     of_ref, ob_ref, s_ref, *, c, w):
    j = pl.program_id(1)

    @pl.when(j == 0)
    def _():
        s_ref[...] = jnp.zeros_like(s_ref)

    gw, n = RW_GROUP, RW_HEAD_DIM
    ng = w // gw
    row = lax.broadcasted_iota(jnp.int32, (c, 1), 0)
    pos = lax.broadcasted_iota(jnp.int32, (1, gw), 1) % n
    bdmask = (lax.broadcasted_iota(jnp.int32, (gw, 1), 0) // n) == (lax.broadcasted_iota(jnp.int32, (1, gw), 1) // n)
    ones_bd = jnp.where(bdmask, 1.0, 0.0).astype(BF16)
    eye = jnp.where(pos == row, 1.0, 0.0)
    nt = (((1,), (1,)), ((), ()))
    tn = (((0,), (0,)), ((), ()))
    chains = [(d, g) for d in range(2) for g in range(ng)]
    refs = ((xf_ref, lwf_ref, af_ref, of_ref), (xb_ref, lwb_ref, ab_ref, ob_ref))

    def bd(x):
        return jnp.where(bdmask, jnp.concatenate([x] * (gw // c), axis=0), 0.0).astype(BF16)

    def gsum(xs):
        pieces = []
        for x in xs:
            hi = x.astype(BF16)
            r1 = x - hi.astype(F32)
            mid = r1.astype(BF16)
            pieces += [hi, mid, (r1 - mid.astype(F32)).astype(BF16)]
        tot = jnp.dot(jnp.concatenate(pieces, axis=0), ones_bd, preferred_element_type=F32)
        return [tot[3 * i * c:(3 * i + 1) * c] + tot[(3 * i + 1) * c:(3 * i + 2) * c] + tot[(3 * i + 2) * c:(3 * i + 3) * c]
                for i in range(len(xs))]

    def mm(x, ybd):
        return jnp.dot(x.astype(BF16), ybd, preferred_element_type=F32)

    st = []
    for d, g in chains:
        x_ref, lw_ref, a_ref, _ = refs[d]
        sl = slice(g * gw, (g + 1) * gw)
        q = dict(sl=sl, rev=d == 1, r=x_ref[0, 0, :, sl], k=x_ref[0, 1, :, sl], v=x_ref[0, 2, :, sl],
                 lw=lw_ref[0, 0, :, sl], a=a_ref[0, 0, :, sl])
        q['kx'] = q['k'] * kk_ref[:, sl]
        q['kt'] = q['k'] * (1.0 + (q['a'] - 1.0) * ka_ref[:, sl])
        st.append(q)
    for q in st:
        q['ss'], q['bon'] = gsum([q['kx'] * q['kx'], q['r'] * q['kt'] * rk_ref[:, q['sl']]])
    for q in st:
        rev = q['rev']
        kk = q['kx'] / jnp.maximum(jnp.sqrt(q['ss']), 1e-12)
        q['b'] = q['a'] * kk
        cw = q['lw']
        sft = 1
        while sft < c:
            if rev:
                cw = cw + jnp.where(row < c - sft, pltpu.roll(cw, c - sft, 0), 0.0)
            else:
                cw = cw + jnp.where(row >= sft, pltpu.roll(cw, sft, 0), 0.0)
            sft *= 2
        q['tot'] = cw[0:1] if rev else cw[c - 1:c]
        invp = jnp.exp(-cw)
        q['e2'] = jnp.exp(q['tot'] - cw)
        q['kr'] = jnp.concatenate([kk * jnp.exp(cw - q['lw']), q['r'] * jnp.exp(cw)], axis=0).astype(BF16)
        q['ktb'] = bd(q['kt'] * invp)
        q['bb'] = bd(q['b'] * invp)
        q['vbd'] = bd(q['v'])
    for q in st:
        strict = (pos > row) if q['rev'] else (pos < row)
        incl = (pos >= row) if q['rev'] else (pos <= row)
        g1 = lax.dot_general(q['kr'], q['ktb'], nt, preferred_element_type=F32)
        g2 = lax.dot_general(q['kr'], q['bb'], nt, preferred_element_type=F32)
        q['m_kt'] = jnp.where(strict, g1[:c], 0.0)
        q['n_kt'] = jnp.where(incl, g1[c:], 0.0)
        q['x'] = jnp.where(strict, g2[:c], 0.0)
        q['n_b'] = jnp.where(incl, g2[c:], 0.0)
        q['t'] = eye - q['x']
        q['xb'] = bd(q['x'])
    for q in st:
        mn = mm(jnp.concatenate([q['m_kt'], q['n_kt']], axis=0), q['vbd'])
        q['mkv'], q['nkv'] = mn[:c], mn[c:]
    lv = 2
    while lv < c:
        for q in st:
            xt = mm(jnp.concatenate([q['x'], q['t']], axis=0), q['xb'])
            q['x'], q['tx'] = xt[:c], xt[c:]
        for q in st:
            q['t'] = q['t'] + mm(q['tx'], q['xb'])
        lv *= 2
        if lv < c:
            for q in st:
                q['xb'] = bd(q['x'])
    for (d, g), q in zip(chains, st):
        q['s0'] = s_ref[d, g]
        q['ka'] = lax.dot_general(q['kr'], q['s0'].astype(BF16), nt, preferred_element_type=F32)
    for q in st:
        q['u'] = mm(q['t'], bd(q['ka'][:c] + q['mkv']))
    for (d, g), q in zip(chains, st):
        e2 = q['e2']
        upd = lax.dot_general(jnp.concatenate([q['v'], q['u']], axis=0).astype(BF16),
                              jnp.concatenate([q['kt'] * e2, -(q['b'] * e2)], axis=0).astype(BF16),
                              tn, preferred_element_type=F32)
        s_ref[d, g] = q['s0'] * jnp.exp(q['tot']) + jnp.where(bdmask, upd, 0.0)
    for q in st:
        q['y'] = q['ka'][c:] + q['nkv'] - mm(q['n_b'], bd(q['u']))
    for q in st:
        (mu,) = gsum([q['y']])
        q['dl'] = q['y'] - mu * (1.0 / n)
    for (d, g), q in zip(chains, st):
        (var,) = gsum([q['dl'] * q['dl']])
        sl = q['sl']
        yn = q['dl'] * lax.rsqrt(var * (1.0 / n) + RW_GN_EPS) * gn0_ref[:, sl] + gn1_ref[:, sl]
        refs[d][3][0, :, sl] = yn + q['bon'] * q['v']


def _rw_chunk(streams, nc, kk, ka, rk, gn):
    b, _, lt, w = streams.shape
    c = RW_CHUNK
    ncb, nb = nc // c, lt // c

    def rblk(j):
        return jnp.where(j < ncb, ncb - 1 - j, nb - 1 - (j - ncb))

    par = pl.BlockSpec((1, w), lambda bi, j: (0, 0))
    shp = jax.ShapeDtypeStruct((b, lt, w), F32)
    return pl.pallas_call(
        functools.partial(_rw_chunk_kernel, c=c, w=w),
        out_shape=[shp, shp],
        grid=(b, nb),
        in_specs=[pl.BlockSpec((1, 3, c, w), lambda bi, j: (bi, 0, j, 0)),
                  pl.BlockSpec((1, 1, c, w), lambda bi, j: (bi, 3, j, 0)),
                  pl.BlockSpec((1, 1, c, w), lambda bi, j: (bi, 5, j, 0)),
                  pl.BlockSpec((1, 3, c, w), lambda bi, j: (bi, 0, rblk(j), 0)),
                  pl.BlockSpec((1, 1, c, w), lambda bi, j: (bi, 4, rblk(j), 0)),
                  pl.BlockSpec((1, 1, c, w), lambda bi, j: (bi, 6, rblk(j), 0)),
                  par, par, par, par, par],
        out_specs=[pl.BlockSpec((1, c, w), lambda bi, j: (bi, j, 0)),
                   pl.BlockSpec((1, c, w), lambda bi, j: (bi, rblk(j), 0))],
        scratch_shapes=[pltpu.VMEM((2, w // RW_GROUP, RW_GROUP, RW_GROUP), F32)],
        compiler_params=_cparams(("arbitrary", "arbitrary"), 32),
        name="rwkv_chunk",
    )(streams, streams, streams, streams, streams, streams, kk.reshape(1, w), ka.reshape(1, w), rk.reshape(1, w),
      gn[0].reshape(1, w), gn[1].reshape(1, w))


def _proj_rope_kernel(a_ref, w_ref, c_ref, s_ref, o_ref):
    j = pl.program_id(1)
    acc = jnp.dot(a_ref[...], w_ref[...], preferred_element_type=F32)

    @pl.when(j < 2)
    def _():
        lane = lax.broadcasted_iota(jnp.int32, (1, LANES), 1)
        first_half = (lane % (2 * ROPE_AXIS_FREQS)) < ROPE_AXIS_FREQS
        c = c_ref[...] * jnp.where(j == 0, DA_QK_DIM ** -0.5 * math.log2(math.e), 1.0)
        s = s_ref[...] * jnp.where(j == 0, DA_QK_DIM ** -0.5 * math.log2(math.e), 1.0)
        for g in range(acc.shape[1] // LANES):
            xs = acc[:, g * LANES:(g + 1) * LANES]
            up = pltpu.roll(xs, LANES - ROPE_AXIS_FREQS, 1)
            dn = pltpu.roll(xs, ROPE_AXIS_FREQS, 1)
            o_ref[:, g * LANES:(g + 1) * LANES] = (xs * c + jnp.where(first_half, up, dn) * s).astype(o_ref.dtype)

    @pl.when(j >= 2)
    def _():
        o_ref[...] = acc.astype(o_ref.dtype)


def _proj_rope(a, w, ct, st, lt):
    m, k = a.shape
    n = w.shape[1]
    tn = n // 3
    tm = _pick(lt, (768, 640, 512, 384, 256, 128))
    per = lt // tm
    return pl.pallas_call(
        _proj_rope_kernel,
        out_shape=jax.ShapeDtypeStruct((m, n), BF16),
        grid=(m // tm, 3),
        in_specs=[pl.BlockSpec((tm, k), lambda i, j: (i, 0)),
                  pl.BlockSpec((k, tn), lambda i, j: (0, j)),
                  pl.BlockSpec((tm, LANES), lambda i, j: (i % per, 0)),
                  pl.BlockSpec((tm, LANES), lambda i, j: (i % per, 0))],
        out_specs=pl.BlockSpec((tm, tn), lambda i, j: (i, j)),
        compiler_params=_cparams(("arbitrary", "arbitrary"), 48),
        name="proj_rope",
    )(a, w, ct, st)


def _rope_tables(seq, nc):
    rows = seq // GRID_W
    row = jnp.broadcast_to(jnp.arange(rows)[:, None], (rows, GRID_W)).reshape(seq)
    col = jnp.broadcast_to(jnp.arange(GRID_W)[None, :], (rows, GRID_W)).reshape(seq)
    inv = 1.0 / (ROPE_THETA ** (jnp.arange(ROPE_AXIS_FREQS, dtype=F32) / ROPE_AXIS_FREQS))
    ang = jnp.stack([row, col], axis=-1).astype(F32)[:, :, None] * inv
    cos, sin = jnp.cos(ang), jnp.sin(ang)
    c64 = jnp.concatenate([cos[:, 0], cos[:, 0], cos[:, 1], cos[:, 1]], axis=-1)
    s64 = jnp.concatenate([-sin[:, 0], sin[:, 0], -sin[:, 1], sin[:, 1]], axis=-1)
    ct = jnp.concatenate([jnp.ones((nc, LANES), F32), jnp.tile(c64, (1, 2))], axis=0)
    st = jnp.concatenate([jnp.zeros((nc, LANES), F32), jnp.tile(s64, (1, 2))], axis=0)
    return ct, st


def _attn_kernel(q_ref, k_ref, v_ref, lam_ref, g_ref, o_ref, *, nct, nc, lam_init, hp, qoff):
    i = pl.program_id(2) + qoff
    lane = lax.broadcasted_iota(jnp.int32, (1, LANES), 1)
    lp = lam_ref[...]
    lam = (jnp.exp(jnp.sum(lp[0:1] * lp[1:2], axis=1, keepdims=True))
           - jnp.exp(jnp.sum(lp[2:3] * lp[3:4], axis=1, keepdims=True)) + lam_init)

    def attend(nk):
        for h in range(hp):
            cols = slice(h * DA_V_DIM, (h + 1) * DA_V_DIM)
            q = q_ref[0, :, cols]
            k = k_ref[0, 0:nk, cols]
            v1 = jnp.concatenate([v_ref[0, 0:nk, cols], jnp.ones((nk, LANES), BF16)], axis=1)

            def one(sel):
                qs = jnp.where(sel, q, jnp.zeros_like(q))
                s = lax.dot_general(qs, k, (((1,), (1,)), ((), ())), preferred_element_type=F32)
                e = jnp.exp2(s - jnp.max(s, axis=-1, keepdims=True))
                ov = jnp.dot(e.astype(BF16), v1, preferred_element_type=F32)
                return ov[:, :DA_V_DIM] / ov[:, DA_V_DIM:]

            o = one(lane < DA_QK_DIM) - lam * one(lane >= DA_QK_DIM)
            on = o * lax.rsqrt(jnp.mean(o * o, axis=-1, keepdims=True) + DA_SUBLN_EPS) * g_ref[...]
            o_ref[0, :, cols] = (on * (1.0 - lam_init)).astype(o_ref.dtype)

    @pl.when(i < nct)
    def _():
        attend(nc)

    @pl.when(i >= nct)
    def _():
        attend(k_ref.shape[1])


def _attention(zbr, nc, lam_p, subln, lam_init, with_ctx, hp=4):
    b, lt, cols = zbr.shape
    w = cols // 3
    heads = w // DA_V_DIM
    tq = min(256, nc)
    nct = nc // tq
    hg = heads // hp
    bw = hp * DA_V_DIM
    qoff = 0 if with_ctx else nct
    kern = functools.partial(_attn_kernel, nct=nct, nc=nc, lam_init=lam_init, hp=hp, qoff=qoff)
    return pl.pallas_call(
        kern,
        out_shape=jax.ShapeDtypeStruct((b, lt - qoff * tq, w), BF16),
        grid=(b, hg, lt // tq - qoff),
        in_specs=[pl.BlockSpec((1, tq, bw), lambda bi, h, i: (bi, i + qoff, h)),
                  pl.BlockSpec((1, lt, bw), lambda bi, h, i: (bi, 0, hg + h)),
                  pl.BlockSpec((1, lt, bw), lambda bi, h, i: (bi, 0, 2 * hg + h)),
                  pl.BlockSpec((4, DA_QK_DIM), lambda bi, h, i: (0, 0)),
                  pl.BlockSpec((1, DA_V_DIM), lambda bi, h, i: (0, 0))],
        out_specs=pl.BlockSpec((1, tq, bw), lambda bi, h, i: (bi, i, h)),
        compiler_params=_cparams(("arbitrary", "arbitrary", "arbitrary"), 58),
        name="diff_attn",
    )(zbr, zbr, zbr, lam_p, subln.reshape(1, DA_V_DIM))


def _seg_masks(c, e):
    row = lax.broadcasted_iota(jnp.int32, (c, e), 0)
    mk = {}
    h = 1
    while h < min(SUBLANES, c):
        odd = (row % (2 * h)) >= h
        for s in range(1, h + 1):
            mk[(h, s, 0)] = jnp.where(jnp.logical_and(odd, (row % h) == s - 1), 1.0, 0.0)
            mk[(h, s, 1)] = jnp.where(jnp.logical_and(~odd, (row % h) == h - s), 1.0, 0.0)
        h *= 2
    return mk


def _seg_scans(lf, mk, c):
    cs, ss = {1: lf}, {1: lf}
    h = 1
    while h < c:
        x, y = cs[h], ss[h]
        if h < SUBLANES:
            x3 = x.reshape(c // SUBLANES, SUBLANES, x.shape[1])
            y3 = y.reshape(c // SUBLANES, SUBLANES, y.shape[1])
            for s in range(1, h + 1):
                x = x + mk[(h, s, 0)] * pltpu.roll(x3, s, 1).reshape(x.shape)
                y = y + mk[(h, s, 1)] * pltpu.roll(y3, SUBLANES - s, 1).reshape(y.shape)
            cs[2 * h], ss[2 * h] = x, y
        else:
            px = [x[j * h:(j + 1) * h] for j in range(c // h)]
            py = [y[j * h:(j + 1) * h] for j in range(c // h)]
            nx = [px[j] + px[j - 1][h - 1:h] if j % 2 == 1 else px[j] for j in range(c // h)]
            ny = [py[j] + py[j + 1][0:1] if j % 2 == 0 else py[j] for j in range(c // h)]
            cs[2 * h], ss[2 * h] = jnp.concatenate(nx, axis=0), jnp.concatenate(ny, axis=0)
        h *= 2
    return cs, ss


def _hg_kernel(q_ref, f_ref, v_ref, lb_ref, o_ref, st_ref, *, tc, c, heads, rev):
    j = pl.program_id(1)

    @pl.when(j == 0)
    def _():
        st_ref[...] = jnp.zeros_like(st_ref)

    ti = lax.broadcasted_iota(jnp.int32, (c, c), 0)
    si = lax.broadcasted_iota(jnp.int32, (c, c), 1)
    nt = (((1,), (1,)), ((), ()))
    tn = (((0,), (0,)), ((), ()))
    mk = _seg_masks(c, HG_EXPAND)
    masks, hh = {}, 1
    while hh < c:
        tb, sb = ti // hh, si // hh
        masks[hh] = (jnp.logical_and(tb % 2 == 0, sb == tb + 1) if rev
                     else jnp.logical_and(tb % 2 == 1, sb == tb - 1))
        hh *= 2

    def head(h, carry):
        cols = pl.ds(pl.multiple_of(h * HG_EXPAND, HG_EXPAND), HG_EXPAND)
        lb = lb_ref[:, cols]
        chunks = range(tc // c)
        for ci in (reversed(chunks) if rev else chunks):
            rows = slice(ci * c, (ci + 1) * c)
            q = _silu(q_ref[0, rows, cols])
            f = lb + (1.0 - lb) * _sigmoid(f_ref[0, rows, cols])
            k = 1.0 - f
            lf = jnp.log(f)
            v = v_ref[0, rows, cols].astype(BF16)
            cs, ss = _seg_scans(lf, mk, c)
            qs, ks = (ss, cs) if rev else (cs, ss)
            att = jnp.where(ti == si, lax.dot_general(q.astype(BF16), k.astype(BF16), nt,
                                                      preferred_element_type=F32), 0.0)
            hh = 1
            while hh < c:
                qe = (q * jnp.exp(qs[hh])).astype(BF16)
                ke = (k * jnp.exp(ks[hh] - lf)).astype(BF16)
                a = lax.dot_general(qe, ke, nt, preferred_element_type=F32)
                att = att + jnp.where(masks[hh], a, 0.0)
                hh *= 2
            st = st_ref[h]
            qe = (q * jnp.exp(qs[c])).astype(BF16)
            o = lax.dot_general(qe, st.astype(BF16), nt, preferred_element_type=F32)
            o = o + jnp.dot(att.astype(BF16), v, preferred_element_type=F32)
            o_ref[0, rows, cols] = o
            ke = (k * jnp.exp(ks[c] - lf)).astype(BF16)
            tot = qs[c][0:1] if rev else qs[c][c - 1:c]
            st_ref[h] = st * jnp.exp(tot) + lax.dot_general(v, ke, tn, preferred_element_type=F32)
        return carry

    lax.fori_loop(0, heads, head, 0, unroll=4)


def _hgrn(zc, nc, lb, rev):
    b, lt, cols = zc.shape
    w = cols // 5
    heads = w // HG_EXPAND
    tc = min(256, nc)
    c = min(HG_CHUNK, tc)
    ncb, nb = nc // tc, lt // tc

    def blk(j):
        return jnp.where(j < ncb, ncb - 1 - j, nb - 1 - (j - ncb)) if rev else j

    fcol = 2 if rev else 1
    return pl.pallas_call(
        functools.partial(_hg_kernel, tc=tc, c=c, heads=heads, rev=rev),
        out_shape=jax.ShapeDtypeStruct((b, lt, w), F32),
        grid=(b, nb),
        in_specs=[pl.BlockSpec((1, tc, w), lambda bi, j: (bi, blk(j), 0)),
                  pl.BlockSpec((1, tc, w), lambda bi, j: (bi, blk(j), fcol)),
                  pl.BlockSpec((1, tc, w), lambda bi, j: (bi, blk(j), 3)),
                  pl.BlockSpec((1, w), lambda bi, j: (0, 0))],
        out_specs=pl.BlockSpec((1, tc, w), lambda bi, j: (bi, blk(j), 0)),
        scratch_shapes=[pltpu.VMEM((heads, HG_EXPAND, HG_EXPAND), F32)],
        compiler_params=_cparams(("arbitrary", "arbitrary"), 32),
        name="hgrn2_bwd" if rev else "hgrn2_fwd",
    )(zc, zc, zc, lb.reshape(1, w))


def _merge_kernel(oaf_ref, oab_ref, ga_ref, yb_ref, of_ref, ob_ref, gc_ref, hn_ref, gl0_ref, gl1_ref, gl2_ref, wb_ref,
                  o_ref, *, heads):
    gls = (gl0_ref, gl1_ref, gl2_ref)
    ya = (oaf_ref[0] + oab_ref[0]) * ga_ref[0]
    oc = of_ref[0] + ob_ref[0]
    parts = []
    for h in range(heads):
        x = oc[:, h * HG_EXPAND:(h + 1) * HG_EXPAND]
        parts.append(x * lax.rsqrt(jnp.mean(x * x, axis=-1, keepdims=True) + NORM_EPS))
    yc = jnp.concatenate(parts, axis=-1) * hn_ref[...] * _silu(gc_ref[0])
    acc = None
    for bi, y in enumerate((ya.astype(BF16), yb_ref[0], yc.astype(BF16))):
        term = _sigmoid_t(gls[bi][0].astype(F32)) * jnp.dot(y, wb_ref[bi], preferred_element_type=F32)
        acc = term if acc is None else acc + term
    o_ref[0] = acc.astype(BF16)


def _merge(oaf, oab, ga, yb, of, ob, zc, hn, gl, wbr, row0, tl=128, tn=2048):
    b, lt, w = ga.shape
    d = wbr.shape[2]
    tl = _pick(math.gcd(lt, row0), (tl, 64))
    r0 = row0 // tl
    heads = w // HG_EXPAND
    nn = d // tn
    row = lambda n, bi, i: (bi, i + r0, 0)
    glspec = lambda br: pl.BlockSpec((1, tl, tn), lambda n, bi, i: (bi, i + r0, br * nn + n))
    return pl.pallas_call(
        functools.partial(_merge_kernel, heads=heads),
        out_shape=jax.ShapeDtypeStruct((b, lt - row0, d), BF16),
        grid=(d // tn, b, lt // tl - r0),
        in_specs=[pl.BlockSpec((1, tl, w), row),
                  pl.BlockSpec((1, tl, w), row),
                  pl.BlockSpec((1, tl, w), row),
                  pl.BlockSpec((1, tl, w), lambda n, bi, i: (bi, i, 0)),
                  pl.BlockSpec((1, tl, w), row),
                  pl.BlockSpec((1, tl, w), row),
                  pl.BlockSpec((1, tl, w), lambda n, bi, i: (bi, i + r0, 4)),
                  pl.BlockSpec((1, w), lambda n, bi, i: (0, 0)),
                  glspec(0), glspec(1), glspec(2),
                  pl.BlockSpec((3, w, tn), lambda n, bi, i: (0, 0, n))],
        out_specs=pl.BlockSpec((1, tl, tn), lambda n, bi, i: (bi, i, n)),
        compiler_params=_cparams(("arbitrary", "arbitrary", "arbitrary"), 48),
        name="branch_merge",
    )(oaf, oab, ga, yb, of, ob, zc, hn, gl, gl, gl, wbr)


def _proj_post_kernel(a_ref, w_ref, x_ref, g_ref, gate_ref, o_ref):
    mx = jnp.dot(a_ref[0], w_ref[...], preferred_element_type=F32)
    mn = mx * lax.rsqrt(jnp.mean(mx * mx, axis=-1, keepdims=True) + NORM_EPS) * g_ref[...]
    o_ref[0] = x_ref[0] + gate_ref[0, 0] * mn


def _proj_post(a, wout, xa, g, gsel, nc, row0, tl=256):
    b, lt, d = xa.shape
    tl = min(tl, nc)
    nct = nc // tl
    r0 = row0 // tl
    return pl.pallas_call(
        _proj_post_kernel,
        out_shape=jax.ShapeDtypeStruct((b, lt, d), F32),
        grid=(b, lt // tl - r0),
        in_specs=[pl.BlockSpec((1, tl, d), lambda bi, i: (bi, i, 0)),
                  pl.BlockSpec((d, d), lambda bi, i: (0, 0)),
                  pl.BlockSpec((1, tl, d), lambda bi, i: (bi, i + r0, 0)),
                  pl.BlockSpec((1, d), lambda bi, i: (0, 0)),
                  pl.BlockSpec((1, 1, 1, d), lambda bi, i: (bi, jnp.where(i + r0 >= nct, 1, 0), 0, 0))],
        out_specs=pl.BlockSpec((1, tl, d), lambda bi, i: (bi, i + r0, 0)),
        input_output_aliases={2: 0},
        compiler_params=_cparams(("arbitrary", "arbitrary"), 48),
        name="out_proj_post",
    )(a, wout, xa, g.reshape(1, d), gsel)


def _rank_kernel(afft_ref, slot_ref, *, n, cap, tw):
    aff = afft_ref[0]
    bits = lax.bitcast_convert_type(aff, jnp.int32)
    e_num = aff.shape[0]

    def count(mask):
        return jnp.sum(jnp.where(mask, 1.0, 0.0), axis=1, keepdims=True)

    def bit_step(i, thr):
        cand = thr | lax.shift_left(jnp.int32(1), 30 - i)
        return jnp.where(count(bits >= cand) >= cap, cand, thr)

    thr = lax.fori_loop(0, 31, bit_step, jnp.zeros((e_num, 1), jnp.int32))
    gt = bits > thr
    eq = bits == thr
    need = cap - count(gt)
    tri = jnp.where(lax.broadcasted_iota(jnp.int32, (tw, tw), 0) < lax.broadcasted_iota(jnp.int32, (tw, tw), 1),
                    1.0, 0.0).astype(BF16)

    def prefix(mask):
        parts, carry = [], jnp.zeros((e_num, 1), F32)
        for j in range(n // tw):
            m = jnp.where(mask[:, j * tw:(j + 1) * tw], 1.0, 0.0)
            parts.append(jnp.dot(m.astype(BF16), tri, preferred_element_type=F32) + carry)
            carry = carry + jnp.sum(m, axis=1, keepdims=True)
        return jnp.concatenate(parts, axis=1)

    sel = jnp.logical_or(gt, jnp.logical_and(eq, prefix(eq) < need))
    slot_ref[0] = jnp.where(sel, prefix(sel), float(cap)).astype(jnp.int32)


def _rank(afft, cap):
    b, e_num, n = afft.shape
    tw = min(LANES, n)
    return pl.pallas_call(
        functools.partial(_rank_kernel, n=n, cap=cap, tw=tw),
        out_shape=jax.ShapeDtypeStruct((b, e_num, n), jnp.int32),
        grid=(b,),
        in_specs=[pl.BlockSpec((1, e_num, n), lambda bi: (bi, 0, 0))],
        out_specs=pl.BlockSpec((1, e_num, n), lambda bi: (bi, 0, 0)),
        compiler_params=_cparams(("arbitrary",), 32),
        name="ec_rank",
    )(afft)


def _gather_kernel(slot_ref, afft_ref, h_ref, o_ref, g_ref, *, cap):
    e = pl.program_id(1)
    sl = slot_ref[0, pl.ds(e, 1), :]
    hit = lax.broadcasted_iota(jnp.int32, (cap, 1), 0) == sl
    o_ref[0] = jnp.dot(jnp.where(hit, 1.0, 0.0).astype(BF16), h_ref[0], preferred_element_type=F32).astype(BF16)
    gate = jnp.sum(jnp.where(hit, afft_ref[0, pl.ds(e, 1), :], 0.0), axis=1, keepdims=True)
    g_ref[0] = jnp.broadcast_to(gate, (cap, LANES))


def _gather(slot, afft, hs, cap):
    b, e_num, n = slot.shape
    d = hs.shape[2]
    return pl.pallas_call(
        functools.partial(_gather_kernel, cap=cap),
        out_shape=[jax.ShapeDtypeStruct((e_num, b * cap, d), BF16),
                   jax.ShapeDtypeStruct((e_num, b * cap, LANES), F32)],
        grid=(b, e_num),
        in_specs=[pl.BlockSpec((1, e_num, n), lambda bi, e: (bi, 0, 0)),
                  pl.BlockSpec((1, e_num, n), lambda bi, e: (bi, 0, 0)),
                  pl.BlockSpec((1, n, d), lambda bi, e: (bi, 0, 0))],
        out_specs=[pl.BlockSpec((1, cap, d), lambda bi, e: (e, bi, 0)),
                   pl.BlockSpec((1, cap, LANES), lambda bi, e: (e, bi, 0))],
        compiler_params=_cparams(("arbitrary", "arbitrary"), 40),
        name="ec_gather",
    )(slot, afft, hs)


def _ffn_kernel(x_ref, g_ref, w1_ref, w3_ref, w2_ref, o_ref, acc_ref):
    f = pl.program_id(2)
    x = x_ref[0]
    h1 = jnp.dot(x, w1_ref[0, 0].astype(BF16), preferred_element_type=F32)
    h3 = jnp.dot(x, w3_ref[0, 0].astype(BF16), preferred_element_type=F32)
    hid = (_silu(h1) * h3).astype(BF16)

    @pl.when(f == 0)
    def _():
        acc_ref[...] = jnp.zeros_like(acc_ref)

    d = acc_ref.shape[1]
    for n0 in range(0, d, FFN_OUT_CHUNK):
        cols = slice(n0, n0 + FFN_OUT_CHUNK)
        acc_ref[:, cols] += jnp.dot(hid, w2_ref[0, 0, :, cols].astype(BF16), preferred_element_type=F32)

    @pl.when(f == pl.num_programs(2) - 1)
    def _():
        o_ref[0] = (acc_ref[...] * g_ref[0, :, 0:1]).astype(BF16)


def _ffn(xe, gate, w1, w3, w2, layer, tf=256):
    e_num, m, d = xe.shape
    ff = w1.shape[3]
    tm = m
    once = pl.Buffered(1)
    return pl.pallas_call(
        _ffn_kernel,
        out_shape=jax.ShapeDtypeStruct((e_num, m, d), BF16),
        grid=(e_num, m // tm, ff // tf),
        in_specs=[pl.BlockSpec((1, tm, d), lambda e, i, f: (e, i, 0), pipeline_mode=once),
                  pl.BlockSpec((1, tm, LANES), lambda e, i, f: (e, i, 0), pipeline_mode=once),
                  pl.BlockSpec((1, 1, d, tf), lambda e, i, f: (layer, e, 0, f)),
                  pl.BlockSpec((1, 1, d, tf), lambda e, i, f: (layer, e, 0, f)),
                  pl.BlockSpec((1, 1, tf, d), lambda e, i, f: (layer, e, f, 0))],
        out_specs=pl.BlockSpec((1, tm, d), lambda e, i, f: (e, i, 0), pipeline_mode=once),
        scratch_shapes=[pltpu.VMEM((tm, d), F32)],
        compiler_params=_cparams(("arbitrary", "arbitrary", "arbitrary"), 58),
        name="ec_ffn",
    )(xe, gate, w1, w3, w2)


def _combine_kernel(slot_ref, ye_ref, x_ref, g_ref, gate_ref, o_ref, *, cap, e_num):
    lane = lax.broadcasted_iota(jnp.int32, (1, e_num), 1)
    pos = lax.broadcasted_iota(jnp.int32, (1, cap), 1)
    sl = slot_ref[0]
    y = None
    for e in range(e_num):
        se = jnp.sum(jnp.where(lane == e, sl, 0), axis=1, keepdims=True)
        pt = jnp.where(se == pos, 1.0, 0.0).astype(BF16)
        part = jnp.dot(pt, ye_ref[e], preferred_element_type=F32)
        y = part if y is None else y + part
    yn = y * lax.rsqrt(jnp.mean(y * y, axis=-1, keepdims=True) + NORM_EPS) * g_ref[...]
    o_ref[0] = x_ref[0] + gate_ref[0, 0] * yn


def _combine(slot_c, ye, xa, g, gsel, seg, row0, slot0, cap, final, tt=256):
    b, n, e_num = slot_c.shape
    d = xa.shape[2]
    tt = _pick(math.gcd(n, row0), (tt, 128, 64))
    assert row0 % tt == 0 and slot0 % cap == 0
    r0, s0 = row0 // tt, slot0 // cap
    return pl.pallas_call(
        functools.partial(_combine_kernel, cap=cap, e_num=e_num),
        out_shape=jax.ShapeDtypeStruct((b, n, d) if final else xa.shape, F32),
        grid=(b, n // tt),
        in_specs=[pl.BlockSpec((1, tt, e_num), lambda bi, j: (bi, j, 0)),
                  pl.BlockSpec((e_num, cap, d), lambda bi, j: (0, s0 + bi, 0)),
                  pl.BlockSpec((1, tt, d), lambda bi, j: (bi, r0 + j, 0)),
                  pl.BlockSpec((1, d), lambda bi, j: (0, 0)),
                  pl.BlockSpec((1, 1, 1, d), lambda bi, j: (bi, seg, 0, 0))],
        out_specs=pl.BlockSpec((1, tt, d), lambda bi, j: (bi, (0 if final else r0) + j, 0)),
        input_output_aliases={} if final else {2: 0},
        compiler_params=_cparams(("arbitrary", "arbitrary"), 56),
        name="ec_combine_post",
    )(slot_c, ye, xa, g.reshape(1, d), gsel)


def _sel(mod_c, mod_x, idx):
    b = mod_x.shape[0]
    mc = jnp.broadcast_to(mod_c[jnp.array(idx)][None], (b, len(idx), mod_c.shape[-1]))
    return jnp.stack([mc, mod_x[:, jnp.array(idx)]], axis=1)


def _layer(xa, nc, mod_x, mod_c, p, lam_init, hg_lb, ct, st, with_ctx):
    b, lt, d = xa.shape
    w = d // 2
    a_cols = 3 * w + 2 * RW_DECAY_RANK + 2 * RW_ICL_RANK + RW_GATE_RANK
    col_b = a_cols
    col_c = col_b + 3 * w
    col_g = col_c + 5 * w

    h = _norm_mod(xa, p['norm_g'][0], _sel(mod_c, mod_x, (0, 1)), nc)
    h2 = h.reshape(b * lt, d)
    w_in = p['w_in']
    za = _matmul(h2, w_in[:, :col_b].astype(BF16), F32, tn=a_cols // 3).reshape(b, lt, a_cols)
    zc = _matmul(h2, w_in[:, col_c:col_g].astype(BF16), F32).reshape(b, lt, 5 * w)
    gl = _matmul(h2, w_in[:, col_g:].astype(BF16), BF16).reshape(b, lt, 3 * d)

    streams, ga = _rw_prep(za, nc, p['rw_mu'], p['rw_w0'], p['rw_wB'], p['rw_a0'], p['rw_aB'], p['rw_gB'])
    oaf, oab = _rw_chunk(streams, nc, p['rw_kk'], p['rw_ka'], p['rw_rk'], p['rw_gn'])

    zbr = _proj_rope(h2, w_in[:, col_b:col_c].astype(BF16), ct, st, lt).reshape(b, lt, 3 * w)
    yb = _attention(zbr, nc, p['da_lambda'], p['da_subln'], lam_init, with_ctx)

    of = _hgrn(zc, nc, hg_lb, rev=False)
    ob = _hgrn(zc, nc, hg_lb, rev=True)

    hn = jnp.tile(p['hg_norm'], w // HG_EXPAND).reshape(1, w)
    skip = 0 if with_ctx else nc
    merged = _merge(oaf, oab, ga, yb, of, ob, zc, hn, gl, p['w_branch'].astype(BF16), skip)
    xa = _proj_post(merged, p['w_out'].astype(BF16), xa, p['norm_g'][1], _sel(mod_c, mod_x, (2,)), nc, skip)

    hm, afft = _norm_mod(xa, p['norm_g'][2], _sel(mod_c, mod_x, (3, 4)), nc, router_t=p['moe_router'].T, row0=skip)
    gsel = _sel(mod_c, mod_x, (5,))
    sets = [(nc, lt - nc, 1)] + ([(0, nc, 0)] if with_ctx else [])
    routed, xes, gates = [], [], []
    for row0, nn, seg in sets:
        cap = EC_CAPACITY_FACTOR * nn // N_EXPERTS
        at = afft[:, :, row0 - skip:row0 - skip + nn]
        slot = _rank(at, cap)
        xe_s, gate_s = _gather(slot, at, hm[:, row0 - skip:row0 - skip + nn], cap)
        xes.append(xe_s)
        gates.append(gate_s)
        routed.append((jnp.swapaxes(slot, 1, 2), row0, seg, cap))
    xe = xes[0] if len(xes) == 1 else jnp.concatenate(xes, axis=1)
    gate = gates[0] if len(gates) == 1 else jnp.concatenate(gates, axis=1)
    ye = _ffn(xe, gate, p['moe_w1'], p['moe_w3'], p['moe_w2'], p['layer'])
    slot0 = 0
    for slot_c, row0, seg, cap in routed:
        xa = _combine(slot_c, ye, xa, p['norm_g'][3], gsel, seg, row0, slot0, cap, final=not with_ctx)
        slot0 += b * cap
    return xa


def kernel(x, c, ctx, c_ctx, w_ada, b_ada, norm_g, w_in, rw_mu, rw_w0, rw_wB, rw_a0, rw_aB, rw_gB, rw_kk, rw_ka, rw_rk, rw_gn, da_lambda, da_subln, hg_lb_logits, hg_norm, w_branch, w_out, moe_router, moe_w1, moe_w3, moe_w2):
    b, seq, d = x.shape
    nc = ctx.shape[1]
    depth = w_ada.shape[0]
    ct, st = _rope_tables(seq, nc)
    lb_w = jax.nn.softmax(hg_lb_logits.astype(F32), axis=0)
    hg_lb = jnp.cumsum(lb_w, axis=0) - lb_w[0]
    rows = ((b + 1 + SUBLANES - 1) // SUBLANES) * SUBLANES
    cc = jnp.zeros((rows, d), F32).at[:b].set(c).at[b].set(c_ctx)
    mod = _modulation(cc, w_ada, b_ada)
    xa = jnp.concatenate([ctx, x], axis=1)
    for l in range(depth):
        p = dict(norm_g=norm_g[l], w_in=w_in[l], rw_mu=rw_mu[l], rw_w0=rw_w0[l], rw_wB=rw_wB[l],
                 rw_a0=rw_a0[l], rw_aB=rw_aB[l], rw_gB=rw_gB[l], rw_kk=rw_kk[l], rw_ka=rw_ka[l],
                 rw_rk=rw_rk[l], rw_gn=rw_gn[l], da_lambda=da_lambda[l], da_subln=da_subln[l],
                 hg_norm=hg_norm[l], w_branch=w_branch[l], w_out=w_out[l], moe_router=moe_router[l],
                 moe_w1=moe_w1, moe_w3=moe_w3, moe_w2=moe_w2, layer=l)
        mod_x = mod[l, :b].reshape(b, 6, d)
        mod_c = mod[l, b].reshape(6, d)
        lam_init = 0.8 - 0.6 * math.exp(-0.3 * l)
        xa = _layer(xa, nc, mod_x, mod_c, p, lam_init, hg_lb[l], ct, st, with_ctx=l < depth - 1)
    return xa
```

```python
import functools
import math

import jax
import jax.numpy as jnp
from jax import lax
from jax.experimental import pallas as pl
from jax.experimental.pallas import tpu as pltpu

F32 = jnp.float32
BF16 = jnp.bfloat16
HIGHEST = lax.Precision.HIGHEST

GRID_W = 64
RW_HEAD_DIM = 64
RW_DECAY_RANK = 64
RW_ICL_RANK = 64
RW_GATE_RANK = 128
RW_GN_EPS = 64e-5
DA_QK_DIM = 64
DA_V_DIM = 128
ROPE_THETA = 10000.0
ROPE_AXIS_FREQS = 16
DA_SUBLN_EPS = 1e-5
HG_EXPAND = 128
N_EXPERTS = 16
EC_CAPACITY_FACTOR = 2
NORM_EPS = 1e-6

LANES = 128
SUBLANES = 8
V7X_VMEM_BYTES = 64 * 1024 * 1024
HG_CHUNK = 64
RW_CHUNK = 64
RW_GROUP = 256
RW_SAMPLES = 2
FFN_OUT_CHUNK = 512


def _cparams(sem, vmem_mb):
    limit = int(vmem_mb * 1024 * 1024)
    assert limit < V7X_VMEM_BYTES
    return pltpu.CompilerParams(dimension_semantics=sem, vmem_limit_bytes=limit)


def _sigmoid(x):
    return 1.0 / (1.0 + jnp.exp(-x))


def _sigmoid_t(x):
    return 0.5 * jnp.tanh(0.5 * x) + 0.5


def _silu(x):
    return x * _sigmoid_t(x)


def _mod_kernel(c_ref, w_ref, b_ref, o_ref):
    a = _silu(c_ref[...]).astype(BF16)
    o_ref[0] = jnp.dot(a, w_ref[0].astype(BF16), preferred_element_type=F32) + b_ref[0]


def _modulation(cc, w_ada, b_ada):
    depth, d, n = w_ada.shape
    rows = cc.shape[0]
    tn = 1024
    return pl.pallas_call(
        _mod_kernel,
        out_shape=jax.ShapeDtypeStruct((depth, rows, n), F32),
        grid=(depth, n // tn),
        in_specs=[pl.BlockSpec((rows, d), lambda l, j: (0, 0)),
                  pl.BlockSpec((1, d, tn), lambda l, j: (l, 0, j)),
                  pl.BlockSpec((1, 1, tn), lambda l, j: (l, 0, j))],
        out_specs=pl.BlockSpec((1, rows, tn), lambda l, j: (l, 0, j)),
        compiler_params=_cparams(("arbitrary", "arbitrary"), 40),
        name="adaln_mod",
    )(cc, w_ada, b_ada.reshape(depth, 1, n))


def _norm_mod_kernel(x_ref, g_ref, ms_ref, *rest, with_router):
    x = x_ref[0]
    xn = x * lax.rsqrt(jnp.mean(x * x, axis=-1, keepdims=True) + NORM_EPS) * g_ref[...]
    h = xn * (1.0 + ms_ref[0, 0, 1:2, :]) + ms_ref[0, 0, 0:1, :]
    if with_router:
        rt_ref, h_ref, aff_ref = rest
        h_ref[0] = h.astype(BF16)
        lt = lax.dot_general(rt_ref[...], h, (((1,), (1,)), ((), ())),
                             precision=HIGHEST, preferred_element_type=F32)
        e = jnp.exp(lt - jnp.max(lt, axis=0, keepdims=True))
        aff_ref[0] = e / jnp.sum(e, axis=0, keepdims=True)
    else:
        (h_ref,) = rest
        h_ref[0] = h.astype(BF16)


def _norm_mod(xa, g, msel, nc, router_t=None, row0=0, tl=256):
    b, lt, d = xa.shape
    tl = min(tl, nc)
    nct = nc // tl
    r0 = row0 // tl
    with_router = router_t is not None
    in_specs = [pl.BlockSpec((1, tl, d), lambda bi, i: (bi, i + r0, 0)),
                pl.BlockSpec((1, d), lambda bi, i: (0, 0)),
                pl.BlockSpec((1, 1, 2, d), lambda bi, i: (bi, jnp.where(i + r0 >= nct, 1, 0), 0, 0))]
    args = [xa, g.reshape(1, d), msel]
    out_shape = [jax.ShapeDtypeStruct((b, lt - row0, d), BF16)]
    out_specs = [pl.BlockSpec((1, tl, d), lambda bi, i: (bi, i, 0))]
    if with_router:
        e = router_t.shape[0]
        in_specs.append(pl.BlockSpec((e, d), lambda bi, i: (0, 0)))
        args.append(router_t)
        out_shape.append(jax.ShapeDtypeStruct((b, e, lt - row0), F32))
        out_specs.append(pl.BlockSpec((1, e, tl), lambda bi, i: (bi, 0, i)))
    res = pl.pallas_call(
        functools.partial(_norm_mod_kernel, with_router=with_router),
        out_shape=out_shape, grid=(b, lt // tl - r0), in_specs=in_specs, out_specs=out_specs,
        compiler_params=_cparams(("arbitrary", "arbitrary"), 32),
        name="norm_mod_router" if with_router else "norm_mod",
    )(*args)
    return res if with_router else res[0]


def _matmul_kernel(a_ref, w_ref, o_ref):
    o_ref[...] = jnp.dot(a_ref[...], w_ref[...], preferred_element_type=F32).astype(o_ref.dtype)


def _pick(n, cands):
    for c in cands:
        if n % c == 0:
            return c
    return n


def _matmul(a, w, out_dtype, tm=1024, tn=1024):
    m, k = a.shape
    n = w.shape[1]
    tm = _pick(m, (tm, 768, 512, 384, 256, 128))
    tn = _pick(n, (tn, 768, 512, 384, 256, 128))
    return pl.pallas_call(
        _matmul_kernel,
        out_shape=jax.ShapeDtypeStruct((m, n), out_dtype),
        grid=(m // tm, n // tn),
        in_specs=[pl.BlockSpec((tm, k), lambda i, j: (i, 0)),
                  pl.BlockSpec((k, tn), lambda i, j: (0, j))],
        out_specs=pl.BlockSpec((tm, tn), lambda i, j: (i, j)),
        compiler_params=_cparams(("arbitrary", "arbitrary"), 48),
        name="matmul",
    )(a, w)


def _rw_prep_kernel(z_ref, zp_ref, zn_ref, mu_ref, w0_ref, wb_ref, a0_ref, ab_ref, gb_ref,
                    o_ref, g_ref, *, tl, nct, nt, w):
    i = pl.program_id(1)
    z = z_ref[0]
    row = lax.broadcasted_iota(jnp.int32, (tl, 1), 0)
    seg_start = jnp.logical_or(i == 0, i == nct)
    seg_end = jnp.logical_or(i == nct - 1, i == nt - 1)
    prev_row = jnp.where(seg_start, 0.0, zp_ref[0, SUBLANES - 1:SUBLANES, :])
    next_row = jnp.where(seg_end, 0.0, zn_ref[0, 0:1, :])
    prev = jnp.where(row == 0, prev_row, pltpu.roll(z, 1, 0))
    nxt = jnp.where(row == tl - 1, next_row, pltpu.roll(z, tl - 1, 0))
    zs = z + mu_ref[0:1, :] * (prev - z) + mu_ref[1:2, :] * (nxt - z)
    o_ref[0, 0] = zs[:, 0:w]
    o_ref[0, 1] = zs[:, w:2 * w]
    o_ref[0, 2] = zs[:, 2 * w:3 * w]
    o4 = 3 * w + 2 * RW_DECAY_RANK
    o5 = o4 + 2 * RW_ICL_RANK
    wd = jnp.tanh(zs[:, 3 * w:o4])
    wl = jnp.dot(wd.astype(BF16), wb_ref[...], preferred_element_type=F32) + w0_ref[...]
    dec = -math.exp(-0.5) * _sigmoid_t(wl)
    o_ref[0, 3] = dec[:, 0:w]
    o_ref[0, 4] = dec[:, w:2 * w]
    al = jnp.dot(zs[:, o4:o5].astype(BF16), ab_ref[...], preferred_element_type=F32) + a0_ref[...]
    av = _sigmoid_t(al)
    o_ref[0, 5] = av[:, 0:w]
    o_ref[0, 6] = av[:, w:2 * w]
    g_ref[0] = jnp.dot(_sigmoid_t(zs[:, o5:]).astype(BF16), gb_ref[...], preferred_element_type=F32)


def _blockdiag2(m):
    r, w = m.shape[1], m.shape[2]
    z = jnp.zeros((r, w), m.dtype)
    return jnp.concatenate([jnp.concatenate([m[0], z], axis=1), jnp.concatenate([z, m[1]], axis=1)], axis=0)


def _rw_prep(za, nc, mu, w0, wb, a0, ab, gb, tl=256):
    b, lt, acols = za.shape
    w = w0.shape[1]
    tl = min(tl, nc)
    nct, nt = nc // tl, lt // tl
    r8 = tl // SUBLANES
    nb8 = lt // SUBLANES
    full = lambda shp: pl.BlockSpec(shp, lambda bi, i: tuple(0 for _ in shp))
    kern = functools.partial(_rw_prep_kernel, tl=tl, nct=nct, nt=nt, w=w)
    return pl.pallas_call(
        kern,
        out_shape=[jax.ShapeDtypeStruct((b, 7, lt, w), F32), jax.ShapeDtypeStruct((b, lt, w), F32)],
        grid=(b, nt),
        in_specs=[pl.BlockSpec((1, tl, acols), lambda bi, i: (bi, i, 0)),
                  pl.BlockSpec((1, SUBLANES, acols), lambda bi, i: (bi, jnp.maximum(i * r8 - 1, 0), 0)),
                  pl.BlockSpec((1, SUBLANES, acols), lambda bi, i: (bi, jnp.minimum((i + 1) * r8, nb8 - 1), 0)),
                  full((2, acols)), full((1, 2 * w)), full((2 * RW_DECAY_RANK, 2 * w)),
                  full((1, 2 * w)), full((2 * RW_ICL_RANK, 2 * w)), full((RW_GATE_RANK, w))],
        out_specs=[pl.BlockSpec((1, 7, tl, w), lambda bi, i: (bi, 0, i, 0)),
                   pl.BlockSpec((1, tl, w), lambda bi, i: (bi, i, 0))],
        compiler_params=_cparams(("arbitrary", "arbitrary"), 48),
        name="rwkv_prep",
    )(za, za, za, mu, w0.reshape(1, 2 * w), _blockdiag2(wb).astype(BF16), a0.reshape(1, 2 * w),
      _blockdiag2(ab).astype(BF16), gb.astype(BF16))


def _rw_chunk_kernel(xf_ref, lwf_ref, af_ref, xb_ref, lwb_ref, ab_ref, kk_ref, ka_ref, rk_ref, gn0_ref, gn1_ref,
                     of_ref, ob_ref, s_ref, *, c, w, nbt):
    j = pl.program_id(1)

    @pl.when(j == 0)
    def _():
        s_ref[...] = jnp.zeros_like(s_ref)

    gw, n = RW_GROUP, RW_HEAD_DIM
    ng = w // gw
    row = lax.broadcasted_iota(jnp.int32, (c, 1), 0)
    pos = lax.broadcasted_iota(jnp.int32, (1, gw), 1) % n
    bdmask = (lax.broadcasted_iota(jnp.int32, (gw, 1), 0) // n) == (lax.broadcasted_iota(jnp.int32, (1, gw), 1) // n)
    ones_bd = jnp.where(bdmask, 1.0, 0.0).astype(BF16)
    eye = jnp.where(pos == row, 1.0, 0.0)
    nt = (((1,), (1,)), ((), ()))
    tn = (((0,), (0,)), ((), ()))
    chains = [(bb, d, g) for bb in range(nbt) for d in range(2) for g in range(ng)]
    refs = ((xf_ref, lwf_ref, af_ref, of_ref), (xb_ref, lwb_ref, ab_ref, ob_ref))

    def bd(x):
        return jnp.where(bdmask, jnp.concatenate([x] * (gw // c), axis=0), 0.0).astype(BF16)

    def gsum(xs):
        pieces = []
        for x in xs:
            hi = x.astype(BF16)
            r1 = x - hi.astype(F32)
            mid = r1.astype(BF16)
            pieces += [hi, mid, (r1 - mid.astype(F32)).astype(BF16)]
        tot = jnp.dot(jnp.concatenate(pieces, axis=0), ones_bd, preferred_element_type=F32)
        return [tot[3 * i * c:(3 * i + 1) * c] + tot[(3 * i + 1) * c:(3 * i + 2) * c] + tot[(3 * i + 2) * c:(3 * i + 3) * c]
                for i in range(len(xs))]

    def mm(x, ybd):
        return jnp.dot(x.astype(BF16), ybd, preferred_element_type=F32)

    st = []
    for bb, d, g in chains:
        x_ref, lw_ref, a_ref, _ = refs[d]
        sl = slice(g * gw, (g + 1) * gw)
        q = dict(sl=sl, rev=d == 1, r=x_ref[bb, 0, :, sl], k=x_ref[bb, 1, :, sl], v=x_ref[bb, 2, :, sl],
                 lw=lw_ref[bb, 0, :, sl], a=a_ref[bb, 0, :, sl])
        q['kx'] = q['k'] * kk_ref[:, sl]
        q['kt'] = q['k'] * (1.0 + (q['a'] - 1.0) * ka_ref[:, sl])
        st.append(q)
    for q in st:
        q['ss'], q['bon'] = gsum([q['kx'] * q['kx'], q['r'] * q['kt'] * rk_ref[:, q['sl']]])
    for q in st:
        rev = q['rev']
        kk = q['kx'] / jnp.maximum(jnp.sqrt(q['ss']), 1e-12)
        q['b'] = q['a'] * kk
        cw = q['lw']
        sft = 1
        while sft < c:
            if rev:
                cw = cw + jnp.where(row < c - sft, pltpu.roll(cw, c - sft, 0), 0.0)
            else:
                cw = cw + jnp.where(row >= sft, pltpu.roll(cw, sft, 0), 0.0)
            sft *= 2
        q['tot'] = cw[0:1] if rev else cw[c - 1:c]
        invp = jnp.exp(-cw)
        q['e2'] = jnp.exp(q['tot'] - cw)
        q['kr'] = jnp.concatenate([kk * jnp.exp(cw - q['lw']), q['r'] * jnp.exp(cw)], axis=0).astype(BF16)
        q['ktb'] = bd(q['kt'] * invp)
        q['bb'] = bd(q['b'] * invp)
        q['vbd'] = bd(q['v'])
    for q in st:
        strict = (pos > row) if q['rev'] else (pos < row)
        incl = (pos >= row) if q['rev'] else (pos <= row)
        g1 = lax.dot_general(q['kr'], q['ktb'], nt, preferred_element_type=F32)
        g2 = lax.dot_general(q['kr'], q['bb'], nt, preferred_element_type=F32)
        q['m_kt'] = jnp.where(strict, g1[:c], 0.0)
        q['n_kt'] = jnp.where(incl, g1[c:], 0.0)
        q['x'] = jnp.where(strict, g2[:c], 0.0)
        q['n_b'] = jnp.where(incl, g2[c:], 0.0)
        q['t'] = eye - q['x']
        q['xb'] = bd(q['x'])
    for q in st:
        mn = mm(jnp.concatenate([q['m_kt'], q['n_kt']], axis=0), q['vbd'])
        q['mkv'], q['nkv'] = mn[:c], mn[c:]
    lv = 2
    while lv < c:
        for q in st:
            xt = mm(jnp.concatenate([q['x'], q['t']], axis=0), q['xb'])
            q['x'], q['tx'] = xt[:c], xt[c:]
        for q in st:
            q['t'] = q['t'] + mm(q['tx'], q['xb'])
        lv *= 2
        if lv < c:
            for q in st:
                q['xb'] = bd(q['x'])
    for (bb, d, g), q in zip(chains, st):
        q['s0'] = s_ref[bb, d, g]
        q['ka'] = lax.dot_general(q['kr'], q['s0'].astype(BF16), nt, preferred_element_type=F32)
    for q in st:
        q['u'] = mm(q['t'], bd(q['ka'][:c] + q['mkv']))
    for (bb, d, g), q in zip(chains, st):
        e2 = q['e2']
        upd = lax.dot_general(jnp.concatenate([q['v'], q['u']], axis=0).astype(BF16),
                              jnp.concatenate([q['kt'] * e2, -(q['b'] * e2)], axis=0).astype(BF16),
                              tn, preferred_element_type=F32)
        s_ref[bb, d, g] = q['s0'] * jnp.exp(q['tot']) + jnp.where(bdmask, upd, 0.0)
    for q in st:
        q['y'] = q['ka'][c:] + q['nkv'] - mm(q['n_b'], bd(q['u']))
    for q in st:
        (mu,) = gsum([q['y']])
        q['dl'] = q['y'] - mu * (1.0 / n)
    for (bb, d, g), q in zip(chains, st):
        (var,) = gsum([q['dl'] * q['dl']])
        sl = q['sl']
        yn = q['dl'] * lax.rsqrt(var * (1.0 / n) + RW_GN_EPS) * gn0_ref[:, sl] + gn1_ref[:, sl]
        refs[d][3][bb, :, sl] = yn + q['bon'] * q['v']


def _rw_chunk(streams, nc, kk, ka, rk, gn):
    b, _, lt, w = streams.shape
    c = RW_CHUNK
    ncb, nb = nc // c, lt // c
    nbt = RW_SAMPLES if b % RW_SAMPLES == 0 else 1

    def rblk(j):
        return jnp.where(j < ncb, ncb - 1 - j, nb - 1 - (j - ncb))

    par = pl.BlockSpec((1, w), lambda bi, j: (0, 0))
    shp = jax.ShapeDtypeStruct((b, lt, w), F32)
    return pl.pallas_call(
        functools.partial(_rw_chunk_kernel, c=c, w=w, nbt=nbt),
        out_shape=[shp, shp],
        grid=(b // nbt, nb),
        in_specs=[pl.BlockSpec((nbt, 3, c, w), lambda bi, j: (bi, 0, j, 0)),
                  pl.BlockSpec((nbt, 1, c, w), lambda bi, j: (bi, 3, j, 0)),
                  pl.BlockSpec((nbt, 1, c, w), lambda bi, j: (bi, 5, j, 0)),
                  pl.BlockSpec((nbt, 3, c, w), lambda bi, j: (bi, 0, rblk(j), 0)),
                  pl.BlockSpec((nbt, 1, c, w), lambda bi, j: (bi, 4, rblk(j), 0)),
                  pl.BlockSpec((nbt, 1, c, w), lambda bi, j: (bi, 6, rblk(j), 0)),
                  par, par, par, par, par],
        out_specs=[pl.BlockSpec((nbt, c, w), lambda bi, j: (bi, j, 0)),
                   pl.BlockSpec((nbt, c, w), lambda bi, j: (bi, rblk(j), 0))],
        scratch_shapes=[pltpu.VMEM((nbt, 2, w // RW_GROUP, RW_GROUP, RW_GROUP), F32)],
        compiler_params=_cparams(("arbitrary", "arbitrary"), 32),
        name="rwkv_chunk",
    )(streams, streams, streams, streams, streams, streams, kk.reshape(1, w), ka.reshape(1, w), rk.reshape(1, w),
      gn[0].reshape(1, w), gn[1].reshape(1, w))


def _proj_rope_kernel(a_ref, w_ref, c_ref, s_ref, o_ref):
    j = pl.program_id(1)
    acc = jnp.dot(a_ref[...], w_ref[...], preferred_element_type=F32)

    @pl.when(j < 2)
    def _():
        lane = lax.broadcasted_iota(jnp.int32, (1, LANES), 1)
        first_half = (lane % (2 * ROPE_AXIS_FREQS)) < ROPE_AXIS_FREQS
        c = c_ref[...] * jnp.where(j == 0, DA_QK_DIM ** -0.5 * math.log2(math.e), 1.0)
        s = s_ref[...] * jnp.where(j == 0, DA_QK_DIM ** -0.5 * math.log2(math.e), 1.0)
        for g in range(acc.shape[1] // LANES):
            xs = acc[:, g * LANES:(g + 1) * LANES]
            up = pltpu.roll(xs, LANES - ROPE_AXIS_FREQS, 1)
            dn = pltpu.roll(xs, ROPE_AXIS_FREQS, 1)
            o_ref[:, g * LANES:(g + 1) * LANES] = (xs * c + jnp.where(first_half, up, dn) * s).astype(o_ref.dtype)

    @pl.when(j >= 2)
    def _():
        o_ref[...] = acc.astype(o_ref.dtype)


def _proj_rope(a, w, ct, st, lt):
    m, k = a.shape
    n = w.shape[1]
    tn = n // 3
    tm = _pick(lt, (768, 640, 512, 384, 256, 128))
    per = lt // tm
    return pl.pallas_call(
        _proj_rope_kernel,
        out_shape=jax.ShapeDtypeStruct((m, n), BF16),
        grid=(m // tm, 3),
        in_specs=[pl.BlockSpec((tm, k), lambda i, j: (i, 0)),
                  pl.BlockSpec((k, tn), lambda i, j: (0, j)),
                  pl.BlockSpec((tm, LANES), lambda i, j: (i % per, 0)),
                  pl.BlockSpec((tm, LANES), lambda i, j: (i % per, 0))],
        out_specs=pl.BlockSpec((tm, tn), lambda i, j: (i, j)),
        compiler_params=_cparams(("arbitrary", "arbitrary"), 48),
        name="proj_rope",
    )(a, w, ct, st)


def _rope_tables(seq, nc):
    rows = seq // GRID_W
    row = jnp.broadcast_to(jnp.arange(rows)[:, None], (rows, GRID_W)).reshape(seq)
    col = jnp.broadcast_to(jnp.arange(GRID_W)[None, :], (rows, GRID_W)).reshape(seq)
    inv = 1.0 / (ROPE_THETA ** (jnp.arange(ROPE_AXIS_FREQS, dtype=F32) / ROPE_AXIS_FREQS))
    ang = jnp.stack([row, col], axis=-1).astype(F32)[:, :, None] * inv
    cos, sin = jnp.cos(ang), jnp.sin(ang)
    c64 = jnp.concatenate([cos[:, 0], cos[:, 0], cos[:, 1], cos[:, 1]], axis=-1)
    s64 = jnp.concatenate([-sin[:, 0], sin[:, 0], -sin[:, 1], sin[:, 1]], axis=-1)
    ct = jnp.concatenate([jnp.ones((nc, LANES), F32), jnp.tile(c64, (1, 2))], axis=0)
    st = jnp.concatenate([jnp.zeros((nc, LANES), F32), jnp.tile(s64, (1, 2))], axis=0)
    return ct, st


def _attn_kernel(q_ref, k_ref, v_ref, lam_ref, g_ref, o_ref, *, nct, nc, lam_init, hp, qoff):
    i = pl.program_id(2) + qoff
    lane = lax.broadcasted_iota(jnp.int32, (1, LANES), 1)
    lp = lam_ref[...]
    lam = (jnp.exp(jnp.sum(lp[0:1] * lp[1:2], axis=1, keepdims=True))
           - jnp.exp(jnp.sum(lp[2:3] * lp[3:4], axis=1, keepdims=True)) + lam_init)

    def attend(nk):
        for h in range(hp):
            cols = slice(h * DA_V_DIM, (h + 1) * DA_V_DIM)
            q = q_ref[0, :, cols]
            k = k_ref[0, 0:nk, cols]
            v1 = jnp.concatenate([v_ref[0, 0:nk, cols], jnp.ones((nk, LANES), BF16)], axis=1)

            def one(sel):
                qs = jnp.where(sel, q, jnp.zeros_like(q))
                s = lax.dot_general(qs, k, (((1,), (1,)), ((), ())), preferred_element_type=F32)
                e = jnp.exp2(s - jnp.max(s, axis=-1, keepdims=True))
                ov = jnp.dot(e.astype(BF16), v1, preferred_element_type=F32)
                return ov[:, :DA_V_DIM] / ov[:, DA_V_DIM:]

            o = one(lane < DA_QK_DIM) - lam * one(lane >= DA_QK_DIM)
            on = o * lax.rsqrt(jnp.mean(o * o, axis=-1, keepdims=True) + DA_SUBLN_EPS) * g_ref[...]
            o_ref[0, :, cols] = (on * (1.0 - lam_init)).astype(o_ref.dtype)

    @pl.when(i < nct)
    def _():
        attend(nc)

    @pl.when(i >= nct)
    def _():
        attend(k_ref.shape[1])


def _attention(zbr, nc, lam_p, subln, lam_init, with_ctx, hp=4):
    b, lt, cols = zbr.shape
    w = cols // 3
    heads = w // DA_V_DIM
    tq = min(256, nc)
    nct = nc // tq
    hg = heads // hp
    bw = hp * DA_V_DIM
    qoff = 0 if with_ctx else nct
    kern = functools.partial(_attn_kernel, nct=nct, nc=nc, lam_init=lam_init, hp=hp, qoff=qoff)
    return pl.pallas_call(
        kern,
        out_shape=jax.ShapeDtypeStruct((b, lt - qoff * tq, w), BF16),
        grid=(b, hg, lt // tq - qoff),
        in_specs=[pl.BlockSpec((1, tq, bw), lambda bi, h, i: (bi, i + qoff, h)),
                  pl.BlockSpec((1, lt, bw), lambda bi, h, i: (bi, 0, hg + h)),
                  pl.BlockSpec((1, lt, bw), lambda bi, h, i: (bi, 0, 2 * hg + h)),
                  pl.BlockSpec((4, DA_QK_DIM), lambda bi, h, i: (0, 0)),
                  pl.BlockSpec((1, DA_V_DIM), lambda bi, h, i: (0, 0))],
        out_specs=pl.BlockSpec((1, tq, bw), lambda bi, h, i: (bi, i, h)),
        compiler_params=_cparams(("arbitrary", "arbitrary", "arbitrary"), 58),
        name="diff_attn",
    )(zbr, zbr, zbr, lam_p, subln.reshape(1, DA_V_DIM))


def _seg_masks(c, e):
    row = lax.broadcasted_iota(jnp.int32, (c, e), 0)
    mk = {}
    h = 1
    while h < min(SUBLANES, c):
        odd = (row % (2 * h)) >= h
        for s in range(1, h + 1):
            mk[(h, s, 0)] = jnp.where(jnp.logical_and(odd, (row % h) == s - 1), 1.0, 0.0)
            mk[(h, s, 1)] = jnp.where(jnp.logical_and(~odd, (row % h) == h - s), 1.0, 0.0)
        h *= 2
    return mk


def _seg_scans(lf, mk, c):
    cs, ss = {1: lf}, {1: lf}
    h = 1
    while h < c:
        x, y = cs[h], ss[h]
        if h < SUBLANES:
            x3 = x.reshape(c // SUBLANES, SUBLANES, x.shape[1])
            y3 = y.reshape(c // SUBLANES, SUBLANES, y.shape[1])
            for s in range(1, h + 1):
                x = x + mk[(h, s, 0)] * pltpu.roll(x3, s, 1).reshape(x.shape)
                y = y + mk[(h, s, 1)] * pltpu.roll(y3, SUBLANES - s, 1).reshape(y.shape)
            cs[2 * h], ss[2 * h] = x, y
        else:
            px = [x[j * h:(j + 1) * h] for j in range(c // h)]
            py = [y[j * h:(j + 1) * h] for j in range(c // h)]
            nx = [px[j] + px[j - 1][h - 1:h] if j % 2 == 1 else px[j] for j in range(c // h)]
            ny = [py[j] + py[j + 1][0:1] if j % 2 == 0 else py[j] for j in range(c // h)]
            cs[2 * h], ss[2 * h] = jnp.concatenate(nx, axis=0), jnp.concatenate(ny, axis=0)
        h *= 2
    return cs, ss


def _hg_kernel(q_ref, f_ref, v_ref, lb_ref, o_ref, st_ref, *, tc, c, heads, rev):
    j = pl.program_id(1)

    @pl.when(j == 0)
    def _():
        st_ref[...] = jnp.zeros_like(st_ref)

    ti = lax.broadcasted_iota(jnp.int32, (c, c), 0)
    si = lax.broadcasted_iota(jnp.int32, (c, c), 1)
    nt = (((1,), (1,)), ((), ()))
    tn = (((0,), (0,)), ((), ()))
    mk = _seg_masks(c, HG_EXPAND)
    masks, hh = {}, 1
    while hh < c:
        tb, sb = ti // hh, si // hh
        masks[hh] = (jnp.logical_and(tb % 2 == 0, sb == tb + 1) if rev
                     else jnp.logical_and(tb % 2 == 1, sb == tb - 1))
        hh *= 2

    def head(h, carry):
        cols = pl.ds(pl.multiple_of(h * HG_EXPAND, HG_EXPAND), HG_EXPAND)
        lb = lb_ref[:, cols]
        chunks = range(tc // c)
        for ci in (reversed(chunks) if rev else chunks):
            rows = slice(ci * c, (ci + 1) * c)
            q = _silu(q_ref[0, rows, cols])
            f = lb + (1.0 - lb) * _sigmoid(f_ref[0, rows, cols])
            k = 1.0 - f
            lf = jnp.log(f)
            v = v_ref[0, rows, cols].astype(BF16)
            cs, ss = _seg_scans(lf, mk, c)
            qs, ks = (ss, cs) if rev else (cs, ss)
            att = jnp.where(ti == si, lax.dot_general(q.astype(BF16), k.astype(BF16), nt,
                                                      preferred_element_type=F32), 0.0)
            hh = 1
            while hh < c:
                qe = (q * jnp.exp(qs[hh])).astype(BF16)
                ke = (k * jnp.exp(ks[hh] - lf)).astype(BF16)
                a = lax.dot_general(qe, ke, nt, preferred_element_type=F32)
                att = att + jnp.where(masks[hh], a, 0.0)
                hh *= 2
            st = st_ref[h]
            qe = (q * jnp.exp(qs[c])).astype(BF16)
            o = lax.dot_general(qe, st.astype(BF16), nt, preferred_element_type=F32)
            o = o + jnp.dot(att.astype(BF16), v, preferred_element_type=F32)
            o_ref[0, rows, cols] = o
            ke = (k * jnp.exp(ks[c] - lf)).astype(BF16)
            tot = qs[c][0:1] if rev else qs[c][c - 1:c]
            st_ref[h] = st * jnp.exp(tot) + lax.dot_general(v, ke, tn, preferred_element_type=F32)
        return carry

    lax.fori_loop(0, heads, head, 0, unroll=4)


def _hgrn(zc, nc, lb, rev):
    b, lt, cols = zc.shape
    w = cols // 5
    heads = w // HG_EXPAND
    tc = min(256, nc)
    c = min(HG_CHUNK, tc)
    ncb, nb = nc // tc, lt // tc

    def blk(j):
        return jnp.where(j < ncb, ncb - 1 - j, nb - 1 - (j - ncb)) if rev else j

    fcol = 2 if rev else 1
    return pl.pallas_call(
        functools.partial(_hg_kernel, tc=tc, c=c, heads=heads, rev=rev),
        out_shape=jax.ShapeDtypeStruct((b, lt, w), F32),
        grid=(b, nb),
        in_specs=[pl.BlockSpec((1, tc, w), lambda bi, j: (bi, blk(j), 0)),
                  pl.BlockSpec((1, tc, w), lambda bi, j: (bi, blk(j), fcol)),
                  pl.BlockSpec((1, tc, w), lambda bi, j: (bi, blk(j), 3)),
                  pl.BlockSpec((1, w), lambda bi, j: (0, 0))],
        out_specs=pl.BlockSpec((1, tc, w), lambda bi, j: (bi, blk(j), 0)),
        scratch_shapes=[pltpu.VMEM((heads, HG_EXPAND, HG_EXPAND), F32)],
        compiler_params=_cparams(("arbitrary", "arbitrary"), 32),
        name="hgrn2_bwd" if rev else "hgrn2_fwd",
    )(zc, zc, zc, lb.reshape(1, w))


def _merge_kernel(oaf_ref, oab_ref, ga_ref, yb_ref, of_ref, ob_ref, gc_ref, hn_ref, gl0_ref, gl1_ref, gl2_ref, wb_ref,
                  o_ref, *, heads):
    gls = (gl0_ref, gl1_ref, gl2_ref)
    ya = (oaf_ref[0] + oab_ref[0]) * ga_ref[0]
    oc = of_ref[0] + ob_ref[0]
    parts = []
    for h in range(heads):
        x = oc[:, h * HG_EXPAND:(h + 1) * HG_EXPAND]
        parts.append(x * lax.rsqrt(jnp.mean(x * x, axis=-1, keepdims=True) + NORM_EPS))
    yc = jnp.concatenate(parts, axis=-1) * hn_ref[...] * _silu(gc_ref[0])
    acc = None
    for bi, y in enumerate((ya.astype(BF16), yb_ref[0], yc.astype(BF16))):
        term = _sigmoid_t(gls[bi][0].astype(F32)) * jnp.dot(y, wb_ref[bi], preferred_element_type=F32)
        acc = term if acc is None else acc + term
    o_ref[0] = acc.astype(BF16)


def _merge(oaf, oab, ga, yb, of, ob, zc, hn, gl, wbr, row0, tl=128, tn=2048):
    b, lt, w = ga.shape
    d = wbr.shape[2]
    tl = _pick(math.gcd(lt, row0), (tl, 64))
    r0 = row0 // tl
    heads = w // HG_EXPAND
    nn = d // tn
    row = lambda n, bi, i: (bi, i + r0, 0)
    glspec = lambda br: pl.BlockSpec((1, tl, tn), lambda n, bi, i: (bi, i + r0, br * nn + n))
    return pl.pallas_call(
        functools.partial(_merge_kernel, heads=heads),
        out_shape=jax.ShapeDtypeStruct((b, lt - row0, d), BF16),
        grid=(d // tn, b, lt // tl - r0),
        in_specs=[pl.BlockSpec((1, tl, w), row),
                  pl.BlockSpec((1, tl, w), row),
                  pl.BlockSpec((1, tl, w), row),
                  pl.BlockSpec((1, tl, w), lambda n, bi, i: (bi, i, 0)),
                  pl.BlockSpec((1, tl, w), row),
                  pl.BlockSpec((1, tl, w), row),
                  pl.BlockSpec((1, tl, w), lambda n, bi, i: (bi, i + r0, 4)),
                  pl.BlockSpec((1, w), lambda n, bi, i: (0, 0)),
                  glspec(0), glspec(1), glspec(2),
                  pl.BlockSpec((3, w, tn), lambda n, bi, i: (0, 0, n))],
        out_specs=pl.BlockSpec((1, tl, tn), lambda n, bi, i: (bi, i, n)),
        compiler_params=_cparams(("arbitrary", "arbitrary", "arbitrary"), 48),
        name="branch_merge",
    )(oaf, oab, ga, yb, of, ob, zc, hn, gl, gl, gl, wbr)


def _proj_post_kernel(a_ref, w_ref, x_ref, g_ref, gate_ref, o_ref):
    mx = jnp.dot(a_ref[0], w_ref[...], preferred_element_type=F32)
    mn = mx * lax.rsqrt(jnp.mean(mx * mx, axis=-1, keepdims=True) + NORM_EPS) * g_ref[...]
    o_ref[0] = x_ref[0] + gate_ref[0, 0] * mn


def _proj_post(a, wout, xa, g, gsel, nc, row0, tl=256):
    b, lt, d = xa.shape
    tl = min(tl, nc)
    nct = nc // tl
    r0 = row0 // tl
    return pl.pallas_call(
        _proj_post_kernel,
        out_shape=jax.ShapeDtypeStruct((b, lt, d), F32),
        grid=(b, lt // tl - r0),
        in_specs=[pl.BlockSpec((1, tl, d), lambda bi, i: (bi, i, 0)),
                  pl.BlockSpec((d, d), lambda bi, i: (0, 0)),
                  pl.BlockSpec((1, tl, d), lambda bi, i: (bi, i + r0, 0)),
                  pl.BlockSpec((1, d), lambda bi, i: (0, 0)),
                  pl.BlockSpec((1, 1, 1, d), lambda bi, i: (bi, jnp.where(i + r0 >= nct, 1, 0), 0, 0))],
        out_specs=pl.BlockSpec((1, tl, d), lambda bi, i: (bi, i + r0, 0)),
        input_output_aliases={2: 0},
        compiler_params=_cparams(("arbitrary", "arbitrary"), 48),
        name="out_proj_post",
    )(a, wout, xa, g.reshape(1, d), gsel)


def _rank_kernel(afft_ref, slot_ref, *, n, cap, tw):
    aff = afft_ref[0]
    bits = lax.bitcast_convert_type(aff, jnp.int32)
    e_num = aff.shape[0]

    def count(mask):
        return jnp.sum(jnp.where(mask, 1.0, 0.0), axis=1, keepdims=True)

    def bit_step(i, thr):
        cand = thr | lax.shift_left(jnp.int32(1), 30 - i)
        return jnp.where(count(bits >= cand) >= cap, cand, thr)

    thr = lax.fori_loop(0, 31, bit_step, jnp.zeros((e_num, 1), jnp.int32))
    gt = bits > thr
    eq = bits == thr
    need = cap - count(gt)
    tri = jnp.where(lax.broadcasted_iota(jnp.int32, (tw, tw), 0) < lax.broadcasted_iota(jnp.int32, (tw, tw), 1),
                    1.0, 0.0).astype(BF16)

    def prefix(mask):
        parts, carry = [], jnp.zeros((e_num, 1), F32)
        for j in range(n // tw):
            m = jnp.where(mask[:, j * tw:(j + 1) * tw], 1.0, 0.0)
            parts.append(jnp.dot(m.astype(BF16), tri, preferred_element_type=F32) + carry)
            carry = carry + jnp.sum(m, axis=1, keepdims=True)
        return jnp.concatenate(parts, axis=1)

    sel = jnp.logical_or(gt, jnp.logical_and(eq, prefix(eq) < need))
    slot_ref[0] = jnp.where(sel, prefix(sel), float(cap)).astype(jnp.int32)


def _rank(afft, cap):
    b, e_num, n = afft.shape
    tw = min(LANES, n)
    return pl.pallas_call(
        functools.partial(_rank_kernel, n=n, cap=cap, tw=tw),
        out_shape=jax.ShapeDtypeStruct((b, e_num, n), jnp.int32),
        grid=(b,),
        in_specs=[pl.BlockSpec((1, e_num, n), lambda bi: (bi, 0, 0))],
        out_specs=pl.BlockSpec((1, e_num, n), lambda bi: (bi, 0, 0)),
        compiler_params=_cparams(("arbitrary",), 32),
        name="ec_rank",
    )(afft)


def _gather_kernel(slot_ref, afft_ref, h_ref, o_ref, g_ref, *, cap):
    e = pl.program_id(1)
    sl = slot_ref[0, pl.ds(e, 1), :]
    hit = lax.broadcasted_iota(jnp.int32, (cap, 1), 0) == sl
    o_ref[0] = jnp.dot(jnp.where(hit, 1.0, 0.0).astype(BF16), h_ref[0], preferred_element_type=F32).astype(BF16)
    gate = jnp.sum(jnp.where(hit, afft_ref[0, pl.ds(e, 1), :], 0.0), axis=1, keepdims=True)
    g_ref[0] = jnp.broadcast_to(gate, (cap, LANES))


def _gather(slot, afft, hs, cap):
    b, e_num, n = slot.shape
    d = hs.shape[2]
    return pl.pallas_call(
        functools.partial(_gather_kernel, cap=cap),
        out_shape=[jax.ShapeDtypeStruct((e_num, b * cap, d), BF16),
                   jax.ShapeDtypeStruct((e_num, b * cap, LANES), F32)],
        grid=(b, e_num),
        in_specs=[pl.BlockSpec((1, e_num, n), lambda bi, e: (bi, 0, 0)),
                  pl.BlockSpec((1, e_num, n), lambda bi, e: (bi, 0, 0)),
                  pl.BlockSpec((1, n, d), lambda bi, e: (bi, 0, 0))],
        out_specs=[pl.BlockSpec((1, cap, d), lambda bi, e: (e, bi, 0)),
                   pl.BlockSpec((1, cap, LANES), lambda bi, e: (e, bi, 0))],
        compiler_params=_cparams(("arbitrary", "arbitrary"), 40),
        name="ec_gather",
    )(slot, afft, hs)


def _ffn_kernel(x_ref, g_ref, w1_ref, w3_ref, w2_ref, o_ref, acc_ref):
    f = pl.program_id(2)
    x = x_ref[0]
    h1 = jnp.dot(x, w1_ref[0, 0].astype(BF16), preferred_element_type=F32)
    h3 = jnp.dot(x, w3_ref[0, 0].astype(BF16), preferred_element_type=F32)
    hid = (_silu(h1) * h3).astype(BF16)

    @pl.when(f == 0)
    def _():
        acc_ref[...] = jnp.zeros_like(acc_ref)

    d = acc_ref.shape[1]
    for n0 in range(0, d, FFN_OUT_CHUNK):
        cols = slice(n0, n0 + FFN_OUT_CHUNK)
        acc_ref[:, cols] += jnp.dot(hid, w2_ref[0, 0, :, cols].astype(BF16), preferred_element_type=F32)

    @pl.when(f == pl.num_programs(2) - 1)
    def _():
        o_ref[0] = (acc_ref[...] * g_ref[0, :, 0:1]).astype(BF16)


def _ffn(xe, gate, w1, w3, w2, layer, tf=256):
    e_num, m, d = xe.shape
    ff = w1.shape[3]
    tm = m
    once = pl.Buffered(1)
    return pl.pallas_call(
        _ffn_kernel,
        out_shape=jax.ShapeDtypeStruct((e_num, m, d), BF16),
        grid=(e_num, m // tm, ff // tf),
        in_specs=[pl.BlockSpec((1, tm, d), lambda e, i, f: (e, i, 0), pipeline_mode=once),
                  pl.BlockSpec((1, tm, LANES), lambda e, i, f: (e, i, 0), pipeline_mode=once),
                  pl.BlockSpec((1, 1, d, tf), lambda e, i, f: (layer, e, 0, f)),
                  pl.BlockSpec((1, 1, d, tf), lambda e, i, f: (layer, e, 0, f)),
                  pl.BlockSpec((1, 1, tf, d), lambda e, i, f: (layer, e, f, 0))],
        out_specs=pl.BlockSpec((1, tm, d), lambda e, i, f: (e, i, 0), pipeline_mode=once),
        scratch_shapes=[pltpu.VMEM((tm, d), F32)],
        compiler_params=_cparams(("arbitrary", "arbitrary", "arbitrary"), 58),
        name="ec_ffn",
    )(xe, gate, w1, w3, w2)


def _combine_kernel(slot_ref, ye_ref, x_ref, g_ref, gate_ref, o_ref, *, cap, e_num):
    lane = lax.broadcasted_iota(jnp.int32, (1, e_num), 1)
    pos = lax.broadcasted_iota(jnp.int32, (1, cap), 1)
    sl = slot_ref[0]
    y = None
    for e in range(e_num):
        se = jnp.sum(jnp.where(lane == e, sl, 0), axis=1, keepdims=True)
        pt = jnp.where(se == pos, 1.0, 0.0).astype(BF16)
        part = jnp.dot(pt, ye_ref[e], preferred_element_type=F32)
        y = part if y is None else y + part
    yn = y * lax.rsqrt(jnp.mean(y * y, axis=-1, keepdims=True) + NORM_EPS) * g_ref[...]
    o_ref[0] = x_ref[0] + gate_ref[0, 0] * yn


def _combine(slot_c, ye, xa, g, gsel, seg, row0, slot0, cap, final, tt=256):
    b, n, e_num = slot_c.shape
    d = xa.shape[2]
    tt = _pick(math.gcd(n, row0), (tt, 128, 64))
    assert row0 % tt == 0 and slot0 % cap == 0
    r0, s0 = row0 // tt, slot0 // cap
    return pl.pallas_call(
        functools.partial(_combine_kernel, cap=cap, e_num=e_num),
        out_shape=jax.ShapeDtypeStruct((b, n, d) if final else xa.shape, F32),
        grid=(b, n // tt),
        in_specs=[pl.BlockSpec((1, tt, e_num), lambda bi, j: (bi, j, 0)),
                  pl.BlockSpec((e_num, cap, d), lambda bi, j: (0, s0 + bi, 0)),
                  pl.BlockSpec((1, tt, d), lambda bi, j: (bi, r0 + j, 0)),
                  pl.BlockSpec((1, d), lambda bi, j: (0, 0)),
                  pl.BlockSpec((1, 1, 1, d), lambda bi, j: (bi, seg, 0, 0))],
        out_specs=pl.BlockSpec((1, tt, d), lambda bi, j: (bi, (0 if final else r0) + j, 0)),
        input_output_aliases={} if final else {2: 0},
        compiler_params=_cparams(("arbitrary", "arbitrary"), 56),
        name="ec_combine_post",
    )(slot_c, ye, xa, g.reshape(1, d), gsel)


def _sel(mod_c, mod_x, idx):
    b = mod_x.shape[0]
    mc = jnp.broadcast_to(mod_c[jnp.array(idx)][None], (b, len(idx), mod_c.shape[-1]))
    return jnp.stack([mc, mod_x[:, jnp.array(idx)]], axis=1)


def _layer(xa, nc, mod_x, mod_c, p, lam_init, hg_lb, ct, st, with_ctx):
    b, lt, d = xa.shape
    w = d // 2
    a_cols = 3 * w + 2 * RW_DECAY_RANK + 2 * RW_ICL_RANK + RW_GATE_RANK
    col_b = a_cols
    col_c = col_b + 3 * w
    col_g = col_c + 5 * w

    h = _norm_mod(xa, p['norm_g'][0], _sel(mod_c, mod_x, (0, 1)), nc)
    h2 = h.reshape(b * lt, d)
    w_in = p['w_in']
    za = _matmul(h2, w_in[:, :col_b].astype(BF16), F32, tn=a_cols // 3).reshape(b, lt, a_cols)
    zc = _matmul(h2, w_in[:, col_c:col_g].astype(BF16), F32).reshape(b, lt, 5 * w)
    gl = _matmul(h2, w_in[:, col_g:].astype(BF16), BF16).reshape(b, lt, 3 * d)

    streams, ga = _rw_prep(za, nc, p['rw_mu'], p['rw_w0'], p['rw_wB'], p['rw_a0'], p['rw_aB'], p['rw_gB'])
    oaf, oab = _rw_chunk(streams, nc, p['rw_kk'], p['rw_ka'], p['rw_rk'], p['rw_gn'])

    zbr = _proj_rope(h2, w_in[:, col_b:col_c].astype(BF16), ct, st, lt).reshape(b, lt, 3 * w)
    yb = _attention(zbr, nc, p['da_lambda'], p['da_subln'], lam_init, with_ctx)

    of = _hgrn(zc, nc, hg_lb, rev=False)
    ob = _hgrn(zc, nc, hg_lb, rev=True)

    hn = jnp.tile(p['hg_norm'], w // HG_EXPAND).reshape(1, w)
    skip = 0 if with_ctx else nc
    merged = _merge(oaf, oab, ga, yb, of, ob, zc, hn, gl, p['w_branch'].astype(BF16), skip)
    xa = _proj_post(merged, p['w_out'].astype(BF16), xa, p['norm_g'][1], _sel(mod_c, mod_x, (2,)), nc, skip)

    hm, afft = _norm_mod(xa, p['norm_g'][2], _sel(mod_c, mod_x, (3, 4)), nc, router_t=p['moe_router'].T, row0=skip)
    gsel = _sel(mod_c, mod_x, (5,))
    sets = [(nc, lt - nc, 1)] + ([(0, nc, 0)] if with_ctx else [])
    routed, xes, gates = [], [], []
    for row0, nn, seg in sets:
        cap = EC_CAPACITY_FACTOR * nn // N_EXPERTS
        at = afft[:, :, row0 - skip:row0 - skip + nn]
        slot = _rank(at, cap)
        xe_s, gate_s = _gather(slot, at, hm[:, row0 - skip:row0 - skip + nn], cap)
        xes.append(xe_s)
        gates.append(gate_s)
        routed.append((jnp.swapaxes(slot, 1, 2), row0, seg, cap))
    xe = xes[0] if len(xes) == 1 else jnp.concatenate(xes, axis=1)
    gate = gates[0] if len(gates) == 1 else jnp.concatenate(gates, axis=1)
    ye = _ffn(xe, gate, p['moe_w1'], p['moe_w3'], p['moe_w2'], p['layer'])
    slot0 = 0
    for slot_c, row0, seg, cap in routed:
        xa = _combine(slot_c, ye, xa, p['norm_g'][3], gsel, seg, row0, slot0, cap, final=not with_ctx)
        slot0 += b * cap
    return xa


def kernel(x, c, ctx, c_ctx, w_ada, b_ada, norm_g, w_in, rw_mu, rw_w0, rw_wB, rw_a0, rw_aB, rw_gB, rw_kk, rw_ka, rw_rk, rw_gn, da_lambda, da_subln, hg_lb_logits, hg_norm, w_branch, w_out, moe_router, moe_w1, moe_w3, moe_w2):
    b, seq, d = x.shape
    nc = ctx.shape[1]
    depth = w_ada.shape[0]
    ct, st = _rope_tables(seq, nc)
    lb_w = jax.nn.softmax(hg_lb_logits.astype(F32), axis=0)
    hg_lb = jnp.cumsum(lb_w, axis=0) - lb_w[0]
    rows = ((b + 1 + SUBLANES - 1) // SUBLANES) * SUBLANES
    cc = jnp.zeros((rows, d), F32).at[:b].set(c).at[b].set(c_ctx)
    mod = _modulation(cc, w_ada, b_ada)
    xa = jnp.concatenate([ctx, x], axis=1)
    for l in range(depth):
        p = dict(norm_g=norm_g[l], w_in=w_in[l], rw_mu=rw_mu[l], rw_w0=rw_w0[l], rw_wB=rw_wB[l],
                 rw_a0=rw_a0[l], rw_aB=rw_aB[l], rw_gB=rw_gB[l], rw_kk=rw_kk[l], rw_ka=rw_ka[l],
                 rw_rk=rw_rk[l], rw_gn=rw_gn[l], da_lambda=da_lambda[l], da_subln=da_subln[l],
                 hg_norm=hg_norm[l], w_branch=w_branch[l], w_out=w_out[l], moe_router=moe_router[l],
                 moe_w1=moe_w1, moe_w3=moe_w3, moe_w2=moe_w2, layer=l)
        mod_x = mod[l, :b].reshape(b, 6, d)
        mod_c = mod[l, b].reshape(6, d)
        lam_init = 0.8 - 0.6 * math.exp(-0.3 * l)
        xa = _layer(xa, nc, mod_x, mod_c, p, lam_init, hg_lb[l], ct, st, with_ctx=l < depth - 1)
    return xa
```

```python
import functools
import math

import jax
import jax.numpy as jnp
from jax import lax
from jax.experimental import pallas as pl
from jax.experimental.pallas import tpu as pltpu

F32 = jnp.float32
BF16 = jnp.bfloat16
HIGHEST = lax.Precision.HIGHEST

GRID_W = 64
RW_HEAD_DIM = 64
RW_DECAY_RANK = 64
RW_ICL_RANK = 64
RW_GATE_RANK = 128
RW_GN_EPS = 64e-5
DA_QK_DIM = 64
DA_V_DIM = 128
ROPE_THETA = 10000.0
ROPE_AXIS_FREQS = 16
DA_SUBLN_EPS = 1e-5
HG_EXPAND = 128
N_EXPERTS = 16
EC_CAPACITY_FACTOR = 2
NORM_EPS = 1e-6

LANES = 128
SUBLANES = 8
V7X_VMEM_BYTES = 64 * 1024 * 1024
HG_CHUNK = 64
RW_CHUNK = 64
RW_GROUP = 256
RW_SAMPLES = 2
FFN_OUT_CHUNK = 512


def _cparams(sem, vmem_mb):
    limit = int(vmem_mb * 1024 * 1024)
    assert limit < V7X_VMEM_BYTES
    return pltpu.CompilerParams(dimension_semantics=sem, vmem_limit_bytes=limit)


def _sigmoid(x):
    return 1.0 / (1.0 + jnp.exp(-x))


def _sigmoid_t(x):
    return 0.5 * jnp.tanh(0.5 * x) + 0.5


def _silu(x):
    return x * _sigmoid_t(x)


def _mod_kernel(c_ref, w_ref, b_ref, o_ref):
    a = _silu(c_ref[...]).astype(BF16)
    o_ref[0] = jnp.dot(a, w_ref[0].astype(BF16), preferred_element_type=F32) + b_ref[0]


def _modulation(cc, w_ada, b_ada):
    depth, d, n = w_ada.shape
    rows = cc.shape[0]
    tn = 1024
    return pl.pallas_call(
        _mod_kernel,
        out_shape=jax.ShapeDtypeStruct((depth, rows, n), F32),
        grid=(depth, n // tn),
        in_specs=[pl.BlockSpec((rows, d), lambda l, j: (0, 0)),
                  pl.BlockSpec((1, d, tn), lambda l, j: (l, 0, j)),
                  pl.BlockSpec((1, 1, tn), lambda l, j: (l, 0, j))],
        out_specs=pl.BlockSpec((1, rows, tn), lambda l, j: (l, 0, j)),
        compiler_params=_cparams(("arbitrary", "arbitrary"), 40),
        name="adaln_mod",
    )(cc, w_ada, b_ada.reshape(depth, 1, n))


def _norm_mod_kernel(x_ref, g_ref, ms_ref, *rest, with_router):
    x = x_ref[0]
    xn = x * lax.rsqrt(jnp.mean(x * x, axis=-1, keepdims=True) + NORM_EPS) * g_ref[...]
    h = xn * (1.0 + ms_ref[0, 0, 1:2, :]) + ms_ref[0, 0, 0:1, :]
    if with_router:
        rt_ref, h_ref, aff_ref = rest
        h_ref[0] = h.astype(BF16)
        lt = lax.dot_general(rt_ref[...], h, (((1,), (1,)), ((), ())),
                             precision=HIGHEST, preferred_element_type=F32)
        e = jnp.exp(lt - jnp.max(lt, axis=0, keepdims=True))
        aff_ref[0] = e / jnp.sum(e, axis=0, keepdims=True)
    else:
        (h_ref,) = rest
        h_ref[0] = h.astype(BF16)


def _norm_mod(xa, g, msel, nc, router_t=None, row0=0, tl=256):
    b, lt, d = xa.shape
    tl = min(tl, nc)
    nct = nc // tl
    r0 = row0 // tl
    with_router = router_t is not None
    in_specs = [pl.BlockSpec((1, tl, d), lambda bi, i: (bi, i + r0, 0)),
                pl.BlockSpec((1, d), lambda bi, i: (0, 0)),
                pl.BlockSpec((1, 1, 2, d), lambda bi, i: (bi, jnp.where(i + r0 >= nct, 1, 0), 0, 0))]
    args = [xa, g.reshape(1, d), msel]
    out_shape = [jax.ShapeDtypeStruct((b, lt - row0, d), BF16)]
    out_specs = [pl.BlockSpec((1, tl, d), lambda bi, i: (bi, i, 0))]
    if with_router:
        e = router_t.shape[0]
        in_specs.append(pl.BlockSpec((e, d), lambda bi, i: (0, 0)))
        args.append(router_t)
        out_shape.append(jax.ShapeDtypeStruct((b, e, lt - row0), F32))
        out_specs.append(pl.BlockSpec((1, e, tl), lambda bi, i: (bi, 0, i)))
    res = pl.pallas_call(
        functools.partial(_norm_mod_kernel, with_router=with_router),
        out_shape=out_shape, grid=(b, lt // tl - r0), in_specs=in_specs, out_specs=out_specs,
        compiler_params=_cparams(("arbitrary", "arbitrary"), 32),
        name="norm_mod_router" if with_router else "norm_mod",
    )(*args)
    return res if with_router else res[0]


def _matmul_kernel(a_ref, w_ref, o_ref):
    o_ref[...] = jnp.dot(a_ref[...], w_ref[...], preferred_element_type=F32).astype(o_ref.dtype)


def _pick(n, cands):
    for c in cands:
        if n % c == 0:
            return c
    return n


def _matmul(a, w, out_dtype, tm=1024, tn=1024):
    m, k = a.shape
    n = w.shape[1]
    tm = _pick(m, (tm, 768, 512, 384, 256, 128))
    tn = _pick(n, (tn, 768, 512, 384, 256, 128))
    return pl.pallas_call(
        _matmul_kernel,
        out_shape=jax.ShapeDtypeStruct((m, n), out_dtype),
        grid=(m // tm, n // tn),
        in_specs=[pl.BlockSpec((tm, k), lambda i, j: (i, 0)),
                  pl.BlockSpec((k, tn), lambda i, j: (0, j))],
        out_specs=pl.BlockSpec((tm, tn), lambda i, j: (i, j)),
        compiler_params=_cparams(("arbitrary", "arbitrary"), 48),
        name="matmul",
    )(a, w)


def _rw_prep_kernel(z_ref, zp_ref, zn_ref, mu_ref, w0_ref, wb_ref, a0_ref, ab_ref, gb_ref,
                    o_ref, g_ref, *, tl, nct, nt, w):
    i = pl.program_id(1)
    z = z_ref[0]
    row = lax.broadcasted_iota(jnp.int32, (tl, 1), 0)
    seg_start = jnp.logical_or(i == 0, i == nct)
    seg_end = jnp.logical_or(i == nct - 1, i == nt - 1)
    prev_row = jnp.where(seg_start, 0.0, zp_ref[0, SUBLANES - 1:SUBLANES, :])
    next_row = jnp.where(seg_end, 0.0, zn_ref[0, 0:1, :])
    prev = jnp.where(row == 0, prev_row, pltpu.roll(z, 1, 0))
    nxt = jnp.where(row == tl - 1, next_row, pltpu.roll(z, tl - 1, 0))
    zs = z + mu_ref[0:1, :] * (prev - z) + mu_ref[1:2, :] * (nxt - z)
    o_ref[0, 0] = zs[:, 0:w]
    o_ref[0, 1] = zs[:, w:2 * w]
    o_ref[0, 2] = zs[:, 2 * w:3 * w]
    o4 = 3 * w + 2 * RW_DECAY_RANK
    o5 = o4 + 2 * RW_ICL_RANK
    wd = jnp.tanh(zs[:, 3 * w:o4])
    wl = jnp.dot(wd.astype(BF16), wb_ref[...], preferred_element_type=F32) + w0_ref[...]
    dec = -math.exp(-0.5) * _sigmoid_t(wl)
    o_ref[0, 3] = dec[:, 0:w]
    o_ref[0, 4] = dec[:, w:2 * w]
    al = jnp.dot(zs[:, o4:o5].astype(BF16), ab_ref[...], preferred_element_type=F32) + a0_ref[...]
    av = _sigmoid_t(al)
    o_ref[0, 5] = av[:, 0:w]
    o_ref[0, 6] = av[:, w:2 * w]
    g_ref[0] = jnp.dot(_sigmoid_t(zs[:, o5:]).astype(BF16), gb_ref[...], preferred_element_type=F32)


def _blockdiag2(m):
    r, w = m.shape[1], m.shape[2]
    z = jnp.zeros((r, w), m.dtype)
    return jnp.concatenate([jnp.concatenate([m[0], z], axis=1), jnp.concatenate([z, m[1]], axis=1)], axis=0)


def _rw_prep(za, nc, mu, w0, wb, a0, ab, gb, tl=256):
    b, lt, acols = za.shape
    w = w0.shape[1]
    tl = min(tl, nc)
    nct, nt = nc // tl, lt // tl
    r8 = tl // SUBLANES
    nb8 = lt // SUBLANES
    full = lambda shp: pl.BlockSpec(shp, lambda bi, i: tuple(0 for _ in shp))
    kern = functools.partial(_rw_prep_kernel, tl=tl, nct=nct, nt=nt, w=w)
    return pl.pallas_call(
        kern,
        out_shape=[jax.ShapeDtypeStruct((b, 7, lt, w), F32), jax.ShapeDtypeStruct((b, lt, w), F32)],
        grid=(b, nt),
        in_specs=[pl.BlockSpec((1, tl, acols), lambda bi, i: (bi, i, 0)),
                  pl.BlockSpec((1, SUBLANES, acols), lambda bi, i: (bi, jnp.maximum(i * r8 - 1, 0), 0)),
                  pl.BlockSpec((1, SUBLANES, acols), lambda bi, i: (bi, jnp.minimum((i + 1) * r8, nb8 - 1), 0)),
                  full((2, acols)), full((1, 2 * w)), full((2 * RW_DECAY_RANK, 2 * w)),
                  full((1, 2 * w)), full((2 * RW_ICL_RANK, 2 * w)), full((RW_GATE_RANK, w))],
        out_specs=[pl.BlockSpec((1, 7, tl, w), lambda bi, i: (bi, 0, i, 0)),
                   pl.BlockSpec((1, tl, w), lambda bi, i: (bi, i, 0))],
        compiler_params=_cparams(("arbitrary", "arbitrary"), 48),
        name="rwkv_prep",
    )(za, za, za, mu, w0.reshape(1, 2 * w), _blockdiag2(wb).astype(BF16), a0.reshape(1, 2 * w),
      _blockdiag2(ab).astype(BF16), gb.astype(BF16))


def _rw_chunk_kernel(xf_ref, lwf_ref, af_ref, xb_ref, lwb_ref, ab_ref, kk_ref, ka_ref, rk_ref, gn0_ref, gn1_ref,
                     of_ref, ob_ref, s_ref, *, c, w, nbt):
    j = pl.program_id(1)

    @pl.when(j == 0)
    def _():
        s_ref[...] = jnp.zeros_like(s_ref)

    gw, n = RW_GROUP, RW_HEAD_DIM
    ng = w // gw
    row = lax.broadcasted_iota(jnp.int32, (c, 1), 0)
    pos = lax.broadcasted_iota(jnp.int32, (1, gw), 1) % n
    bdmask = (lax.broadcasted_iota(jnp.int32, (gw, 1), 0) // n) == (lax.broadcasted_iota(jnp.int32, (1, gw), 1) // n)
    ones_bd = jnp.where(bdmask, 1.0, 0.0).astype(BF16)
    eye = jnp.where(pos == row, 1.0, 0.0)
    nt = (((1,), (1,)), ((), ()))
    tn = (((0,), (0,)), ((), ()))
    chains = [(bb, d, g) for bb in range(nbt) for d in range(2) for g in range(ng)]
    refs = ((xf_ref, lwf_ref, af_ref, of_ref), (xb_ref, lwb_ref, ab_ref, ob_ref))

    def bd(x):
        return jnp.where(bdmask, jnp.concatenate([x] * (gw // c), axis=0), 0.0).astype(BF16)

    def gsum(xs):
        pieces = []
        for x in xs:
            hi = x.astype(BF16)
            r1 = x - hi.astype(F32)
            mid = r1.astype(BF16)
            pieces += [hi, mid, (r1 - mid.astype(F32)).astype(BF16)]
        tot = jnp.dot(jnp.concatenate(pieces, axis=0), ones_bd, preferred_element_type=F32)
        return [tot[3 * i * c:(3 * i + 1) * c] + tot[(3 * i + 1) * c:(3 * i + 2) * c] + tot[(3 * i + 2) * c:(3 * i + 3) * c]
                for i in range(len(xs))]

    def mm(x, ybd):
        return jnp.dot(x.astype(BF16), ybd, preferred_element_type=F32)

    st = []
    for bb, d, g in chains:
        x_ref, lw_ref, a_ref, _ = refs[d]
        sl = slice(g * gw, (g + 1) * gw)
        q = dict(sl=sl, rev=d == 1, r=x_ref[bb, 0, :, sl], k=x_ref[bb, 1, :, sl], v=x_ref[bb, 2, :, sl],
                 lw=lw_ref[bb, 0, :, sl], a=a_ref[bb, 0, :, sl])
        q['kx'] = q['k'] * kk_ref[:, sl]
        q['kt'] = q['k'] * (1.0 + (q['a'] - 1.0) * ka_ref[:, sl])
        st.append(q)
    for q in st:
        q['ss'], q['bon'] = gsum([q['kx'] * q['kx'], q['r'] * q['kt'] * rk_ref[:, q['sl']]])
    for q in st:
        rev = q['rev']
        kk = q['kx'] / jnp.maximum(jnp.sqrt(q['ss']), 1e-12)
        q['b'] = q['a'] * kk
        cw = q['lw']
        sft = 1
        while sft < c:
            if rev:
                cw = cw + jnp.where(row < c - sft, pltpu.roll(cw, c - sft, 0), 0.0)
            else:
                cw = cw + jnp.where(row >= sft, pltpu.roll(cw, sft, 0), 0.0)
            sft *= 2
        q['tot'] = cw[0:1] if rev else cw[c - 1:c]
        invp = jnp.exp(-cw)
        q['e2'] = jnp.exp(q['tot'] - cw)
        q['kr'] = jnp.concatenate([kk * jnp.exp(cw - q['lw']), q['r'] * jnp.exp(cw)], axis=0).astype(BF16)
        q['ktb'] = bd(q['kt'] * invp)
        q['bb'] = bd(q['b'] * invp)
        q['vbd'] = bd(q['v'])
    for q in st:
        strict = (pos > row) if q['rev'] else (pos < row)
        incl = (pos >= row) if q['rev'] else (pos <= row)
        g1 = lax.dot_general(q['kr'], q['ktb'], nt, preferred_element_type=F32)
        g2 = lax.dot_general(q['kr'], q['bb'], nt, preferred_element_type=F32)
        q['m_kt'] = jnp.where(strict, g1[:c], 0.0)
        q['n_kt'] = jnp.where(incl, g1[c:], 0.0)
        q['x'] = jnp.where(strict, g2[:c], 0.0)
        q['n_b'] = jnp.where(incl, g2[c:], 0.0)
        q['t'] = eye - q['x']
        q['xb'] = bd(q['x'])
    for q in st:
        mn = mm(jnp.concatenate([q['m_kt'], q['n_kt']], axis=0), q['vbd'])
        q['mkv'], q['nkv'] = mn[:c], mn[c:]
    lv = 2
    while lv < c:
        for q in st:
            xt = mm(jnp.concatenate([q['x'], q['t']], axis=0), q['xb'])
            q['x'], q['tx'] = xt[:c], xt[c:]
        for q in st:
            q['t'] = q['t'] + mm(q['tx'], q['xb'])
        lv *= 2
        if lv < c:
            for q in st:
                q['xb'] = bd(q['x'])
    for (bb, d, g), q in zip(chains, st):
        q['s0'] = s_ref[bb, d, g]
        q['ka'] = lax.dot_general(q['kr'], q['s0'].astype(BF16), nt, preferred_element_type=F32)
    for q in st:
        q['u'] = mm(q['t'], bd(q['ka'][:c] + q['mkv']))
    for (bb, d, g), q in zip(chains, st):
        e2 = q['e2']
        upd = lax.dot_general(jnp.concatenate([q['v'], q['u']], axis=0).astype(BF16),
                              jnp.concatenate([q['kt'] * e2, -(q['b'] * e2)], axis=0).astype(BF16),
                              tn, preferred_element_type=F32)
        s_ref[bb, d, g] = q['s0'] * jnp.exp(q['tot']) + jnp.where(bdmask, upd, 0.0)
    for q in st:
        q['y'] = q['ka'][c:] + q['nkv'] - mm(q['n_b'], bd(q['u']))
    for q in st:
        (mu,) = gsum([q['y']])
        q['dl'] = q['y'] - mu * (1.0 / n)
    for (bb, d, g), q in zip(chains, st):
        (var,) = gsum([q['dl'] * q['dl']])
        sl = q['sl']
        yn = q['dl'] * lax.rsqrt(var * (1.0 / n) + RW_GN_EPS) * gn0_ref[:, sl] + gn1_ref[:, sl]
        refs[d][3][bb, :, sl] = yn + q['bon'] * q['v']


def _rw_chunk(streams, nc, kk, ka, rk, gn):
    b, _, lt, w = streams.shape
    c = RW_CHUNK
    ncb, nb = nc // c, lt // c
    nbt = RW_SAMPLES if b % RW_SAMPLES == 0 else 1

    def rblk(j):
        return jnp.where(j < ncb, ncb - 1 - j, nb - 1 - (j - ncb))

    par = pl.BlockSpec((1, w), lambda bi, j: (0, 0))
    shp = jax.ShapeDtypeStruct((b, lt, w), F32)
    return pl.pallas_call(
        functools.partial(_rw_chunk_kernel, c=c, w=w, nbt=nbt),
        out_shape=[shp, shp],
        grid=(b // nbt, nb),
        in_specs=[pl.BlockSpec((nbt, 3, c, w), lambda bi, j: (bi, 0, j, 0)),
                  pl.BlockSpec((nbt, 1, c, w), lambda bi, j: (bi, 3, j, 0)),
                  pl.BlockSpec((nbt, 1, c, w), lambda bi, j: (bi, 5, j, 0)),
                  pl.BlockSpec((nbt, 3, c, w), lambda bi, j: (bi, 0, rblk(j), 0)),
                  pl.BlockSpec((nbt, 1, c, w), lambda bi, j: (bi, 4, rblk(j), 0)),
                  pl.BlockSpec((nbt, 1, c, w), lambda bi, j: (bi, 6, rblk(j), 0)),
                  par, par, par, par, par],
        out_specs=[pl.BlockSpec((nbt, c, w), lambda bi, j: (bi, j, 0)),
                   pl.BlockSpec((nbt, c, w), lambda bi, j: (bi, rblk(j), 0))],
        scratch_shapes=[pltpu.VMEM((nbt, 2, w // RW_GROUP, RW_GROUP, RW_GROUP), F32)],
        compiler_params=_cparams(("arbitrary", "arbitrary"), 32),
        name="rwkv_chunk",
    )(streams, streams, streams, streams, streams, streams, kk.reshape(1, w), ka.reshape(1, w), rk.reshape(1, w),
      gn[0].reshape(1, w), gn[1].reshape(1, w))


def _proj_rope_kernel(a_ref, w_ref, c_ref, s_ref, o_ref):
    j = pl.program_id(1)
    acc = jnp.dot(a_ref[...], w_ref[...], preferred_element_type=F32)

    @pl.when(j < 2)
    def _():
        lane = lax.broadcasted_iota(jnp.int32, (1, LANES), 1)
        first_half = (lane % (2 * ROPE_AXIS_FREQS)) < ROPE_AXIS_FREQS
        c = c_ref[...] * jnp.where(j == 0, DA_QK_DIM ** -0.5 * math.log2(math.e), 1.0)
        s = s_ref[...] * jnp.where(j == 0, DA_QK_DIM ** -0.5 * math.log2(math.e), 1.0)
        for g in range(acc.shape[1] // LANES):
            xs = acc[:, g * LANES:(g + 1) * LANES]
            up = pltpu.roll(xs, LANES - ROPE_AXIS_FREQS, 1)
            dn = pltpu.roll(xs, ROPE_AXIS_FREQS, 1)
            o_ref[:, g * LANES:(g + 1) * LANES] = (xs * c + jnp.where(first_half, up, dn) * s).astype(o_ref.dtype)

    @pl.when(j >= 2)
    def _():
        o_ref[...] = acc.astype(o_ref.dtype)


def _proj_rope(a, w, ct, st, lt):
    m, k = a.shape
    n = w.shape[1]
    tn = n // 3
    tm = _pick(lt, (768, 640, 512, 384, 256, 128))
    per = lt // tm
    return pl.pallas_call(
        _proj_rope_kernel,
        out_shape=jax.ShapeDtypeStruct((m, n), BF16),
        grid=(m // tm, 3),
        in_specs=[pl.BlockSpec((tm, k), lambda i, j: (i, 0)),
                  pl.BlockSpec((k, tn), lambda i, j: (0, j)),
                  pl.BlockSpec((tm, LANES), lambda i, j: (i % per, 0)),
                  pl.BlockSpec((tm, LANES), lambda i, j: (i % per, 0))],
        out_specs=pl.BlockSpec((tm, tn), lambda i, j: (i, j)),
        compiler_params=_cparams(("arbitrary", "arbitrary"), 48),
        name="proj_rope",
    )(a, w, ct, st)


def _rope_tables(seq, nc):
    rows = seq // GRID_W
    row = jnp.broadcast_to(jnp.arange(rows)[:, None], (rows, GRID_W)).reshape(seq)
    col = jnp.broadcast_to(jnp.arange(GRID_W)[None, :], (rows, GRID_W)).reshape(seq)
    inv = 1.0 / (ROPE_THETA ** (jnp.arange(ROPE_AXIS_FREQS, dtype=F32) / ROPE_AXIS_FREQS))
    ang = jnp.stack([row, col], axis=-1).astype(F32)[:, :, None] * inv
    cos, sin = jnp.cos(ang), jnp.sin(ang)
    c64 = jnp.concatenate([cos[:, 0], cos[:, 0], cos[:, 1], cos[:, 1]], axis=-1)
    s64 = jnp.concatenate([-sin[:, 0], sin[:, 0], -sin[:, 1], sin[:, 1]], axis=-1)
    ct = jnp.concatenate([jnp.ones((nc, LANES), F32), jnp.tile(c64, (1, 2))], axis=0)
    st = jnp.concatenate([jnp.zeros((nc, LANES), F32), jnp.tile(s64, (1, 2))], axis=0)
    return ct, st


def _attn_kernel(q_ref, k_ref, v_ref, lam_ref, g_ref, o_ref, *, nct, nc, lam_init, hp, qoff):
    i = pl.program_id(2) + qoff
    lane = lax.broadcasted_iota(jnp.int32, (1, LANES), 1)
    lp = lam_ref[...]
    lam = (jnp.exp(jnp.sum(lp[0:1] * lp[1:2], axis=1, keepdims=True))
           - jnp.exp(jnp.sum(lp[2:3] * lp[3:4], axis=1, keepdims=True)) + lam_init)

    def attend(nk):
        for h in range(hp):
            cols = slice(h * DA_V_DIM, (h + 1) * DA_V_DIM)
            q = q_ref[0, :, cols]
            k = k_ref[0, 0:nk, cols]
            v1 = jnp.concatenate([v_ref[0, 0:nk, cols], jnp.ones((nk, LANES), BF16)], axis=1)

            def one(sel):
                qs = jnp.where(sel, q, jnp.zeros_like(q))
                s = lax.dot_general(qs, k, (((1,), (1,)), ((), ())), preferred_element_type=F32)
                e = jnp.exp2(s - jnp.max(s, axis=-1, keepdims=True))
                ov = jnp.dot(e.astype(BF16), v1, preferred_element_type=F32)
                return ov[:, :DA_V_DIM] / ov[:, DA_V_DIM:]

            o = one(lane < DA_QK_DIM) - lam * one(lane >= DA_QK_DIM)
            on = o * lax.rsqrt(jnp.mean(o * o, axis=-1, keepdims=True) + DA_SUBLN_EPS) * g_ref[...]
            o_ref[0, :, cols] = (on * (1.0 - lam_init)).astype(o_ref.dtype)

    @pl.when(i < nct)
    def _():
        attend(nc)

    @pl.when(i >= nct)
    def _():
        attend(k_ref.shape[1])


def _attention(zbr, nc, lam_p, subln, lam_init, with_ctx, hp=4):
    b, lt, cols = zbr.shape
    w = cols // 3
    heads = w // DA_V_DIM
    tq = min(256, nc)
    nct = nc // tq
    hg = heads // hp
    bw = hp * DA_V_DIM
    qoff = 0 if with_ctx else nct
    kern = functools.partial(_attn_kernel, nct=nct, nc=nc, lam_init=lam_init, hp=hp, qoff=qoff)
    return pl.pallas_call(
        kern,
        out_shape=jax.ShapeDtypeStruct((b, lt - qoff * tq, w), BF16),
        grid=(b, hg, lt // tq - qoff),
        in_specs=[pl.BlockSpec((1, tq, bw), lambda bi, h, i: (bi, i + qoff, h)),
                  pl.BlockSpec((1, lt, bw), lambda bi, h, i: (bi, 0, hg + h)),
                  pl.BlockSpec((1, lt, bw), lambda bi, h, i: (bi, 0, 2 * hg + h)),
                  pl.BlockSpec((4, DA_QK_DIM), lambda bi, h, i: (0, 0)),
                  pl.BlockSpec((1, DA_V_DIM), lambda bi, h, i: (0, 0))],
        out_specs=pl.BlockSpec((1, tq, bw), lambda bi, h, i: (bi, i, h)),
        compiler_params=_cparams(("arbitrary", "arbitrary", "arbitrary"), 58),
        name="diff_attn",
    )(zbr, zbr, zbr, lam_p, subln.reshape(1, DA_V_DIM))


def _seg_masks(c, e):
    row = lax.broadcasted_iota(jnp.int32, (c, e), 0)
    mk = {}
    h = 1
    while h < min(SUBLANES, c):
        odd = (row % (2 * h)) >= h
        for s in range(1, h + 1):
            mk[(h, s, 0)] = jnp.where(jnp.logical_and(odd, (row % h) == s - 1), 1.0, 0.0)
            mk[(h, s, 1)] = jnp.where(jnp.logical_and(~odd, (row % h) == h - s), 1.0, 0.0)
        h *= 2
    return mk


def _seg_scans(lf, mk, c):
    cs, ss = {1: lf}, {1: lf}
    h = 1
    while h < c:
        x, y = cs[h], ss[h]
        if h < SUBLANES:
            x3 = x.reshape(c // SUBLANES, SUBLANES, x.shape[1])
            y3 = y.reshape(c // SUBLANES, SUBLANES, y.shape[1])
            for s in range(1, h + 1):
                x = x + mk[(h, s, 0)] * pltpu.roll(x3, s, 1).reshape(x.shape)
                y = y + mk[(h, s, 1)] * pltpu.roll(y3, SUBLANES - s, 1).reshape(y.shape)
            cs[2 * h], ss[2 * h] = x, y
        else:
            px = [x[j * h:(j + 1) * h] for j in range(c // h)]
            py = [y[j * h:(j + 1) * h] for j in range(c // h)]
            nx = [px[j] + px[j - 1][h - 1:h] if j % 2 == 1 else px[j] for j in range(c // h)]
            ny = [py[j] + py[j + 1][0:1] if j % 2 == 0 else py[j] for j in range(c // h)]
            cs[2 * h], ss[2 * h] = jnp.concatenate(nx, axis=0), jnp.concatenate(ny, axis=0)
        h *= 2
    return cs, ss


def _hg_kernel(q_ref, f_ref, v_ref, lb_ref, o_ref, st_ref, *, tc, c, heads, rev):
    j = pl.program_id(1)

    @pl.when(j == 0)
    def _():
        st_ref[...] = jnp.zeros_like(st_ref)

    ti = lax.broadcasted_iota(jnp.int32, (c, c), 0)
    si = lax.broadcasted_iota(jnp.int32, (c, c), 1)
    nt = (((1,), (1,)), ((), ()))
    tn = (((0,), (0,)), ((), ()))
    mk = _seg_masks(c, HG_EXPAND)
    masks, hh = {}, 1
    while hh < c:
        tb, sb = ti // hh, si // hh
        masks[hh] = (jnp.logical_and(tb % 2 == 0, sb == tb + 1) if rev
                     else jnp.logical_and(tb % 2 == 1, sb == tb - 1))
        hh *= 2

    def head(h, carry):
        cols = pl.ds(pl.multiple_of(h * HG_EXPAND, HG_EXPAND), HG_EXPAND)
        lb = lb_ref[:, cols]
        chunks = range(tc // c)
        for ci in (reversed(chunks) if rev else chunks):
            rows = slice(ci * c, (ci + 1) * c)
            q = _silu(q_ref[0, rows, cols])
            f = lb + (1.0 - lb) * _sigmoid(f_ref[0, rows, cols])
            k = 1.0 - f
            lf = jnp.log(f)
            v = v_ref[0, rows, cols].astype(BF16)
            cs, ss = _seg_scans(lf, mk, c)
            qs, ks = (ss, cs) if rev else (cs, ss)
            att = jnp.where(ti == si, lax.dot_general(q.astype(BF16), k.astype(BF16), nt,
                                                      preferred_element_type=F32), 0.0)
            hh = 1
            while hh < c:
                qe = (q * jnp.exp(qs[hh])).astype(BF16)
                ke = (k * jnp.exp(ks[hh] - lf)).astype(BF16)
                a = lax.dot_general(qe, ke, nt, preferred_element_type=F32)
                att = att + jnp.where(masks[hh], a, 0.0)
                hh *= 2
            st = st_ref[h]
            qe = (q * jnp.exp(qs[c])).astype(BF16)
            o = lax.dot_general(qe, st.astype(BF16), nt, preferred_element_type=F32)
            o = o + jnp.dot(att.astype(BF16), v, preferred_element_type=F32)
            o_ref[0, rows, cols] = o
            ke = (k * jnp.exp(ks[c] - lf)).astype(BF16)
            tot = qs[c][0:1] if rev else qs[c][c - 1:c]
            st_ref[h] = st * jnp.exp(tot) + lax.dot_general(v, ke, tn, preferred_element_type=F32)
        return carry

    lax.fori_loop(0, heads, head, 0, unroll=4)


def _hgrn(zc, nc, lb, rev):
    b, lt, cols = zc.shape
    w = cols // 5
    heads = w // HG_EXPAND
    tc = min(256, nc)
    c = min(HG_CHUNK, tc)
    ncb, nb = nc // tc, lt // tc

    def blk(j):
        return jnp.where(j < ncb, ncb - 1 - j, nb - 1 - (j - ncb)) if rev else j

    fcol = 2 if rev else 1
    return pl.pallas_call(
        functools.partial(_hg_kernel, tc=tc, c=c, heads=heads, rev=rev),
        out_shape=jax.ShapeDtypeStruct((b, lt, w), F32),
        grid=(b, nb),
        in_specs=[pl.BlockSpec((1, tc, w), lambda bi, j: (bi, blk(j), 0)),
                  pl.BlockSpec((1, tc, w), lambda bi, j: (bi, blk(j), fcol)),
                  pl.BlockSpec((1, tc, w), lambda bi, j: (bi, blk(j), 3)),
                  pl.BlockSpec((1, w), lambda bi, j: (0, 0))],
        out_specs=pl.BlockSpec((1, tc, w), lambda bi, j: (bi, blk(j), 0)),
        scratch_shapes=[pltpu.VMEM((heads, HG_EXPAND, HG_EXPAND), F32)],
        compiler_params=_cparams(("arbitrary", "arbitrary"), 32),
        name="hgrn2_bwd" if rev else "hgrn2_fwd",
    )(zc, zc, zc, lb.reshape(1, w))


def _merge_kernel(oaf_ref, oab_ref, ga_ref, yb_ref, of_ref, ob_ref, gc_ref, hn_ref, gl0_ref, gl1_ref, gl2_ref, wb_ref,
                  o_ref, *, heads):
    gls = (gl0_ref, gl1_ref, gl2_ref)
    ya = (oaf_ref[0] + oab_ref[0]) * ga_ref[0]
    oc = of_ref[0] + ob_ref[0]
    parts = []
    for h in range(heads):
        x = oc[:, h * HG_EXPAND:(h + 1) * HG_EXPAND]
        parts.append(x * lax.rsqrt(jnp.mean(x * x, axis=-1, keepdims=True) + NORM_EPS))
    yc = jnp.concatenate(parts, axis=-1) * hn_ref[...] * _silu(gc_ref[0])
    acc = None
    for bi, y in enumerate((ya.astype(BF16), yb_ref[0], yc.astype(BF16))):
        term = _sigmoid_t(gls[bi][0].astype(F32)) * jnp.dot(y, wb_ref[bi], preferred_element_type=F32)
        acc = term if acc is None else acc + term
    o_ref[0] = acc.astype(BF16)


def _merge(oaf, oab, ga, yb, of, ob, zc, hn, gl, wbr, row0, tl=128, tn=2048):
    b, lt, w = ga.shape
    d = wbr.shape[2]
    tl = _pick(math.gcd(lt, row0), (tl, 64))
    r0 = row0 // tl
    heads = w // HG_EXPAND
    nn = d // tn
    row = lambda n, bi, i: (bi, i + r0, 0)
    glspec = lambda br: pl.BlockSpec((1, tl, tn), lambda n, bi, i: (bi, i + r0, br * nn + n))
    return pl.pallas_call(
        functools.partial(_merge_kernel, heads=heads),
        out_shape=jax.ShapeDtypeStruct((b, lt - row0, d), BF16),
        grid=(d // tn, b, lt // tl - r0),
        in_specs=[pl.BlockSpec((1, tl, w), row),
                  pl.BlockSpec((1, tl, w), row),
                  pl.BlockSpec((1, tl, w), row),
                  pl.BlockSpec((1, tl, w), lambda n, bi, i: (bi, i, 0)),
                  pl.BlockSpec((1, tl, w), row),
                  pl.BlockSpec((1, tl, w), row),
                  pl.BlockSpec((1, tl, w), lambda n, bi, i: (bi, i + r0, 4)),
                  pl.BlockSpec((1, w), lambda n, bi, i: (0, 0)),
                  glspec(0), glspec(1), glspec(2),
                  pl.BlockSpec((3, w, tn), lambda n, bi, i: (0, 0, n))],
        out_specs=pl.BlockSpec((1, tl, tn), lambda n, bi, i: (bi, i, n)),
        compiler_params=_cparams(("arbitrary", "arbitrary", "arbitrary"), 48),
        name="branch_merge",
    )(oaf, oab, ga, yb, of, ob, zc, hn, gl, gl, gl, wbr)


def _proj_post_kernel(a_ref, w_ref, x_ref, g_ref, gate_ref, o_ref):
    mx = jnp.dot(a_ref[0], w_ref[...], preferred_element_type=F32)
    mn = mx * lax.rsqrt(jnp.mean(mx * mx, axis=-1, keepdims=True) + NORM_EPS) * g_ref[...]
    o_ref[0] = x_ref[0] + gate_ref[0, 0] * mn


def _proj_post(a, wout, xa, g, gsel, nc, row0, tl=256):
    b, lt, d = xa.shape
    tl = min(tl, nc)
    nct = nc // tl
    r0 = row0 // tl
    return pl.pallas_call(
        _proj_post_kernel,
        out_shape=jax.ShapeDtypeStruct((b, lt, d), F32),
        grid=(b, lt // tl - r0),
        in_specs=[pl.BlockSpec((1, tl, d), lambda bi, i: (bi, i, 0)),
                  pl.BlockSpec((d, d), lambda bi, i: (0, 0)),
                  pl.BlockSpec((1, tl, d), lambda bi, i: (bi, i + r0, 0)),
                  pl.BlockSpec((1, d), lambda bi, i: (0, 0)),
                  pl.BlockSpec((1, 1, 1, d), lambda bi, i: (bi, jnp.where(i + r0 >= nct, 1, 0), 0, 0))],
        out_specs=pl.BlockSpec((1, tl, d), lambda bi, i: (bi, i + r0, 0)),
        input_output_aliases={2: 0},
        compiler_params=_cparams(("arbitrary", "arbitrary"), 48),
        name="out_proj_post",
    )(a, wout, xa, g.reshape(1, d), gsel)


def _rank_kernel(afft_ref, slot_ref, *, n, cap, tw):
    aff = afft_ref[0]
    bits = lax.bitcast_convert_type(aff, jnp.int32)
    e_num = aff.shape[0]

    def count(mask):
        return jnp.sum(jnp.where(mask, 1.0, 0.0), axis=1, keepdims=True)

    def bit_step(i, thr):
        cand = thr | lax.shift_left(jnp.int32(1), 30 - i)
        return jnp.where(count(bits >= cand) >= cap, cand, thr)

    thr = lax.fori_loop(0, 31, bit_step, jnp.zeros((e_num, 1), jnp.int32))
    gt = bits > thr
    eq = bits == thr
    need = cap - count(gt)
    tri = jnp.where(lax.broadcasted_iota(jnp.int32, (tw, tw), 0) < lax.broadcasted_iota(jnp.int32, (tw, tw), 1),
                    1.0, 0.0).astype(BF16)

    def prefix(mask):
        parts, carry = [], jnp.zeros((e_num, 1), F32)
        for j in range(n // tw):
            m = jnp.where(mask[:, j * tw:(j + 1) * tw], 1.0, 0.0)
            parts.append(jnp.dot(m.astype(BF16), tri, preferred_element_type=F32) + carry)
            carry = carry + jnp.sum(m, axis=1, keepdims=True)
        return jnp.concatenate(parts, axis=1)

    sel = jnp.logical_or(gt, jnp.logical_and(eq, prefix(eq) < need))
    slot_ref[0] = jnp.where(sel, prefix(sel), float(cap)).astype(jnp.int32)


def _rank(afft, cap):
    b, e_num, n = afft.shape
    tw = min(LANES, n)
    return pl.pallas_call(
        functools.partial(_rank_kernel, n=n, cap=cap, tw=tw),
        out_shape=jax.ShapeDtypeStruct((b, e_num, n), jnp.int32),
        grid=(b,),
        in_specs=[pl.BlockSpec((1, e_num, n), lambda bi: (bi, 0, 0))],
        out_specs=pl.BlockSpec((1, e_num, n), lambda bi: (bi, 0, 0)),
        compiler_params=_cparams(("arbitrary",), 32),
        name="ec_rank",
    )(afft)


def _gather_kernel(slot_ref, afft_ref, h_ref, o_ref, g_ref, *, cap):
    e = pl.program_id(1)
    sl = slot_ref[0, pl.ds(e, 1), :]
    hit = lax.broadcasted_iota(jnp.int32, (cap, 1), 0) == sl
    o_ref[0] = jnp.dot(jnp.where(hit, 1.0, 0.0).astype(BF16), h_ref[0], preferred_element_type=F32).astype(BF16)
    gate = jnp.sum(jnp.where(hit, afft_ref[0, pl.ds(e, 1), :], 0.0), axis=1, keepdims=True)
    g_ref[0] = jnp.broadcast_to(gate, (cap, LANES))


def _gather(slot, afft, hs, cap):
    b, e_num, n = slot.shape
    d = hs.shape[2]
    return pl.pallas_call(
        functools.partial(_gather_kernel, cap=cap),
        out_shape=[jax.ShapeDtypeStruct((e_num, b * cap, d), BF16),
                   jax.ShapeDtypeStruct((e_num, b * cap, LANES), F32)],
        grid=(b, e_num),
        in_specs=[pl.BlockSpec((1, e_num, n), lambda bi, e: (bi, 0, 0)),
                  pl.BlockSpec((1, e_num, n), lambda bi, e: (bi, 0, 0)),
                  pl.BlockSpec((1, n, d), lambda bi, e: (bi, 0, 0))],
        out_specs=[pl.BlockSpec((1, cap, d), lambda bi, e: (e, bi, 0)),
                   pl.BlockSpec((1, cap, LANES), lambda bi, e: (e, bi, 0))],
        compiler_params=_cparams(("arbitrary", "arbitrary"), 40),
        name="ec_gather",
    )(slot, afft, hs)


def _ffn_kernel(x_ref, g_ref, w1_ref, w3_ref, w2_ref, o_ref, acc_ref):
    f = pl.program_id(2)
    x = x_ref[0]
    h1 = jnp.dot(x, w1_ref[0, 0].astype(BF16), preferred_element_type=F32)
    h3 = jnp.dot(x, w3_ref[0, 0].astype(BF16), preferred_element_type=F32)
    hid = (_silu(h1) * h3).astype(BF16)

    @pl.when(f == 0)
    def _():
        acc_ref[...] = jnp.zeros_like(acc_ref)

    d = acc_ref.shape[1]
    for n0 in range(0, d, FFN_OUT_CHUNK):
        cols = slice(n0, n0 + FFN_OUT_CHUNK)
        acc_ref[:, cols] += jnp.dot(hid, w2_ref[0, 0, :, cols].astype(BF16), preferred_element_type=F32)

    @pl.when(f == pl.num_programs(2) - 1)
    def _():
        o_ref[0] = (acc_ref[...] * g_ref[0, :, 0:1]).astype(BF16)


def _ffn(xe, gate, w1, w3, w2, layer, tf=512):
    e_num, m, d = xe.shape
    ff = w1.shape[3]
    tm = _pick(m, (1152, 1024, 768, 512, 384, 256, 128, 64, 32, 16, 8))
    once = pl.Buffered(1)
    return pl.pallas_call(
        _ffn_kernel,
        out_shape=jax.ShapeDtypeStruct((e_num, m, d), BF16),
        grid=(e_num, m // tm, ff // tf),
        in_specs=[pl.BlockSpec((1, tm, d), lambda e, i, f: (e, i, 0)),
                  pl.BlockSpec((1, tm, LANES), lambda e, i, f: (e, i, 0)),
                  pl.BlockSpec((1, 1, d, tf), lambda e, i, f: (layer, e, 0, f)),
                  pl.BlockSpec((1, 1, d, tf), lambda e, i, f: (layer, e, 0, f)),
                  pl.BlockSpec((1, 1, tf, d), lambda e, i, f: (layer, e, f, 0))],
        out_specs=pl.BlockSpec((1, tm, d), lambda e, i, f: (e, i, 0), pipeline_mode=once),
        scratch_shapes=[pltpu.VMEM((tm, d), F32)],
        compiler_params=_cparams(("arbitrary", "arbitrary", "arbitrary"), 58),
        name="ec_ffn",
    )(xe, gate, w1, w3, w2)


def _combine_kernel(slot_ref, ye_ref, x_ref, g_ref, gate_ref, o_ref, *, cap, e_num):
    lane = lax.broadcasted_iota(jnp.int32, (1, e_num), 1)
    pos = lax.broadcasted_iota(jnp.int32, (1, cap), 1)
    sl = slot_ref[0]
    y = None
    for e in range(e_num):
        se = jnp.sum(jnp.where(lane == e, sl, 0), axis=1, keepdims=True)
        pt = jnp.where(se == pos, 1.0, 0.0).astype(BF16)
        part = jnp.dot(pt, ye_ref[e], preferred_element_type=F32)
        y = part if y is None else y + part
    yn = y * lax.rsqrt(jnp.mean(y * y, axis=-1, keepdims=True) + NORM_EPS) * g_ref[...]
    o_ref[0] = x_ref[0] + gate_ref[0, 0] * yn


def _combine(slot_c, ye, xa, g, gsel, seg, row0, slot0, cap, final, tt=256):
    b, n, e_num = slot_c.shape
    d = xa.shape[2]
    tt = _pick(math.gcd(n, row0), (tt, 128, 64))
    assert row0 % tt == 0 and slot0 % cap == 0
    r0, s0 = row0 // tt, slot0 // cap
    return pl.pallas_call(
        functools.partial(_combine_kernel, cap=cap, e_num=e_num),
        out_shape=jax.ShapeDtypeStruct((b, n, d) if final else xa.shape, F32),
        grid=(b, n // tt),
        in_specs=[pl.BlockSpec((1, tt, e_num), lambda bi, j: (bi, j, 0)),
                  pl.BlockSpec((e_num, cap, d), lambda bi, j: (0, s0 + bi, 0)),
                  pl.BlockSpec((1, tt, d), lambda bi, j: (bi, r0 + j, 0)),
                  pl.BlockSpec((1, d), lambda bi, j: (0, 0)),
                  pl.BlockSpec((1, 1, 1, d), lambda bi, j: (bi, seg, 0, 0))],
        out_specs=pl.BlockSpec((1, tt, d), lambda bi, j: (bi, (0 if final else r0) + j, 0)),
        input_output_aliases={} if final else {2: 0},
        compiler_params=_cparams(("arbitrary", "arbitrary"), 56),
        name="ec_combine_post",
    )(slot_c, ye, xa, g.reshape(1, d), gsel)


def _sel(mod_c, mod_x, idx):
    b = mod_x.shape[0]
    mc = jnp.broadcast_to(mod_c[jnp.array(idx)][None], (b, len(idx), mod_c.shape[-1]))
    return jnp.stack([mc, mod_x[:, jnp.array(idx)]], axis=1)


def _layer(xa, nc, mod_x, mod_c, p, lam_init, hg_lb, ct, st, with_ctx):
    b, lt, d = xa.shape
    w = d // 2
    a_cols = 3 * w + 2 * RW_DECAY_RANK + 2 * RW_ICL_RANK + RW_GATE_RANK
    col_b = a_cols
    col_c = col_b + 3 * w
    col_g = col_c + 5 * w

    h = _norm_mod(xa, p['norm_g'][0], _sel(mod_c, mod_x, (0, 1)), nc)
    h2 = h.reshape(b * lt, d)
    w_in = p['w_in']
    za = _matmul(h2, w_in[:, :col_b].astype(BF16), F32, tn=a_cols // 3).reshape(b, lt, a_cols)
    zc = _matmul(h2, w_in[:, col_c:col_g].astype(BF16), F32).reshape(b, lt, 5 * w)
    gl = _matmul(h2, w_in[:, col_g:].astype(BF16), BF16).reshape(b, lt, 3 * d)

    streams, ga = _rw_prep(za, nc, p['rw_mu'], p['rw_w0'], p['rw_wB'], p['rw_a0'], p['rw_aB'], p['rw_gB'])
    oaf, oab = _rw_chunk(streams, nc, p['rw_kk'], p['rw_ka'], p['rw_rk'], p['rw_gn'])

    zbr = _proj_rope(h2, w_in[:, col_b:col_c].astype(BF16), ct, st, lt).reshape(b, lt, 3 * w)
    yb = _attention(zbr, nc, p['da_lambda'], p['da_subln'], lam_init, with_ctx)

    of = _hgrn(zc, nc, hg_lb, rev=False)
    ob = _hgrn(zc, nc, hg_lb, rev=True)

    hn = jnp.tile(p['hg_norm'], w // HG_EXPAND).reshape(1, w)
    skip = 0 if with_ctx else nc
    merged = _merge(oaf, oab, ga, yb, of, ob, zc, hn, gl, p['w_branch'].astype(BF16), skip)
    xa = _proj_post(merged, p['w_out'].astype(BF16), xa, p['norm_g'][1], _sel(mod_c, mod_x, (2,)), nc, skip)

    hm, afft = _norm_mod(xa, p['norm_g'][2], _sel(mod_c, mod_x, (3, 4)), nc, router_t=p['moe_router'].T, row0=skip)
    gsel = _sel(mod_c, mod_x, (5,))
    sets = [(nc, lt - nc, 1)] + ([(0, nc, 0)] if with_ctx else [])
    routed, xes, gates = [], [], []
    for row0, nn, seg in sets:
        cap = EC_CAPACITY_FACTOR * nn // N_EXPERTS
        at = afft[:, :, row0 - skip:row0 - skip + nn]
        slot = _rank(at, cap)
        xe_s, gate_s = _gather(slot, at, hm[:, row0 - skip:row0 - skip + nn], cap)
        xes.append(xe_s)
        gates.append(gate_s)
        routed.append((jnp.swapaxes(slot, 1, 2), row0, seg, cap))
    xe = xes[0] if len(xes) == 1 else jnp.concatenate(xes, axis=1)
    gate = gates[0] if len(gates) == 1 else jnp.concatenate(gates, axis=1)
    ye = _ffn(xe, gate, p['moe_w1'], p['moe_w3'], p['moe_w2'], p['layer'])
    slot0 = 0
    for slot_c, row0, seg, cap in routed:
        xa = _combine(slot_c, ye, xa, p['norm_g'][3], gsel, seg, row0, slot0, cap, final=not with_ctx)
        slot0 += b * cap
    return xa


def kernel(x, c, ctx, c_ctx, w_ada, b_ada, norm_g, w_in, rw_mu, rw_w0, rw_wB, rw_a0, rw_aB, rw_gB, rw_kk, rw_ka, rw_rk, rw_gn, da_lambda, da_subln, hg_lb_logits, hg_norm, w_branch, w_out, moe_router, moe_w1, moe_w3, moe_w2):
    b, seq, d = x.shape
    nc = ctx.shape[1]
    depth = w_ada.shape[0]
    ct, st = _rope_tables(seq, nc)
    lb_w = jax.nn.softmax(hg_lb_logits.astype(F32), axis=0)
    hg_lb = jnp.cumsum(lb_w, axis=0) - lb_w[0]
    rows = ((b + 1 + SUBLANES - 1) // SUBLANES) * SUBLANES
    cc = jnp.zeros((rows, d), F32).at[:b].set(c).at[b].set(c_ctx)
    mod = _modulation(cc, w_ada, b_ada)
    xa = jnp.concatenate([ctx, x], axis=1)
    for l in range(depth):
        p = dict(norm_g=norm_g[l], w_in=w_in[l], rw_mu=rw_mu[l], rw_w0=rw_w0[l], rw_wB=rw_wB[l],
                 rw_a0=rw_a0[l], rw_aB=rw_aB[l], rw_gB=rw_gB[l], rw_kk=rw_kk[l], rw_ka=rw_ka[l],
                 rw_rk=rw_rk[l], rw_gn=rw_gn[l], da_lambda=da_lambda[l], da_subln=da_subln[l],
                 hg_norm=hg_norm[l], w_branch=w_branch[l], w_out=w_out[l], moe_router=moe_router[l],
                 moe_w1=moe_w1, moe_w3=moe_w3, moe_w2=moe_w2, layer=l)
        mod_x = mod[l, :b].reshape(b, 6, d)
        mod_c = mod[l, b].reshape(6, d)
        lam_init = 0.8 - 0.6 * math.exp(-0.3 * l)
        xa = _layer(xa, nc, mod_x, mod_c, p, lam_init, hg_lb[l], ct, st, with_ctx=l < depth - 1)
    return xa
```

```python
import functools
import math

import jax
import jax.numpy as jnp
from jax import lax
from jax.experimental import pallas as pl
from jax.experimental.pallas import tpu as pltpu

F32 = jnp.float32
BF16 = jnp.bfloat16
HIGHEST = lax.Precision.HIGHEST

GRID_W = 64
RW_HEAD_DIM = 64
RW_DECAY_RANK = 64
RW_ICL_RANK = 64
RW_GATE_RANK = 128
RW_GN_EPS = 64e-5
DA_QK_DIM = 64
DA_V_DIM = 128
ROPE_THETA = 10000.0
ROPE_AXIS_FREQS = 16
DA_SUBLN_EPS = 1e-5
HG_EXPAND = 128
N_EXPERTS = 16
EC_CAPACITY_FACTOR = 2
NORM_EPS = 1e-6

LANES = 128
SUBLANES = 8
V7X_VMEM_BYTES = 64 * 1024 * 1024
HG_CHUNK = 64
RW_CHUNK = 64
RW_GROUP = 256
RW_SAMPLES = 2
FFN_OUT_CHUNK = 512


def _cparams(sem, vmem_mb):
    limit = int(vmem_mb * 1024 * 1024)
    assert limit < V7X_VMEM_BYTES
    return pltpu.CompilerParams(dimension_semantics=sem, vmem_limit_bytes=limit)


def _sigmoid(x):
    return 1.0 / (1.0 + jnp.exp(-x))


def _sigmoid_t(x):
    return 0.5 * jnp.tanh(0.5 * x) + 0.5


def _silu(x):
    return x * _sigmoid_t(x)


def _mod_kernel(c_ref, w_ref, b_ref, o_ref):
    a = _silu(c_ref[...]).astype(BF16)
    o_ref[0] = jnp.dot(a, w_ref[0].astype(BF16), preferred_element_type=F32) + b_ref[0]


def _modulation(cc, w_ada, b_ada):
    depth, d, n = w_ada.shape
    rows = cc.shape[0]
    tn = 1024
    return pl.pallas_call(
        _mod_kernel,
        out_shape=jax.ShapeDtypeStruct((depth, rows, n), F32),
        grid=(depth, n // tn),
        in_specs=[pl.BlockSpec((rows, d), lambda l, j: (0, 0)),
                  pl.BlockSpec((1, d, tn), lambda l, j: (l, 0, j)),
                  pl.BlockSpec((1, 1, tn), lambda l, j: (l, 0, j))],
        out_specs=pl.BlockSpec((1, rows, tn), lambda l, j: (l, 0, j)),
        compiler_params=_cparams(("arbitrary", "arbitrary"), 40),
        name="adaln_mod",
    )(cc, w_ada, b_ada.reshape(depth, 1, n))


def _norm_mod_kernel(x_ref, g_ref, ms_ref, *rest, with_router):
    x = x_ref[0]
    xn = x * lax.rsqrt(jnp.mean(x * x, axis=-1, keepdims=True) + NORM_EPS) * g_ref[...]
    h = xn * (1.0 + ms_ref[0, 0, 1:2, :]) + ms_ref[0, 0, 0:1, :]
    if with_router:
        rt_ref, h_ref, aff_ref = rest
        h_ref[0] = h.astype(BF16)
        lt = lax.dot_general(rt_ref[...], h, (((1,), (1,)), ((), ())),
                             precision=HIGHEST, preferred_element_type=F32)
        e = jnp.exp(lt - jnp.max(lt, axis=0, keepdims=True))
        aff_ref[0] = e / jnp.sum(e, axis=0, keepdims=True)
    else:
        (h_ref,) = rest
        h_ref[0] = h.astype(BF16)


def _norm_mod(xa, g, msel, nc, router_t=None, row0=0, tl=256):
    b, lt, d = xa.shape
    tl = min(tl, nc)
    nct = nc // tl
    r0 = row0 // tl
    with_router = router_t is not None
    in_specs = [pl.BlockSpec((1, tl, d), lambda bi, i: (bi, i + r0, 0)),
                pl.BlockSpec((1, d), lambda bi, i: (0, 0)),
                pl.BlockSpec((1, 1, 2, d), lambda bi, i: (bi, jnp.where(i + r0 >= nct, 1, 0), 0, 0))]
    args = [xa, g.reshape(1, d), msel]
    out_shape = [jax.ShapeDtypeStruct((b, lt - row0, d), BF16)]
    out_specs = [pl.BlockSpec((1, tl, d), lambda bi, i: (bi, i, 0))]
    if with_router:
        e = router_t.shape[0]
        in_specs.append(pl.BlockSpec((e, d), lambda bi, i: (0, 0)))
        args.append(router_t)
        out_shape.append(jax.ShapeDtypeStruct((b, e, lt - row0), F32))
        out_specs.append(pl.BlockSpec((1, e, tl), lambda bi, i: (bi, 0, i)))
    res = pl.pallas_call(
        functools.partial(_norm_mod_kernel, with_router=with_router),
        out_shape=out_shape, grid=(b, lt // tl - r0), in_specs=in_specs, out_specs=out_specs,
        compiler_params=_cparams(("arbitrary", "arbitrary"), 32),
        name="norm_mod_router" if with_router else "norm_mod",
    )(*args)
    return res if with_router else res[0]


def _matmul_kernel(a_ref, w_ref, o_ref):
    o_ref[...] = jnp.dot(a_ref[...], w_ref[...], preferred_element_type=F32).astype(o_ref.dtype)


def _pick(n, cands):
    for c in cands:
        if n % c == 0:
            return c
    return n


def _matmul(a, w, out_dtype, tm=1024, tn=1024):
    m, k = a.shape
    n = w.shape[1]
    tm = _pick(m, (tm, 768, 512, 384, 256, 128))
    tn = _pick(n, (tn, 768, 512, 384, 256, 128))
    return pl.pallas_call(
        _matmul_kernel,
        out_shape=jax.ShapeDtypeStruct((m, n), out_dtype),
        grid=(m // tm, n // tn),
        in_specs=[pl.BlockSpec((tm, k), lambda i, j: (i, 0)),
                  pl.BlockSpec((k, tn), lambda i, j: (0, j))],
        out_specs=pl.BlockSpec((tm, tn), lambda i, j: (i, j)),
        compiler_params=_cparams(("arbitrary", "arbitrary"), 48),
        name="matmul",
    )(a, w)


def _rw_prep_kernel(z_ref, zp_ref, zn_ref, mu_ref, w0_ref, wb_ref, a0_ref, ab_ref, gb_ref,
                    o_ref, g_ref, *, tl, nct, nt, w):
    i = pl.program_id(1)
    z = z_ref[0]
    row = lax.broadcasted_iota(jnp.int32, (tl, 1), 0)
    seg_start = jnp.logical_or(i == 0, i == nct)
    seg_end = jnp.logical_or(i == nct - 1, i == nt - 1)
    prev_row = jnp.where(seg_start, 0.0, zp_ref[0, SUBLANES - 1:SUBLANES, :])
    next_row = jnp.where(seg_end, 0.0, zn_ref[0, 0:1, :])
    prev = jnp.where(row == 0, prev_row, pltpu.roll(z, 1, 0))
    nxt = jnp.where(row == tl - 1, next_row, pltpu.roll(z, tl - 1, 0))
    zs = z + mu_ref[0:1, :] * (prev - z) + mu_ref[1:2, :] * (nxt - z)
    o_ref[0, 0] = zs[:, 0:w]
    o_ref[0, 1] = zs[:, w:2 * w]
    o_ref[0, 2] = zs[:, 2 * w:3 * w]
    o4 = 3 * w + 2 * RW_DECAY_RANK
    o5 = o4 + 2 * RW_ICL_RANK
    wd = jnp.tanh(zs[:, 3 * w:o4])
    wl = jnp.dot(wd.astype(BF16), wb_ref[...], preferred_element_type=F32) + w0_ref[...]
    dec = -math.exp(-0.5) * _sigmoid_t(wl)
    o_ref[0, 3] = dec[:, 0:w]
    o_ref[0, 4] = dec[:, w:2 * w]
    al = jnp.dot(zs[:, o4:o5].astype(BF16), ab_ref[...], preferred_element_type=F32) + a0_ref[...]
    av = _sigmoid_t(al)
    o_ref[0, 5] = av[:, 0:w]
    o_ref[0, 6] = av[:, w:2 * w]
    g_ref[0] = jnp.dot(_sigmoid_t(zs[:, o5:]).astype(BF16), gb_ref[...], preferred_element_type=F32)


def _blockdiag2(m):
    r, w = m.shape[1], m.shape[2]
    z = jnp.zeros((r, w), m.dtype)
    return jnp.concatenate([jnp.concatenate([m[0], z], axis=1), jnp.concatenate([z, m[1]], axis=1)], axis=0)


def _rw_prep(za, nc, mu, w0, wb, a0, ab, gb, tl=256):
    b, lt, acols = za.shape
    w = w0.shape[1]
    tl = min(tl, nc)
    nct, nt = nc // tl, lt // tl
    r8 = tl // SUBLANES
    nb8 = lt // SUBLANES
    full = lambda shp: pl.BlockSpec(shp, lambda bi, i: tuple(0 for _ in shp))
    kern = functools.partial(_rw_prep_kernel, tl=tl, nct=nct, nt=nt, w=w)
    return pl.pallas_call(
        kern,
        out_shape=[jax.ShapeDtypeStruct((b, 7, lt, w), F32), jax.ShapeDtypeStruct((b, lt, w), F32)],
        grid=(b, nt),
        in_specs=[pl.BlockSpec((1, tl, acols), lambda bi, i: (bi, i, 0)),
                  pl.BlockSpec((1, SUBLANES, acols), lambda bi, i: (bi, jnp.maximum(i * r8 - 1, 0), 0)),
                  pl.BlockSpec((1, SUBLANES, acols), lambda bi, i: (bi, jnp.minimum((i + 1) * r8, nb8 - 1), 0)),
                  full((2, acols)), full((1, 2 * w)), full((2 * RW_DECAY_RANK, 2 * w)),
                  full((1, 2 * w)), full((2 * RW_ICL_RANK, 2 * w)), full((RW_GATE_RANK, w))],
        out_specs=[pl.BlockSpec((1, 7, tl, w), lambda bi, i: (bi, 0, i, 0)),
                   pl.BlockSpec((1, tl, w), lambda bi, i: (bi, i, 0))],
        compiler_params=_cparams(("arbitrary", "arbitrary"), 48),
        name="rwkv_prep",
    )(za, za, za, mu, w0.reshape(1, 2 * w), _blockdiag2(wb).astype(BF16), a0.reshape(1, 2 * w),
      _blockdiag2(ab).astype(BF16), gb.astype(BF16))


def _rw_chunk_kernel(xf_ref, lwf_ref, af_ref, xb_ref, lwb_ref, ab_ref, kk_ref, ka_ref, rk_ref, gn0_ref, gn1_ref,
                     of_ref, ob_ref, s_ref, *, c, w, nbt):
    j = pl.program_id(1)

    @pl.when(j == 0)
    def _():
        s_ref[...] = jnp.zeros_like(s_ref)

    gw, n = RW_GROUP, RW_HEAD_DIM
    ng = w // gw
    row = lax.broadcasted_iota(jnp.int32, (c, 1), 0)
    pos = lax.broadcasted_iota(jnp.int32, (1, gw), 1) % n
    bdmask = (lax.broadcasted_iota(jnp.int32, (gw, 1), 0) // n) == (lax.broadcasted_iota(jnp.int32, (1, gw), 1) // n)
    ones_bd = jnp.where(bdmask, 1.0, 0.0).astype(BF16)
    eye = jnp.where(pos == row, 1.0, 0.0)
    nt = (((1,), (1,)), ((), ()))
    tn = (((0,), (0,)), ((), ()))
    chains = [(bb, d, g) for bb in range(nbt) for d in range(2) for g in range(ng)]
    refs = ((xf_ref, lwf_ref, af_ref, of_ref), (xb_ref, lwb_ref, ab_ref, ob_ref))

    def bd(x):
        return jnp.where(bdmask, jnp.concatenate([x] * (gw // c), axis=0), 0.0).astype(BF16)

    def gsum(xs):
        pieces = []
        for x in xs:
            hi = x.astype(BF16)
            r1 = x - hi.astype(F32)
            mid = r1.astype(BF16)
            pieces += [hi, mid, (r1 - mid.astype(F32)).astype(BF16)]
        tot = jnp.dot(jnp.concatenate(pieces, axis=0), ones_bd, preferred_element_type=F32)
        return [tot[3 * i * c:(3 * i + 1) * c] + tot[(3 * i + 1) * c:(3 * i + 2) * c] + tot[(3 * i + 2) * c:(3 * i + 3) * c]
                for i in range(len(xs))]

    def mm(x, ybd):
        return jnp.dot(x.astype(BF16), ybd, preferred_element_type=F32)

    st = []
    for bb, d, g in chains:
        x_ref, lw_ref, a_ref, _ = refs[d]
        sl = slice(g * gw, (g + 1) * gw)
        q = dict(sl=sl, rev=d == 1, r=x_ref[bb, 0, :, sl], k=x_ref[bb, 1, :, sl], v=x_ref[bb, 2, :, sl],
                 lw=lw_ref[bb, 0, :, sl], a=a_ref[bb, 0, :, sl])
        q['kx'] = q['k'] * kk_ref[:, sl]
        q['kt'] = q['k'] * (1.0 + (q['a'] - 1.0) * ka_ref[:, sl])
        st.append(q)
    for q in st:
        q['ss'], q['bon'] = gsum([q['kx'] * q['kx'], q['r'] * q['kt'] * rk_ref[:, q['sl']]])
    for q in st:
        rev = q['rev']
        kk = q['kx'] / jnp.maximum(jnp.sqrt(q['ss']), 1e-12)
        q['b'] = q['a'] * kk
        cw = q['lw']
        sft = 1
        while sft < c:
            if rev:
                cw = cw + jnp.where(row < c - sft, pltpu.roll(cw, c - sft, 0), 0.0)
            else:
                cw = cw + jnp.where(row >= sft, pltpu.roll(cw, sft, 0), 0.0)
            sft *= 2
        q['tot'] = cw[0:1] if rev else cw[c - 1:c]
        invp = jnp.exp(-cw)
        q['e2'] = jnp.exp(q['tot'] - cw)
        q['kr'] = jnp.concatenate([kk * jnp.exp(cw - q['lw']), q['r'] * jnp.exp(cw)], axis=0).astype(BF16)
        q['ktb'] = bd(q['kt'] * invp)
        q['bb'] = bd(q['b'] * invp)
        q['vbd'] = bd(q['v'])
    for q in st:
        strict = (pos > row) if q['rev'] else (pos < row)
        incl = (pos >= row) if q['rev'] else (pos <= row)
        g1 = lax.dot_general(q['kr'], q['ktb'], nt, preferred_element_type=F32)
        g2 = lax.dot_general(q['kr'], q['bb'], nt, preferred_element_type=F32)
        q['m_kt'] = jnp.where(strict, g1[:c], 0.0)
        q['n_kt'] = jnp.where(incl, g1[c:], 0.0)
        q['x'] = jnp.where(strict, g2[:c], 0.0)
        q['n_b'] = jnp.where(incl, g2[c:], 0.0)
        q['t'] = eye - q['x']
        q['xb'] = bd(q['x'])
    for q in st:
        mn = mm(jnp.concatenate([q['m_kt'], q['n_kt']], axis=0), q['vbd'])
        q['mkv'], q['nkv'] = mn[:c], mn[c:]
    lv = 2
    while lv < c:
        for q in st:
            xt = mm(jnp.concatenate([q['x'], q['t']], axis=0), q['xb'])
            q['x'], q['tx'] = xt[:c], xt[c:]
        for q in st:
            q['t'] = q['t'] + mm(q['tx'], q['xb'])
        lv *= 2
        if lv < c:
            for q in st:
                q['xb'] = bd(q['x'])
    for (bb, d, g), q in zip(chains, st):
        q['s0'] = s_ref[bb, d, g]
        q['ka'] = lax.dot_general(q['kr'], q['s0'].astype(BF16), nt, preferred_element_type=F32)
    for q in st:
        q['u'] = mm(q['t'], bd(q['ka'][:c] + q['mkv']))
    for (bb, d, g), q in zip(chains, st):
        e2 = q['e2']
        upd = lax.dot_general(jnp.concatenate([q['v'], q['u']], axis=0).astype(BF16),
                              jnp.concatenate([q['kt'] * e2, -(q['b'] * e2)], axis=0).astype(BF16),
                              tn, preferred_element_type=F32)
        s_ref[bb, d, g] = q['s0'] * jnp.exp(q['tot']) + jnp.where(bdmask, upd, 0.0)
    for q in st:
        q['y'] = q['ka'][c:] + q['nkv'] - mm(q['n_b'], bd(q['u']))
    for q in st:
        (mu,) = gsum([q['y']])
        q['dl'] = q['y'] - mu * (1.0 / n)
    for (bb, d, g), q in zip(chains, st):
        (var,) = gsum([q['dl'] * q['dl']])
        sl = q['sl']
        yn = q['dl'] * lax.rsqrt(var * (1.0 / n) + RW_GN_EPS) * gn0_ref[:, sl] + gn1_ref[:, sl]
        refs[d][3][bb, :, sl] = yn + q['bon'] * q['v']


def _rw_chunk(streams, nc, kk, ka, rk, gn):
    b, _, lt, w = streams.shape
    c = RW_CHUNK
    ncb, nb = nc // c, lt // c
    nbt = RW_SAMPLES if b % RW_SAMPLES == 0 else 1

    def rblk(j):
        return jnp.where(j < ncb, ncb - 1 - j, nb - 1 - (j - ncb))

    par = pl.BlockSpec((1, w), lambda bi, j: (0, 0))
    shp = jax.ShapeDtypeStruct((b, lt, w), F32)
    return pl.pallas_call(
        functools.partial(_rw_chunk_kernel, c=c, w=w, nbt=nbt),
        out_shape=[shp, shp],
        grid=(b // nbt, nb),
        in_specs=[pl.BlockSpec((nbt, 3, c, w), lambda bi, j: (bi, 0, j, 0)),
                  pl.BlockSpec((nbt, 1, c, w), lambda bi, j: (bi, 3, j, 0)),
                  pl.BlockSpec((nbt, 1, c, w), lambda bi, j: (bi, 5, j, 0)),
                  pl.BlockSpec((nbt, 3, c, w), lambda bi, j: (bi, 0, rblk(j), 0)),
                  pl.BlockSpec((nbt, 1, c, w), lambda bi, j: (bi, 4, rblk(j), 0)),
                  pl.BlockSpec((nbt, 1, c, w), lambda bi, j: (bi, 6, rblk(j), 0)),
                  par, par, par, par, par],
        out_specs=[pl.BlockSpec((nbt, c, w), lambda bi, j: (bi, j, 0)),
                   pl.BlockSpec((nbt, c, w), lambda bi, j: (bi, rblk(j), 0))],
        scratch_shapes=[pltpu.VMEM((nbt, 2, w // RW_GROUP, RW_GROUP, RW_GROUP), F32)],
        compiler_params=_cparams(("arbitrary", "arbitrary"), 32),
        name="rwkv_chunk",
    )(streams, streams, streams, streams, streams, streams, kk.reshape(1, w), ka.reshape(1, w), rk.reshape(1, w),
      gn[0].reshape(1, w), gn[1].reshape(1, w))


def _proj_rope_kernel(a_ref, w_ref, c_ref, s_ref, o_ref):
    j = pl.program_id(1)
    acc = jnp.dot(a_ref[...], w_ref[...], preferred_element_type=F32)

    @pl.when(j < 2)
    def _():
        lane = lax.broadcasted_iota(jnp.int32, (1, LANES), 1)
        first_half = (lane % (2 * ROPE_AXIS_FREQS)) < ROPE_AXIS_FREQS
        c = c_ref[...] * jnp.where(j == 0, DA_QK_DIM ** -0.5 * math.log2(math.e), 1.0)
        s = s_ref[...] * jnp.where(j == 0, DA_QK_DIM ** -0.5 * math.log2(math.e), 1.0)
        for g in range(acc.shape[1] // LANES):
            xs = acc[:, g * LANES:(g + 1) * LANES]
            up = pltpu.roll(xs, LANES - ROPE_AXIS_FREQS, 1)
            dn = pltpu.roll(xs, ROPE_AXIS_FREQS, 1)
            o_ref[:, g * LANES:(g + 1) * LANES] = (xs * c + jnp.where(first_half, up, dn) * s).astype(o_ref.dtype)

    @pl.when(j >= 2)
    def _():
        o_ref[...] = acc.astype(o_ref.dtype)


def _proj_rope(a, w, ct, st, lt):
    m, k = a.shape
    n = w.shape[1]
    tn = n // 3
    tm = _pick(lt, (768, 640, 512, 384, 256, 128))
    per = lt // tm
    return pl.pallas_call(
        _proj_rope_kernel,
        out_shape=jax.ShapeDtypeStruct((m, n), BF16),
        grid=(m // tm, 3),
        in_specs=[pl.BlockSpec((tm, k), lambda i, j: (i, 0)),
                  pl.BlockSpec((k, tn), lambda i, j: (0, j)),
                  pl.BlockSpec((tm, LANES), lambda i, j: (i % per, 0)),
                  pl.BlockSpec((tm, LANES), lambda i, j: (i % per, 0))],
        out_specs=pl.BlockSpec((tm, tn), lambda i, j: (i, j)),
        compiler_params=_cparams(("arbitrary", "arbitrary"), 48),
        name="proj_rope",
    )(a, w, ct, st)


def _rope_tables(seq, nc):
    rows = seq // GRID_W
    row = jnp.broadcast_to(jnp.arange(rows)[:, None], (rows, GRID_W)).reshape(seq)
    col = jnp.broadcast_to(jnp.arange(GRID_W)[None, :], (rows, GRID_W)).reshape(seq)
    inv = 1.0 / (ROPE_THETA ** (jnp.arange(ROPE_AXIS_FREQS, dtype=F32) / ROPE_AXIS_FREQS))
    ang = jnp.stack([row, col], axis=-1).astype(F32)[:, :, None] * inv
    cos, sin = jnp.cos(ang), jnp.sin(ang)
    c64 = jnp.concatenate([cos[:, 0], cos[:, 0], cos[:, 1], cos[:, 1]], axis=-1)
    s64 = jnp.concatenate([-sin[:, 0], sin[:, 0], -sin[:, 1], sin[:, 1]], axis=-1)
    ct = jnp.concatenate([jnp.ones((nc, LANES), F32), jnp.tile(c64, (1, 2))], axis=0)
    st = jnp.concatenate([jnp.zeros((nc, LANES), F32), jnp.tile(s64, (1, 2))], axis=0)
    return ct, st


def _attn_kernel(q_ref, k_ref, v_ref, lam_ref, g_ref, o_ref, *, nct, nc, lam_init, hp, qoff):
    i = pl.program_id(2) + qoff
    lane = lax.broadcasted_iota(jnp.int32, (1, LANES), 1)
    lp = lam_ref[...]
    lam = (jnp.exp(jnp.sum(lp[0:1] * lp[1:2], axis=1, keepdims=True))
           - jnp.exp(jnp.sum(lp[2:3] * lp[3:4], axis=1, keepdims=True)) + lam_init)

    def attend(nk):
        for h in range(hp):
            cols = slice(h * DA_V_DIM, (h + 1) * DA_V_DIM)
            q = q_ref[0, :, cols]
            k = k_ref[0, 0:nk, cols]
            v1 = jnp.concatenate([v_ref[0, 0:nk, cols], jnp.ones((nk, LANES), BF16)], axis=1)

            def one(sel):
                qs = jnp.where(sel, q, jnp.zeros_like(q))
                s = lax.dot_general(qs, k, (((1,), (1,)), ((), ())), preferred_element_type=F32)
                e = jnp.exp2(s - jnp.max(s, axis=-1, keepdims=True))
                ov = jnp.dot(e.astype(BF16), v1, preferred_element_type=F32)
                return ov[:, :DA_V_DIM] / ov[:, DA_V_DIM:]

            o = one(lane < DA_QK_DIM) - lam * one(lane >= DA_QK_DIM)
            on = o * lax.rsqrt(jnp.mean(o * o, axis=-1, keepdims=True) + DA_SUBLN_EPS) * g_ref[...]
            o_ref[0, :, cols] = (on * (1.0 - lam_init)).astype(o_ref.dtype)

    @pl.when(i < nct)
    def _():
        attend(nc)

    @pl.when(i >= nct)
    def _():
        attend(k_ref.shape[1])


def _attention(zbr, nc, lam_p, subln, lam_init, with_ctx, hp=4):
    b, lt, cols = zbr.shape
    w = cols // 3
    heads = w // DA_V_DIM
    tq = min(256, nc)
    nct = nc // tq
    hg = heads // hp
    bw = hp * DA_V_DIM
    qoff = 0 if with_ctx else nct
    kern = functools.partial(_attn_kernel, nct=nct, nc=nc, lam_init=lam_init, hp=hp, qoff=qoff)
    return pl.pallas_call(
        kern,
        out_shape=jax.ShapeDtypeStruct((b, lt - qoff * tq, w), BF16),
        grid=(b, hg, lt // tq - qoff),
        in_specs=[pl.BlockSpec((1, tq, bw), lambda bi, h, i: (bi, i + qoff, h)),
                  pl.BlockSpec((1, lt, bw), lambda bi, h, i: (bi, 0, hg + h)),
                  pl.BlockSpec((1, lt, bw), lambda bi, h, i: (bi, 0, 2 * hg + h)),
                  pl.BlockSpec((4, DA_QK_DIM), lambda bi, h, i: (0, 0)),
                  pl.BlockSpec((1, DA_V_DIM), lambda bi, h, i: (0, 0))],
        out_specs=pl.BlockSpec((1, tq, bw), lambda bi, h, i: (bi, i, h)),
        compiler_params=_cparams(("arbitrary", "arbitrary", "arbitrary"), 58),
        name="diff_attn",
    )(zbr, zbr, zbr, lam_p, subln.reshape(1, DA_V_DIM))


def _seg_masks(c, e):
    row = lax.broadcasted_iota(jnp.int32, (c, e), 0)
    mk = {}
    h = 1
    while h < min(SUBLANES, c):
        odd = (row % (2 * h)) >= h
        for s in range(1, h + 1):
            mk[(h, s, 0)] = jnp.where(jnp.logical_and(odd, (row % h) == s - 1), 1.0, 0.0)
            mk[(h, s, 1)] = jnp.where(jnp.logical_and(~odd, (row % h) == h - s), 1.0, 0.0)
        h *= 2
    return mk


def _seg_scans(lf, mk, c):
    cs, ss = {1: lf}, {1: lf}
    h = 1
    while h < c:
        x, y = cs[h], ss[h]
        if h < SUBLANES:
            x3 = x.reshape(c // SUBLANES, SUBLANES, x.shape[1])
            y3 = y.reshape(c // SUBLANES, SUBLANES, y.shape[1])
            for s in range(1, h + 1):
                x = x + mk[(h, s, 0)] * pltpu.roll(x3, s, 1).reshape(x.shape)
                y = y + mk[(h, s, 1)] * pltpu.roll(y3, SUBLANES - s, 1).reshape(y.shape)
            cs[2 * h], ss[2 * h] = x, y
        else:
            px = [x[j * h:(j + 1) * h] for j in range(c // h)]
            py = [y[j * h:(j + 1) * h] for j in range(c // h)]
            nx = [px[j] + px[j - 1][h - 1:h] if j % 2 == 1 else px[j] for j in range(c // h)]
            ny = [py[j] + py[j + 1][0:1] if j % 2 == 0 else py[j] for j in range(c // h)]
            cs[2 * h], ss[2 * h] = jnp.concatenate(nx, axis=0), jnp.concatenate(ny, axis=0)
        h *= 2
    return cs, ss


def _hg_kernel(q_ref, f_ref, v_ref, lb_ref, o_ref, st_ref, *, tc, c, heads, rev):
    j = pl.program_id(1)

    @pl.when(j == 0)
    def _():
        st_ref[...] = jnp.zeros_like(st_ref)

    ti = lax.broadcasted_iota(jnp.int32, (c, c), 0)
    si = lax.broadcasted_iota(jnp.int32, (c, c), 1)
    nt = (((1,), (1,)), ((), ()))
    tn = (((0,), (0,)), ((), ()))
    mk = _seg_masks(c, HG_EXPAND)
    masks, hh = {}, 1
    while hh < c:
        tb, sb = ti // hh, si // hh
        masks[hh] = (jnp.logical_and(tb % 2 == 0, sb == tb + 1) if rev
                     else jnp.logical_and(tb % 2 == 1, sb == tb - 1))
        hh *= 2

    def head(h, carry):
        cols = pl.ds(pl.multiple_of(h * HG_EXPAND, HG_EXPAND), HG_EXPAND)
        lb = lb_ref[:, cols]
        chunks = range(tc // c)
        for ci in (reversed(chunks) if rev else chunks):
            rows = slice(ci * c, (ci + 1) * c)
            q = _silu(q_ref[0, rows, cols])
            f = lb + (1.0 - lb) * _sigmoid(f_ref[0, rows, cols])
            k = 1.0 - f
            lf = jnp.log(f)
            v = v_ref[0, rows, cols].astype(BF16)
            cs, ss = _seg_scans(lf, mk, c)
            qs, ks = (ss, cs) if rev else (cs, ss)
            att = jnp.where(ti == si, lax.dot_general(q.astype(BF16), k.astype(BF16), nt,
                                                      preferred_element_type=F32), 0.0)
            hh = 1
            while hh < c:
                qe = (q * jnp.exp(qs[hh])).astype(BF16)
                ke = (k * jnp.exp(ks[hh] - lf)).astype(BF16)
                a = lax.dot_general(qe, ke, nt, preferred_element_type=F32)
                att = att + jnp.where(masks[hh], a, 0.0)
                hh *= 2
            st = st_ref[h]
            qe = (q * jnp.exp(qs[c])).astype(BF16)
            o = lax.dot_general(qe, st.astype(BF16), nt, preferred_element_type=F32)
            o = o + jnp.dot(att.astype(BF16), v, preferred_element_type=F32)
            o_ref[0, rows, cols] = o
            ke = (k * jnp.exp(ks[c] - lf)).astype(BF16)
            tot = qs[c][0:1] if rev else qs[c][c - 1:c]
            st_ref[h] = st * jnp.exp(tot) + lax.dot_general(v, ke, tn, preferred_element_type=F32)
        return carry

    lax.fori_loop(0, heads, head, 0, unroll=True)


def _hgrn(zc, nc, lb, rev):
    b, lt, cols = zc.shape
    w = cols // 5
    heads = w // HG_EXPAND
    tc = min(256, nc)
    c = min(HG_CHUNK, tc)
    ncb, nb = nc // tc, lt // tc

    def blk(j):
        return jnp.where(j < ncb, ncb - 1 - j, nb - 1 - (j - ncb)) if rev else j

    fcol = 2 if rev else 1
    return pl.pallas_call(
        functools.partial(_hg_kernel, tc=tc, c=c, heads=heads, rev=rev),
        out_shape=jax.ShapeDtypeStruct((b, lt, w), F32),
        grid=(b, nb),
        in_specs=[pl.BlockSpec((1, tc, w), lambda bi, j: (bi, blk(j), 0)),
                  pl.BlockSpec((1, tc, w), lambda bi, j: (bi, blk(j), fcol)),
                  pl.BlockSpec((1, tc, w), lambda bi, j: (bi, blk(j), 3)),
                  pl.BlockSpec((1, w), lambda bi, j: (0, 0))],
        out_specs=pl.BlockSpec((1, tc, w), lambda bi, j: (bi, blk(j), 0)),
        scratch_shapes=[pltpu.VMEM((heads, HG_EXPAND, HG_EXPAND), F32)],
        compiler_params=_cparams(("arbitrary", "arbitrary"), 32),
        name="hgrn2_bwd" if rev else "hgrn2_fwd",
    )(zc, zc, zc, lb.reshape(1, w))


def _merge_kernel(oaf_ref, oab_ref, ga_ref, yb_ref, of_ref, ob_ref, gc_ref, hn_ref, gl0_ref, gl1_ref, gl2_ref, wb_ref,
                  o_ref, *, heads):
    gls = (gl0_ref, gl1_ref, gl2_ref)
    ya = (oaf_ref[0] + oab_ref[0]) * ga_ref[0]
    oc = of_ref[0] + ob_ref[0]
    parts = []
    for h in range(heads):
        x = oc[:, h * HG_EXPAND:(h + 1) * HG_EXPAND]
        parts.append(x * lax.rsqrt(jnp.mean(x * x, axis=-1, keepdims=True) + NORM_EPS))
    yc = jnp.concatenate(parts, axis=-1) * hn_ref[...] * _silu(gc_ref[0])
    acc = None
    for bi, y in enumerate((ya.astype(BF16), yb_ref[0], yc.astype(BF16))):
        term = _sigmoid_t(gls[bi][0].astype(F32)) * jnp.dot(y, wb_ref[bi], preferred_element_type=F32)
        acc = term if acc is None else acc + term
    o_ref[0] = acc.astype(BF16)


def _merge(oaf, oab, ga, yb, of, ob, zc, hn, gl, wbr, row0, tl=256, tn=2048):
    b, lt, w = ga.shape
    d = wbr.shape[2]
    tl = _pick(math.gcd(lt, row0), (tl, 64))
    r0 = row0 // tl
    heads = w // HG_EXPAND
    nn = d // tn
    row = lambda n, bi, i: (bi, i + r0, 0)
    glspec = lambda br: pl.BlockSpec((1, tl, tn), lambda n, bi, i: (bi, i + r0, br * nn + n))
    return pl.pallas_call(
        functools.partial(_merge_kernel, heads=heads),
        out_shape=jax.ShapeDtypeStruct((b, lt - row0, d), BF16),
        grid=(d // tn, b, lt // tl - r0),
        in_specs=[pl.BlockSpec((1, tl, w), row),
                  pl.BlockSpec((1, tl, w), row),
                  pl.BlockSpec((1, tl, w), row),
                  pl.BlockSpec((1, tl, w), lambda n, bi, i: (bi, i, 0)),
                  pl.BlockSpec((1, tl, w), row),
                  pl.BlockSpec((1, tl, w), row),
                  pl.BlockSpec((1, tl, w), lambda n, bi, i: (bi, i + r0, 4)),
                  pl.BlockSpec((1, w), lambda n, bi, i: (0, 0)),
                  glspec(0), glspec(1), glspec(2),
                  pl.BlockSpec((3, w, tn), lambda n, bi, i: (0, 0, n))],
        out_specs=pl.BlockSpec((1, tl, tn), lambda n, bi, i: (bi, i, n)),
        compiler_params=_cparams(("arbitrary", "arbitrary", "arbitrary"), 58),
        name="branch_merge",
    )(oaf, oab, ga, yb, of, ob, zc, hn, gl, gl, gl, wbr)


def _proj_post_kernel(a_ref, w_ref, x_ref, g_ref, gate_ref, o_ref):
    mx = jnp.dot(a_ref[0], w_ref[...], preferred_element_type=F32)
    mn = mx * lax.rsqrt(jnp.mean(mx * mx, axis=-1, keepdims=True) + NORM_EPS) * g_ref[...]
    o_ref[0] = x_ref[0] + gate_ref[0, 0] * mn


def _proj_post(a, wout, xa, g, gsel, nc, row0, tl=256):
    b, lt, d = xa.shape
    tl = min(tl, nc)
    nct = nc // tl
    r0 = row0 // tl
    return pl.pallas_call(
        _proj_post_kernel,
        out_shape=jax.ShapeDtypeStruct((b, lt, d), F32),
        grid=(b, lt // tl - r0),
        in_specs=[pl.BlockSpec((1, tl, d), lambda bi, i: (bi, i, 0)),
                  pl.BlockSpec((d, d), lambda bi, i: (0, 0)),
                  pl.BlockSpec((1, tl, d), lambda bi, i: (bi, i + r0, 0)),
                  pl.BlockSpec((1, d), lambda bi, i: (0, 0)),
                  pl.BlockSpec((1, 1, 1, d), lambda bi, i: (bi, jnp.where(i + r0 >= nct, 1, 0), 0, 0))],
        out_specs=pl.BlockSpec((1, tl, d), lambda bi, i: (bi, i + r0, 0)),
        input_output_aliases={2: 0},
        compiler_params=_cparams(("arbitrary", "arbitrary"), 48),
        name="out_proj_post",
    )(a, wout, xa, g.reshape(1, d), gsel)


def _rank_kernel(afft_ref, slot_ref, *, n, cap, tw):
    aff = afft_ref[0]
    bits = lax.bitcast_convert_type(aff, jnp.int32)
    e_num = aff.shape[0]

    def count(mask):
        return jnp.sum(jnp.where(mask, 1.0, 0.0), axis=1, keepdims=True)

    def bit_step(i, thr):
        cand = thr | lax.shift_left(jnp.int32(1), 30 - i)
        return jnp.where(count(bits >= cand) >= cap, cand, thr)

    thr = lax.fori_loop(0, 31, bit_step, jnp.zeros((e_num, 1), jnp.int32))
    gt = bits > thr
    eq = bits == thr
    need = cap - count(gt)
    tri = jnp.where(lax.broadcasted_iota(jnp.int32, (tw, tw), 0) < lax.broadcasted_iota(jnp.int32, (tw, tw), 1),
                    1.0, 0.0).astype(BF16)

    def prefix(mask):
        parts, carry = [], jnp.zeros((e_num, 1), F32)
        for j in range(n // tw):
            m = jnp.where(mask[:, j * tw:(j + 1) * tw], 1.0, 0.0)
            parts.append(jnp.dot(m.astype(BF16), tri, preferred_element_type=F32) + carry)
            carry = carry + jnp.sum(m, axis=1, keepdims=True)
        return jnp.concatenate(parts, axis=1)

    sel = jnp.logical_or(gt, jnp.logical_and(eq, prefix(eq) < need))
    slot_ref[0] = jnp.where(sel, prefix(sel), float(cap)).astype(jnp.int32)


def _rank(afft, cap):
    b, e_num, n = afft.shape
    tw = min(LANES, n)
    return pl.pallas_call(
        functools.partial(_rank_kernel, n=n, cap=cap, tw=tw),
        out_shape=jax.ShapeDtypeStruct((b, e_num, n), jnp.int32),
        grid=(b,),
        in_specs=[pl.BlockSpec((1, e_num, n), lambda bi: (bi, 0, 0))],
        out_specs=pl.BlockSpec((1, e_num, n), lambda bi: (bi, 0, 0)),
        compiler_params=_cparams(("arbitrary",), 32),
        name="ec_rank",
    )(afft)


def _gather_kernel(slot_ref, afft_ref, h_ref, o_ref, g_ref, *, cap):
    e = pl.program_id(1)
    sl = slot_ref[0, pl.ds(e, 1), :]
    hit = lax.broadcasted_iota(jnp.int32, (cap, 1), 0) == sl
    o_ref[0] = jnp.dot(jnp.where(hit, 1.0, 0.0).astype(BF16), h_ref[0], preferred_element_type=F32).astype(BF16)
    gate = jnp.sum(jnp.where(hit, afft_ref[0, pl.ds(e, 1), :], 0.0), axis=1, keepdims=True)
    g_ref[0] = jnp.broadcast_to(gate, (cap, LANES))


def _gather(slot, afft, hs, cap):
    b, e_num, n = slot.shape
    d = hs.shape[2]
    return pl.pallas_call(
        functools.partial(_gather_kernel, cap=cap),
        out_shape=[jax.ShapeDtypeStruct((e_num, b * cap, d), BF16),
                   jax.ShapeDtypeStruct((e_num, b * cap, LANES), F32)],
        grid=(b, e_num),
        in_specs=[pl.BlockSpec((1, e_num, n), lambda bi, e: (bi, 0, 0)),
                  pl.BlockSpec((1, e_num, n), lambda bi, e: (bi, 0, 0)),
                  pl.BlockSpec((1, n, d), lambda bi, e: (bi, 0, 0))],
        out_specs=[pl.BlockSpec((1, cap, d), lambda bi, e: (e, bi, 0)),
                   pl.BlockSpec((1, cap, LANES), lambda bi, e: (e, bi, 0))],
        compiler_params=_cparams(("arbitrary", "arbitrary"), 40),
        name="ec_gather",
    )(slot, afft, hs)


def _ffn_kernel(x_ref, g_ref, w1_ref, w3_ref, w2_ref, o_ref, acc_ref):
    f = pl.program_id(2)
    x = x_ref[0]
    h1 = jnp.dot(x, w1_ref[0, 0].astype(BF16), preferred_element_type=F32)
    h3 = jnp.dot(x, w3_ref[0, 0].astype(BF16), preferred_element_type=F32)
    hid = (_silu(h1) * h3).astype(BF16)

    @pl.when(f == 0)
    def _():
        acc_ref[...] = jnp.zeros_like(acc_ref)

    d = acc_ref.shape[1]
    for n0 in range(0, d, FFN_OUT_CHUNK):
        cols = slice(n0, n0 + FFN_OUT_CHUNK)
        acc_ref[:, cols] += jnp.dot(hid, w2_ref[0, 0, :, cols].astype(BF16), preferred_element_type=F32)

    @pl.when(f == pl.num_programs(2) - 1)
    def _():
        o_ref[0] = (acc_ref[...] * g_ref[0, :, 0:1]).astype(BF16)


def _ffn(xe, gate, w1, w3, w2, layer, tf=512):
    e_num, m, d = xe.shape
    ff = w1.shape[3]
    tm = _pick(m, (1152, 1024, 768, 512, 384, 256, 128, 64, 32, 16, 8))
    once = pl.Buffered(1)
    return pl.pallas_call(
        _ffn_kernel,
        out_shape=jax.ShapeDtypeStruct((e_num, m, d), BF16),
        grid=(e_num, m // tm, ff // tf),
        in_specs=[pl.BlockSpec((1, tm, d), lambda e, i, f: (e, i, 0)),
                  pl.BlockSpec((1, tm, LANES), lambda e, i, f: (e, i, 0)),
                  pl.BlockSpec((1, 1, d, tf), lambda e, i, f: (layer, e, 0, f)),
                  pl.BlockSpec((1, 1, d, tf), lambda e, i, f: (layer, e, 0, f)),
                  pl.BlockSpec((1, 1, tf, d), lambda e, i, f: (layer, e, f, 0))],
        out_specs=pl.BlockSpec((1, tm, d), lambda e, i, f: (e, i, 0), pipeline_mode=once),
        scratch_shapes=[pltpu.VMEM((tm, d), F32)],
        compiler_params=_cparams(("arbitrary", "arbitrary", "arbitrary"), 58),
        name="ec_ffn",
    )(xe, gate, w1, w3, w2)


def _combine_kernel(slot_ref, ye_ref, x_ref, g_ref, gate_ref, o_ref, *, cap, e_num):
    lane = lax.broadcasted_iota(jnp.int32, (1, e_num), 1)
    pos = lax.broadcasted_iota(jnp.int32, (1, cap), 1)
    sl = slot_ref[0]
    y = None
    for e in range(e_num):
        se = jnp.sum(jnp.where(lane == e, sl, 0), axis=1, keepdims=True)
        pt = jnp.where(se == pos, 1.0, 0.0).astype(BF16)
        part = jnp.dot(pt, ye_ref[e], preferred_element_type=F32)
        y = part if y is None else y + part
    yn = y * lax.rsqrt(jnp.mean(y * y, axis=-1, keepdims=True) + NORM_EPS) * g_ref[...]
    o_ref[0] = x_ref[0] + gate_ref[0, 0] * yn


def _combine(slot_c, ye, xa, g, gsel, seg, row0, slot0, cap, final, tt=256):
    b, n, e_num = slot_c.shape
    d = xa.shape[2]
    tt = _pick(math.gcd(n, row0), (tt, 128, 64))
    assert row0 % tt == 0 and slot0 % cap == 0
    r0, s0 = row0 // tt, slot0 // cap
    return pl.pallas_call(
        functools.partial(_combine_kernel, cap=cap, e_num=e_num),
        out_shape=jax.ShapeDtypeStruct((b, n, d) if final else xa.shape, F32),
        grid=(b, n // tt),
        in_specs=[pl.BlockSpec((1, tt, e_num), lambda bi, j: (bi, j, 0)),
                  pl.BlockSpec((e_num, cap, d), lambda bi, j: (0, s0 + bi, 0)),
                  pl.BlockSpec((1, tt, d), lambda bi, j: (bi, r0 + j, 0)),
                  pl.BlockSpec((1, d), lambda bi, j: (0, 0)),
                  pl.BlockSpec((1, 1, 1, d), lambda bi, j: (bi, seg, 0, 0))],
        out_specs=pl.BlockSpec((1, tt, d), lambda bi, j: (bi, (0 if final else r0) + j, 0)),
        input_output_aliases={} if final else {2: 0},
        compiler_params=_cparams(("arbitrary", "arbitrary"), 56),
        name="ec_combine_post",
    )(slot_c, ye, xa, g.reshape(1, d), gsel)


def _sel(mod_c, mod_x, idx):
    b = mod_x.shape[0]
    mc = jnp.broadcast_to(mod_c[jnp.array(idx)][None], (b, len(idx), mod_c.shape[-1]))
    return jnp.stack([mc, mod_x[:, jnp.array(idx)]], axis=1)


def _layer(xa, nc, mod_x, mod_c, p, lam_init, hg_lb, ct, st, with_ctx):
    b, lt, d = xa.shape
    w = d // 2
    a_cols = 3 * w + 2 * RW_DECAY_RANK + 2 * RW_ICL_RANK + RW_GATE_RANK
    col_b = a_cols
    col_c = col_b + 3 * w
    col_g = col_c + 5 * w

    h = _norm_mod(xa, p['norm_g'][0], _sel(mod_c, mod_x, (0, 1)), nc)
    h2 = h.reshape(b * lt, d)
    w_in = p['w_in']
    za = _matmul(h2, w_in[:, :col_b].astype(BF16), F32, tn=a_cols // 3).reshape(b, lt, a_cols)
    zc = _matmul(h2, w_in[:, col_c:col_g].astype(BF16), F32).reshape(b, lt, 5 * w)
    gl = _matmul(h2, w_in[:, col_g:].astype(BF16), BF16).reshape(b, lt, 3 * d)

    streams, ga = _rw_prep(za, nc, p['rw_mu'], p['rw_w0'], p['rw_wB'], p['rw_a0'], p['rw_aB'], p['rw_gB'])
    oaf, oab = _rw_chunk(streams, nc, p['rw_kk'], p['rw_ka'], p['rw_rk'], p['rw_gn'])

    zbr = _proj_rope(h2, w_in[:, col_b:col_c].astype(BF16), ct, st, lt).reshape(b, lt, 3 * w)
    yb = _attention(zbr, nc, p['da_lambda'], p['da_subln'], lam_init, with_ctx)

    of = _hgrn(zc, nc, hg_lb, rev=False)
    ob = _hgrn(zc, nc, hg_lb, rev=True)

    hn = jnp.tile(p['hg_norm'], w // HG_EXPAND).reshape(1, w)
    skip = 0 if with_ctx else nc
    merged = _merge(oaf, oab, ga, yb, of, ob, zc, hn, gl, p['w_branch'].astype(BF16), skip)
    xa = _proj_post(merged, p['w_out'].astype(BF16), xa, p['norm_g'][1], _sel(mod_c, mod_x, (2,)), nc, skip)

    hm, afft = _norm_mod(xa, p['norm_g'][2], _sel(mod_c, mod_x, (3, 4)), nc, router_t=p['moe_router'].T, row0=skip)
    gsel = _sel(mod_c, mod_x, (5,))
    sets = [(nc, lt - nc, 1)] + ([(0, nc, 0)] if with_ctx else [])
    routed, xes, gates = [], [], []
    for row0, nn, seg in sets:
        cap = EC_CAPACITY_FACTOR * nn // N_EXPERTS
        at = afft[:, :, row0 - skip:row0 - skip + nn]
        slot = _rank(at, cap)
        xe_s, gate_s = _gather(slot, at, hm[:, row0 - skip:row0 - skip + nn], cap)
        xes.append(xe_s)
        gates.append(gate_s)
        routed.append((jnp.swapaxes(slot, 1, 2), row0, seg, cap))
    xe = xes[0] if len(xes) == 1 else jnp.concatenate(xes, axis=1)
    gate = gates[0] if len(gates) == 1 else jnp.concatenate(gates, axis=1)
    ye = _ffn(xe, gate, p['moe_w1'], p['moe_w3'], p['moe_w2'], p['layer'])
    slot0 = 0
    for slot_c, row0, seg, cap in routed:
        xa = _combine(slot_c, ye, xa, p['norm_g'][3], gsel, seg, row0, slot0, cap, final=not with_ctx)
        slot0 += b * cap
    return xa


def kernel(x, c, ctx, c_ctx, w_ada, b_ada, norm_g, w_in, rw_mu, rw_w0, rw_wB, rw_a0, rw_aB, rw_gB, rw_kk, rw_ka, rw_rk, rw_gn, da_lambda, da_subln, hg_lb_logits, hg_norm, w_branch, w_out, moe_router, moe_w1, moe_w3, moe_w2):
    b, seq, d = x.shape
    nc = ctx.shape[1]
    depth = w_ada.shape[0]
    ct, st = _rope_tables(seq, nc)
    lb_w = jax.nn.softmax(hg_lb_logits.astype(F32), axis=0)
    hg_lb = jnp.cumsum(lb_w, axis=0) - lb_w[0]
    rows = ((b + 1 + SUBLANES - 1) // SUBLANES) * SUBLANES
    cc = jnp.zeros((rows, d), F32).at[:b].set(c).at[b].set(c_ctx)
    mod = _modulation(cc, w_ada, b_ada)
    xa = jnp.concatenate([ctx, x], axis=1)
    for l in range(depth):
        p = dict(norm_g=norm_g[l], w_in=w_in[l], rw_mu=rw_mu[l], rw_w0=rw_w0[l], rw_wB=rw_wB[l],
                 rw_a0=rw_a0[l], rw_aB=rw_aB[l], rw_gB=rw_gB[l], rw_kk=rw_kk[l], rw_ka=rw_ka[l],
                 rw_rk=rw_rk[l], rw_gn=rw_gn[l], da_lambda=da_lambda[l], da_subln=da_subln[l],
                 hg_norm=hg_norm[l], w_branch=w_branch[l], w_out=w_out[l], moe_router=moe_router[l],
                 moe_w1=moe_w1, moe_w3=moe_w3, moe_w2=moe_w2, layer=l)
        mod_x = mod[l, :b].reshape(b, 6, d)
        mod_c = mod[l, b].reshape(6, d)
        lam_init = 0.8 - 0.6 * math.exp(-0.3 * l)
        xa = _layer(xa, nc, mod_x, mod_c, p, lam_init, hg_lb[l], ct, st, with_ctx=l < depth - 1)
    return xa
```

```python
import functools
import math

import jax
import jax.numpy as jnp
from jax import lax
from jax.experimental import pallas as pl
from jax.experimental.pallas import tpu as pltpu

F32 = jnp.float32
BF16 = jnp.bfloat16
HIGHEST = lax.Precision.HIGHEST

GRID_W = 64
RW_HEAD_DIM = 64
RW_DECAY_RANK = 64
RW_ICL_RANK = 64
RW_GATE_RANK = 128
RW_GN_EPS = 64e-5
DA_QK_DIM = 64
DA_V_DIM = 128
ROPE_THETA = 10000.0
ROPE_AXIS_FREQS = 16
DA_SUBLN_EPS = 1e-5
HG_EXPAND = 128
N_EXPERTS = 16
EC_CAPACITY_FACTOR = 2
NORM_EPS = 1e-6

LANES = 128
SUBLANES = 8
V7X_VMEM_BYTES = 64 * 1024 * 1024
HG_CHUNK = 64
RW_CHUNK = 64
RW_GROUP = 256
RW_SAMPLES = 4
FFN_OUT_CHUNK = 512


def _cparams(sem, vmem_mb):
    limit = int(vmem_mb * 1024 * 1024)
    assert limit < V7X_VMEM_BYTES
    return pltpu.CompilerParams(dimension_semantics=sem, vmem_limit_bytes=limit)


def _sigmoid(x):
    return 1.0 / (1.0 + jnp.exp(-x))


def _sigmoid_t(x):
    return 0.5 * jnp.tanh(0.5 * x) + 0.5


def _silu(x):
    return x * _sigmoid_t(x)


def _mod_kernel(c_ref, w_ref, b_ref, o_ref):
    a = _silu(c_ref[...]).astype(BF16)
    o_ref[0] = jnp.dot(a, w_ref[0].astype(BF16), preferred_element_type=F32) + b_ref[0]


def _modulation(cc, w_ada, b_ada):
    depth, d, n = w_ada.shape
    rows = cc.shape[0]
    tn = 1024
    return pl.pallas_call(
        _mod_kernel,
        out_shape=jax.ShapeDtypeStruct((depth, rows, n), F32),
        grid=(depth, n // tn),
        in_specs=[pl.BlockSpec((rows, d), lambda l, j: (0, 0)),
                  pl.BlockSpec((1, d, tn), lambda l, j: (l, 0, j)),
                  pl.BlockSpec((1, 1, tn), lambda l, j: (l, 0, j))],
        out_specs=pl.BlockSpec((1, rows, tn), lambda l, j: (l, 0, j)),
        compiler_params=_cparams(("arbitrary", "arbitrary"), 40),
        name="adaln_mod",
    )(cc, w_ada, b_ada.reshape(depth, 1, n))


def _norm_mod_kernel(x_ref, g_ref, ms_ref, *rest, with_router):
    x = x_ref[0]
    xn = x * lax.rsqrt(jnp.mean(x * x, axis=-1, keepdims=True) + NORM_EPS) * g_ref[...]
    h = xn * (1.0 + ms_ref[0, 0, 1:2, :]) + ms_ref[0, 0, 0:1, :]
    if with_router:
        rt_ref, h_ref, aff_ref = rest
        h_ref[0] = h.astype(BF16)
        lt = lax.dot_general(rt_ref[...], h, (((1,), (1,)), ((), ())),
                             precision=HIGHEST, preferred_element_type=F32)
        e = jnp.exp(lt - jnp.max(lt, axis=0, keepdims=True))
        aff_ref[0] = e / jnp.sum(e, axis=0, keepdims=True)
    else:
        (h_ref,) = rest
        h_ref[0] = h.astype(BF16)


def _norm_mod(xa, g, msel, nc, router_t=None, row0=0, tl=256):
    b, lt, d = xa.shape
    tl = min(tl, nc)
    nct = nc // tl
    r0 = row0 // tl
    with_router = router_t is not None
    in_specs = [pl.BlockSpec((1, tl, d), lambda bi, i: (bi, i + r0, 0)),
                pl.BlockSpec((1, d), lambda bi, i: (0, 0)),
                pl.BlockSpec((1, 1, 2, d), lambda bi, i: (bi, jnp.where(i + r0 >= nct, 1, 0), 0, 0))]
    args = [xa, g.reshape(1, d), msel]
    out_shape = [jax.ShapeDtypeStruct((b, lt - row0, d), BF16)]
    out_specs = [pl.BlockSpec((1, tl, d), lambda bi, i: (bi, i, 0))]
    if with_router:
        e = router_t.shape[0]
        in_specs.append(pl.BlockSpec((e, d), lambda bi, i: (0, 0)))
        args.append(router_t)
        out_shape.append(jax.ShapeDtypeStruct((b, e, lt - row0), F32))
        out_specs.append(pl.BlockSpec((1, e, tl), lambda bi, i: (bi, 0, i)))
    res = pl.pallas_call(
        functools.partial(_norm_mod_kernel, with_router=with_router),
        out_shape=out_shape, grid=(b, lt // tl - r0), in_specs=in_specs, out_specs=out_specs,
        compiler_params=_cparams(("arbitrary", "arbitrary"), 32),
        name="norm_mod_router" if with_router else "norm_mod",
    )(*args)
    return res if with_router else res[0]


def _matmul_kernel(a_ref, w_ref, o_ref):
    o_ref[...] = jnp.dot(a_ref[...], w_ref[...], preferred_element_type=F32).astype(o_ref.dtype)


def _pick(n, cands):
    for c in cands:
        if n % c == 0:
            return c
    return n


def _matmul(a, w, out_dtype, tm=1024, tn=1024):
    m, k = a.shape
    n = w.shape[1]
    tm = _pick(m, (tm, 768, 512, 384, 256, 128))
    tn = _pick(n, (tn, 768, 512, 384, 256, 128))
    return pl.pallas_call(
        _matmul_kernel,
        out_shape=jax.ShapeDtypeStruct((m, n), out_dtype),
        grid=(m // tm, n // tn),
        in_specs=[pl.BlockSpec((tm, k), lambda i, j: (i, 0)),
                  pl.BlockSpec((k, tn), lambda i, j: (0, j))],
        out_specs=pl.BlockSpec((tm, tn), lambda i, j: (i, j)),
        compiler_params=_cparams(("arbitrary", "arbitrary"), 48),
        name="matmul",
    )(a, w)


def _rw_prep_kernel(z_ref, zp_ref, zn_ref, mu_ref, w0_ref, wb_ref, a0_ref, ab_ref, gb_ref,
                    o_ref, g_ref, *, tl, nct, nt, w):
    i = pl.program_id(1)
    z = z_ref[0]
    row = lax.broadcasted_iota(jnp.int32, (tl, 1), 0)
    seg_start = jnp.logical_or(i == 0, i == nct)
    seg_end = jnp.logical_or(i == nct - 1, i == nt - 1)
    prev_row = jnp.where(seg_start, 0.0, zp_ref[0, SUBLANES - 1:SUBLANES, :])
    next_row = jnp.where(seg_end, 0.0, zn_ref[0, 0:1, :])
    prev = jnp.where(row == 0, prev_row, pltpu.roll(z, 1, 0))
    nxt = jnp.where(row == tl - 1, next_row, pltpu.roll(z, tl - 1, 0))
    zs = z + mu_ref[0:1, :] * (prev - z) + mu_ref[1:2, :] * (nxt - z)
    o_ref[0, 0] = zs[:, 0:w]
    o_ref[0, 1] = zs[:, w:2 * w]
    o_ref[0, 2] = zs[:, 2 * w:3 * w]
    o4 = 3 * w + 2 * RW_DECAY_RANK
    o5 = o4 + 2 * RW_ICL_RANK
    wd = jnp.tanh(zs[:, 3 * w:o4])
    wl = jnp.dot(wd.astype(BF16), wb_ref[...], preferred_element_type=F32) + w0_ref[...]
    dec = -math.exp(-0.5) * _sigmoid_t(wl)
    o_ref[0, 3] = dec[:, 0:w]
    o_ref[0, 4] = dec[:, w:2 * w]
    al = jnp.dot(zs[:, o4:o5].astype(BF16), ab_ref[...], preferred_element_type=F32) + a0_ref[...]
    av = _sigmoid_t(al)
    o_ref[0, 5] = av[:, 0:w]
    o_ref[0, 6] = av[:, w:2 * w]
    g_ref[0] = jnp.dot(_sigmoid_t(zs[:, o5:]).astype(BF16), gb_ref[...], preferred_element_type=F32)


def _blockdiag2(m):
    r, w = m.shape[1], m.shape[2]
    z = jnp.zeros((r, w), m.dtype)
    return jnp.concatenate([jnp.concatenate([m[0], z], axis=1), jnp.concatenate([z, m[1]], axis=1)], axis=0)


def _rw_prep(za, nc, mu, w0, wb, a0, ab, gb, tl=256):
    b, lt, acols = za.shape
    w = w0.shape[1]
    tl = min(tl, nc)
    nct, nt = nc // tl, lt // tl
    r8 = tl // SUBLANES
    nb8 = lt // SUBLANES
    full = lambda shp: pl.BlockSpec(shp, lambda bi, i: tuple(0 for _ in shp))
    kern = functools.partial(_rw_prep_kernel, tl=tl, nct=nct, nt=nt, w=w)
    return pl.pallas_call(
        kern,
        out_shape=[jax.ShapeDtypeStruct((b, 7, lt, w), F32), jax.ShapeDtypeStruct((b, lt, w), F32)],
        grid=(b, nt),
        in_specs=[pl.BlockSpec((1, tl, acols), lambda bi, i: (bi, i, 0)),
                  pl.BlockSpec((1, SUBLANES, acols), lambda bi, i: (bi, jnp.maximum(i * r8 - 1, 0), 0)),
                  pl.BlockSpec((1, SUBLANES, acols), lambda bi, i: (bi, jnp.minimum((i + 1) * r8, nb8 - 1), 0)),
                  full((2, acols)), full((1, 2 * w)), full((2 * RW_DECAY_RANK, 2 * w)),
                  full((1, 2 * w)), full((2 * RW_ICL_RANK, 2 * w)), full((RW_GATE_RANK, w))],
        out_specs=[pl.BlockSpec((1, 7, tl, w), lambda bi, i: (bi, 0, i, 0)),
                   pl.BlockSpec((1, tl, w), lambda bi, i: (bi, i, 0))],
        compiler_params=_cparams(("arbitrary", "arbitrary"), 48),
        name="rwkv_prep",
    )(za, za, za, mu, w0.reshape(1, 2 * w), _blockdiag2(wb).astype(BF16), a0.reshape(1, 2 * w),
      _blockdiag2(ab).astype(BF16), gb.astype(BF16))


def _rw_chunk_kernel(xf_ref, lwf_ref, af_ref, xb_ref, lwb_ref, ab_ref, kk_ref, ka_ref, rk_ref, gn0_ref, gn1_ref,
                     of_ref, ob_ref, s_ref, *, c, w, nbt):
    j = pl.program_id(1)

    @pl.when(j == 0)
    def _():
        s_ref[...] = jnp.zeros_like(s_ref)

    gw, n = RW_GROUP, RW_HEAD_DIM
    ng = w // gw
    row = lax.broadcasted_iota(jnp.int32, (c, 1), 0)
    pos = lax.broadcasted_iota(jnp.int32, (1, gw), 1) % n
    bdmask = (lax.broadcasted_iota(jnp.int32, (gw, 1), 0) // n) == (lax.broadcasted_iota(jnp.int32, (1, gw), 1) // n)
    ones_bd = jnp.where(bdmask, 1.0, 0.0).astype(BF16)
    eye = jnp.where(pos == row, 1.0, 0.0)
    nt = (((1,), (1,)), ((), ()))
    tn = (((0,), (0,)), ((), ()))
    chains = [(bb, d, g) for bb in range(nbt) for d in range(2) for g in range(ng)]
    refs = ((xf_ref, lwf_ref, af_ref, of_ref), (xb_ref, lwb_ref, ab_ref, ob_ref))

    def bd(x):
        return jnp.where(bdmask, jnp.concatenate([x] * (gw // c), axis=0), 0.0).astype(BF16)

    def gsum(xs):
        pieces = []
        for x in xs:
            hi = x.astype(BF16)
            r1 = x - hi.astype(F32)
            mid = r1.astype(BF16)
            pieces += [hi, mid, (r1 - mid.astype(F32)).astype(BF16)]
        tot = jnp.dot(jnp.concatenate(pieces, axis=0), ones_bd, preferred_element_type=F32)
        return [tot[3 * i * c:(3 * i + 1) * c] + tot[(3 * i + 1) * c:(3 * i + 2) * c] + tot[(3 * i + 2) * c:(3 * i + 3) * c]
                for i in range(len(xs))]

    def mm(x, ybd):
        return jnp.dot(x.astype(BF16), ybd, preferred_element_type=F32)

    st = []
    for bb, d, g in chains:
        x_ref, lw_ref, a_ref, _ = refs[d]
        sl = slice(g * gw, (g + 1) * gw)
        q = dict(sl=sl, rev=d == 1, r=x_ref[bb, 0, :, sl], k=x_ref[bb, 1, :, sl], v=x_ref[bb, 2, :, sl],
                 lw=lw_ref[bb, 0, :, sl], a=a_ref[bb, 0, :, sl])
        q['kx'] = q['k'] * kk_ref[:, sl]
        q['kt'] = q['k'] * (1.0 + (q['a'] - 1.0) * ka_ref[:, sl])
        st.append(q)
    for q in st:
        q['ss'], q['bon'] = gsum([q['kx'] * q['kx'], q['r'] * q['kt'] * rk_ref[:, q['sl']]])
    for q in st:
        rev = q['rev']
        kk = q['kx'] / jnp.maximum(jnp.sqrt(q['ss']), 1e-12)
        q['b'] = q['a'] * kk
        cw = q['lw']
        sft = 1
        while sft < c:
            if rev:
                cw = cw + jnp.where(row < c - sft, pltpu.roll(cw, c - sft, 0), 0.0)
            else:
                cw = cw + jnp.where(row >= sft, pltpu.roll(cw, sft, 0), 0.0)
            sft *= 2
        q['tot'] = cw[0:1] if rev else cw[c - 1:c]
        invp = jnp.exp(-cw)
        q['e2'] = jnp.exp(q['tot'] - cw)
        q['kr'] = jnp.concatenate([kk * jnp.exp(cw - q['lw']), q['r'] * jnp.exp(cw)], axis=0).astype(BF16)
        q['ktb'] = bd(q['kt'] * invp)
        q['bb'] = bd(q['b'] * invp)
        q['vbd'] = bd(q['v'])
    for q in st:
        strict = (pos > row) if q['rev'] else (pos < row)
        incl = (pos >= row) if q['rev'] else (pos <= row)
        g1 = lax.dot_general(q['kr'], q['ktb'], nt, preferred_element_type=F32)
        g2 = lax.dot_general(q['kr'], q['bb'], nt, preferred_element_type=F32)
        q['m_kt'] = jnp.where(strict, g1[:c], 0.0)
        q['n_kt'] = jnp.where(incl, g1[c:], 0.0)
        q['x'] = jnp.where(strict, g2[:c], 0.0)
        q['n_b'] = jnp.where(incl, g2[c:], 0.0)
        q['t'] = eye - q['x']
        q['xb'] = bd(q['x'])
    for q in st:
        mn = mm(jnp.concatenate([q['m_kt'], q['n_kt']], axis=0), q['vbd'])
        q['mkv'], q['nkv'] = mn[:c], mn[c:]
    lv = 2
    while lv < c:
        for q in st:
            xt = mm(jnp.concatenate([q['x'], q['t']], axis=0), q['xb'])
            q['x'], q['tx'] = xt[:c], xt[c:]
        for q in st:
            q['t'] = q['t'] + mm(q['tx'], q['xb'])
        lv *= 2
        if lv < c:
            for q in st:
                q['xb'] = bd(q['x'])
    for (bb, d, g), q in zip(chains, st):
        q['s0'] = s_ref[bb, d, g]
        q['ka'] = lax.dot_general(q['kr'], q['s0'].astype(BF16), nt, preferred_element_type=F32)
    for q in st:
        q['u'] = mm(q['t'], bd(q['ka'][:c] + q['mkv']))
    for (bb, d, g), q in zip(chains, st):
        e2 = q['e2']
        upd = lax.dot_general(jnp.concatenate([q['v'], q['u']], axis=0).astype(BF16),
                              jnp.concatenate([q['kt'] * e2, -(q['b'] * e2)], axis=0).astype(BF16),
                              tn, preferred_element_type=F32)
        s_ref[bb, d, g] = q['s0'] * jnp.exp(q['tot']) + jnp.where(bdmask, upd, 0.0)
    for q in st:
        q['y'] = q['ka'][c:] + q['nkv'] - mm(q['n_b'], bd(q['u']))
    for q in st:
        (mu,) = gsum([q['y']])
        q['dl'] = q['y'] - mu * (1.0 / n)
    for (bb, d, g), q in zip(chains, st):
        (var,) = gsum([q['dl'] * q['dl']])
        sl = q['sl']
        yn = q['dl'] * lax.rsqrt(var * (1.0 / n) + RW_GN_EPS) * gn0_ref[:, sl] + gn1_ref[:, sl]
        refs[d][3][bb, :, sl] = yn + q['bon'] * q['v']


def _rw_chunk(streams, nc, kk, ka, rk, gn):
    b, _, lt, w = streams.shape
    c = RW_CHUNK
    ncb, nb = nc // c, lt // c
    nbt = RW_SAMPLES if b % RW_SAMPLES == 0 else 1

    def rblk(j):
        return jnp.where(j < ncb, ncb - 1 - j, nb - 1 - (j - ncb))

    par = pl.BlockSpec((1, w), lambda bi, j: (0, 0))
    shp = jax.ShapeDtypeStruct((b, lt, w), F32)
    return pl.pallas_call(
        functools.partial(_rw_chunk_kernel, c=c, w=w, nbt=nbt),
        out_shape=[shp, shp],
        grid=(b // nbt, nb),
        in_specs=[pl.BlockSpec((nbt, 3, c, w), lambda bi, j: (bi, 0, j, 0)),
                  pl.BlockSpec((nbt, 1, c, w), lambda bi, j: (bi, 3, j, 0)),
                  pl.BlockSpec((nbt, 1, c, w), lambda bi, j: (bi, 5, j, 0)),
                  pl.BlockSpec((nbt, 3, c, w), lambda bi, j: (bi, 0, rblk(j), 0)),
                  pl.BlockSpec((nbt, 1, c, w), lambda bi, j: (bi, 4, rblk(j), 0)),
                  pl.BlockSpec((nbt, 1, c, w), lambda bi, j: (bi, 6, rblk(j), 0)),
                  par, par, par, par, par],
        out_specs=[pl.BlockSpec((nbt, c, w), lambda bi, j: (bi, j, 0)),
                   pl.BlockSpec((nbt, c, w), lambda bi, j: (bi, rblk(j), 0))],
        scratch_shapes=[pltpu.VMEM((nbt, 2, w // RW_GROUP, RW_GROUP, RW_GROUP), F32)],
        compiler_params=_cparams(("arbitrary", "arbitrary"), 58),
        name="rwkv_chunk",
    )(streams, streams, streams, streams, streams, streams, kk.reshape(1, w), ka.reshape(1, w), rk.reshape(1, w),
      gn[0].reshape(1, w), gn[1].reshape(1, w))


def _proj_rope_kernel(a_ref, w_ref, c_ref, s_ref, o_ref):
    j = pl.program_id(1)
    acc = jnp.dot(a_ref[...], w_ref[...], preferred_element_type=F32)

    @pl.when(j < 2)
    def _():
        lane = lax.broadcasted_iota(jnp.int32, (1, LANES), 1)
        first_half = (lane % (2 * ROPE_AXIS_FREQS)) < ROPE_AXIS_FREQS
        c = c_ref[...] * jnp.where(j == 0, DA_QK_DIM ** -0.5 * math.log2(math.e), 1.0)
        s = s_ref[...] * jnp.where(j == 0, DA_QK_DIM ** -0.5 * math.log2(math.e), 1.0)
        for g in range(acc.shape[1] // LANES):
            xs = acc[:, g * LANES:(g + 1) * LANES]
            up = pltpu.roll(xs, LANES - ROPE_AXIS_FREQS, 1)
            dn = pltpu.roll(xs, ROPE_AXIS_FREQS, 1)
            o_ref[:, g * LANES:(g + 1) * LANES] = (xs * c + jnp.where(first_half, up, dn) * s).astype(o_ref.dtype)

    @pl.when(j >= 2)
    def _():
        o_ref[...] = acc.astype(o_ref.dtype)


def _proj_rope(a, w, ct, st, lt):
    m, k = a.shape
    n = w.shape[1]
    tn = n // 3
    tm = _pick(lt, (768, 640, 512, 384, 256, 128))
    per = lt // tm
    return pl.pallas_call(
        _proj_rope_kernel,
        out_shape=jax.ShapeDtypeStruct((m, n), BF16),
        grid=(m // tm, 3),
        in_specs=[pl.BlockSpec((tm, k), lambda i, j: (i, 0)),
                  pl.BlockSpec((k, tn), lambda i, j: (0, j)),
                  pl.BlockSpec((tm, LANES), lambda i, j: (i % per, 0)),
                  pl.BlockSpec((tm, LANES), lambda i, j: (i % per, 0))],
        out_specs=pl.BlockSpec((tm, tn), lambda i, j: (i, j)),
        compiler_params=_cparams(("arbitrary", "arbitrary"), 48),
        name="proj_rope",
    )(a, w, ct, st)


def _rope_tables(seq, nc):
    rows = seq // GRID_W
    row = jnp.broadcast_to(jnp.arange(rows)[:, None], (rows, GRID_W)).reshape(seq)
    col = jnp.broadcast_to(jnp.arange(GRID_W)[None, :], (rows, GRID_W)).reshape(seq)
    inv = 1.0 / (ROPE_THETA ** (jnp.arange(ROPE_AXIS_FREQS, dtype=F32) / ROPE_AXIS_FREQS))
    ang = jnp.stack([row, col], axis=-1).astype(F32)[:, :, None] * inv
    cos, sin = jnp.cos(ang), jnp.sin(ang)
    c64 = jnp.concatenate([cos[:, 0], cos[:, 0], cos[:, 1], cos[:, 1]], axis=-1)
    s64 = jnp.concatenate([-sin[:, 0], sin[:, 0], -sin[:, 1], sin[:, 1]], axis=-1)
    ct = jnp.concatenate([jnp.ones((nc, LANES), F32), jnp.tile(c64, (1, 2))], axis=0)
    st = jnp.concatenate([jnp.zeros((nc, LANES), F32), jnp.tile(s64, (1, 2))], axis=0)
    return ct, st


def _attn_kernel(q_ref, k_ref, v_ref, lam_ref, g_ref, o_ref, *, nct, nc, lam_init, hp, qoff):
    i = pl.program_id(2) + qoff
    lane = lax.broadcasted_iota(jnp.int32, (1, LANES), 1)
    lp = lam_ref[...]
    lam = (jnp.exp(jnp.sum(lp[0:1] * lp[1:2], axis=1, keepdims=True))
           - jnp.exp(jnp.sum(lp[2:3] * lp[3:4], axis=1, keepdims=True)) + lam_init)

    def attend(nk):
        for h in range(hp):
            cols = slice(h * DA_V_DIM, (h + 1) * DA_V_DIM)
            q = q_ref[0, :, cols]
            k = k_ref[0, 0:nk, cols]
            v1 = jnp.concatenate([v_ref[0, 0:nk, cols], jnp.ones((nk, LANES), BF16)], axis=1)

            def one(sel):
                qs = jnp.where(sel, q, jnp.zeros_like(q))
                s = lax.dot_general(qs, k, (((1,), (1,)), ((), ())), preferred_element_type=F32)
                e = jnp.exp2(s - jnp.max(s, axis=-1, keepdims=True))
                ov = jnp.dot(e.astype(BF16), v1, preferred_element_type=F32)
                return ov[:, :DA_V_DIM] / ov[:, DA_V_DIM:]

            o = one(lane < DA_QK_DIM) - lam * one(lane >= DA_QK_DIM)
            on = o * lax.rsqrt(jnp.mean(o * o, axis=-1, keepdims=True) + DA_SUBLN_EPS) * g_ref[...]
            o_ref[0, :, cols] = (on * (1.0 - lam_init)).astype(o_ref.dtype)

    @pl.when(i < nct)
    def _():
        attend(nc)

    @pl.when(i >= nct)
    def _():
        attend(k_ref.shape[1])


def _attention(zbr, nc, lam_p, subln, lam_init, with_ctx, hp=4):
    b, lt, cols = zbr.shape
    w = cols // 3
    heads = w // DA_V_DIM
    tq = min(256, nc)
    nct = nc // tq
    hg = heads // hp
    bw = hp * DA_V_DIM
    qoff = 0 if with_ctx else nct
    kern = functools.partial(_attn_kernel, nct=nct, nc=nc, lam_init=lam_init, hp=hp, qoff=qoff)
    return pl.pallas_call(
        kern,
        out_shape=jax.ShapeDtypeStruct((b, lt - qoff * tq, w), BF16),
        grid=(b, hg, lt // tq - qoff),
        in_specs=[pl.BlockSpec((1, tq, bw), lambda bi, h, i: (bi, i + qoff, h)),
                  pl.BlockSpec((1, lt, bw), lambda bi, h, i: (bi, 0, hg + h)),
                  pl.BlockSpec((1, lt, bw), lambda bi, h, i: (bi, 0, 2 * hg + h)),
                  pl.BlockSpec((4, DA_QK_DIM), lambda bi, h, i: (0, 0)),
                  pl.BlockSpec((1, DA_V_DIM), lambda bi, h, i: (0, 0))],
        out_specs=pl.BlockSpec((1, tq, bw), lambda bi, h, i: (bi, i, h)),
        compiler_params=_cparams(("arbitrary", "arbitrary", "arbitrary"), 58),
        name="diff_attn",
    )(zbr, zbr, zbr, lam_p, subln.reshape(1, DA_V_DIM))


def _seg_masks(c, e):
    row = lax.broadcasted_iota(jnp.int32, (c, e), 0)
    mk = {}
    h = 1
    while h < min(SUBLANES, c):
        odd = (row % (2 * h)) >= h
        for s in range(1, h + 1):
            mk[(h, s, 0)] = jnp.where(jnp.logical_and(odd, (row % h) == s - 1), 1.0, 0.0)
            mk[(h, s, 1)] = jnp.where(jnp.logical_and(~odd, (row % h) == h - s), 1.0, 0.0)
        h *= 2
    return mk


def _seg_scans(lf, mk, c):
    cs, ss = {1: lf}, {1: lf}
    h = 1
    while h < c:
        x, y = cs[h], ss[h]
        if h < SUBLANES:
            x3 = x.reshape(c // SUBLANES, SUBLANES, x.shape[1])
            y3 = y.reshape(c // SUBLANES, SUBLANES, y.shape[1])
            for s in range(1, h + 1):
                x = x + mk[(h, s, 0)] * pltpu.roll(x3, s, 1).reshape(x.shape)
                y = y + mk[(h, s, 1)] * pltpu.roll(y3, SUBLANES - s, 1).reshape(y.shape)
            cs[2 * h], ss[2 * h] = x, y
        else:
            px = [x[j * h:(j + 1) * h] for j in range(c // h)]
            py = [y[j * h:(j + 1) * h] for j in range(c // h)]
            nx = [px[j] + px[j - 1][h - 1:h] if j % 2 == 1 else px[j] for j in range(c // h)]
            ny = [py[j] + py[j + 1][0:1] if j % 2 == 0 else py[j] for j in range(c // h)]
            cs[2 * h], ss[2 * h] = jnp.concatenate(nx, axis=0), jnp.concatenate(ny, axis=0)
        h *= 2
    return cs, ss


def _hg_kernel(q_ref, f_ref, v_ref, lb_ref, o_ref, st_ref, *, tc, c, heads, rev):
    j = pl.program_id(1)

    @pl.when(j == 0)
    def _():
        st_ref[...] = jnp.zeros_like(st_ref)

    ti = lax.broadcasted_iota(jnp.int32, (c, c), 0)
    si = lax.broadcasted_iota(jnp.int32, (c, c), 1)
    nt = (((1,), (1,)), ((), ()))
    tn = (((0,), (0,)), ((), ()))
    mk = _seg_masks(c, HG_EXPAND)
    masks, hh = {}, 1
    while hh < c:
        tb, sb = ti // hh, si // hh
        masks[hh] = (jnp.logical_and(tb % 2 == 0, sb == tb + 1) if rev
                     else jnp.logical_and(tb % 2 == 1, sb == tb - 1))
        hh *= 2

    def head(h, carry):
        cols = pl.ds(pl.multiple_of(h * HG_EXPAND, HG_EXPAND), HG_EXPAND)
        lb = lb_ref[:, cols]
        chunks = range(tc // c)
        for ci in (reversed(chunks) if rev else chunks):
            rows = slice(ci * c, (ci + 1) * c)
            q = _silu(q_ref[0, rows, cols])
            f = lb + (1.0 - lb) * _sigmoid(f_ref[0, rows, cols])
            k = 1.0 - f
            lf = jnp.log(f)
            v = v_ref[0, rows, cols].astype(BF16)
            cs, ss = _seg_scans(lf, mk, c)
            qs, ks = (ss, cs) if rev else (cs, ss)
            att = jnp.where(ti == si, lax.dot_general(q.astype(BF16), k.astype(BF16), nt,
                                                      preferred_element_type=F32), 0.0)
            hh = 1
            while hh < c:
                qe = (q * jnp.exp(qs[hh])).astype(BF16)
                ke = (k * jnp.exp(ks[hh] - lf)).astype(BF16)
                a = lax.dot_general(qe, ke, nt, preferred_element_type=F32)
                att = att + jnp.where(masks[hh], a, 0.0)
                hh *= 2
            st = st_ref[h]
            qe = (q * jnp.exp(qs[c])).astype(BF16)
            o = lax.dot_general(qe, st.astype(BF16), nt, preferred_element_type=F32)
            o = o + jnp.dot(att.astype(BF16), v, preferred_element_type=F32)
            o_ref[0, rows, cols] = o
            ke = (k * jnp.exp(ks[c] - lf)).astype(BF16)
            tot = qs[c][0:1] if rev else qs[c][c - 1:c]
            st_ref[h] = st * jnp.exp(tot) + lax.dot_general(v, ke, tn, preferred_element_type=F32)
        return carry

    lax.fori_loop(0, heads, head, 0, unroll=True)


def _hgrn(zc, nc, lb, rev):
    b, lt, cols = zc.shape
    w = cols // 5
    heads = w // HG_EXPAND
    tc = min(256, nc)
    c = min(HG_CHUNK, tc)
    ncb, nb = nc // tc, lt // tc

    def blk(j):
        return jnp.where(j < ncb, ncb - 1 - j, nb - 1 - (j - ncb)) if rev else j

    fcol = 2 if rev else 1
    return pl.pallas_call(
        functools.partial(_hg_kernel, tc=tc, c=c, heads=heads, rev=rev),
        out_shape=jax.ShapeDtypeStruct((b, lt, w), F32),
        grid=(b, nb),
        in_specs=[pl.BlockSpec((1, tc, w), lambda bi, j: (bi, blk(j), 0)),
                  pl.BlockSpec((1, tc, w), lambda bi, j: (bi, blk(j), fcol)),
                  pl.BlockSpec((1, tc, w), lambda bi, j: (bi, blk(j), 3)),
                  pl.BlockSpec((1, w), lambda bi, j: (0, 0))],
        out_specs=pl.BlockSpec((1, tc, w), lambda bi, j: (bi, blk(j), 0)),
        scratch_shapes=[pltpu.VMEM((heads, HG_EXPAND, HG_EXPAND), F32)],
        compiler_params=_cparams(("arbitrary", "arbitrary"), 32),
        name="hgrn2_bwd" if rev else "hgrn2_fwd",
    )(zc, zc, zc, lb.reshape(1, w))


def _merge_kernel(oaf_ref, oab_ref, ga_ref, yb_ref, of_ref, ob_ref, gc_ref, hn_ref, gl0_ref, gl1_ref, gl2_ref, wb_ref,
                  o_ref, *, heads):
    gls = (gl0_ref, gl1_ref, gl2_ref)
    ya = (oaf_ref[0] + oab_ref[0]) * ga_ref[0]
    oc = of_ref[0] + ob_ref[0]
    parts = []
    for h in range(heads):
        x = oc[:, h * HG_EXPAND:(h + 1) * HG_EXPAND]
        parts.append(x * lax.rsqrt(jnp.mean(x * x, axis=-1, keepdims=True) + NORM_EPS))
    yc = jnp.concatenate(parts, axis=-1) * hn_ref[...] * _silu(gc_ref[0])
    acc = None
    for bi, y in enumerate((ya.astype(BF16), yb_ref[0], yc.astype(BF16))):
        term = _sigmoid_t(gls[bi][0].astype(F32)) * jnp.dot(y, wb_ref[bi], preferred_element_type=F32)
        acc = term if acc is None else acc + term
    o_ref[0] = acc.astype(BF16)


def _merge(oaf, oab, ga, yb, of, ob, zc, hn, gl, wbr, row0, tl=256, tn=2048):
    b, lt, w = ga.shape
    d = wbr.shape[2]
    tl = _pick(math.gcd(lt, row0), (tl, 64))
    r0 = row0 // tl
    heads = w // HG_EXPAND
    nn = d // tn
    row = lambda n, bi, i: (bi, i + r0, 0)
    glspec = lambda br: pl.BlockSpec((1, tl, tn), lambda n, bi, i: (bi, i + r0, br * nn + n))
    return pl.pallas_call(
        functools.partial(_merge_kernel, heads=heads),
        out_shape=jax.ShapeDtypeStruct((b, lt - row0, d), BF16),
        grid=(d // tn, b, lt // tl - r0),
        in_specs=[pl.BlockSpec((1, tl, w), row),
                  pl.BlockSpec((1, tl, w), row),
                  pl.BlockSpec((1, tl, w), row),
                  pl.BlockSpec((1, tl, w), lambda n, bi, i: (bi, i, 0)),
                  pl.BlockSpec((1, tl, w), row),
                  pl.BlockSpec((1, tl, w), row),
                  pl.BlockSpec((1, tl, w), lambda n, bi, i: (bi, i + r0, 4)),
                  pl.BlockSpec((1, w), lambda n, bi, i: (0, 0)),
                  glspec(0), glspec(1), glspec(2),
                  pl.BlockSpec((3, w, tn), lambda n, bi, i: (0, 0, n))],
        out_specs=pl.BlockSpec((1, tl, tn), lambda n, bi, i: (bi, i, n)),
        compiler_params=_cparams(("arbitrary", "arbitrary", "arbitrary"), 58),
        name="branch_merge",
    )(oaf, oab, ga, yb, of, ob, zc, hn, gl, gl, gl, wbr)


def _proj_post_kernel(a_ref, w_ref, x_ref, g_ref, gate_ref, o_ref):
    mx = jnp.dot(a_ref[0], w_ref[...], preferred_element_type=F32)
    mn = mx * lax.rsqrt(jnp.mean(mx * mx, axis=-1, keepdims=True) + NORM_EPS) * g_ref[...]
    o_ref[0] = x_ref[0] + gate_ref[0, 0] * mn


def _proj_post(a, wout, xa, g, gsel, nc, row0, tl=256):
    b, lt, d = xa.shape
    tl = min(tl, nc)
    nct = nc // tl
    r0 = row0 // tl
    return pl.pallas_call(
        _proj_post_kernel,
        out_shape=jax.ShapeDtypeStruct((b, lt, d), F32),
        grid=(b, lt // tl - r0),
        in_specs=[pl.BlockSpec((1, tl, d), lambda bi, i: (bi, i, 0)),
                  pl.BlockSpec((d, d), lambda bi, i: (0, 0)),
                  pl.BlockSpec((1, tl, d), lambda bi, i: (bi, i + r0, 0)),
                  pl.BlockSpec((1, d), lambda bi, i: (0, 0)),
                  pl.BlockSpec((1, 1, 1, d), lambda bi, i: (bi, jnp.where(i + r0 >= nct, 1, 0), 0, 0))],
        out_specs=pl.BlockSpec((1, tl, d), lambda bi, i: (bi, i + r0, 0)),
        input_output_aliases={2: 0},
        compiler_params=_cparams(("arbitrary", "arbitrary"), 48),
        name="out_proj_post",
    )(a, wout, xa, g.reshape(1, d), gsel)


def _rank_kernel(afft_ref, slot_ref, *, n, cap, tw):
    aff = afft_ref[0]
    bits = lax.bitcast_convert_type(aff, jnp.int32)
    e_num = aff.shape[0]

    def count(mask):
        return jnp.sum(jnp.where(mask, 1.0, 0.0), axis=1, keepdims=True)

    def bit_step(i, thr):
        cand = thr | lax.shift_left(jnp.int32(1), 30 - i)
        return jnp.where(count(bits >= cand) >= cap, cand, thr)

    thr = lax.fori_loop(0, 31, bit_step, jnp.zeros((e_num, 1), jnp.int32))
    gt = bits > thr
    eq = bits == thr
    need = cap - count(gt)
    tri = jnp.where(lax.broadcasted_iota(jnp.int32, (tw, tw), 0) < lax.broadcasted_iota(jnp.int32, (tw, tw), 1),
                    1.0, 0.0).astype(BF16)

    def prefix(mask):
        parts, carry = [], jnp.zeros((e_num, 1), F32)
        for j in range(n // tw):
            m = jnp.where(mask[:, j * tw:(j + 1) * tw], 1.0, 0.0)
            parts.append(jnp.dot(m.astype(BF16), tri, preferred_element_type=F32) + carry)
            carry = carry + jnp.sum(m, axis=1, keepdims=True)
        return jnp.concatenate(parts, axis=1)

    sel = jnp.logical_or(gt, jnp.logical_and(eq, prefix(eq) < need))
    slot_ref[0] = jnp.where(sel, prefix(sel), float(cap)).astype(jnp.int32)


def _rank(afft, cap):
    b, e_num, n = afft.shape
    tw = min(LANES, n)
    return pl.pallas_call(
        functools.partial(_rank_kernel, n=n, cap=cap, tw=tw),
        out_shape=jax.ShapeDtypeStruct((b, e_num, n), jnp.int32),
        grid=(b,),
        in_specs=[pl.BlockSpec((1, e_num, n), lambda bi: (bi, 0, 0))],
        out_specs=pl.BlockSpec((1, e_num, n), lambda bi: (bi, 0, 0)),
        compiler_params=_cparams(("arbitrary",), 32),
        name="ec_rank",
    )(afft)


def _gather_kernel(slot_ref, afft_ref, h_ref, o_ref, g_ref, *, cap):
    e = pl.program_id(1)
    sl = slot_ref[0, pl.ds(e, 1), :]
    hit = lax.broadcasted_iota(jnp.int32, (cap, 1), 0) == sl
    o_ref[0] = jnp.dot(jnp.where(hit, 1.0, 0.0).astype(BF16), h_ref[0], preferred_element_type=F32).astype(BF16)
    gate = jnp.sum(jnp.where(hit, afft_ref[0, pl.ds(e, 1), :], 0.0), axis=1, keepdims=True)
    g_ref[0] = jnp.broadcast_to(gate, (cap, LANES))


def _gather(slot, afft, hs, cap):
    b, e_num, n = slot.shape
    d = hs.shape[2]
    return pl.pallas_call(
        functools.partial(_gather_kernel, cap=cap),
        out_shape=[jax.ShapeDtypeStruct((e_num, b * cap, d), BF16),
                   jax.ShapeDtypeStruct((e_num, b * cap, LANES), F32)],
        grid=(b, e_num),
        in_specs=[pl.BlockSpec((1, e_num, n), lambda bi, e: (bi, 0, 0)),
                  pl.BlockSpec((1, e_num, n), lambda bi, e: (bi, 0, 0)),
                  pl.BlockSpec((1, n, d), lambda bi, e: (bi, 0, 0))],
        out_specs=[pl.BlockSpec((1, cap, d), lambda bi, e: (e, bi, 0)),
                   pl.BlockSpec((1, cap, LANES), lambda bi, e: (e, bi, 0))],
        compiler_params=_cparams(("arbitrary", "arbitrary"), 40),
        name="ec_gather",
    )(slot, afft, hs)


def _ffn_kernel(x_ref, g_ref, w1_ref, w3_ref, w2_ref, o_ref, acc_ref):
    f = pl.program_id(2)
    x = x_ref[0]
    h1 = jnp.dot(x, w1_ref[0, 0].astype(BF16), preferred_element_type=F32)
    h3 = jnp.dot(x, w3_ref[0, 0].astype(BF16), preferred_element_type=F32)
    hid = (_silu(h1) * h3).astype(BF16)

    @pl.when(f == 0)
    def _():
        acc_ref[...] = jnp.zeros_like(acc_ref)

    d = acc_ref.shape[1]
    for n0 in range(0, d, FFN_OUT_CHUNK):
        cols = slice(n0, n0 + FFN_OUT_CHUNK)
        acc_ref[:, cols] += jnp.dot(hid, w2_ref[0, 0, :, cols].astype(BF16), preferred_element_type=F32)

    @pl.when(f == pl.num_programs(2) - 1)
    def _():
        o_ref[0] = (acc_ref[...] * g_ref[0, :, 0:1]).astype(BF16)


def _ffn(xe, gate, w1, w3, w2, layer, tf=512):
    e_num, m, d = xe.shape
    ff = w1.shape[3]
    tm = _pick(m, (1152, 1024, 768, 512, 384, 256, 128, 64, 32, 16, 8))
    once = pl.Buffered(1)
    return pl.pallas_call(
        _ffn_kernel,
        out_shape=jax.ShapeDtypeStruct((e_num, m, d), BF16),
        grid=(e_num, m // tm, ff // tf),
        in_specs=[pl.BlockSpec((1, tm, d), lambda e, i, f: (e, i, 0)),
                  pl.BlockSpec((1, tm, LANES), lambda e, i, f: (e, i, 0)),
                  pl.BlockSpec((1, 1, d, tf), lambda e, i, f: (layer, e, 0, f)),
                  pl.BlockSpec((1, 1, d, tf), lambda e, i, f: (layer, e, 0, f)),
                  pl.BlockSpec((1, 1, tf, d), lambda e, i, f: (layer, e, f, 0))],
        out_specs=pl.BlockSpec((1, tm, d), lambda e, i, f: (e, i, 0), pipeline_mode=once),
        scratch_shapes=[pltpu.VMEM((tm, d), F32)],
        compiler_params=_cparams(("arbitrary", "arbitrary", "arbitrary"), 58),
        name="ec_ffn",
    )(xe, gate, w1, w3, w2)


def _combine_kernel(slot_ref, ye_ref, x_ref, g_ref, gate_ref, o_ref, *, cap, e_num):
    lane = lax.broadcasted_iota(jnp.int32, (1, e_num), 1)
    pos = lax.broadcasted_iota(jnp.int32, (1, cap), 1)
    sl = slot_ref[0]
    y = None
    for e in range(e_num):
        se = jnp.sum(jnp.where(lane == e, sl, 0), axis=1, keepdims=True)
        pt = jnp.where(se == pos, 1.0, 0.0).astype(BF16)
        part = jnp.dot(pt, ye_ref[e], preferred_element_type=F32)
        y = part if y is None else y + part
    yn = y * lax.rsqrt(jnp.mean(y * y, axis=-1, keepdims=True) + NORM_EPS) * g_ref[...]
    o_ref[0] = x_ref[0] + gate_ref[0, 0] * yn


def _combine(slot_c, ye, xa, g, gsel, seg, row0, slot0, cap, final, tt=256):
    b, n, e_num = slot_c.shape
    d = xa.shape[2]
    tt = _pick(math.gcd(n, row0), (tt, 128, 64))
    assert row0 % tt == 0 and slot0 % cap == 0
    r0, s0 = row0 // tt, slot0 // cap
    return pl.pallas_call(
        functools.partial(_combine_kernel, cap=cap, e_num=e_num),
        out_shape=jax.ShapeDtypeStruct((b, n, d) if final else xa.shape, F32),
        grid=(b, n // tt),
        in_specs=[pl.BlockSpec((1, tt, e_num), lambda bi, j: (bi, j, 0)),
                  pl.BlockSpec((e_num, cap, d), lambda bi, j: (0, s0 + bi, 0)),
                  pl.BlockSpec((1, tt, d), lambda bi, j: (bi, r0 + j, 0)),
                  pl.BlockSpec((1, d), lambda bi, j: (0, 0)),
                  pl.BlockSpec((1, 1, 1, d), lambda bi, j: (bi, seg, 0, 0))],
        out_specs=pl.BlockSpec((1, tt, d), lambda bi, j: (bi, (0 if final else r0) + j, 0)),
        input_output_aliases={} if final else {2: 0},
        compiler_params=_cparams(("arbitrary", "arbitrary"), 56),
        name="ec_combine_post",
    )(slot_c, ye, xa, g.reshape(1, d), gsel)


def _sel(mod_c, mod_x, idx):
    b = mod_x.shape[0]
    mc = jnp.broadcast_to(mod_c[jnp.array(idx)][None], (b, len(idx), mod_c.shape[-1]))
    return jnp.stack([mc, mod_x[:, jnp.array(idx)]], axis=1)


def _layer(xa, nc, mod_x, mod_c, p, lam_init, hg_lb, ct, st, with_ctx):
    b, lt, d = xa.shape
    w = d // 2
    a_cols = 3 * w + 2 * RW_DECAY_RANK + 2 * RW_ICL_RANK + RW_GATE_RANK
    col_b = a_cols
    col_c = col_b + 3 * w
    col_g = col_c + 5 * w

    h = _norm_mod(xa, p['norm_g'][0], _sel(mod_c, mod_x, (0, 1)), nc)
    h2 = h.reshape(b * lt, d)
    w_in = p['w_in']
    za = _matmul(h2, w_in[:, :col_b].astype(BF16), F32, tn=a_cols // 3).reshape(b, lt, a_cols)
    zc = _matmul(h2, w_in[:, col_c:col_g].astype(BF16), F32).reshape(b, lt, 5 * w)
    gl = _matmul(h2, w_in[:, col_g:].astype(BF16), BF16).reshape(b, lt, 3 * d)

    streams, ga = _rw_prep(za, nc, p['rw_mu'], p['rw_w0'], p['rw_wB'], p['rw_a0'], p['rw_aB'], p['rw_gB'])
    oaf, oab = _rw_chunk(streams, nc, p['rw_kk'], p['rw_ka'], p['rw_rk'], p['rw_gn'])

    zbr = _proj_rope(h2, w_in[:, col_b:col_c].astype(BF16), ct, st, lt).reshape(b, lt, 3 * w)
    yb = _attention(zbr, nc, p['da_lambda'], p['da_subln'], lam_init, with_ctx)

    of = _hgrn(zc, nc, hg_lb, rev=False)
    ob = _hgrn(zc, nc, hg_lb, rev=True)

    hn = jnp.tile(p['hg_norm'], w // HG_EXPAND).reshape(1, w)
    skip = 0 if with_ctx else nc
    merged = _merge(oaf, oab, ga, yb, of, ob, zc, hn, gl, p['w_branch'].astype(BF16), skip)
    xa = _proj_post(merged, p['w_out'].astype(BF16), xa, p['norm_g'][1], _sel(mod_c, mod_x, (2,)), nc, skip)

    hm, afft = _norm_mod(xa, p['norm_g'][2], _sel(mod_c, mod_x, (3, 4)), nc, router_t=p['moe_router'].T, row0=skip)
    gsel = _sel(mod_c, mod_x, (5,))
    sets = [(nc, lt - nc, 1)] + ([(0, nc, 0)] if with_ctx else [])
    routed, xes, gates = [], [], []
    for row0, nn, seg in sets:
        cap = EC_CAPACITY_FACTOR * nn // N_EXPERTS
        at = afft[:, :, row0 - skip:row0 - skip + nn]
        slot = _rank(at, cap)
        xe_s, gate_s = _gather(slot, at, hm[:, row0 - skip:row0 - skip + nn], cap)
        xes.append(xe_s)
        gates.append(gate_s)
        routed.append((jnp.swapaxes(slot, 1, 2), row0, seg, cap))
    xe = xes[0] if len(xes) == 1 else jnp.concatenate(xes, axis=1)
    gate = gates[0] if len(gates) == 1 else jnp.concatenate(gates, axis=1)
    ye = _ffn(xe, gate, p['moe_w1'], p['moe_w3'], p['moe_w2'], p['layer'])
    slot0 = 0
    for slot_c, row0, seg, cap in routed:
        xa = _combine(slot_c, ye, xa, p['norm_g'][3], gsel, seg, row0, slot0, cap, final=not with_ctx)
        slot0 += b * cap
    return xa


def kernel(x, c, ctx, c_ctx, w_ada, b_ada, norm_g, w_in, rw_mu, rw_w0, rw_wB, rw_a0, rw_aB, rw_gB, rw_kk, rw_ka, rw_rk, rw_gn, da_lambda, da_subln, hg_lb_logits, hg_norm, w_branch, w_out, moe_router, moe_w1, moe_w3, moe_w2):
    b, seq, d = x.shape
    nc = ctx.shape[1]
    depth = w_ada.shape[0]
    ct, st = _rope_tables(seq, nc)
    lb_w = jax.nn.softmax(hg_lb_logits.astype(F32), axis=0)
    hg_lb = jnp.cumsum(lb_w, axis=0) - lb_w[0]
    rows = ((b + 1 + SUBLANES - 1) // SUBLANES) * SUBLANES
    cc = jnp.zeros((rows, d), F32).at[:b].set(c).at[b].set(c_ctx)
    mod = _modulation(cc, w_ada, b_ada)
    xa = jnp.concatenate([ctx, x], axis=1)
    for l in range(depth):
        p = dict(norm_g=norm_g[l], w_in=w_in[l], rw_mu=rw_mu[l], rw_w0=rw_w0[l], rw_wB=rw_wB[l],
                 rw_a0=rw_a0[l], rw_aB=rw_aB[l], rw_gB=rw_gB[l], rw_kk=rw_kk[l], rw_ka=rw_ka[l],
                 rw_rk=rw_rk[l], rw_gn=rw_gn[l], da_lambda=da_lambda[l], da_subln=da_subln[l],
                 hg_norm=hg_norm[l], w_branch=w_branch[l], w_out=w_out[l], moe_router=moe_router[l],
                 moe_w1=moe_w1, moe_w3=moe_w3, moe_w2=moe_w2, layer=l)
        mod_x = mod[l, :b].reshape(b, 6, d)
        mod_c = mod[l, b].reshape(6, d)
        lam_init = 0.8 - 0.6 * math.exp(-0.3 * l)
        xa = _layer(xa, nc, mod_x, mod_c, p, lam_init, hg_lb[l], ct, st, with_ctx=l < depth - 1)
    return xa
```

```python
import functools
import math

import jax
import jax.numpy as jnp
from jax import lax
from jax.experimental import pallas as pl
from jax.experimental.pallas import tpu as pltpu

F32 = jnp.float32
BF16 = jnp.bfloat16
HIGHEST = lax.Precision.HIGHEST

GRID_W = 64
RW_HEAD_DIM = 64
RW_DECAY_RANK = 64
RW_ICL_RANK = 64
RW_GATE_RANK = 128
RW_GN_EPS = 64e-5
DA_QK_DIM = 64
DA_V_DIM = 128
ROPE_THETA = 10000.0
ROPE_AXIS_FREQS = 16
DA_SUBLN_EPS = 1e-5
HG_EXPAND = 128
N_EXPERTS = 16
EC_CAPACITY_FACTOR = 2
NORM_EPS = 1e-6

LANES = 128
SUBLANES = 8
V7X_VMEM_BYTES = 64 * 1024 * 1024
HG_CHUNK = 64
RW_CHUNK = 64
RW_GROUP = 256
RW_SAMPLES = 4
FFN_OUT_CHUNK = 512


def _cparams(sem, vmem_mb):
    limit = int(vmem_mb * 1024 * 1024)
    assert limit < V7X_VMEM_BYTES
    return pltpu.CompilerParams(dimension_semantics=sem, vmem_limit_bytes=limit)


def _sigmoid(x):
    return 1.0 / (1.0 + jnp.exp(-x))


def _sigmoid_t(x):
    return 0.5 * jnp.tanh(0.5 * x) + 0.5


def _silu(x):
    return x * _sigmoid_t(x)


def _mod_kernel(c_ref, w_ref, b_ref, o_ref):
    a = _silu(c_ref[...]).astype(BF16)
    o_ref[0] = jnp.dot(a, w_ref[0].astype(BF16), preferred_element_type=F32) + b_ref[0]


def _modulation(cc, w_ada, b_ada):
    depth, d, n = w_ada.shape
    rows = cc.shape[0]
    tn = 1024
    return pl.pallas_call(
        _mod_kernel,
        out_shape=jax.ShapeDtypeStruct((depth, rows, n), F32),
        grid=(depth, n // tn),
        in_specs=[pl.BlockSpec((rows, d), lambda l, j: (0, 0)),
                  pl.BlockSpec((1, d, tn), lambda l, j: (l, 0, j)),
                  pl.BlockSpec((1, 1, tn), lambda l, j: (l, 0, j))],
        out_specs=pl.BlockSpec((1, rows, tn), lambda l, j: (l, 0, j)),
        compiler_params=_cparams(("arbitrary", "arbitrary"), 40),
        name="adaln_mod",
    )(cc, w_ada, b_ada.reshape(depth, 1, n))


def _norm_mod_kernel(x_ref, g_ref, ms_ref, *rest, with_router):
    x = x_ref[0]
    xn = x * lax.rsqrt(jnp.mean(x * x, axis=-1, keepdims=True) + NORM_EPS) * g_ref[...]
    h = xn * (1.0 + ms_ref[0, 0, 1:2, :]) + ms_ref[0, 0, 0:1, :]
    if with_router:
        rt_ref, h_ref, aff_ref = rest
        h_ref[0] = h.astype(BF16)
        lt = lax.dot_general(rt_ref[...], h, (((1,), (1,)), ((), ())),
                             precision=HIGHEST, preferred_element_type=F32)
        e = jnp.exp(lt - jnp.max(lt, axis=0, keepdims=True))
        aff_ref[0] = e / jnp.sum(e, axis=0, keepdims=True)
    else:
        (h_ref,) = rest
        h_ref[0] = h.astype(BF16)


def _norm_mod(xa, g, msel, nc, router_t=None, row0=0, tl=256):
    b, lt, d = xa.shape
    tl = min(tl, nc)
    nct = nc // tl
    r0 = row0 // tl
    with_router = router_t is not None
    in_specs = [pl.BlockSpec((1, tl, d), lambda bi, i: (bi, i + r0, 0)),
                pl.BlockSpec((1, d), lambda bi, i: (0, 0)),
                pl.BlockSpec((1, 1, 2, d), lambda bi, i: (bi, jnp.where(i + r0 >= nct, 1, 0), 0, 0))]
    args = [xa, g.reshape(1, d), msel]
    out_shape = [jax.ShapeDtypeStruct((b, lt - row0, d), BF16)]
    out_specs = [pl.BlockSpec((1, tl, d), lambda bi, i: (bi, i, 0))]
    if with_router:
        e = router_t.shape[0]
        in_specs.append(pl.BlockSpec((e, d), lambda bi, i: (0, 0)))
        args.append(router_t)
        out_shape.append(jax.ShapeDtypeStruct((b, e, lt - row0), F32))
        out_specs.append(pl.BlockSpec((1, e, tl), lambda bi, i: (bi, 0, i)))
    res = pl.pallas_call(
        functools.partial(_norm_mod_kernel, with_router=with_router),
        out_shape=out_shape, grid=(b, lt // tl - r0), in_specs=in_specs, out_specs=out_specs,
        compiler_params=_cparams(("arbitrary", "arbitrary"), 32),
        name="norm_mod_router" if with_router else "norm_mod",
    )(*args)
    return res if with_router else res[0]


def _matmul_kernel(a_ref, w_ref, o_ref):
    o_ref[...] = jnp.dot(a_ref[...], w_ref[...], preferred_element_type=F32).astype(o_ref.dtype)


def _pick(n, cands):
    for c in cands:
        if n % c == 0:
            return c
    return n


def _matmul(a, w, out_dtype, tm=1024, tn=1024):
    m, k = a.shape
    n = w.shape[1]
    tm = _pick(m, (tm, 768, 512, 384, 256, 128))
    tn = _pick(n, (tn, 768, 512, 384, 256, 128))
    return pl.pallas_call(
        _matmul_kernel,
        out_shape=jax.ShapeDtypeStruct((m, n), out_dtype),
        grid=(m // tm, n // tn),
        in_specs=[pl.BlockSpec((tm, k), lambda i, j: (i, 0)),
                  pl.BlockSpec((k, tn), lambda i, j: (0, j))],
        out_specs=pl.BlockSpec((tm, tn), lambda i, j: (i, j)),
        compiler_params=_cparams(("arbitrary", "arbitrary"), 48),
        name="matmul",
    )(a, w)


def _rw_prep_kernel(z_ref, zp_ref, zn_ref, mu_ref, w0_ref, wb_ref, a0_ref, ab_ref, gb_ref,
                    o_ref, g_ref, *, tl, nct, nt, w):
    i = pl.program_id(1)
    z = z_ref[0]
    row = lax.broadcasted_iota(jnp.int32, (tl, 1), 0)
    seg_start = jnp.logical_or(i == 0, i == nct)
    seg_end = jnp.logical_or(i == nct - 1, i == nt - 1)
    prev_row = jnp.where(seg_start, 0.0, zp_ref[0, SUBLANES - 1:SUBLANES, :])
    next_row = jnp.where(seg_end, 0.0, zn_ref[0, 0:1, :])
    prev = jnp.where(row == 0, prev_row, pltpu.roll(z, 1, 0))
    nxt = jnp.where(row == tl - 1, next_row, pltpu.roll(z, tl - 1, 0))
    zs = z + mu_ref[0:1, :] * (prev - z) + mu_ref[1:2, :] * (nxt - z)
    o_ref[0, 0] = zs[:, 0:w]
    o_ref[0, 1] = zs[:, w:2 * w]
    o_ref[0, 2] = zs[:, 2 * w:3 * w]
    o4 = 3 * w + 2 * RW_DECAY_RANK
    o5 = o4 + 2 * RW_ICL_RANK
    wd = jnp.tanh(zs[:, 3 * w:o4])
    wl = jnp.dot(wd.astype(BF16), wb_ref[...], preferred_element_type=F32) + w0_ref[...]
    dec = -math.exp(-0.5) * _sigmoid_t(wl)
    o_ref[0, 3] = dec[:, 0:w]
    o_ref[0, 4] = dec[:, w:2 * w]
    al = jnp.dot(zs[:, o4:o5].astype(BF16), ab_ref[...], preferred_element_type=F32) + a0_ref[...]
    av = _sigmoid_t(al)
    o_ref[0, 5] = av[:, 0:w]
    o_ref[0, 6] = av[:, w:2 * w]
    g_ref[0] = jnp.dot(_sigmoid_t(zs[:, o5:]).astype(BF16), gb_ref[...], preferred_element_type=F32)


def _blockdiag2(m):
    r, w = m.shape[1], m.shape[2]
    z = jnp.zeros((r, w), m.dtype)
    return jnp.concatenate([jnp.concatenate([m[0], z], axis=1), jnp.concatenate([z, m[1]], axis=1)], axis=0)


def _rw_prep(za, nc, mu, w0, wb, a0, ab, gb, tl=256):
    b, lt, acols = za.shape
    w = w0.shape[1]
    tl = min(tl, nc)
    nct, nt = nc // tl, lt // tl
    r8 = tl // SUBLANES
    nb8 = lt // SUBLANES
    full = lambda shp: pl.BlockSpec(shp, lambda bi, i: tuple(0 for _ in shp))
    kern = functools.partial(_rw_prep_kernel, tl=tl, nct=nct, nt=nt, w=w)
    return pl.pallas_call(
        kern,
        out_shape=[jax.ShapeDtypeStruct((b, 7, lt, w), F32), jax.ShapeDtypeStruct((b, lt, w), F32)],
        grid=(b, nt),
        in_specs=[pl.BlockSpec((1, tl, acols), lambda bi, i: (bi, i, 0)),
                  pl.BlockSpec((1, SUBLANES, acols), lambda bi, i: (bi, jnp.maximum(i * r8 - 1, 0), 0)),
                  pl.BlockSpec((1, SUBLANES, acols), lambda bi, i: (bi, jnp.minimum((i + 1) * r8, nb8 - 1), 0)),
                  full((2, acols)), full((1, 2 * w)), full((2 * RW_DECAY_RANK, 2 * w)),
                  full((1, 2 * w)), full((2 * RW_ICL_RANK, 2 * w)), full((RW_GATE_RANK, w))],
        out_specs=[pl.BlockSpec((1, 7, tl, w), lambda bi, i: (bi, 0, i, 0)),
                   pl.BlockSpec((1, tl, w), lambda bi, i: (bi, i, 0))],
        compiler_params=_cparams(("arbitrary", "arbitrary"), 48),
        name="rwkv_prep",
    )(za, za, za, mu, w0.reshape(1, 2 * w), _blockdiag2(wb).astype(BF16), a0.reshape(1, 2 * w),
      _blockdiag2(ab).astype(BF16), gb.astype(BF16))


def _rw_chunk_kernel(xf_ref, lwf_ref, af_ref, xb_ref, lwb_ref, ab_ref, kk_ref, ka_ref, rk_ref, gn0_ref, gn1_ref,
                     of_ref, ob_ref, s_ref, *, c, w, nbt):
    j = pl.program_id(1)

    @pl.when(j == 0)
    def _():
        s_ref[...] = jnp.zeros_like(s_ref)

    gw, n = RW_GROUP, RW_HEAD_DIM
    ng = w // gw
    row = lax.broadcasted_iota(jnp.int32, (c, 1), 0)
    pos = lax.broadcasted_iota(jnp.int32, (1, gw), 1) % n
    bdmask = (lax.broadcasted_iota(jnp.int32, (gw, 1), 0) // n) == (lax.broadcasted_iota(jnp.int32, (1, gw), 1) // n)
    ones_bd = jnp.where(bdmask, 1.0, 0.0).astype(BF16)
    eye = jnp.where(pos == row, 1.0, 0.0)
    nt = (((1,), (1,)), ((), ()))
    tn = (((0,), (0,)), ((), ()))
    chains = [(bb, d, g) for bb in range(nbt) for d in range(2) for g in range(ng)]
    refs = ((xf_ref, lwf_ref, af_ref, of_ref), (xb_ref, lwb_ref, ab_ref, ob_ref))

    def bd(x):
        return jnp.where(bdmask, jnp.concatenate([x] * (gw // c), axis=0), 0.0).astype(BF16)

    def gsum(xs):
        pieces = []
        for x in xs:
            hi = x.astype(BF16)
            r1 = x - hi.astype(F32)
            mid = r1.astype(BF16)
            pieces += [hi, mid, (r1 - mid.astype(F32)).astype(BF16)]
        tot = jnp.dot(jnp.concatenate(pieces, axis=0), ones_bd, preferred_element_type=F32)
        return [tot[3 * i * c:(3 * i + 1) * c] + tot[(3 * i + 1) * c:(3 * i + 2) * c] + tot[(3 * i + 2) * c:(3 * i + 3) * c]
                for i in range(len(xs))]

    def mm(x, ybd):
        return jnp.dot(x.astype(BF16), ybd, preferred_element_type=F32)

    st = []
    for bb, d, g in chains:
        x_ref, lw_ref, a_ref, _ = refs[d]
        sl = slice(g * gw, (g + 1) * gw)
        q = dict(sl=sl, rev=d == 1, r=x_ref[bb, 0, :, sl], k=x_ref[bb, 1, :, sl], v=x_ref[bb, 2, :, sl],
                 lw=lw_ref[bb, 0, :, sl], a=a_ref[bb, 0, :, sl])
        q['kx'] = q['k'] * kk_ref[:, sl]
        q['kt'] = q['k'] * (1.0 + (q['a'] - 1.0) * ka_ref[:, sl])
        st.append(q)
    for q in st:
        q['ss'], q['bon'] = gsum([q['kx'] * q['kx'], q['r'] * q['kt'] * rk_ref[:, q['sl']]])
    for q in st:
        rev = q['rev']
        kk = q['kx'] / jnp.maximum(jnp.sqrt(q['ss']), 1e-12)
        q['b'] = q['a'] * kk
        cw = q['lw']
        sft = 1
        while sft < c:
            if rev:
                cw = cw + jnp.where(row < c - sft, pltpu.roll(cw, c - sft, 0), 0.0)
            else:
                cw = cw + jnp.where(row >= sft, pltpu.roll(cw, sft, 0), 0.0)
            sft *= 2
        q['tot'] = cw[0:1] if rev else cw[c - 1:c]
        invp = jnp.exp(-cw)
        q['e2'] = jnp.exp(q['tot'] - cw)
        q['kr'] = jnp.concatenate([kk * jnp.exp(cw - q['lw']), q['r'] * jnp.exp(cw)], axis=0).astype(BF16)
        q['ktb'] = bd(q['kt'] * invp)
        q['bb'] = bd(q['b'] * invp)
        q['vbd'] = bd(q['v'])
    for q in st:
        strict = (pos > row) if q['rev'] else (pos < row)
        incl = (pos >= row) if q['rev'] else (pos <= row)
        g1 = lax.dot_general(q['kr'], q['ktb'], nt, preferred_element_type=F32)
        g2 = lax.dot_general(q['kr'], q['bb'], nt, preferred_element_type=F32)
        q['m_kt'] = jnp.where(strict, g1[:c], 0.0)
        q['n_kt'] = jnp.where(incl, g1[c:], 0.0)
        q['x'] = jnp.where(strict, g2[:c], 0.0)
        q['n_b'] = jnp.where(incl, g2[c:], 0.0)
        q['t'] = eye - q['x']
        q['xb'] = bd(q['x'])
    for q in st:
        mn = mm(jnp.concatenate([q['m_kt'], q['n_kt']], axis=0), q['vbd'])
        q['mkv'], q['nkv'] = mn[:c], mn[c:]
    lv = 2
    while lv < c:
        for q in st:
            xt = mm(jnp.concatenate([q['x'], q['t']], axis=0), q['xb'])
            q['x'], q['tx'] = xt[:c], xt[c:]
        for q in st:
            q['t'] = q['t'] + mm(q['tx'], q['xb'])
        lv *= 2
        if lv < c:
            for q in st:
                q['xb'] = bd(q['x'])
    for (bb, d, g), q in zip(chains, st):
        q['s0'] = s_ref[bb, d, g]
        q['ka'] = lax.dot_general(q['kr'], q['s0'].astype(BF16), nt, preferred_element_type=F32)
    for q in st:
        q['u'] = mm(q['t'], bd(q['ka'][:c] + q['mkv']))
    for (bb, d, g), q in zip(chains, st):
        e2 = q['e2']
        upd = lax.dot_general(jnp.concatenate([q['v'], q['u']], axis=0).astype(BF16),
                              jnp.concatenate([q['kt'] * e2, -(q['b'] * e2)], axis=0).astype(BF16),
                              tn, preferred_element_type=F32)
        s_ref[bb, d, g] = q['s0'] * jnp.exp(q['tot']) + jnp.where(bdmask, upd, 0.0)
    for q in st:
        q['y'] = q['ka'][c:] + q['nkv'] - mm(q['n_b'], bd(q['u']))
    for q in st:
        (mu,) = gsum([q['y']])
        q['dl'] = q['y'] - mu * (1.0 / n)
    for (bb, d, g), q in zip(chains, st):
        (var,) = gsum([q['dl'] * q['dl']])
        sl = q['sl']
        yn = q['dl'] * lax.rsqrt(var * (1.0 / n) + RW_GN_EPS) * gn0_ref[:, sl] + gn1_ref[:, sl]
        refs[d][3][bb, :, sl] = (yn + q['bon'] * q['v']).astype(BF16)


def _rw_chunk(streams, nc, kk, ka, rk, gn):
    b, _, lt, w = streams.shape
    c = RW_CHUNK
    ncb, nb = nc // c, lt // c
    nbt = RW_SAMPLES if b % RW_SAMPLES == 0 else 1

    def rblk(j):
        return jnp.where(j < ncb, ncb - 1 - j, nb - 1 - (j - ncb))

    par = pl.BlockSpec((1, w), lambda bi, j: (0, 0))
    shp = jax.ShapeDtypeStruct((b, lt, w), BF16)
    return pl.pallas_call(
        functools.partial(_rw_chunk_kernel, c=c, w=w, nbt=nbt),
        out_shape=[shp, shp],
        grid=(b // nbt, nb),
        in_specs=[pl.BlockSpec((nbt, 3, c, w), lambda bi, j: (bi, 0, j, 0)),
                  pl.BlockSpec((nbt, 1, c, w), lambda bi, j: (bi, 3, j, 0)),
                  pl.BlockSpec((nbt, 1, c, w), lambda bi, j: (bi, 5, j, 0)),
                  pl.BlockSpec((nbt, 3, c, w), lambda bi, j: (bi, 0, rblk(j), 0)),
                  pl.BlockSpec((nbt, 1, c, w), lambda bi, j: (bi, 4, rblk(j), 0)),
                  pl.BlockSpec((nbt, 1, c, w), lambda bi, j: (bi, 6, rblk(j), 0)),
                  par, par, par, par, par],
        out_specs=[pl.BlockSpec((nbt, c, w), lambda bi, j: (bi, j, 0)),
                   pl.BlockSpec((nbt, c, w), lambda bi, j: (bi, rblk(j), 0))],
        scratch_shapes=[pltpu.VMEM((nbt, 2, w // RW_GROUP, RW_GROUP, RW_GROUP), F32)],
        compiler_params=_cparams(("arbitrary", "arbitrary"), 58),
        name="rwkv_chunk",
    )(streams, streams, streams, streams, streams, streams, kk.reshape(1, w), ka.reshape(1, w), rk.reshape(1, w),
      gn[0].reshape(1, w), gn[1].reshape(1, w))


def _proj_rope_kernel(a_ref, w_ref, c_ref, s_ref, o_ref):
    j = pl.program_id(1)
    acc = jnp.dot(a_ref[...], w_ref[...], preferred_element_type=F32)

    @pl.when(j < 2)
    def _():
        lane = lax.broadcasted_iota(jnp.int32, (1, LANES), 1)
        first_half = (lane % (2 * ROPE_AXIS_FREQS)) < ROPE_AXIS_FREQS
        c = c_ref[...] * jnp.where(j == 0, DA_QK_DIM ** -0.5 * math.log2(math.e), 1.0)
        s = s_ref[...] * jnp.where(j == 0, DA_QK_DIM ** -0.5 * math.log2(math.e), 1.0)
        for g in range(acc.shape[1] // LANES):
            xs = acc[:, g * LANES:(g + 1) * LANES]
            up = pltpu.roll(xs, LANES - ROPE_AXIS_FREQS, 1)
            dn = pltpu.roll(xs, ROPE_AXIS_FREQS, 1)
            o_ref[:, g * LANES:(g + 1) * LANES] = (xs * c + jnp.where(first_half, up, dn) * s).astype(o_ref.dtype)

    @pl.when(j >= 2)
    def _():
        o_ref[...] = acc.astype(o_ref.dtype)


def _proj_rope(a, w, ct, st, lt):
    m, k = a.shape
    n = w.shape[1]
    tn = n // 3
    tm = _pick(lt, (768, 640, 512, 384, 256, 128))
    per = lt // tm
    return pl.pallas_call(
        _proj_rope_kernel,
        out_shape=jax.ShapeDtypeStruct((m, n), BF16),
        grid=(m // tm, 3),
        in_specs=[pl.BlockSpec((tm, k), lambda i, j: (i, 0)),
                  pl.BlockSpec((k, tn), lambda i, j: (0, j)),
                  pl.BlockSpec((tm, LANES), lambda i, j: (i % per, 0)),
                  pl.BlockSpec((tm, LANES), lambda i, j: (i % per, 0))],
        out_specs=pl.BlockSpec((tm, tn), lambda i, j: (i, j)),
        compiler_params=_cparams(("arbitrary", "arbitrary"), 48),
        name="proj_rope",
    )(a, w, ct, st)


def _rope_tables(seq, nc):
    rows = seq // GRID_W
    row = jnp.broadcast_to(jnp.arange(rows)[:, None], (rows, GRID_W)).reshape(seq)
    col = jnp.broadcast_to(jnp.arange(GRID_W)[None, :], (rows, GRID_W)).reshape(seq)
    inv = 1.0 / (ROPE_THETA ** (jnp.arange(ROPE_AXIS_FREQS, dtype=F32) / ROPE_AXIS_FREQS))
    ang = jnp.stack([row, col], axis=-1).astype(F32)[:, :, None] * inv
    cos, sin = jnp.cos(ang), jnp.sin(ang)
    c64 = jnp.concatenate([cos[:, 0], cos[:, 0], cos[:, 1], cos[:, 1]], axis=-1)
    s64 = jnp.concatenate([-sin[:, 0], sin[:, 0], -sin[:, 1], sin[:, 1]], axis=-1)
    ct = jnp.concatenate([jnp.ones((nc, LANES), F32), jnp.tile(c64, (1, 2))], axis=0)
    st = jnp.concatenate([jnp.zeros((nc, LANES), F32), jnp.tile(s64, (1, 2))], axis=0)
    return ct, st


def _attn_kernel(q_ref, k_ref, v_ref, lam_ref, g_ref, o_ref, *, nct, nc, lam_init, hp, qoff):
    i = pl.program_id(2) + qoff
    lane = lax.broadcasted_iota(jnp.int32, (1, LANES), 1)
    lp = lam_ref[...]
    lam = (jnp.exp(jnp.sum(lp[0:1] * lp[1:2], axis=1, keepdims=True))
           - jnp.exp(jnp.sum(lp[2:3] * lp[3:4], axis=1, keepdims=True)) + lam_init)

    def attend(nk):
        for h in range(hp):
            cols = slice(h * DA_V_DIM, (h + 1) * DA_V_DIM)
            q = q_ref[0, :, cols]
            k = k_ref[0, 0:nk, cols]
            v1 = jnp.concatenate([v_ref[0, 0:nk, cols], jnp.ones((nk, LANES), BF16)], axis=1)

            def one(sel):
                qs = jnp.where(sel, q, jnp.zeros_like(q))
                s = lax.dot_general(qs, k, (((1,), (1,)), ((), ())), preferred_element_type=F32)
                e = jnp.exp2(s - jnp.max(s, axis=-1, keepdims=True))
                ov = jnp.dot(e.astype(BF16), v1, preferred_element_type=F32)
                return ov[:, :DA_V_DIM] / ov[:, DA_V_DIM:]

            o = one(lane < DA_QK_DIM) - lam * one(lane >= DA_QK_DIM)
            on = o * lax.rsqrt(jnp.mean(o * o, axis=-1, keepdims=True) + DA_SUBLN_EPS) * g_ref[...]
            o_ref[0, :, cols] = (on * (1.0 - lam_init)).astype(o_ref.dtype)

    @pl.when(i < nct)
    def _():
        attend(nc)

    @pl.when(i >= nct)
    def _():
        attend(k_ref.shape[1])


def _attention(zbr, nc, lam_p, subln, lam_init, with_ctx, hp=4):
    b, lt, cols = zbr.shape
    w = cols // 3
    heads = w // DA_V_DIM
    tq = min(256, nc)
    nct = nc // tq
    hg = heads // hp
    bw = hp * DA_V_DIM
    qoff = 0 if with_ctx else nct
    kern = functools.partial(_attn_kernel, nct=nct, nc=nc, lam_init=lam_init, hp=hp, qoff=qoff)
    return pl.pallas_call(
        kern,
        out_shape=jax.ShapeDtypeStruct((b, lt - qoff * tq, w), BF16),
        grid=(b, hg, lt // tq - qoff),
        in_specs=[pl.BlockSpec((1, tq, bw), lambda bi, h, i: (bi, i + qoff, h)),
                  pl.BlockSpec((1, lt, bw), lambda bi, h, i: (bi, 0, hg + h)),
                  pl.BlockSpec((1, lt, bw), lambda bi, h, i: (bi, 0, 2 * hg + h)),
                  pl.BlockSpec((4, DA_QK_DIM), lambda bi, h, i: (0, 0)),
                  pl.BlockSpec((1, DA_V_DIM), lambda bi, h, i: (0, 0))],
        out_specs=pl.BlockSpec((1, tq, bw), lambda bi, h, i: (bi, i, h)),
        compiler_params=_cparams(("arbitrary", "arbitrary", "arbitrary"), 58),
        name="diff_attn",
    )(zbr, zbr, zbr, lam_p, subln.reshape(1, DA_V_DIM))


def _seg_masks(c, e):
    row = lax.broadcasted_iota(jnp.int32, (c, e), 0)
    mk = {}
    h = 1
    while h < min(SUBLANES, c):
        odd = (row % (2 * h)) >= h
        for s in range(1, h + 1):
            mk[(h, s, 0)] = jnp.where(jnp.logical_and(odd, (row % h) == s - 1), 1.0, 0.0)
            mk[(h, s, 1)] = jnp.where(jnp.logical_and(~odd, (row % h) == h - s), 1.0, 0.0)
        h *= 2
    return mk


def _seg_scans(lf, mk, c):
    cs, ss = {1: lf}, {1: lf}
    h = 1
    while h < c:
        x, y = cs[h], ss[h]
        if h < SUBLANES:
            x3 = x.reshape(c // SUBLANES, SUBLANES, x.shape[1])
            y3 = y.reshape(c // SUBLANES, SUBLANES, y.shape[1])
            for s in range(1, h + 1):
                x = x + mk[(h, s, 0)] * pltpu.roll(x3, s, 1).reshape(x.shape)
                y = y + mk[(h, s, 1)] * pltpu.roll(y3, SUBLANES - s, 1).reshape(y.shape)
            cs[2 * h], ss[2 * h] = x, y
        else:
            px = [x[j * h:(j + 1) * h] for j in range(c // h)]
            py = [y[j * h:(j + 1) * h] for j in range(c // h)]
            nx = [px[j] + px[j - 1][h - 1:h] if j % 2 == 1 else px[j] for j in range(c // h)]
            ny = [py[j] + py[j + 1][0:1] if j % 2 == 0 else py[j] for j in range(c // h)]
            cs[2 * h], ss[2 * h] = jnp.concatenate(nx, axis=0), jnp.concatenate(ny, axis=0)
        h *= 2
    return cs, ss


def _hg_kernel(q_ref, f_ref, v_ref, lb_ref, o_ref, st_ref, *, tc, c, heads, rev):
    j = pl.program_id(1)

    @pl.when(j == 0)
    def _():
        st_ref[...] = jnp.zeros_like(st_ref)

    ti = lax.broadcasted_iota(jnp.int32, (c, c), 0)
    si = lax.broadcasted_iota(jnp.int32, (c, c), 1)
    nt = (((1,), (1,)), ((), ()))
    tn = (((0,), (0,)), ((), ()))
    mk = _seg_masks(c, HG_EXPAND)
    masks, hh = {}, 1
    while hh < c:
        tb, sb = ti // hh, si // hh
        masks[hh] = (jnp.logical_and(tb % 2 == 0, sb == tb + 1) if rev
                     else jnp.logical_and(tb % 2 == 1, sb == tb - 1))
        hh *= 2

    def head(h, carry):
        cols = pl.ds(pl.multiple_of(h * HG_EXPAND, HG_EXPAND), HG_EXPAND)
        lb = lb_ref[:, cols]
        chunks = range(tc // c)
        for ci in (reversed(chunks) if rev else chunks):
            rows = slice(ci * c, (ci + 1) * c)
            q = _silu(q_ref[0, rows, cols])
            f = lb + (1.0 - lb) * _sigmoid(f_ref[0, rows, cols])
            k = 1.0 - f
            lf = jnp.log(f)
            v = v_ref[0, rows, cols].astype(BF16)
            cs, ss = _seg_scans(lf, mk, c)
            qs, ks = (ss, cs) if rev else (cs, ss)
            att = jnp.where(ti == si, lax.dot_general(q.astype(BF16), k.astype(BF16), nt,
                                                      preferred_element_type=F32), 0.0)
            hh = 1
            while hh < c:
                qe = (q * jnp.exp(qs[hh])).astype(BF16)
                ke = (k * jnp.exp(ks[hh] - lf)).astype(BF16)
                a = lax.dot_general(qe, ke, nt, preferred_element_type=F32)
                att = att + jnp.where(masks[hh], a, 0.0)
                hh *= 2
            st = st_ref[h]
            qe = (q * jnp.exp(qs[c])).astype(BF16)
            o = lax.dot_general(qe, st.astype(BF16), nt, preferred_element_type=F32)
            o = o + jnp.dot(att.astype(BF16), v, preferred_element_type=F32)
            o_ref[0, rows, cols] = o.astype(BF16)
            ke = (k * jnp.exp(ks[c] - lf)).astype(BF16)
            tot = qs[c][0:1] if rev else qs[c][c - 1:c]
            st_ref[h] = st * jnp.exp(tot) + lax.dot_general(v, ke, tn, preferred_element_type=F32)
        return carry

    lax.fori_loop(0, heads, head, 0, unroll=True)


def _hgrn(zc, nc, lb, rev):
    b, lt, cols = zc.shape
    w = cols // 5
    heads = w // HG_EXPAND
    tc = min(256, nc)
    c = min(HG_CHUNK, tc)
    ncb, nb = nc // tc, lt // tc

    def blk(j):
        return jnp.where(j < ncb, ncb - 1 - j, nb - 1 - (j - ncb)) if rev else j

    fcol = 2 if rev else 1
    return pl.pallas_call(
        functools.partial(_hg_kernel, tc=tc, c=c, heads=heads, rev=rev),
        out_shape=jax.ShapeDtypeStruct((b, lt, w), BF16),
        grid=(b, nb),
        in_specs=[pl.BlockSpec((1, tc, w), lambda bi, j: (bi, blk(j), 0)),
                  pl.BlockSpec((1, tc, w), lambda bi, j: (bi, blk(j), fcol)),
                  pl.BlockSpec((1, tc, w), lambda bi, j: (bi, blk(j), 3)),
                  pl.BlockSpec((1, w), lambda bi, j: (0, 0))],
        out_specs=pl.BlockSpec((1, tc, w), lambda bi, j: (bi, blk(j), 0)),
        scratch_shapes=[pltpu.VMEM((heads, HG_EXPAND, HG_EXPAND), F32)],
        compiler_params=_cparams(("arbitrary", "arbitrary"), 32),
        name="hgrn2_bwd" if rev else "hgrn2_fwd",
    )(zc, zc, zc, lb.reshape(1, w))


def _merge_kernel(oaf_ref, oab_ref, ga_ref, yb_ref, of_ref, ob_ref, gc_ref, hn_ref, gl0_ref, gl1_ref, gl2_ref, wb_ref,
                  o_ref, *, heads):
    gls = (gl0_ref, gl1_ref, gl2_ref)
    ya = (oaf_ref[0].astype(F32) + oab_ref[0].astype(F32)) * ga_ref[0]
    oc = of_ref[0].astype(F32) + ob_ref[0].astype(F32)
    parts = []
    for h in range(heads):
        x = oc[:, h * HG_EXPAND:(h + 1) * HG_EXPAND]
        parts.append(x * lax.rsqrt(jnp.mean(x * x, axis=-1, keepdims=True) + NORM_EPS))
    yc = jnp.concatenate(parts, axis=-1) * hn_ref[...] * _silu(gc_ref[0])
    acc = None
    for bi, y in enumerate((ya.astype(BF16), yb_ref[0], yc.astype(BF16))):
        term = _sigmoid_t(gls[bi][0].astype(F32)) * jnp.dot(y, wb_ref[bi], preferred_element_type=F32)
        acc = term if acc is None else acc + term
    o_ref[0] = acc.astype(BF16)


def _merge(oaf, oab, ga, yb, of, ob, zc, hn, gl, wbr, row0, tl=256, tn=2048):
    b, lt, w = ga.shape
    d = wbr.shape[2]
    tl = _pick(math.gcd(lt, row0), (tl, 64))
    r0 = row0 // tl
    heads = w // HG_EXPAND
    nn = d // tn
    row = lambda n, bi, i: (bi, i + r0, 0)
    glspec = lambda br: pl.BlockSpec((1, tl, tn), lambda n, bi, i: (bi, i + r0, br * nn + n))
    return pl.pallas_call(
        functools.partial(_merge_kernel, heads=heads),
        out_shape=jax.ShapeDtypeStruct((b, lt - row0, d), BF16),
        grid=(d // tn, b, lt // tl - r0),
        in_specs=[pl.BlockSpec((1, tl, w), row),
                  pl.BlockSpec((1, tl, w), row),
                  pl.BlockSpec((1, tl, w), row),
                  pl.BlockSpec((1, tl, w), lambda n, bi, i: (bi, i, 0)),
                  pl.BlockSpec((1, tl, w), row),
                  pl.BlockSpec((1, tl, w), row),
                  pl.BlockSpec((1, tl, w), lambda n, bi, i: (bi, i + r0, 4)),
                  pl.BlockSpec((1, w), lambda n, bi, i: (0, 0)),
                  glspec(0), glspec(1), glspec(2),
                  pl.BlockSpec((3, w, tn), lambda n, bi, i: (0, 0, n))],
        out_specs=pl.BlockSpec((1, tl, tn), lambda n, bi, i: (bi, i, n)),
        compiler_params=_cparams(("arbitrary", "arbitrary", "arbitrary"), 58),
        name="branch_merge",
    )(oaf, oab, ga, yb, of, ob, zc, hn, gl, gl, gl, wbr)


def _proj_post_kernel(a_ref, w_ref, x_ref, g_ref, gate_ref, o_ref):
    mx = jnp.dot(a_ref[0], w_ref[...], preferred_element_type=F32)
    mn = mx * lax.rsqrt(jnp.mean(mx * mx, axis=-1, keepdims=True) + NORM_EPS) * g_ref[...]
    o_ref[0] = x_ref[0] + gate_ref[0, 0] * mn


def _proj_post(a, wout, xa, g, gsel, nc, row0, tl=256):
    b, lt, d = xa.shape
    tl = min(tl, nc)
    nct = nc // tl
    r0 = row0 // tl
    return pl.pallas_call(
        _proj_post_kernel,
        out_shape=jax.ShapeDtypeStruct((b, lt, d), F32),
        grid=(b, lt // tl - r0),
        in_specs=[pl.BlockSpec((1, tl, d), lambda bi, i: (bi, i, 0)),
                  pl.BlockSpec((d, d), lambda bi, i: (0, 0)),
                  pl.BlockSpec((1, tl, d), lambda bi, i: (bi, i + r0, 0)),
                  pl.BlockSpec((1, d), lambda bi, i: (0, 0)),
                  pl.BlockSpec((1, 1, 1, d), lambda bi, i: (bi, jnp.where(i + r0 >= nct, 1, 0), 0, 0))],
        out_specs=pl.BlockSpec((1, tl, d), lambda bi, i: (bi, i + r0, 0)),
        input_output_aliases={2: 0},
        compiler_params=_cparams(("arbitrary", "arbitrary"), 48),
        name="out_proj_post",
    )(a, wout, xa, g.reshape(1, d), gsel)


def _rank_kernel(afft_ref, slot_ref, *, n, cap, tw):
    aff = afft_ref[0]
    bits = lax.bitcast_convert_type(aff, jnp.int32)
    e_num = aff.shape[0]

    def count(mask):
        return jnp.sum(jnp.where(mask, 1.0, 0.0), axis=1, keepdims=True)

    def bit_step(i, thr):
        cand = thr | lax.shift_left(jnp.int32(1), 30 - i)
        return jnp.where(count(bits >= cand) >= cap, cand, thr)

    thr = lax.fori_loop(0, 31, bit_step, jnp.zeros((e_num, 1), jnp.int32))
    gt = bits > thr
    eq = bits == thr
    need = cap - count(gt)
    tri = jnp.where(lax.broadcasted_iota(jnp.int32, (tw, tw), 0) < lax.broadcasted_iota(jnp.int32, (tw, tw), 1),
                    1.0, 0.0).astype(BF16)

    def prefix(mask):
        parts, carry = [], jnp.zeros((e_num, 1), F32)
        for j in range(n // tw):
            m = jnp.where(mask[:, j * tw:(j + 1) * tw], 1.0, 0.0)
            parts.append(jnp.dot(m.astype(BF16), tri, preferred_element_type=F32) + carry)
            carry = carry + jnp.sum(m, axis=1, keepdims=True)
        return jnp.concatenate(parts, axis=1)

    sel = jnp.logical_or(gt, jnp.logical_and(eq, prefix(eq) < need))
    slot_ref[0] = jnp.where(sel, prefix(sel), float(cap)).astype(jnp.int32)


def _rank(afft, cap):
    b, e_num, n = afft.shape
    tw = min(LANES, n)
    return pl.pallas_call(
        functools.partial(_rank_kernel, n=n, cap=cap, tw=tw),
        out_shape=jax.ShapeDtypeStruct((b, e_num, n), jnp.int32),
        grid=(b,),
        in_specs=[pl.BlockSpec((1, e_num, n), lambda bi: (bi, 0, 0))],
        out_specs=pl.BlockSpec((1, e_num, n), lambda bi: (bi, 0, 0)),
        compiler_params=_cparams(("arbitrary",), 32),
        name="ec_rank",
    )(afft)


def _gather_kernel(slot_ref, afft_ref, h_ref, o_ref, g_ref, *, cap):
    e = pl.program_id(1)
    sl = slot_ref[0, pl.ds(e, 1), :]
    hit = lax.broadcasted_iota(jnp.int32, (cap, 1), 0) == sl
    o_ref[0] = jnp.dot(jnp.where(hit, 1.0, 0.0).astype(BF16), h_ref[0], preferred_element_type=F32).astype(BF16)
    gate = jnp.sum(jnp.where(hit, afft_ref[0, pl.ds(e, 1), :], 0.0), axis=1, keepdims=True)
    g_ref[0] = jnp.broadcast_to(gate, (cap, LANES))


def _gather(slot, afft, hs, cap):
    b, e_num, n = slot.shape
    d = hs.shape[2]
    return pl.pallas_call(
        functools.partial(_gather_kernel, cap=cap),
        out_shape=[jax.ShapeDtypeStruct((e_num, b * cap, d), BF16),
                   jax.ShapeDtypeStruct((e_num, b * cap, LANES), F32)],
        grid=(b, e_num),
        in_specs=[pl.BlockSpec((1, e_num, n), lambda bi, e: (bi, 0, 0)),
                  pl.BlockSpec((1, e_num, n), lambda bi, e: (bi, 0, 0)),
                  pl.BlockSpec((1, n, d), lambda bi, e: (bi, 0, 0))],
        out_specs=[pl.BlockSpec((1, cap, d), lambda bi, e: (e, bi, 0)),
                   pl.BlockSpec((1, cap, LANES), lambda bi, e: (e, bi, 0))],
        compiler_params=_cparams(("arbitrary", "arbitrary"), 40),
        name="ec_gather",
    )(slot, afft, hs)


def _ffn_kernel(x_ref, g_ref, w1_ref, w3_ref, w2_ref, o_ref, acc_ref):
    f = pl.program_id(2)
    x = x_ref[0]
    h1 = jnp.dot(x, w1_ref[0, 0].astype(BF16), preferred_element_type=F32)
    h3 = jnp.dot(x, w3_ref[0, 0].astype(BF16), preferred_element_type=F32)
    hid = (_silu(h1) * h3).astype(BF16)

    @pl.when(f == 0)
    def _():
        acc_ref[...] = jnp.zeros_like(acc_ref)

    d = acc_ref.shape[1]
    for n0 in range(0, d, FFN_OUT_CHUNK):
        cols = slice(n0, n0 + FFN_OUT_CHUNK)
        acc_ref[:, cols] += jnp.dot(hid, w2_ref[0, 0, :, cols].astype(BF16), preferred_element_type=F32)

    @pl.when(f == pl.num_programs(2) - 1)
    def _():
        o_ref[0] = (acc_ref[...] * g_ref[0, :, 0:1]).astype(BF16)


def _ffn(xe, gate, w1, w3, w2, layer, tf=512):
    e_num, m, d = xe.shape
    ff = w1.shape[3]
    tm = _pick(m, (1152, 1024, 768, 512, 384, 256, 128, 64, 32, 16, 8))
    once = pl.Buffered(1)
    return pl.pallas_call(
        _ffn_kernel,
        out_shape=jax.ShapeDtypeStruct((e_num, m, d), BF16),
        grid=(e_num, m // tm, ff // tf),
        in_specs=[pl.BlockSpec((1, tm, d), lambda e, i, f: (e, i, 0)),
                  pl.BlockSpec((1, tm, LANES), lambda e, i, f: (e, i, 0)),
                  pl.BlockSpec((1, 1, d, tf), lambda e, i, f: (layer, e, 0, f)),
                  pl.BlockSpec((1, 1, d, tf), lambda e, i, f: (layer, e, 0, f)),
                  pl.BlockSpec((1, 1, tf, d), lambda e, i, f: (layer, e, f, 0))],
        out_specs=pl.BlockSpec((1, tm, d), lambda e, i, f: (e, i, 0), pipeline_mode=once),
        scratch_shapes=[pltpu.VMEM((tm, d), F32)],
        compiler_params=_cparams(("arbitrary", "arbitrary", "arbitrary"), 58),
        name="ec_ffn",
    )(xe, gate, w1, w3, w2)


def _combine_kernel(slot_ref, ye_ref, x_ref, g_ref, gate_ref, o_ref, *, cap, e_num):
    lane = lax.broadcasted_iota(jnp.int32, (1, e_num), 1)
    pos = lax.broadcasted_iota(jnp.int32, (1, cap), 1)
    sl = slot_ref[0]
    y = None
    for e in range(e_num):
        se = jnp.sum(jnp.where(lane == e, sl, 0), axis=1, keepdims=True)
        pt = jnp.where(se == pos, 1.0, 0.0).astype(BF16)
        part = jnp.dot(pt, ye_ref[e], preferred_element_type=F32)
        y = part if y is None else y + part
    yn = y * lax.rsqrt(jnp.mean(y * y, axis=-1, keepdims=True) + NORM_EPS) * g_ref[...]
    o_ref[0] = x_ref[0] + gate_ref[0, 0] * yn


def _combine(slot_c, ye, xa, g, gsel, seg, row0, slot0, cap, final, tt=256):
    b, n, e_num = slot_c.shape
    d = xa.shape[2]
    tt = _pick(math.gcd(n, row0), (tt, 128, 64))
    assert row0 % tt == 0 and slot0 % cap == 0
    r0, s0 = row0 // tt, slot0 // cap
    return pl.pallas_call(
        functools.partial(_combine_kernel, cap=cap, e_num=e_num),
        out_shape=jax.ShapeDtypeStruct((b, n, d) if final else xa.shape, F32),
        grid=(b, n // tt),
        in_specs=[pl.BlockSpec((1, tt, e_num), lambda bi, j: (bi, j, 0)),
                  pl.BlockSpec((e_num, cap, d), lambda bi, j: (0, s0 + bi, 0)),
                  pl.BlockSpec((1, tt, d), lambda bi, j: (bi, r0 + j, 0)),
                  pl.BlockSpec((1, d), lambda bi, j: (0, 0)),
                  pl.BlockSpec((1, 1, 1, d), lambda bi, j: (bi, seg, 0, 0))],
        out_specs=pl.BlockSpec((1, tt, d), lambda bi, j: (bi, (0 if final else r0) + j, 0)),
        input_output_aliases={} if final else {2: 0},
        compiler_params=_cparams(("arbitrary", "arbitrary"), 56),
        name="ec_combine_post",
    )(slot_c, ye, xa, g.reshape(1, d), gsel)


def _sel(mod_c, mod_x, idx):
    b = mod_x.shape[0]
    mc = jnp.broadcast_to(mod_c[jnp.array(idx)][None], (b, len(idx), mod_c.shape[-1]))
    return jnp.stack([mc, mod_x[:, jnp.array(idx)]], axis=1)


def _layer(xa, nc, mod_x, mod_c, p, lam_init, hg_lb, ct, st, with_ctx):
    b, lt, d = xa.shape
    w = d // 2
    a_cols = 3 * w + 2 * RW_DECAY_RANK + 2 * RW_ICL_RANK + RW_GATE_RANK
    col_b = a_cols
    col_c = col_b + 3 * w
    col_g = col_c + 5 * w

    h = _norm_mod(xa, p['norm_g'][0], _sel(mod_c, mod_x, (0, 1)), nc)
    h2 = h.reshape(b * lt, d)
    w_in = p['w_in']
    za = _matmul(h2, w_in[:, :col_b].astype(BF16), F32, tn=a_cols // 3).reshape(b, lt, a_cols)
    zc = _matmul(h2, w_in[:, col_c:col_g].astype(BF16), F32).reshape(b, lt, 5 * w)
    gl = _matmul(h2, w_in[:, col_g:].astype(BF16), BF16).reshape(b, lt, 3 * d)

    streams, ga = _rw_prep(za, nc, p['rw_mu'], p['rw_w0'], p['rw_wB'], p['rw_a0'], p['rw_aB'], p['rw_gB'])
    oaf, oab = _rw_chunk(streams, nc, p['rw_kk'], p['rw_ka'], p['rw_rk'], p['rw_gn'])

    zbr = _proj_rope(h2, w_in[:, col_b:col_c].astype(BF16), ct, st, lt).reshape(b, lt, 3 * w)
    yb = _attention(zbr, nc, p['da_lambda'], p['da_subln'], lam_init, with_ctx)

    of = _hgrn(zc, nc, hg_lb, rev=False)
    ob = _hgrn(zc, nc, hg_lb, rev=True)

    hn = jnp.tile(p['hg_norm'], w // HG_EXPAND).reshape(1, w)
    skip = 0 if with_ctx else nc
    merged = _merge(oaf, oab, ga, yb, of, ob, zc, hn, gl, p['w_branch'].astype(BF16), skip)
    xa = _proj_post(merged, p['w_out'].astype(BF16), xa, p['norm_g'][1], _sel(mod_c, mod_x, (2,)), nc, skip)

    hm, afft = _norm_mod(xa, p['norm_g'][2], _sel(mod_c, mod_x, (3, 4)), nc, router_t=p['moe_router'].T, row0=skip)
    gsel = _sel(mod_c, mod_x, (5,))
    sets = [(nc, lt - nc, 1)] + ([(0, nc, 0)] if with_ctx else [])
    routed, xes, gates = [], [], []
    for row0, nn, seg in sets:
        cap = EC_CAPACITY_FACTOR * nn // N_EXPERTS
        at = afft[:, :, row0 - skip:row0 - skip + nn]
        slot = _rank(at, cap)
        xe_s, gate_s = _gather(slot, at, hm[:, row0 - skip:row0 - skip + nn], cap)
        xes.append(xe_s)
        gates.append(gate_s)
        routed.append((jnp.swapaxes(slot, 1, 2), row0, seg, cap))
    xe = xes[0] if len(xes) == 1 else jnp.concatenate(xes, axis=1)
    gate = gates[0] if len(gates) == 1 else jnp.concatenate(gates, axis=1)
    ye = _ffn(xe, gate, p['moe_w1'], p['moe_w3'], p['moe_w2'], p['layer'])
    slot0 = 0
    for slot_c, row0, seg, cap in routed:
        xa = _combine(slot_c, ye, xa, p['norm_g'][3], gsel, seg, row0, slot0, cap, final=not with_ctx)
        slot0 += b * cap
    return xa


def kernel(x, c, ctx, c_ctx, w_ada, b_ada, norm_g, w_in, rw_mu, rw_w0, rw_wB, rw_a0, rw_aB, rw_gB, rw_kk, rw_ka, rw_rk, rw_gn, da_lambda, da_subln, hg_lb_logits, hg_norm, w_branch, w_out, moe_router, moe_w1, moe_w3, moe_w2):
    b, seq, d = x.shape
    nc = ctx.shape[1]
    depth = w_ada.shape[0]
    ct, st = _rope_tables(seq, nc)
    lb_w = jax.nn.softmax(hg_lb_logits.astype(F32), axis=0)
    hg_lb = jnp.cumsum(lb_w, axis=0) - lb_w[0]
    rows = ((b + 1 + SUBLANES - 1) // SUBLANES) * SUBLANES
    cc = jnp.zeros((rows, d), F32).at[:b].set(c).at[b].set(c_ctx)
    mod = _modulation(cc, w_ada, b_ada)
    xa = jnp.concatenate([ctx, x], axis=1)
    for l in range(depth):
        p = dict(norm_g=norm_g[l], w_in=w_in[l], rw_mu=rw_mu[l], rw_w0=rw_w0[l], rw_wB=rw_wB[l],
                 rw_a0=rw_a0[l], rw_aB=rw_aB[l], rw_gB=rw_gB[l], rw_kk=rw_kk[l], rw_ka=rw_ka[l],
                 rw_rk=rw_rk[l], rw_gn=rw_gn[l], da_lambda=da_lambda[l], da_subln=da_subln[l],
                 hg_norm=hg_norm[l], w_branch=w_branch[l], w_out=w_out[l], moe_router=moe_router[l],
                 moe_w1=moe_w1, moe_w3=moe_w3, moe_w2=moe_w2, layer=l)
        mod_x = mod[l, :b].reshape(b, 6, d)
        mod_c = mod[l, b].reshape(6, d)
        lam_init = 0.8 - 0.6 * math.exp(-0.3 * l)
        xa = _layer(xa, nc, mod_x, mod_c, p, lam_init, hg_lb[l], ct, st, with_ctx=l < depth - 1)
    return xa
```
